```python
import math
import jax, jax.numpy as jnp
from jax import lax
import numpy as np

D_MODEL = 1024
BATCH = 1
SEQ = 16384
DEPTH = 2

CTX_LEN = 256
GRID_W = 64
N_MIXERS = 2
N_HYENA_LAYERS = (DEPTH + 1) // 2
N_ATTN_LAYERS = DEPTH // 2
N_MOD = 6
NORM_EPS = 1e-6
HY_ORDER = 2
HY_SHORT = 3
HY_EMB_BANDS = 16
HY_EMB_DIM = 1 + 2 * HY_EMB_BANDS
HY_FILTER_HIDDEN = 64
HY_DECAY_FAST = 0.3
HY_DECAY_SLOW = 1.5
HY_DECAY_TARGET = 1e-2
HY_FILTER_EPS = 1e-6
DA_HEADS = 8
DA_HEAD_DIM = D_MODEL // DA_HEADS // 2
DA_V_DIM = 2 * DA_HEAD_DIM
ROPE_AXIS_DIM = DA_HEAD_DIM // 2
ROPE_THETA = 10000.0
Q_BLOCK = 128
SUBLN_EPS = 1e-5
N_EXPERTS = 16
EC_CAPACITY = 2
D_EXPERT = 1024

kernel_name = "hyena_diffattn_ec_moe_diffusion_trunk"


def rms_norm(x, g, eps=NORM_EPS):
    xf = x.astype(jnp.float32)
    y = xf * lax.rsqrt(jnp.mean(xf * xf, axis=-1, keepdims=True) + eps) * g.astype(jnp.float32)
    return y.astype(x.dtype)


def adaln(cond, w, b):
    return (jax.nn.silu(cond) @ w + b).reshape(cond.shape[0], N_MOD, 1, D_MODEL)


def modulate(h, shift, scale):
    return h * (1.0 + scale) + shift


def short_conv(u, w, b):
    L = u.shape[1]
    pad = HY_SHORT // 2
    up = jnp.pad(u, ((0, 0), (pad, pad), (0, 0)))
    out = b
    for k in range(HY_SHORT):
        out = out + up[:, k:k + L] * w[k]
    return out


def hyena_filter_spectrum(L, w1, b1, w2, b2, w3, b3, w4, freq):
    f32 = jnp.float32
    t = jnp.linspace(0.0, 1.0, L, dtype=f32)[:, None]
    w = 2.0 * math.pi * jnp.arange(L, dtype=f32)[:, None] / L
    bands = jnp.linspace(1e-4, HY_EMB_BANDS - 1, HY_EMB_BANDS, dtype=f32)[None]
    z = jnp.concatenate([t, jnp.cos(w * bands), -jnp.sin(w * bands)], axis=-1)
    fr = freq.astype(f32)
    hid = jnp.sin(fr * (z @ w1.astype(f32) + b1.astype(f32)))
    hid = jnp.sin(fr * (hid @ w2.astype(f32) + b2.astype(f32)))
    hid = jnp.sin(fr * (hid @ w3.astype(f32) + b3.astype(f32)))
    h = (hid @ w4.astype(f32)).reshape(L, HY_ORDER, 2, D_MODEL)
    max_decay = math.log(HY_DECAY_TARGET) / HY_DECAY_FAST
    min_decay = math.log(HY_DECAY_TARGET) / HY_DECAY_SLOW
    deltas = jnp.linspace(min_decay, max_decay, D_MODEL, dtype=f32)
    decay = jnp.exp(-t * jnp.abs(deltas))
    h = h * decay[:, None, None, :]
    k = jnp.concatenate([h[:, :, 0], jnp.zeros((1, HY_ORDER, D_MODEL), f32), h[:0:-1, :, 1]], axis=0)
    k = k / (jnp.sum(jnp.abs(k), axis=0, keepdims=True) + HY_FILTER_EPS)
    return jnp.fft.rfft(k, axis=0)


def long_conv(z, K):
    L = z.shape[1]
    Z = jnp.fft.rfft(z.astype(jnp.float32), n=2 * L, axis=1)
    y = jnp.fft.irfft(Z * K[None], n=2 * L, axis=1)[:, :L]
    return y.astype(z.dtype)


def hyena_mix(h, w_in, b_in, conv_w, conv_b, f_w1, f_b1, f_w2, f_b2, f_w3, f_b3, f_w4, f_freq,
              skip, w_out, b_out):
    L = h.shape[1]
    u = short_conv(h @ w_in + b_in, conv_w, conv_b)
    v, x1, x2 = jnp.split(u, 3, axis=-1)
    K = hyena_filter_spectrum(L, f_w1, f_b1, f_w2, f_b2, f_w3, f_b3, f_w4, f_freq)
    z = v
    for o, gate in enumerate((x1, x2)):
        z = gate * (long_conv(z, K[:, o]) + skip[o] * z)
    return z @ w_out + b_out


def axial_rope(x, row, col):
    f32 = jnp.float32
    inv = ROPE_THETA ** (-jnp.arange(0, ROPE_AXIS_DIM, 2, dtype=f32) / ROPE_AXIS_DIM)

    def rot(xa, pos):
        ang = pos.astype(f32)[:, None] * inv[None]
        cos = jnp.cos(ang)[None, :, None, :]
        sin = jnp.sin(ang)[None, :, None, :]
        a1, a2 = jnp.split(xa.astype(f32), 2, axis=-1)
        return jnp.concatenate([a1 * cos - a2 * sin, a2 * cos + a1 * sin], axis=-1)

    xr, xc = jnp.split(x, 2, axis=-1)
    return jnp.concatenate([rot(xr, row), rot(xc, col)], axis=-1).astype(x.dtype)


def diff_core(q, k, v, lam):
    B, Q = q.shape[0], q.shape[1]
    s = jnp.einsum("bqcd,bkcd->bcqk", q, k).astype(jnp.float32) * (DA_HEAD_DIM ** -0.5)
    p = jax.nn.softmax(s, axis=-1).reshape(B, DA_HEADS, 2, Q, k.shape[1])
    a = p[:, :, 0] - lam * p[:, :, 1]
    return jnp.einsum("bhqk,bkhe->bqhe", a.astype(v.dtype), v)


def diff_attention(hx, hc, row, col, lam_init, need_ctx_out, w_qkv, q_norm, k_norm,
                   lam_q1, lam_k1, lam_q2, lam_k2, subln, w_out):
    B, N, _ = hx.shape
    f32 = jnp.float32
    lam = (jnp.exp(jnp.sum(lam_q1.astype(f32) * lam_k1.astype(f32)))
           - jnp.exp(jnp.sum(lam_q2.astype(f32) * lam_k2.astype(f32))) + lam_init)

    def heads_q(q):
        return rms_norm(q.reshape(q.shape[0], q.shape[1], 2 * DA_HEADS, DA_HEAD_DIM), q_norm)

    def heads_k(k):
        return rms_norm(k.reshape(k.shape[0], k.shape[1], 2 * DA_HEADS, DA_HEAD_DIM), k_norm)

    def heads_v(v):
        return v.reshape(v.shape[0], v.shape[1], DA_HEADS, DA_V_DIM)

    def finish(o):
        o = rms_norm(o, subln, SUBLN_EPS) * (1.0 - lam_init)
        return o.reshape(o.shape[0], o.shape[1], D_MODEL) @ w_out

    qx, kx, vx = jnp.split(hx @ w_qkv, 3, axis=-1)
    qx = axial_rope(heads_q(qx), row, col)
    kx = axial_rope(heads_k(kx), row, col)
    vx = heads_v(vx)
    kc, vc = jnp.split(hc @ w_qkv[:, D_MODEL:], 2, axis=-1)
    kc = heads_k(kc)
    vc = heads_v(vc)
    k_all = jnp.concatenate([kc, kx], axis=1)
    v_all = jnp.concatenate([vc, vx], axis=1)
    n_blk = N // Q_BLOCK
    qb = qx.reshape(B, n_blk, Q_BLOCK, 2 * DA_HEADS, DA_HEAD_DIM).swapaxes(0, 1)
    ob = lax.map(lambda qq: diff_core(qq, k_all, v_all, lam), qb)
    yx = finish(ob.swapaxes(0, 1).reshape(B, N, DA_HEADS, DA_V_DIM))
    yc = None
    if need_ctx_out:
        qc = heads_q(hc @ w_qkv[:, :D_MODEL])
        yc = finish(diff_core(qc, kc, vc, lam))
    return yx, yc


def ec_moe(h, w_router, w_gate, w_up, w_down):
    B, N, _ = h.shape
    cap = EC_CAPACITY * N // N_EXPERTS
    aff = jax.nn.softmax((h @ w_router).astype(jnp.float32), axis=-1)
    g, idx = lax.top_k(aff.swapaxes(1, 2), cap)
    bidx = jnp.arange(B)[:, None]
    xe = h[bidx, idx.reshape(B, -1)].reshape(B, N_EXPERTS, cap, D_MODEL)
    a = jax.nn.silu(jnp.einsum("becd,edf->becf", xe, w_gate)) * jnp.einsum("becd,edf->becf", xe, w_up)
    y = jnp.einsum("becf,efd->becd", a, w_down) * g[..., None].astype(h.dtype)
    return jnp.zeros_like(h).at[jnp.arange(B)[:, None, None], idx].add(y)


def setup_inputs(seed: int = 0) -> dict:
    key = jax.random.key(seed)
    ks = jax.random.split(key, 40)
    D, F, E = D_MODEL, D_EXPERT, N_EXPERTS
    NHy, NA = N_HYENA_LAYERS, N_ATTN_LAYERS

    def nrm(i, shape, scale):
        return jax.random.normal(ks[i], shape, jnp.float32) * scale

    return {
        "x": nrm(0, (BATCH, SEQ, D), 1.0),
        "c": nrm(1, (BATCH, D), 1.0),
        "ctx": nrm(2, (BATCH, CTX_LEN, D), 1.0),
        "c_ctx": nrm(3, (D,), 1.0),
        "ada_w": nrm(4, (DEPTH, D, N_MOD * D), 0.5 * D ** -0.5),
        "ada_b": nrm(5, (DEPTH, N_MOD * D), 0.02),
        "norm_mix": 1.0 + nrm(6, (DEPTH, D), 0.02),
        "norm_ffn": 1.0 + nrm(7, (DEPTH, D), 0.02),
        "hy_w_in": nrm(8, (NHy, D, 3 * D), D ** -0.5),
        "hy_b_in": nrm(9, (NHy, 3 * D), 0.02),
        "hy_conv_w": nrm(10, (NHy, HY_SHORT, 3 * D), HY_SHORT ** -0.5),
        "hy_conv_b": nrm(11, (NHy, 3 * D), 0.02),
        "hy_f_w1": nrm(12, (NHy, HY_EMB_DIM, HY_FILTER_HIDDEN), HY_EMB_DIM ** -0.5),
        "hy_f_b1": nrm(13, (NHy, HY_FILTER_HIDDEN), 0.1),
        "hy_f_w2": nrm(14, (NHy, HY_FILTER_HIDDEN, HY_FILTER_HIDDEN), HY_FILTER_HIDDEN ** -0.5),
        "hy_f_b2": nrm(15, (NHy, HY_FILTER_HIDDEN), 0.1),
        "hy_f_w3": nrm(16, (NHy, HY_FILTER_HIDDEN, HY_FILTER_HIDDEN), HY_FILTER_HIDDEN ** -0.5),
        "hy_f_b3": nrm(17, (NHy, HY_FILTER_HIDDEN), 0.1),
        "hy_f_w4": nrm(18, (NHy, HY_FILTER_HIDDEN, HY_ORDER * 2 * D), HY_FILTER_HIDDEN ** -0.5),
        "hy_f_freq": 1.0 + nrm(19, (NHy, HY_FILTER_HIDDEN), 0.02),
        "hy_skip": nrm(20, (NHy, HY_ORDER, D), 0.5),
        "hy_w_out": nrm(21, (NHy, D, D), D ** -0.5),
        "hy_b_out": nrm(22, (NHy, D), 0.02),
        "da_w_qkv": nrm(23, (NA, D, 3 * D), D ** -0.5),
        "da_q_norm": 1.0 + nrm(24, (NA, DA_HEAD_DIM), 0.02),
        "da_k_norm": 1.0 + nrm(25, (NA, DA_HEAD_DIM), 0.02),
        "da_lam_q1": nrm(26, (NA, DA_HEAD_DIM), 0.1),
        "da_lam_k1": nrm(27, (NA, DA_HEAD_DIM), 0.1),
        "da_lam_q2": nrm(28, (NA, DA_HEAD_DIM), 0.1),
        "da_lam_k2": nrm(29, (NA, DA_HEAD_DIM), 0.1),
        "da_subln": 1.0 + nrm(30, (NA, DA_V_DIM), 0.02),
        "da_w_out": nrm(31, (NA, D, D), D ** -0.5),
        "moe_router": nrm(32, (DEPTH, D, E), D ** -0.5),
        "moe_w_gate": nrm(33, (DEPTH, E, D, F), D ** -0.5),
        "moe_w_up": nrm(34, (DEPTH, E, D, F), D ** -0.5),
        "moe_w_down": nrm(35, (DEPTH, E, F, D), F ** -0.5),
    }


def reference(x, c, ctx, c_ctx, ada_w, ada_b, norm_mix, norm_ffn,
              hy_w_in, hy_b_in, hy_conv_w, hy_conv_b, hy_f_w1, hy_f_b1, hy_f_w2, hy_f_b2,
              hy_f_w3, hy_f_b3, hy_f_w4, hy_f_freq, hy_skip, hy_w_out, hy_b_out,
              da_w_qkv, da_q_norm, da_k_norm, da_lam_q1, da_lam_k1, da_lam_q2, da_lam_k2,
              da_subln, da_w_out, moe_router, moe_w_gate, moe_w_up, moe_w_down):
    n_lat = x.shape[1]
    rows = n_lat // GRID_W
    row = jnp.repeat(jnp.arange(rows, dtype=jnp.int32), GRID_W)
    col = jnp.tile(jnp.arange(GRID_W, dtype=jnp.int32), rows)
    for i in range(DEPTH):
        last = i == DEPTH - 1
        j = i // N_MIXERS
        mx = adaln(c, ada_w[i], ada_b[i])
        mc = adaln(c_ctx[None], ada_w[i], ada_b[i])
        hx = modulate(rms_norm(x, norm_mix[i]), mx[:, 0], mx[:, 1])
        hc = modulate(rms_norm(ctx, norm_mix[i]), mc[:, 0], mc[:, 1])
        if i % N_MIXERS == 0:
            hp = (hy_w_in[j], hy_b_in[j], hy_conv_w[j], hy_conv_b[j], hy_f_w1[j], hy_f_b1[j],
                  hy_f_w2[j], hy_f_b2[j], hy_f_w3[j], hy_f_b3[j], hy_f_w4[j], hy_f_freq[j],
                  hy_skip[j], hy_w_out[j], hy_b_out[j])
            yx = hyena_mix(hx, *hp)
            yc = None if last else hyena_mix(hc, *hp)
        else:
            lam_init = 0.8 - 0.6 * math.exp(-0.3 * i)
            yx, yc = diff_attention(hx, hc, row, col, lam_init, not last, da_w_qkv[j],
                                    da_q_norm[j], da_k_norm[j], da_lam_q1[j], da_lam_k1[j],
                                    da_lam_q2[j], da_lam_k2[j], da_subln[j], da_w_out[j])
        x = x + mx[:, 2] * yx
        mp = (moe_router[i], moe_w_gate[i], moe_w_up[i], moe_w_down[i])
        if not last:
            ctx = ctx + mc[:, 2] * yc
            ctx = ctx + mc[:, 5] * ec_moe(modulate(rms_norm(ctx, norm_ffn[i]), mc[:, 3], mc[:, 4]), *mp)
        x = x + mx[:, 5] * ec_moe(modulate(rms_norm(x, norm_ffn[i]), mx[:, 3], mx[:, 4]), *mp)
    return x
```

```python
import functools
import math

import jax
import jax.numpy as jnp
import numpy as np
from jax import lax
from jax.experimental import pallas as pl
from jax.experimental.pallas import tpu as pltpu

F32 = jnp.float32
BF16 = jnp.bfloat16
I32 = jnp.int32
HIGHEST = lax.Precision.HIGHEST

D_MODEL = 1024
N_MOD = 6
NORM_EPS = 1e-6
GRID_W = 64
HY_ORDER = 2
HY_SHORT = 3
HY_EMB_BANDS = 16
HY_EMB_DIM = 1 + 2 * HY_EMB_BANDS
HY_FILTER_HIDDEN = 64
HY_DECAY_FAST = 0.3
HY_DECAY_SLOW = 1.5
HY_DECAY_TARGET = 1e-2
HY_FILTER_EPS = 1e-6
Z_WIDTH = 40
Z_SIGN_COL = 33
DA_HEADS = 8
DA_HEAD_DIM = 64
DA_V_DIM = 128
ROPE_AXIS_DIM = 32
ROPE_THETA = 10000.0
SUBLN_EPS = 1e-5
N_EXPERTS = 16
EC_CAPACITY = 2
D_EXPERT = 1024
TOK_BLK = 256
ROW_ALIGN = 16
WIN = TOK_BLK + ROW_ALIGN

VMEM_LIMIT = 56 * 1024 * 1024


def _cp(*sem):
    return pltpu.CompilerParams(dimension_semantics=sem, vmem_limit_bytes=VMEM_LIMIT)


def _const_spec(shape):
    nd = len(shape)
    return pl.BlockSpec(shape, lambda *_: (0,) * nd)


def _dot(a, b):
    return jnp.dot(a, b, preferred_element_type=F32)


def _dot_nt(a, b):
    return lax.dot_general(a, b, (((1,), (1,)), ((), ())), preferred_element_type=F32)


def _norm_mod(x, mod, eps=NORM_EPS):
    ms = jnp.mean(x * x, axis=-1, keepdims=True)
    return x * lax.rsqrt(ms + eps) * (mod[0:1] * (1.0 + mod[1:2])) + mod[2:3]


def _lane_tile(x, reps):
    return jnp.concatenate([x] * reps, axis=1) if reps > 1 else x


def _adaln_kernel(c_ref, w_ref, b_ref, o_ref):
    c = c_ref[...]
    s = c / (1.0 + jnp.exp(-c))
    o_ref[0] = jnp.dot(s, w_ref[0], precision=HIGHEST, preferred_element_type=F32) + b_ref[0]


def _adaln(cond8, ada_w, ada_b):
    depth, d, nout = ada_w.shape
    tn = 1536
    return pl.pallas_call(
        _adaln_kernel,
        grid=(depth, nout // tn),
        in_specs=[_const_spec((8, d)),
                  pl.BlockSpec((1, d, tn), lambda l, j: (l, 0, j)),
                  pl.BlockSpec((1, 1, tn), lambda l, j: (l, 0, j))],
        out_specs=pl.BlockSpec((1, 8, tn), lambda l, j: (l, 0, j)),
        out_shape=jax.ShapeDtypeStruct((depth, 8, nout), F32),
        compiler_params=_cp("parallel", "parallel"),
        name="adaln",
    )(cond8, ada_w, ada_b.reshape(depth, 1, nout))


def _mod_rows(norm_g, mods, row, k0):
    d = D_MODEL
    shift = mods[row, k0 * d:(k0 + 1) * d]
    scale = mods[row, (k0 + 1) * d:(k0 + 2) * d]
    z = jnp.zeros((5, d), F32)
    return jnp.concatenate([norm_g[None], scale[None], shift[None], z], axis=0)


HALO = 16


def _hy_in_kernel(x_ref, xp_ref, xn_ref, mod_ref, w_ref, b_ref, cw_ref, v_ref, x1_ref, x2_ref, *, tm, n_rows):
    i = pl.program_id(0)
    mod = mod_ref[...]
    hm = _norm_mod(x_ref[...], mod).astype(BF16)
    hp = _norm_mod(xp_ref[...], mod).astype(BF16)
    hn = _norm_mod(xn_ref[...], mod).astype(BF16)
    hcat = jnp.concatenate([hp, hm, hn], axis=0)
    row = lax.broadcasted_iota(I32, (tm + 2 * HALO, 1), 0) + (i * tm - HALO)
    valid = jnp.logical_and(row >= 0, row < n_rows)
    d = D_MODEL
    for c, o_ref in enumerate((v_ref, x1_ref, x2_ref)):
        u = _dot(hcat, w_ref[:, c * d:(c + 1) * d]) + b_ref[:, c * d:(c + 1) * d]
        u = jnp.where(valid, u, 0.0)
        cw = cw_ref[:, c * d:(c + 1) * d]
        y = (cw[3:4] + cw[0:1] * u[HALO - 1:HALO - 1 + tm] + cw[1:2] * u[HALO:HALO + tm]
             + cw[2:3] * u[HALO + 1:HALO + 1 + tm])
        o_ref[...] = y.astype(BF16)


def _hy_in(x, mod, w_bf, b_in, conv_w, conv_b):
    n, d = x.shape
    tm = min(512, n)
    nh = n // HALO
    cw = jnp.concatenate([conv_w, conv_b[None], jnp.zeros((4, 3 * d), F32)], axis=0)
    out = jax.ShapeDtypeStruct((n, d), BF16)
    row_spec = pl.BlockSpec((tm, d), lambda i: (i, 0))
    return pl.pallas_call(
        functools.partial(_hy_in_kernel, tm=tm, n_rows=n),
        grid=(n // tm,),
        in_specs=[row_spec,
                  pl.BlockSpec((HALO, d), lambda i: (jnp.maximum(i * (tm // HALO) - 1, 0), 0)),
                  pl.BlockSpec((HALO, d), lambda i: (jnp.minimum((i + 1) * (tm // HALO), nh - 1), 0)),
                  _const_spec((8, d)), _const_spec((d, 3 * d)), _const_spec((1, 3 * d)),
                  _const_spec((8, 3 * d))],
        out_specs=[row_spec, row_spec, row_spec],
        out_shape=[out, out, out],
        compiler_params=_cp("parallel"),
        name="hyena_in",
    )(x, x, x, mod, w_bf, b_in.reshape(1, 3 * d), cw)


def _filter_kernel(z_ref, w1_ref, b1_ref, w2_ref, b2_ref, w3_ref, b3_ref, fr_ref, w4_ref, dl_ref,
                   k_ref, asum_ref):
    i = pl.program_id(0)
    z = z_ref[...]
    fr = fr_ref[...]

    def lin(a, w_ref, b_ref):
        return jnp.dot(a, w_ref[...], precision=HIGHEST, preferred_element_type=F32) + b_ref[...]

    hid = jnp.sin(fr * lin(z, w1_ref, b1_ref))
    hid = jnp.sin(fr * lin(hid, w2_ref, b2_ref))
    hid = jnp.sin(fr * lin(hid, w3_ref, b3_ref))
    h = _dot(hid.astype(BF16), w4_ref[0])
    t = z[:, 0:1]
    sgn = z[:, Z_SIGN_COL:Z_SIGN_COL + 1]
    k = h * jnp.exp(-t * dl_ref[...]) * sgn

    @pl.when(i == 0)
    def _():
        asum_ref[...] = jnp.zeros_like(asum_ref)

    asum_ref[0:1, :] += jnp.sum(jnp.abs(k), axis=0, keepdims=True)
    k_ref[...] = k.astype(BF16)


def _filter_positions(seq):
    r = np.arange(2 * seq)
    pos = np.where(r < seq, r, 2 * seq - r).astype(np.float64)
    pos = np.minimum(pos, seq - 1)
    t = pos / (seq - 1)
    w = 2.0 * np.pi * pos / seq
    bands = np.linspace(1e-4, HY_EMB_BANDS - 1, HY_EMB_BANDS)[None]
    z = np.zeros((2 * seq, Z_WIDTH), np.float64)
    z[:, 0] = t
    z[:, 1:1 + HY_EMB_BANDS] = np.cos(w[:, None] * bands)
    z[:, 1 + HY_EMB_BANDS:HY_EMB_DIM] = -np.sin(w[:, None] * bands)
    z[:, Z_SIGN_COL] = np.where(r < seq, 1.0, np.where(r == seq, 0.0, -1.0))
    return jnp.asarray(z.astype(np.float32))


def _hyena_filter_time(seq, f_w1, f_b1, f_w2, f_b2, f_w3, f_b3, f_w4, f_freq):
    d, hid = D_MODEL, HY_FILTER_HIDDEN
    od = HY_ORDER * d
    tr = min(512, seq)
    z = _filter_positions(seq)
    w1p = jnp.concatenate([f_w1, jnp.zeros((Z_WIDTH - HY_EMB_DIM, hid), F32)], axis=0)
    w4d = f_w4.reshape(hid, HY_ORDER, 2, d).transpose(2, 0, 1, 3).reshape(2, hid, od).astype(BF16)
    max_decay = math.log(HY_DECAY_TARGET) / HY_DECAY_FAST
    min_decay = math.log(HY_DECAY_TARGET) / HY_DECAY_SLOW
    deltas = np.abs(np.linspace(min_decay, max_decay, d, dtype=np.float32))
    dl = jnp.asarray(np.tile(deltas, HY_ORDER)[None])
    tiles_per_dir = seq // tr
    k_t, asum = pl.pallas_call(
        _filter_kernel,
        grid=(2 * seq // tr,),
        in_specs=[pl.BlockSpec((tr, Z_WIDTH), lambda i: (i, 0)),
                  _const_spec((Z_WIDTH, hid)), _const_spec((1, hid)),
                  _const_spec((hid, hid)), _const_spec((1, hid)),
                  _const_spec((hid, hid)), _const_spec((1, hid)),
                  _const_spec((1, hid)),
                  pl.BlockSpec((1, hid, od), lambda i: (i // tiles_per_dir, 0, 0)),
                  _const_spec((1, od))],
        out_specs=[pl.BlockSpec((tr, od), lambda i: (i, 0)), _const_spec((8, od))],
        out_shape=[jax.ShapeDtypeStruct((2 * seq, od), BF16), jax.ShapeDtypeStruct((8, od), F32)],
        compiler_params=_cp("arbitrary"),
        name="hyena_filter",
    )(z, w1p, f_b1.reshape(1, hid), f_w2, f_b2.reshape(1, hid), f_w3, f_b3.reshape(1, hid),
      f_freq.reshape(1, hid), w4d, dl)
    return k_t, asum


def _fft_factors(seq):
    n = 2 * seq
    n1 = 256 if n >= 32768 else 32
    n2 = n // n1
    assert n1 * n2 == n and n2 % 16 == 0 and n1 % 32 == 0
    return n1, n2


@functools.lru_cache(maxsize=None)
def _fft_consts(seq):
    n = 2 * seq
    n1, n2 = _fft_factors(seq)
    k1 = np.arange(n1, dtype=np.float64)[:, None] + 0.5
    th1 = 2.0 * np.pi * k1 * np.arange(n1, dtype=np.float64)[None] / n1
    f1_full = np.concatenate([np.cos(th1), -np.sin(th1)], axis=0)
    f1_half = f1_full[:, :n1 // 2]
    tw = 2.0 * np.pi * k1 * np.arange(n2, dtype=np.float64)[None] / n
    tre, tim = np.cos(tw), -np.sin(tw)
    h2 = n2 // 2
    th2 = 2.0 * np.pi * np.arange(h2, dtype=np.float64)[:, None] * np.arange(n2, dtype=np.float64)[None] / n2
    c2, s2 = np.cos(th2), np.sin(th2)
    m2f = np.block([[c2, s2], [-s2, c2]])
    m2i = np.block([[c2.T, -s2.T], [s2.T, c2.T]])
    thb = th1[:, :n1 // 2].T
    gi = (2.0 / n) * np.concatenate([np.cos(thb), -np.sin(thb)], axis=1)
    bf = lambda a: jnp.asarray(a.astype(np.float32)).astype(BF16)
    f32 = lambda a: jnp.asarray(a.astype(np.float32))
    return dict(f1_full=bf(f1_full), f1_half=bf(f1_half), tre=f32(tre), tim=f32(tim),
                treT=f32(tre.T), timT=f32(tim.T), m2f=bf(m2f), m2i=bf(m2i), gi=bf(gi))


def _pick_col(tbl, idx):
    lane = lax.broadcasted_iota(I32, tbl.shape, 1)
    return jnp.sum(jnp.where(lane == idx, tbl, 0.0), axis=1, keepdims=True)


def _fft_s1_kernel(f1_ref, x_ref, tre_ref, tim_ref, o_ref, *, n1):
    j = pl.program_id(0)
    a = _dot(f1_ref[...], x_ref[...])
    are, aim = a[:n1], a[n1:]
    tre = _pick_col(tre_ref[...], j)
    tim = _pick_col(tim_ref[...], j)
    o_ref[0] = (are * tre - aim * tim).astype(BF16)
    o_ref[1] = (are * tim + aim * tre).astype(BF16)


def _fft_stage1(x2d, f1, tre, tim, n1, n2, chans):
    r = x2d.shape[0]
    tc = D_MODEL
    g = chans // tc
    return pl.pallas_call(
        functools.partial(_fft_s1_kernel, n1=n1),
        grid=(n2, g),
        in_specs=[_const_spec((2 * n1, r)),
                  pl.BlockSpec((r, tc), lambda j, c: (0, j * g + c)),
                  _const_spec((n1, n2)), _const_spec((n1, n2))],
        out_specs=pl.BlockSpec((2, n1, tc), lambda j, c: (0, 0, j * g + c)),
        out_shape=jax.ShapeDtypeStruct((2, n1, n2 * chans), BF16),
        compiler_params=_cp("parallel", "parallel"),
        name="fft_stage1",
    )(f1, x2d, tre, tim)


def _fft_s2_kernel(m2f_ref, b_ref, o_ref):
    b = jnp.concatenate([b_ref[0, 0], b_ref[1, 0]], axis=0)
    o_ref[0] = _dot(m2f_ref[...], b)


def _fft_stage2(b4, m2f, n1, n2, chans):
    tc = D_MODEL
    return pl.pallas_call(
        _fft_s2_kernel,
        grid=(n1, chans // tc),
        in_specs=[_const_spec((n2, 2 * n2)),
                  pl.BlockSpec((2, 1, n2, tc), lambda k, c: (0, k, 0, c))],
        out_specs=pl.BlockSpec((1, n2, tc), lambda k, c: (k, 0, c)),
        out_shape=jax.ShapeDtypeStruct((n1, n2, chans), F32),
        compiler_params=_cp("parallel", "parallel"),
        name="fft_stage2",
    )(m2f, b4)


def _fft_mid_kernel(m2f_ref, m2i_ref, b_ref, k_ref, inv_ref, treT_ref, timT_ref, o_ref, *, n2):
    k1 = pl.program_id(0)
    h2 = n2 // 2
    b = jnp.concatenate([b_ref[0, 0], b_ref[1, 0]], axis=0)
    x = _dot(m2f_ref[...], b)
    kk = k_ref[0] * inv_ref[...]
    xre, xim = x[:h2], x[h2:]
    kre, kim = kk[:h2], kk[h2:]
    y = jnp.concatenate([xre * kre - xim * kim, xre * kim + xim * kre], axis=0).astype(BF16)
    c = _dot(m2i_ref[...], y)
    cre, cim = c[:n2], c[n2:]
    tre = _pick_col(treT_ref[...], k1)
    tim = _pick_col(timT_ref[...], k1)
    o_ref[0, 0] = (cre * tre + cim * tim).astype(BF16)
    o_ref[1, 0] = (cim * tre - cre * tim).astype(BF16)


def _fft_mid(b4, kspec, inv_norm, order, cst, n1, n2):
    d = D_MODEL
    return pl.pallas_call(
        functools.partial(_fft_mid_kernel, n2=n2),
        grid=(n1,),
        in_specs=[_const_spec((n2, 2 * n2)), _const_spec((2 * n2, n2)),
                  pl.BlockSpec((2, 1, n2, d), lambda k: (0, k, 0, 0)),
                  pl.BlockSpec((1, n2, d), lambda k: (k, 0, order)),
                  pl.BlockSpec((1, d), lambda k: (0, order)),
                  _const_spec((n2, n1)), _const_spec((n2, n1))],
        out_specs=pl.BlockSpec((2, 1, n2, d), lambda k: (0, k, 0, 0)),
        out_shape=jax.ShapeDtypeStruct((2, n1, n2, d), BF16),
        compiler_params=_cp("parallel"),
        name="fft_mid",
    )(cst["m2f"], cst["m2i"], b4, kspec, inv_norm, cst["treT"], cst["timT"])


def _fft_last_kernel(gi_ref, c_ref, gate_ref, z_ref, skip_ref, o_ref):
    c = jnp.concatenate([c_ref[0], c_ref[1]], axis=0)
    y = _dot(gi_ref[...], c)
    z = z_ref[...].astype(F32)
    o_ref[...] = (gate_ref[...].astype(F32) * (y + skip_ref[...] * z)).astype(BF16)


def _fft_last(c3, gate2d, z2d, skip_row, gi, n1, n2):
    d = D_MODEL
    r = n1 // 2
    col = pl.BlockSpec((r, d), lambda j: (0, j))
    return pl.pallas_call(
        _fft_last_kernel,
        grid=(n2,),
        in_specs=[_const_spec((r, 2 * n1)),
                  pl.BlockSpec((2, n1, d), lambda j: (0, 0, j)),
                  col, col, _const_spec((1, d))],
        out_specs=col,
        out_shape=jax.ShapeDtypeStruct((r, n2 * d), BF16),
        compiler_params=_cp("parallel"),
        name="fft_last",
    )(gi, c3, gate2d, z2d, skip_row)


def _long_conv_gate(z_in, gate, skip_row, kspec, inv_norm, order, seq):
    d = D_MODEL
    n1, n2 = _fft_factors(seq)
    cst = _fft_consts(seq)
    z2d = z_in.reshape(n1 // 2, n2 * d)
    b = _fft_stage1(z2d, cst["f1_half"], cst["tre"], cst["tim"], n1, n2, d)
    c = _fft_mid(b.reshape(2, n1, n2, d), kspec, inv_norm, order, cst, n1, n2)
    out = _fft_last(c.reshape(2, n1, n2 * d), gate.reshape(n1 // 2, n2 * d), z2d, skip_row, cst["gi"], n1, n2)
    return out.reshape(seq, d)


def _filter_spectrum(seq, fparams):
    d = D_MODEL
    od = HY_ORDER * d
    n1, n2 = _fft_factors(seq)
    cst = _fft_consts(seq)
    k_t, asum = _hyena_filter_time(seq, *fparams)
    b = _fft_stage1(k_t.reshape(n1, n2 * od), cst["f1_full"], cst["tre"], cst["tim"], n1, n2, od)
    kspec = _fft_stage2(b.reshape(2, n1, n2, od), cst["m2f"], n1, n2, od)
    return kspec, asum


def _proj_res_kernel(a_ref, w_ref, b_ref, g_ref, x_ref, o_ref):
    y = _dot(a_ref[...], w_ref[...]) + b_ref[...]
    o_ref[...] = x_ref[...] + g_ref[...] * y


def _proj_res(a, w_bf, b_row, gate_row, xres):
    n, d = xres.shape
    tm = min(512, n)
    row = pl.BlockSpec((tm, d), lambda i: (i, 0))
    return pl.pallas_call(
        _proj_res_kernel,
        grid=(n // tm,),
        in_specs=[row, _const_spec((d, d)), _const_spec((1, d)), _const_spec((1, d)), row],
        out_specs=row,
        out_shape=jax.ShapeDtypeStruct((n, d), F32),
        compiler_params=_cp("parallel"),
        name="proj_residual",
    )(a, w_bf, b_row, gate_row, xres)


def _ffn_in_kernel(x_ref, mod_ref, wt_ref, h_ref, aff_ref):
    h = _norm_mod(x_ref[...], mod_ref[...])
    hi = h.astype(BF16)
    lo = (h - hi.astype(F32)).astype(BF16)
    wt = wt_ref[...]
    whi = wt.astype(BF16)
    wlo = (wt - whi.astype(F32)).astype(BF16)
    logits = _dot_nt(whi, hi) + (_dot_nt(whi, lo) + _dot_nt(wlo, hi))
    m = jnp.max(logits, axis=0, keepdims=True)
    p = jnp.exp(logits - m)
    aff_ref[...] = p / jnp.sum(p, axis=0, keepdims=True)
    h_ref[...] = hi


def _ffn_in(x, mod, w_router):
    n, d = x.shape
    e = N_EXPERTS
    tm = min(512, n)
    return pl.pallas_call(
        _ffn_in_kernel,
        grid=(n // tm,),
        in_specs=[pl.BlockSpec((tm, d), lambda i: (i, 0)), _const_spec((8, d)), _const_spec((e, d))],
        out_specs=[pl.BlockSpec((tm, d), lambda i: (i, 0)), pl.BlockSpec((e, tm), lambda i: (0, i))],
        out_shape=[jax.ShapeDtypeStruct((n, d), BF16), jax.ShapeDtypeStruct((e, n), F32)],
        compiler_params=_cp("parallel"),
        name="moe_router",
    )(x, mod, w_router.T)


def _select_kernel(a_ref, pos_ref, s0_ref, sel_ref, *, cap, nblk):
    e = N_EXPERTS
    bits = pltpu.bitcast(a_ref[...], I32)

    def bisect(i, thr):
        cand = thr | jnp.left_shift(jnp.int32(1), 30 - i)
        cnt = jnp.sum(jnp.where(bits >= cand, 1.0, 0.0), axis=1, keepdims=True)
        return jnp.where(cnt >= cap, cand, thr)

    thr = lax.fori_loop(0, 31, bisect, jnp.zeros((e, 1), I32))
    n_gt = jnp.sum(jnp.where(bits > thr, 1.0, 0.0), axis=1, keepdims=True)
    need = cap - n_gt
    r = lax.broadcasted_iota(I32, (TOK_BLK, TOK_BLK), 0)
    c = lax.broadcasted_iota(I32, (TOK_BLK, TOK_BLK), 1)
    upper = jnp.where(r < c, 1.0, 0.0).astype(BF16)

    def pass1(j, carry):
        sl = pl.ds(pl.multiple_of(j * TOK_BLK, TOK_BLK), TOK_BLK)
        bj = pltpu.bitcast(a_ref[:, sl], I32)
        eq = jnp.where(bj == thr, 1.0, 0.0)
        rank = _dot(eq.astype(BF16), upper) + carry
        keep = jnp.logical_or(bj > thr, jnp.logical_and(bj == thr, rank < need))
        sel_ref[:, sl] = jnp.where(keep, 1.0, 0.0)
        return carry + jnp.sum(eq, axis=1, keepdims=True)

    lax.fori_loop(0, nblk, pass1, jnp.zeros((e, 1), F32))

    def pass2(j, carry):
        sl = pl.ds(pl.multiple_of(j * TOK_BLK, TOK_BLK), TOK_BLK)
        s = sel_ref[:, sl]
        slot = _dot(s.astype(BF16), upper) + carry
        pos_ref[:, sl] = jnp.where(s > 0.5, slot, -1.0).astype(I32)
        s0_ref[j] = jnp.broadcast_to(carry, (e, 128)).astype(I32)
        return carry + jnp.sum(s, axis=1, keepdims=True)

    lax.fori_loop(0, nblk, pass2, jnp.zeros((e, 1), F32))


def _select(aff_t, cap):
    e, n = aff_t.shape
    nblk = n // TOK_BLK
    return pl.pallas_call(
        functools.partial(_select_kernel, cap=cap, nblk=nblk),
        out_shape=[jax.ShapeDtypeStruct((e, n), I32), jax.ShapeDtypeStruct((nblk, e, 128), I32)],
        scratch_shapes=[pltpu.VMEM((e, n), F32)],
        compiler_params=pltpu.CompilerParams(vmem_limit_bytes=VMEM_LIMIT),
        name="moe_select",
    )(aff_t)


def _window_start(s0_ref, blk, e_idx, cap):
    s0 = s0_ref[blk * N_EXPERTS + e_idx]
    s0a = lax.shift_left(lax.shift_right_logical(s0, 4), 4)
    return pl.multiple_of(jnp.minimum(s0a, cap - ROW_ALIGN), ROW_ALIGN)


def _gather_kernel(s0_ref, pos_ref, h_ref, xe_ref, *, cap, sub):
    e_idx = pl.program_id(0)
    c = pl.program_id(1)

    @pl.when(c == 0)
    def _():
        xe_ref[...] = jnp.zeros_like(xe_ref)

    rows = lax.broadcasted_iota(I32, (WIN, TOK_BLK), 0)

    def body(i, carry):
        start = _window_start(s0_ref, c * sub + i, e_idx, cap)
        tok = pl.ds(pl.multiple_of(i * TOK_BLK, TOK_BLK), TOK_BLK)
        rel = pos_ref[0, :, tok] - start
        onehot = jnp.where(rows == rel, 1.0, 0.0).astype(BF16)
        got = _dot(onehot, h_ref[tok, :]).astype(BF16)
        win = pl.ds(start, WIN)
        xe_ref[0, win, :] = xe_ref[0, win, :] + got
        return carry

    lax.fori_loop(0, sub, body, 0)


def _gather(s0_flat, pos, h, cap, cap_pad):
    e, n = pos.shape
    d = h.shape[1]
    chunk = min(2048, n)
    sub = chunk // TOK_BLK
    return pl.pallas_call(
        functools.partial(_gather_kernel, cap=cap, sub=sub),
        grid_spec=pltpu.PrefetchScalarGridSpec(
            num_scalar_prefetch=1,
            grid=(e, n // chunk),
            in_specs=[pl.BlockSpec((1, 1, chunk), lambda ei, c, s0: (ei, 0, c)),
                      pl.BlockSpec((chunk, d), lambda ei, c, s0: (c, 0))],
            out_specs=pl.BlockSpec((1, cap_pad, d), lambda ei, c, s0: (ei, 0, 0)),
        ),
        out_shape=jax.ShapeDtypeStruct((e, cap_pad, d), BF16),
        compiler_params=_cp("parallel", "arbitrary"),
        name="moe_gather",
    )(s0_flat, pos.reshape(e, 1, n), h)


def _expert_kernel(x_ref, wg_ref, wu_ref, wd_ref, y_ref, *, n_real):
    j = pl.program_id(1)

    @pl.when(j < n_real)
    def _():
        x = x_ref[0]
        g = _dot(x, wg_ref[0])
        u = _dot(x, wu_ref[0])
        a = (g / (1.0 + jnp.exp(-g))) * u
        y_ref[0] = _dot(a.astype(BF16), wd_ref[0]).astype(BF16)

    @pl.when(j >= n_real)
    def _():
        y_ref[0] = jnp.zeros_like(y_ref[0])


def _experts(xe, wg, wu, wd, cap):
    e, cap_pad, d = xe.shape
    f = wg.shape[2]
    tm = min(256, cap)
    tile = pl.BlockSpec((1, tm, d), lambda ei, j: (ei, j, 0))
    return pl.pallas_call(
        functools.partial(_expert_kernel, n_real=cap // tm),
        grid=(e, cap_pad // tm),
        in_specs=[tile,
                  pl.BlockSpec((1, d, f), lambda ei, j: (ei, 0, 0)),
                  pl.BlockSpec((1, d, f), lambda ei, j: (ei, 0, 0)),
                  pl.BlockSpec((1, f, d), lambda ei, j: (ei, 0, 0))],
        out_specs=tile,
        out_shape=jax.ShapeDtypeStruct((e, cap_pad, d), BF16),
        compiler_params=_cp("parallel", "arbitrary"),
        name="moe_experts",
    )(xe, wg, wu, wd)


def _combine_kernel(s0_ref, post_ref, gt_ref, y_ref, x_ref, gate_ref, o_ref, *, cap, sub):
    c = pl.program_id(0)
    e_idx = pl.program_id(1)

    @pl.when(e_idx == 0)
    def _():
        o_ref[...] = jnp.zeros_like(o_ref)

    lanes = lax.broadcasted_iota(I32, (TOK_BLK, WIN), 1)
    elane = lax.broadcasted_iota(I32, (TOK_BLK, N_EXPERTS), 1) == e_idx

    def body(i, carry):
        start = _window_start(s0_ref, c * sub + i, e_idx, cap)
        tok = pl.ds(pl.multiple_of(i * TOK_BLK, TOK_BLK), TOK_BLK)
        pcol = jnp.sum(jnp.where(elane, post_ref[tok, :].astype(F32), 0.0), axis=1, keepdims=True)
        gcol = jnp.sum(jnp.where(elane, gt_ref[tok, :], 0.0), axis=1, keepdims=True)
        rel = pcol.astype(I32) - start
        onehot = jnp.where(lanes == rel, 1.0, 0.0).astype(BF16)
        strip = y_ref[0, pl.ds(start, WIN), :]
        o_ref[tok, :] = o_ref[tok, :] + gcol * _dot(onehot, strip)
        return carry

    lax.fori_loop(0, sub, body, 0)

    @pl.when(e_idx == N_EXPERTS - 1)
    def _():
        o_ref[...] = x_ref[...] + gate_ref[...] * o_ref[...]


def _combine(s0_flat, pos_t, g_t, y, xres, gate_row, cap):
    n, d = xres.shape
    e, cap_pad, _ = y.shape
    chunk = min(2048, n)
    sub = chunk // TOK_BLK
    return pl.pallas_call(
        functools.partial(_combine_kernel, cap=cap, sub=sub),
        grid_spec=pltpu.PrefetchScalarGridSpec(
            num_scalar_prefetch=1,
            grid=(n // chunk, e),
            in_specs=[pl.BlockSpec((chunk, e), lambda c, ei, s0: (c, 0)),
                      pl.BlockSpec((chunk, e), lambda c, ei, s0: (c, 0)),
                      pl.BlockSpec((1, cap_pad, d), lambda c, ei, s0: (ei, 0, 0)),
                      pl.BlockSpec((chunk, d), lambda c, ei, s0: (c, 0)),
                      pl.BlockSpec((1, d), lambda c, ei, s0: (0, 0))],
            out_specs=pl.BlockSpec((chunk, d), lambda c, ei, s0: (c, 0)),
        ),
        out_shape=jax.ShapeDtypeStruct((n, d), F32),
        compiler_params=_cp("parallel", "arbitrary"),
        name="moe_combine",
    )(s0_flat, pos_t, g_t, y, xres, gate_row)


def _moe_block(x, mod, gate_row, w_router, wg, wu, wd):
    n = x.shape[0]
    cap = EC_CAPACITY * n // N_EXPERTS
    cap_pad = cap + TOK_BLK
    h, aff_t = _ffn_in(x, mod, w_router)
    pos, s0 = _select(aff_t, cap)
    s0_flat = s0[:, :, 0].reshape(-1)
    xe = _gather(s0_flat, pos, h, cap, cap_pad)
    y = _experts(xe, wg, wu, wd, cap)
    return _combine(s0_flat, pos.T, aff_t.T, y, x, gate_row, cap)


QK_SCALE = (DA_HEAD_DIM ** -0.5) * math.log2(math.e)


def _group_rms(u, gsum_ref, gain, eps):
    sq = u * u
    hi = sq.astype(BF16)
    lo = (sq - hi.astype(F32)).astype(BF16)
    ss = _dot(hi, gsum_ref[...]) + _dot(lo, gsum_ref[...])
    return u * lax.rsqrt(ss * (1.0 / DA_HEAD_DIM) + eps) * gain


def _rope(u, cos, sin_signed):
    d = u.shape[1]
    half = ROPE_AXIS_DIM // 2
    lane = lax.broadcasted_iota(I32, u.shape, 1)
    first = (lane & half) == 0
    swapped = jnp.where(first, pltpu.roll(u, d - half, 1), pltpu.roll(u, half, 1))
    return u * cos + swapped * sin_signed


def _qkv_kernel(*refs, rope):
    if rope:
        x_ref, mod_ref, w_ref, gsum_ref, qn_ref, kn_ref, cos_ref, sin_ref, q_ref, k_ref, v_ref = refs
    else:
        x_ref, mod_ref, w_ref, gsum_ref, qn_ref, kn_ref, q_ref, k_ref, v_ref = refs
    d = D_MODEL
    h = _norm_mod(x_ref[...], mod_ref[...]).astype(BF16)
    if rope:
        reps = d // cos_ref.shape[1]
        cos = _lane_tile(cos_ref[...], reps)
        sin = _lane_tile(sin_ref[...], reps)
    for part, (o_ref, gain_ref) in enumerate(((q_ref, qn_ref), (k_ref, kn_ref))):
        u = _dot(h, w_ref[:, part * d:(part + 1) * d])
        u = _group_rms(u, gsum_ref, gain_ref[...], NORM_EPS)
        if rope:
            u = _rope(u, cos, sin)
        if part == 0:
            u = u * QK_SCALE
        o_ref[...] = u.astype(BF16)
    v_ref[...] = _dot(h, w_ref[:, 2 * d:]).astype(BF16)


def _qkv(x, mod, w_bf, gsum, qn_row, kn_row, cos=None, sin=None):
    n, d = x.shape
    tm = min(512, n)
    rope = cos is not None
    row = pl.BlockSpec((tm, d), lambda i: (i, 0))
    in_specs = [row, _const_spec((8, d)), _const_spec((d, 3 * d)), _const_spec((d, d)),
                _const_spec((1, d)), _const_spec((1, d))]
    args = [x, mod, w_bf, gsum, qn_row, kn_row]
    if rope:
        tw = cos.shape[1]
        in_specs += [pl.BlockSpec((tm, tw), lambda i: (i, 0)), pl.BlockSpec((tm, tw), lambda i: (i, 0))]
        args += [cos, sin]
    out = jax.ShapeDtypeStruct((n, d), BF16)
    return pl.pallas_call(
        functools.partial(_qkv_kernel, rope=rope),
        grid=(n // tm,),
        in_specs=in_specs,
        out_specs=[row, row, row],
        out_shape=[out, out, out],
        compiler_params=_cp("parallel"),
        name="attn_qkv",
    )(*args)


NEG_BIG = -1e30


def _attn_kernel(q_ref, k_ref, v_ref, lam_ref, sub_ref, o_ref, m_scr, l_scr, acc_scr, *, lam_init):
    j = pl.program_id(1)
    nj = pl.num_programs(1)
    hd, vd = DA_HEAD_DIM, DA_V_DIM

    @pl.when(j == 0)
    def _():
        m_scr[...] = jnp.full_like(m_scr, NEG_BIG)
        l_scr[...] = jnp.zeros_like(l_scr)
        acc_scr[...] = jnp.zeros_like(acc_scr)

    lane = lax.broadcasted_iota(I32, (q_ref.shape[0], vd), 1)

    def head(h, carry):
        cols = pl.ds(pl.multiple_of(h * vd, vd), vd)
        qb = q_ref[:, cols]
        kb = k_ref[:, cols]
        vb = v_ref[:, cols]
        for comp in range(2):
            idx = 2 * h + comp
            in_comp = (lane >= comp * hd) & (lane < (comp + 1) * hd)
            qm = jnp.where(in_comp, qb, jnp.zeros_like(qb))
            s = _dot_nt(qm, kb)
            m_prev = m_scr[idx]
            m_new = jnp.maximum(m_prev, jnp.max(s, axis=1, keepdims=True))
            alpha = jnp.exp2(m_prev - m_new)
            p = jnp.exp2(s - m_new[:, 0:1])
            l_scr[idx] = alpha * l_scr[idx] + jnp.sum(p, axis=1, keepdims=True)
            acc_scr[idx] = alpha * acc_scr[idx] + _dot(p.astype(BF16), vb)
            m_scr[idx] = m_new
        return carry

    lax.fori_loop(0, DA_HEADS, head, 0)

    @pl.when(j == nj - 1)
    def _():
        lp = lam_ref[...]
        lam = (jnp.exp(jnp.sum(lp[0:1] * lp[1:2], axis=1, keepdims=True))
               - jnp.exp(jnp.sum(lp[2:3] * lp[3:4], axis=1, keepdims=True)) + lam_init)
        for h in range(DA_HEADS):
            o = acc_scr[2 * h] / l_scr[2 * h] - lam * (acc_scr[2 * h + 1] / l_scr[2 * h + 1])
            ms = jnp.mean(o * o, axis=1, keepdims=True)
            o = o * lax.rsqrt(ms + SUBLN_EPS) * (sub_ref[...] * (1.0 - lam_init))
            o_ref[:, h * vd:(h + 1) * vd] = o.astype(BF16)


def _attention(q, k_all, v_all, lam_rows, subln_row, lam_init):
    n, d = q.shape
    nk = k_all.shape[0]
    tq = min(512, n)
    tk = 1280 if nk % 1280 == 0 else 256
    assert nk % tk == 0
    nc = 2 * DA_HEADS
    return pl.pallas_call(
        functools.partial(_attn_kernel, lam_init=lam_init),
        grid=(n // tq, nk // tk),
        in_specs=[pl.BlockSpec((tq, d), lambda i, j: (i, 0)),
                  pl.BlockSpec((tk, d), lambda i, j: (j, 0)),
                  pl.BlockSpec((tk, d), lambda i, j: (j, 0)),
                  _const_spec((8, DA_HEAD_DIM)), _const_spec((1, DA_V_DIM))],
        out_specs=pl.BlockSpec((tq, d), lambda i, j: (i, 0)),
        out_shape=jax.ShapeDtypeStruct((n, d), BF16),
        scratch_shapes=[pltpu.VMEM((nc, tq, DA_V_DIM), F32), pltpu.VMEM((nc, tq, DA_V_DIM), F32),
                        pltpu.VMEM((nc, tq, DA_V_DIM), F32)],
        compiler_params=_cp("parallel", "arbitrary"),
        name="diff_attention",
    )(q, k_all, v_all, lam_rows, subln_row)


def _rope_tables(n):
    t = jnp.arange(n, dtype=I32)
    row = (t // GRID_W).astype(F32)
    col = (t % GRID_W).astype(F32)
    inv = ROPE_THETA ** (-jnp.arange(0, ROPE_AXIS_DIM, 2, dtype=F32) / ROPE_AXIS_DIM)
    ar = row[:, None] * inv[None]
    ac = col[:, None] * inv[None]
    cos = jnp.concatenate([jnp.cos(ar), jnp.cos(ar), jnp.cos(ac), jnp.cos(ac)], axis=1)
    sin = jnp.concatenate([-jnp.sin(ar), jnp.sin(ar), -jnp.sin(ac), jnp.sin(ac)], axis=1)
    return jnp.concatenate([cos, cos], axis=1), jnp.concatenate([sin, sin], axis=1)


def _hyena_layer(x, mod, gate_row, kspec, inv_norm, w_in_bf, b_in, conv_w, conv_b, skip, w_out_bf, b_out):
    seq = x.shape[0]
    v, x1, x2 = _hy_in(x, mod, w_in_bf, b_in, conv_w, conv_b)
    z = _long_conv_gate(v, x1, skip[0:1], kspec, inv_norm, 0, seq)
    z = _long_conv_gate(z, x2, skip[1:2], kspec, inv_norm, 1, seq)
    return _proj_res(z, w_out_bf, b_out.reshape(1, -1), gate_row, x)


def kernel(x, c, ctx, c_ctx, ada_w, ada_b, norm_mix, norm_ffn, hy_w_in, hy_b_in, hy_conv_w, hy_conv_b, hy_f_w1, hy_f_b1, hy_f_w2, hy_f_b2, hy_f_w3, hy_f_b3, hy_f_w4, hy_f_freq, hy_skip, hy_w_out, hy_b_out, da_w_qkv, da_q_norm, da_k_norm, da_lam_q1, da_lam_k1, da_lam_q2, da_lam_k2, da_subln, da_w_out, moe_router, moe_w_gate, moe_w_up, moe_w_down):
    d = D_MODEL
    depth = ada_w.shape[0]
    assert x.shape[0] == 1 and x.shape[2] == d
    xs = x[0]
    cs = ctx[0]
    cond8 = jnp.concatenate([c[0:1], c_ctx[None], jnp.zeros((6, d), F32)], axis=0)
    mods = _adaln(cond8, ada_w, ada_b)

    def mod_slice(i, row, k):
        return mods[i, row, k * d:(k + 1) * d][None]

    for i in range(depth):
        last = i == depth - 1
        j = i // 2
        mix_x = _mod_rows(norm_mix[i], mods[i], 0, 0)
        mix_c = _mod_rows(norm_mix[i], mods[i], 1, 0)
        if i % 2 == 0:
            fparams = (hy_f_w1[j], hy_f_b1[j], hy_f_w2[j], hy_f_b2[j], hy_f_w3[j], hy_f_b3[j], hy_f_w4[j],
                       hy_f_freq[j])
            shared = (hy_w_in[j].astype(BF16), hy_b_in[j], hy_conv_w[j], hy_conv_b[j], hy_skip[j],
                      hy_w_out[j].astype(BF16), hy_b_out[j])
            kspec, asum = _filter_spectrum(xs.shape[0], fparams)
            inv_norm = 1.0 / (asum[0:1] + HY_FILTER_EPS)
            new_x = _hyena_layer(xs, mix_x, mod_slice(i, 0, 2), kspec, inv_norm, *shared)
            if not last:
                kspec_c, asum_c = _filter_spectrum(cs.shape[0], fparams)
                inv_c = 1.0 / (asum_c[0:1] + HY_FILTER_EPS)
                cs = _hyena_layer(cs, mix_c, mod_slice(i, 1, 2), kspec_c, inv_c, *shared)
            xs = new_x
        else:
            lam_init = 0.8 - 0.6 * math.exp(-0.3 * i)
            w_qkv = da_w_qkv[j].astype(BF16)
            gidx = np.arange(d) // DA_HEAD_DIM
            gsum = jnp.asarray((gidx[:, None] == gidx[None]).astype(np.float32)).astype(BF16)
            qn = jnp.tile(da_q_norm[j], 2 * DA_HEADS)[None]
            kn = jnp.tile(da_k_norm[j], 2 * DA_HEADS)[None]
            cos, sin = _rope_tables(xs.shape[0])
            qx, kx, vx = _qkv(xs, mix_x, w_qkv, gsum, qn, kn, cos, sin)
            qc, kc, vc = _qkv(cs, mix_c, w_qkv, gsum, qn, kn)
            k_all = jnp.concatenate([kc, kx], axis=0)
            v_all = jnp.concatenate([vc, vx], axis=0)
            lam_rows = jnp.concatenate([da_lam_q1[j][None], da_lam_k1[j][None], da_lam_q2[j][None],
                                        da_lam_k2[j][None], jnp.zeros((4, DA_HEAD_DIM), F32)], axis=0)
            w_out = da_w_out[j].astype(BF16)
            zero_b = jnp.zeros((1, d), F32)
            ox = _attention(qx, k_all, v_all, lam_rows, da_subln[j][None], lam_init)
            new_x = _proj_res(ox, w_out, zero_b, mod_slice(i, 0, 2), xs)
            if not last:
                oc = _attention(qc, kc, vc, lam_rows, da_subln[j][None], lam_init)
                cs = _proj_res(oc, w_out, zero_b, mod_slice(i, 1, 2), cs)
            xs = new_x
        wg = moe_w_gate[i].astype(BF16)
        wu = moe_w_up[i].astype(BF16)
        wd = moe_w_down[i].astype(BF16)
        if not last:
            cs = _moe_block(cs, _mod_rows(norm_ffn[i], mods[i], 1, 3), mod_slice(i, 1, 5), moe_router[i], wg, wu, wd)
        xs = _moe_block(xs, _mod_rows(norm_ffn[i], mods[i], 0, 3), mod_slice(i, 0, 5), moe_router[i], wg, wu, wd)
    return xs[None]
```

```python
import functools
import math

import jax
import jax.numpy as jnp
import numpy as np
from jax import lax
from jax.experimental import pallas as pl
from jax.experimental.pallas import tpu as pltpu

F32 = jnp.float32
BF16 = jnp.bfloat16
I32 = jnp.int32
HIGHEST = lax.Precision.HIGHEST

D_MODEL = 1024
N_MOD = 6
NORM_EPS = 1e-6
GRID_W = 64
HY_ORDER = 2
HY_SHORT = 3
HY_EMB_BANDS = 16
HY_EMB_DIM = 1 + 2 * HY_EMB_BANDS
HY_FILTER_HIDDEN = 64
HY_DECAY_FAST = 0.3
HY_DECAY_SLOW = 1.5
HY_DECAY_TARGET = 1e-2
HY_FILTER_EPS = 1e-6
Z_WIDTH = 40
Z_SIGN_COL = 33
DA_HEADS = 8
DA_HEAD_DIM = 64
DA_V_DIM = 128
ROPE_AXIS_DIM = 32
ROPE_THETA = 10000.0
SUBLN_EPS = 1e-5
N_EXPERTS = 16
EC_CAPACITY = 2
D_EXPERT = 1024
TOK_BLK = 256
ROW_ALIGN = 16
WIN = TOK_BLK + ROW_ALIGN

VMEM_LIMIT = 56 * 1024 * 1024


def _cp(*sem):
    return pltpu.CompilerParams(dimension_semantics=sem, vmem_limit_bytes=VMEM_LIMIT)


def _const_spec(shape):
    nd = len(shape)
    return pl.BlockSpec(shape, lambda *_: (0,) * nd)


def _dot(a, b):
    return jnp.dot(a, b, preferred_element_type=F32)


def _dot_nt(a, b):
    return lax.dot_general(a, b, (((1,), (1,)), ((), ())), preferred_element_type=F32)


def _norm_mod(x, mod, eps=NORM_EPS):
    ms = jnp.mean(x * x, axis=-1, keepdims=True)
    return x * lax.rsqrt(ms + eps) * (mod[0:1] * (1.0 + mod[1:2])) + mod[2:3]


def _lane_tile(x, reps):
    return jnp.concatenate([x] * reps, axis=1) if reps > 1 else x


def _adaln_kernel(c_ref, w_ref, b_ref, o_ref):
    c = c_ref[...]
    s = c / (1.0 + jnp.exp(-c))
    o_ref[0] = jnp.dot(s, w_ref[0], precision=HIGHEST, preferred_element_type=F32) + b_ref[0]


def _adaln(cond8, ada_w, ada_b):
    depth, d, nout = ada_w.shape
    tn = 1536
    return pl.pallas_call(
        _adaln_kernel,
        grid=(depth, nout // tn),
        in_specs=[_const_spec((8, d)),
                  pl.BlockSpec((1, d, tn), lambda l, j: (l, 0, j)),
                  pl.BlockSpec((1, 1, tn), lambda l, j: (l, 0, j))],
        out_specs=pl.BlockSpec((1, 8, tn), lambda l, j: (l, 0, j)),
        out_shape=jax.ShapeDtypeStruct((depth, 8, nout), F32),
        compiler_params=_cp("parallel", "parallel"),
        name="adaln",
    )(cond8, ada_w, ada_b.reshape(depth, 1, nout))


def _mod_rows(norm_g, mods, row, k0):
    d = D_MODEL
    shift = mods[row, k0 * d:(k0 + 1) * d]
    scale = mods[row, (k0 + 1) * d:(k0 + 2) * d]
    z = jnp.zeros((5, d), F32)
    return jnp.concatenate([norm_g[None], scale[None], shift[None], z], axis=0)


HALO = 16


def _hy_in_kernel(x_ref, xp_ref, xn_ref, mod_ref, w_ref, b_ref, cw_ref, v_ref, x1_ref, x2_ref, *, tm, n_rows):
    i = pl.program_id(0)
    mod = mod_ref[...]
    hm = _norm_mod(x_ref[...], mod).astype(BF16)
    hp = _norm_mod(xp_ref[...], mod).astype(BF16)
    hn = _norm_mod(xn_ref[...], mod).astype(BF16)
    hcat = jnp.concatenate([hp, hm, hn], axis=0)
    row = lax.broadcasted_iota(I32, (tm + 2 * HALO, 1), 0) + (i * tm - HALO)
    valid = jnp.logical_and(row >= 0, row < n_rows)
    d = D_MODEL
    for c, o_ref in enumerate((v_ref, x1_ref, x2_ref)):
        u = _dot(hcat, w_ref[:, c * d:(c + 1) * d]) + b_ref[:, c * d:(c + 1) * d]
        u = jnp.where(valid, u, 0.0)
        cw = cw_ref[:, c * d:(c + 1) * d]
        y = (cw[3:4] + cw[0:1] * u[HALO - 1:HALO - 1 + tm] + cw[1:2] * u[HALO:HALO + tm]
             + cw[2:3] * u[HALO + 1:HALO + 1 + tm])
        o_ref[...] = y.astype(BF16)


def _hy_in(x, mod, w_bf, b_in, conv_w, conv_b):
    n, d = x.shape
    tm = min(512, n)
    nh = n // HALO
    cw = jnp.concatenate([conv_w, conv_b[None], jnp.zeros((4, 3 * d), F32)], axis=0)
    out = jax.ShapeDtypeStruct((n, d), BF16)
    row_spec = pl.BlockSpec((tm, d), lambda i: (i, 0))
    return pl.pallas_call(
        functools.partial(_hy_in_kernel, tm=tm, n_rows=n),
        grid=(n // tm,),
        in_specs=[row_spec,
                  pl.BlockSpec((HALO, d), lambda i: (jnp.maximum(i * (tm // HALO) - 1, 0), 0)),
                  pl.BlockSpec((HALO, d), lambda i: (jnp.minimum((i + 1) * (tm // HALO), nh - 1), 0)),
                  _const_spec((8, d)), _const_spec((d, 3 * d)), _const_spec((1, 3 * d)),
                  _const_spec((8, 3 * d))],
        out_specs=[row_spec, row_spec, row_spec],
        out_shape=[out, out, out],
        compiler_params=_cp("parallel"),
        name="hyena_in",
    )(x, x, x, mod, w_bf, b_in.reshape(1, 3 * d), cw)


def _filter_kernel(z_ref, w1_ref, b1_ref, w2_ref, b2_ref, w3_ref, b3_ref, fr_ref, w4_ref, dl_ref,
                   k_ref, asum_ref):
    i = pl.program_id(0)
    z = z_ref[...]
    fr = fr_ref[...]

    def lin(a, w_ref, b_ref):
        return jnp.dot(a, w_ref[...], precision=HIGHEST, preferred_element_type=F32) + b_ref[...]

    hid = jnp.sin(fr * lin(z, w1_ref, b1_ref))
    hid = jnp.sin(fr * lin(hid, w2_ref, b2_ref))
    hid = jnp.sin(fr * lin(hid, w3_ref, b3_ref))
    h = _dot(hid.astype(BF16), w4_ref[0])
    t = z[:, 0:1]
    sgn = z[:, Z_SIGN_COL:Z_SIGN_COL + 1]
    k = h * jnp.exp(-t * dl_ref[...]) * sgn

    @pl.when(i == 0)
    def _():
        asum_ref[...] = jnp.zeros_like(asum_ref)

    asum_ref[0:1, :] += jnp.sum(jnp.abs(k), axis=0, keepdims=True)
    k_ref[...] = k.astype(BF16)


def _filter_positions(seq):
    r = np.arange(2 * seq)
    pos = np.where(r < seq, r, 2 * seq - r).astype(np.float64)
    pos = np.minimum(pos, seq - 1)
    t = pos / (seq - 1)
    w = 2.0 * np.pi * pos / seq
    bands = np.linspace(1e-4, HY_EMB_BANDS - 1, HY_EMB_BANDS)[None]
    z = np.zeros((2 * seq, Z_WIDTH), np.float64)
    z[:, 0] = t
    z[:, 1:1 + HY_EMB_BANDS] = np.cos(w[:, None] * bands)
    z[:, 1 + HY_EMB_BANDS:HY_EMB_DIM] = -np.sin(w[:, None] * bands)
    z[:, Z_SIGN_COL] = np.where(r < seq, 1.0, np.where(r == seq, 0.0, -1.0))
    return jnp.asarray(z.astype(np.float32))


def _hyena_filter_time(seq, f_w1, f_b1, f_w2, f_b2, f_w3, f_b3, f_w4, f_freq):
    d, hid = D_MODEL, HY_FILTER_HIDDEN
    od = HY_ORDER * d
    tr = min(512, seq)
    z = _filter_positions(seq)
    w1p = jnp.concatenate([f_w1, jnp.zeros((Z_WIDTH - HY_EMB_DIM, hid), F32)], axis=0)
    w4d = f_w4.reshape(hid, HY_ORDER, 2, d).transpose(2, 0, 1, 3).reshape(2, hid, od).astype(BF16)
    max_decay = math.log(HY_DECAY_TARGET) / HY_DECAY_FAST
    min_decay = math.log(HY_DECAY_TARGET) / HY_DECAY_SLOW
    deltas = np.abs(np.linspace(min_decay, max_decay, d, dtype=np.float32))
    dl = jnp.asarray(np.tile(deltas, HY_ORDER)[None])
    tiles_per_dir = seq // tr
    k_t, asum = pl.pallas_call(
        _filter_kernel,
        grid=(2 * seq // tr,),
        in_specs=[pl.BlockSpec((tr, Z_WIDTH), lambda i: (i, 0)),
                  _const_spec((Z_WIDTH, hid)), _const_spec((1, hid)),
                  _const_spec((hid, hid)), _const_spec((1, hid)),
                  _const_spec((hid, hid)), _const_spec((1, hid)),
                  _const_spec((1, hid)),
                  pl.BlockSpec((1, hid, od), lambda i: (i // tiles_per_dir, 0, 0)),
                  _const_spec((1, od))],
        out_specs=[pl.BlockSpec((tr, od), lambda i: (i, 0)), _const_spec((8, od))],
        out_shape=[jax.ShapeDtypeStruct((2 * seq, od), BF16), jax.ShapeDtypeStruct((8, od), F32)],
        compiler_params=_cp("arbitrary"),
        name="hyena_filter",
    )(z, w1p, f_b1.reshape(1, hid), f_w2, f_b2.reshape(1, hid), f_w3, f_b3.reshape(1, hid),
      f_freq.reshape(1, hid), w4d, dl)
    return k_t, asum


def _fft_factors(seq):
    n = 2 * seq
    n1 = 256 if n >= 32768 else 32
    n2 = n // n1
    assert n1 * n2 == n and n2 % 16 == 0 and n1 % 32 == 0
    return n1, n2


@functools.lru_cache(maxsize=None)
def _fft_consts(seq):
    n = 2 * seq
    n1, n2 = _fft_factors(seq)
    k1 = np.arange(n1, dtype=np.float64)[:, None] + 0.5
    th1 = 2.0 * np.pi * k1 * np.arange(n1, dtype=np.float64)[None] / n1
    f1_full = np.concatenate([np.cos(th1), -np.sin(th1)], axis=0)
    f1_half = f1_full[:, :n1 // 2]
    tw = 2.0 * np.pi * k1 * np.arange(n2, dtype=np.float64)[None] / n
    tre, tim = np.cos(tw), -np.sin(tw)
    h2 = n2 // 2
    th2 = 2.0 * np.pi * np.arange(h2, dtype=np.float64)[:, None] * np.arange(n2, dtype=np.float64)[None] / n2
    c2, s2 = np.cos(th2), np.sin(th2)
    m2f = np.block([[c2, s2], [-s2, c2]])
    m2i = np.block([[c2.T, -s2.T], [s2.T, c2.T]])
    thb = th1[:, :n1 // 2].T
    gi = (2.0 / n) * np.concatenate([np.cos(thb), -np.sin(thb)], axis=1)
    bf = lambda a: jnp.asarray(a.astype(np.float32)).astype(BF16)
    f32 = lambda a: jnp.asarray(a.astype(np.float32))
    return dict(f1_full=bf(f1_full), f1_half=bf(f1_half), tre=f32(tre), tim=f32(tim),
                treT=f32(tre.T), timT=f32(tim.T), m2f=bf(m2f), m2i=bf(m2i), gi=bf(gi))


def _pick_col(tbl, idx):
    lane = lax.broadcasted_iota(I32, tbl.shape, 1)
    return jnp.sum(jnp.where(lane == idx, tbl, 0.0), axis=1, keepdims=True)


def _fft_s1_kernel(f1_ref, x_ref, tre_ref, tim_ref, o_ref, *, n1):
    j = pl.program_id(0)
    a = _dot(f1_ref[...], x_ref[...])
    are, aim = a[:n1], a[n1:]
    tre = _pick_col(tre_ref[...], j)
    tim = _pick_col(tim_ref[...], j)
    o_ref[0] = (are * tre - aim * tim).astype(BF16)
    o_ref[1] = (are * tim + aim * tre).astype(BF16)


def _fft_stage1(x2d, f1, tre, tim, n1, n2, chans):
    r = x2d.shape[0]
    tc = D_MODEL
    g = chans // tc
    return pl.pallas_call(
        functools.partial(_fft_s1_kernel, n1=n1),
        grid=(n2, g),
        in_specs=[_const_spec((2 * n1, r)),
                  pl.BlockSpec((r, tc), lambda j, c: (0, j * g + c)),
                  _const_spec((n1, n2)), _const_spec((n1, n2))],
        out_specs=pl.BlockSpec((2, n1, tc), lambda j, c: (0, 0, j * g + c)),
        out_shape=jax.ShapeDtypeStruct((2, n1, n2 * chans), BF16),
        compiler_params=_cp("parallel", "parallel"),
        name="fft_stage1",
    )(f1, x2d, tre, tim)


def _fft_s2_kernel(m2f_ref, b_ref, o_ref):
    b = jnp.concatenate([b_ref[0, 0], b_ref[1, 0]], axis=0)
    o_ref[0] = _dot(m2f_ref[...], b)


def _fft_stage2(b4, m2f, n1, n2, chans):
    tc = D_MODEL
    return pl.pallas_call(
        _fft_s2_kernel,
        grid=(n1, chans // tc),
        in_specs=[_const_spec((n2, 2 * n2)),
                  pl.BlockSpec((2, 1, n2, tc), lambda k, c: (0, k, 0, c))],
        out_specs=pl.BlockSpec((1, n2, tc), lambda k, c: (k, 0, c)),
        out_shape=jax.ShapeDtypeStruct((n1, n2, chans), F32),
        compiler_params=_cp("parallel", "parallel"),
        name="fft_stage2",
    )(m2f, b4)


def _fft_mid_kernel(m2f_ref, m2i_ref, b_ref, k_ref, inv_ref, treT_ref, timT_ref, o_ref, *, n2):
    k1 = pl.program_id(0)
    h2 = n2 // 2
    b = jnp.concatenate([b_ref[0, 0], b_ref[1, 0]], axis=0)
    x = _dot(m2f_ref[...], b)
    kk = k_ref[0] * inv_ref[...]
    xre, xim = x[:h2], x[h2:]
    kre, kim = kk[:h2], kk[h2:]
    y = jnp.concatenate([xre * kre - xim * kim, xre * kim + xim * kre], axis=0).astype(BF16)
    c = _dot(m2i_ref[...], y)
    cre, cim = c[:n2], c[n2:]
    tre = _pick_col(treT_ref[...], k1)
    tim = _pick_col(timT_ref[...], k1)
    o_ref[0, 0] = (cre * tre + cim * tim).astype(BF16)
    o_ref[1, 0] = (cim * tre - cre * tim).astype(BF16)


def _fft_mid(b4, kspec, inv_norm, order, cst, n1, n2):
    d = D_MODEL
    return pl.pallas_call(
        functools.partial(_fft_mid_kernel, n2=n2),
        grid=(n1,),
        in_specs=[_const_spec((n2, 2 * n2)), _const_spec((2 * n2, n2)),
                  pl.BlockSpec((2, 1, n2, d), lambda k: (0, k, 0, 0)),
                  pl.BlockSpec((1, n2, d), lambda k: (k, 0, order)),
                  pl.BlockSpec((1, d), lambda k: (0, order)),
                  _const_spec((n2, n1)), _const_spec((n2, n1))],
        out_specs=pl.BlockSpec((2, 1, n2, d), lambda k: (0, k, 0, 0)),
        out_shape=jax.ShapeDtypeStruct((2, n1, n2, d), BF16),
        compiler_params=_cp("parallel"),
        name="fft_mid",
    )(cst["m2f"], cst["m2i"], b4, kspec, inv_norm, cst["treT"], cst["timT"])


def _fft_last_kernel(gi_ref, c_ref, gate_ref, z_ref, skip_ref, o_ref):
    c = jnp.concatenate([c_ref[0], c_ref[1]], axis=0)
    y = _dot(gi_ref[...], c)
    z = z_ref[...].astype(F32)
    o_ref[...] = (gate_ref[...].astype(F32) * (y + skip_ref[...] * z)).astype(BF16)


def _fft_last(c3, gate2d, z2d, skip_row, gi, n1, n2):
    d = D_MODEL
    r = n1 // 2
    col = pl.BlockSpec((r, d), lambda j: (0, j))
    return pl.pallas_call(
        _fft_last_kernel,
        grid=(n2,),
        in_specs=[_const_spec((r, 2 * n1)),
                  pl.BlockSpec((2, n1, d), lambda j: (0, 0, j)),
                  col, col, _const_spec((1, d))],
        out_specs=col,
        out_shape=jax.ShapeDtypeStruct((r, n2 * d), BF16),
        compiler_params=_cp("parallel"),
        name="fft_last",
    )(gi, c3, gate2d, z2d, skip_row)


def _long_conv_gate(z_in, gate, skip_row, kspec, inv_norm, order, seq):
    d = D_MODEL
    n1, n2 = _fft_factors(seq)
    cst = _fft_consts(seq)
    z2d = z_in.reshape(n1 // 2, n2 * d)
    b = _fft_stage1(z2d, cst["f1_half"], cst["tre"], cst["tim"], n1, n2, d)
    c = _fft_mid(b.reshape(2, n1, n2, d), kspec, inv_norm, order, cst, n1, n2)
    out = _fft_last(c.reshape(2, n1, n2 * d), gate.reshape(n1 // 2, n2 * d), z2d, skip_row, cst["gi"], n1, n2)
    return out.reshape(seq, d)


def _filter_spectrum(seq, fparams):
    d = D_MODEL
    od = HY_ORDER * d
    n1, n2 = _fft_factors(seq)
    cst = _fft_consts(seq)
    k_t, asum = _hyena_filter_time(seq, *fparams)
    b = _fft_stage1(k_t.reshape(n1, n2 * od), cst["f1_full"], cst["tre"], cst["tim"], n1, n2, od)
    kspec = _fft_stage2(b.reshape(2, n1, n2, od), cst["m2f"], n1, n2, od)
    return kspec, asum


def _proj_res_kernel(a_ref, w_ref, b_ref, g_ref, x_ref, o_ref):
    y = _dot(a_ref[...], w_ref[...]) + b_ref[...]
    o_ref[...] = x_ref[...] + g_ref[...] * y


def _proj_res(a, w_bf, b_row, gate_row, xres):
    n, d = xres.shape
    tm = min(512, n)
    row = pl.BlockSpec((tm, d), lambda i: (i, 0))
    return pl.pallas_call(
        _proj_res_kernel,
        grid=(n // tm,),
        in_specs=[row, _const_spec((d, d)), _const_spec((1, d)), _const_spec((1, d)), row],
        out_specs=row,
        out_shape=jax.ShapeDtypeStruct((n, d), F32),
        compiler_params=_cp("parallel"),
        name="proj_residual",
    )(a, w_bf, b_row, gate_row, xres)


def _ffn_in_kernel(x_ref, mod_ref, wt_ref, h_ref, aff_ref):
    h = _norm_mod(x_ref[...], mod_ref[...])
    hi = h.astype(BF16)
    lo = (h - hi.astype(F32)).astype(BF16)
    wt = wt_ref[...]
    whi = wt.astype(BF16)
    wlo = (wt - whi.astype(F32)).astype(BF16)
    logits = _dot_nt(whi, hi) + (_dot_nt(whi, lo) + _dot_nt(wlo, hi))
    m = jnp.max(logits, axis=0, keepdims=True)
    p = jnp.exp(logits - m)
    aff_ref[...] = p / jnp.sum(p, axis=0, keepdims=True)
    h_ref[...] = hi


def _ffn_in(x, mod, w_router):
    n, d = x.shape
    e = N_EXPERTS
    tm = min(512, n)
    return pl.pallas_call(
        _ffn_in_kernel,
        grid=(n // tm,),
        in_specs=[pl.BlockSpec((tm, d), lambda i: (i, 0)), _const_spec((8, d)), _const_spec((e, d))],
        out_specs=[pl.BlockSpec((tm, d), lambda i: (i, 0)), pl.BlockSpec((e, tm), lambda i: (0, i))],
        out_shape=[jax.ShapeDtypeStruct((n, d), BF16), jax.ShapeDtypeStruct((e, n), F32)],
        compiler_params=_cp("parallel"),
        name="moe_router",
    )(x, mod, w_router.T)


def _select_kernel(a_ref, pos_ref, s0_ref, sel_ref, *, cap, nblk):
    e = N_EXPERTS
    bits = pltpu.bitcast(a_ref[...], I32)

    def bisect(i, thr):
        cand = thr | jnp.left_shift(jnp.int32(1), 30 - i)
        cnt = jnp.sum(jnp.where(bits >= cand, 1.0, 0.0), axis=1, keepdims=True)
        return jnp.where(cnt >= cap, cand, thr)

    thr = lax.fori_loop(0, 31, bisect, jnp.zeros((e, 1), I32))
    n_gt = jnp.sum(jnp.where(bits > thr, 1.0, 0.0), axis=1, keepdims=True)
    need = cap - n_gt
    r = lax.broadcasted_iota(I32, (TOK_BLK, TOK_BLK), 0)
    c = lax.broadcasted_iota(I32, (TOK_BLK, TOK_BLK), 1)
    upper = jnp.where(r < c, 1.0, 0.0).astype(BF16)

    def pass1(j, carry):
        sl = pl.ds(pl.multiple_of(j * TOK_BLK, TOK_BLK), TOK_BLK)
        bj = pltpu.bitcast(a_ref[:, sl], I32)
        eq = jnp.where(bj == thr, 1.0, 0.0)
        rank = _dot(eq.astype(BF16), upper) + carry
        keep = jnp.logical_or(bj > thr, jnp.logical_and(bj == thr, rank < need))
        sel_ref[:, sl] = jnp.where(keep, 1.0, 0.0)
        return carry + jnp.sum(eq, axis=1, keepdims=True)

    lax.fori_loop(0, nblk, pass1, jnp.zeros((e, 1), F32))

    def pass2(j, carry):
        sl = pl.ds(pl.multiple_of(j * TOK_BLK, TOK_BLK), TOK_BLK)
        s = sel_ref[:, sl]
        slot = _dot(s.astype(BF16), upper) + carry
        pos_ref[:, sl] = jnp.where(s > 0.5, slot, -1.0).astype(I32)
        s0_ref[j] = jnp.broadcast_to(carry, (e, 128)).astype(I32)
        return carry + jnp.sum(s, axis=1, keepdims=True)

    lax.fori_loop(0, nblk, pass2, jnp.zeros((e, 1), F32))


def _select(aff_t, cap):
    e, n = aff_t.shape
    nblk = n // TOK_BLK
    return pl.pallas_call(
        functools.partial(_select_kernel, cap=cap, nblk=nblk),
        out_shape=[jax.ShapeDtypeStruct((e, n), I32), jax.ShapeDtypeStruct((nblk, e, 128), I32)],
        scratch_shapes=[pltpu.VMEM((e, n), F32)],
        compiler_params=pltpu.CompilerParams(vmem_limit_bytes=VMEM_LIMIT),
        name="moe_select",
    )(aff_t)


def _window_start(s0_ref, blk, e_idx, cap):
    s0 = s0_ref[blk * N_EXPERTS + e_idx]
    s0a = lax.shift_left(lax.shift_right_logical(s0, 4), 4)
    return pl.multiple_of(jnp.minimum(s0a, cap - ROW_ALIGN), ROW_ALIGN)


def _gather_kernel(s0_ref, pos_ref, h_ref, xe_ref, *, cap, sub):
    e_idx = pl.program_id(0)
    c = pl.program_id(1)

    @pl.when(c == 0)
    def _():
        xe_ref[...] = jnp.zeros_like(xe_ref)

    rows = lax.broadcasted_iota(I32, (WIN, TOK_BLK), 0)

    def body(i, carry):
        start = _window_start(s0_ref, c * sub + i, e_idx, cap)
        tok = pl.ds(pl.multiple_of(i * TOK_BLK, TOK_BLK), TOK_BLK)
        rel = pos_ref[0, :, tok] - start
        onehot = jnp.where(rows == rel, 1.0, 0.0).astype(BF16)
        got = _dot(onehot, h_ref[tok, :]).astype(BF16)
        win = pl.ds(start, WIN)
        xe_ref[0, win, :] = xe_ref[0, win, :] + got
        return carry

    lax.fori_loop(0, sub, body, 0)


def _gather(s0_flat, pos, h, cap, cap_pad):
    e, n = pos.shape
    d = h.shape[1]
    chunk = min(2048, n)
    sub = chunk // TOK_BLK
    return pl.pallas_call(
        functools.partial(_gather_kernel, cap=cap, sub=sub),
        grid_spec=pltpu.PrefetchScalarGridSpec(
            num_scalar_prefetch=1,
            grid=(e, n // chunk),
            in_specs=[pl.BlockSpec((1, 1, chunk), lambda ei, c, s0: (ei, 0, c)),
                      pl.BlockSpec((chunk, d), lambda ei, c, s0: (c, 0))],
            out_specs=pl.BlockSpec((1, cap_pad, d), lambda ei, c, s0: (ei, 0, 0)),
        ),
        out_shape=jax.ShapeDtypeStruct((e, cap_pad, d), BF16),
        compiler_params=_cp("parallel", "arbitrary"),
        name="moe_gather",
    )(s0_flat, pos.reshape(e, 1, n), h)


def _expert_kernel(x_ref, wg_ref, wu_ref, wd_ref, y_ref, *, n_real):
    j = pl.program_id(1)

    @pl.when(j < n_real)
    def _():
        x = x_ref[0]
        g = _dot(x, wg_ref[0])
        u = _dot(x, wu_ref[0])
        a = (g / (1.0 + jnp.exp(-g))) * u
        y_ref[0] = _dot(a.astype(BF16), wd_ref[0]).astype(BF16)

    @pl.when(j >= n_real)
    def _():
        y_ref[0] = jnp.zeros_like(y_ref[0])


def _experts(xe, wg, wu, wd, cap):
    e, cap_pad, d = xe.shape
    f = wg.shape[2]
    tm = min(256, cap)
    tile = pl.BlockSpec((1, tm, d), lambda ei, j: (ei, j, 0))
    return pl.pallas_call(
        functools.partial(_expert_kernel, n_real=cap // tm),
        grid=(e, cap_pad // tm),
        in_specs=[tile,
                  pl.BlockSpec((1, d, f), lambda ei, j: (ei, 0, 0)),
                  pl.BlockSpec((1, d, f), lambda ei, j: (ei, 0, 0)),
                  pl.BlockSpec((1, f, d), lambda ei, j: (ei, 0, 0))],
        out_specs=tile,
        out_shape=jax.ShapeDtypeStruct((e, cap_pad, d), BF16),
        compiler_params=_cp("parallel", "arbitrary"),
        name="moe_experts",
    )(xe, wg, wu, wd)


def _combine_kernel(s0_ref, post_ref, gt_ref, y_ref, x_ref, gate_ref, o_ref, *, cap, sub):
    c = pl.program_id(0)
    e_idx = pl.program_id(1)

    @pl.when(e_idx == 0)
    def _():
        o_ref[...] = jnp.zeros_like(o_ref)

    lanes = lax.broadcasted_iota(I32, (TOK_BLK, WIN), 1)
    elane = lax.broadcasted_iota(I32, (TOK_BLK, N_EXPERTS), 1) == e_idx

    def body(i, carry):
        start = _window_start(s0_ref, c * sub + i, e_idx, cap)
        tok = pl.ds(pl.multiple_of(i * TOK_BLK, TOK_BLK), TOK_BLK)
        pcol = jnp.sum(jnp.where(elane, post_ref[tok, :].astype(F32), 0.0), axis=1, keepdims=True)
        gcol = jnp.sum(jnp.where(elane, gt_ref[tok, :], 0.0), axis=1, keepdims=True)
        rel = pcol.astype(I32) - start
        onehot = jnp.where(lanes == rel, 1.0, 0.0).astype(BF16)
        strip = y_ref[0, pl.ds(start, WIN), :]
        o_ref[tok, :] = o_ref[tok, :] + gcol * _dot(onehot, strip)
        return carry

    lax.fori_loop(0, sub, body, 0)

    @pl.when(e_idx == N_EXPERTS - 1)
    def _():
        o_ref[...] = x_ref[...] + gate_ref[...] * o_ref[...]


def _combine(s0_flat, pos_t, g_t, y, xres, gate_row, cap):
    n, d = xres.shape
    e, cap_pad, _ = y.shape
    chunk = min(2048, n)
    sub = chunk // TOK_BLK
    return pl.pallas_call(
        functools.partial(_combine_kernel, cap=cap, sub=sub),
        grid_spec=pltpu.PrefetchScalarGridSpec(
            num_scalar_prefetch=1,
            grid=(n // chunk, e),
            in_specs=[pl.BlockSpec((chunk, e), lambda c, ei, s0: (c, 0)),
                      pl.BlockSpec((chunk, e), lambda c, ei, s0: (c, 0)),
                      pl.BlockSpec((1, cap_pad, d), lambda c, ei, s0: (ei, 0, 0)),
                      pl.BlockSpec((chunk, d), lambda c, ei, s0: (c, 0)),
                      pl.BlockSpec((1, d), lambda c, ei, s0: (0, 0))],
            out_specs=pl.BlockSpec((chunk, d), lambda c, ei, s0: (c, 0)),
        ),
        out_shape=jax.ShapeDtypeStruct((n, d), F32),
        compiler_params=_cp("parallel", "arbitrary"),
        name="moe_combine",
    )(s0_flat, pos_t, g_t, y, xres, gate_row)


def _moe_block(x, mod, gate_row, w_router, wg, wu, wd):
    n = x.shape[0]
    cap = EC_CAPACITY * n // N_EXPERTS
    cap_pad = cap + TOK_BLK
    h, aff_t = _ffn_in(x, mod, w_router)
    pos, s0 = _select(aff_t, cap)
    s0_flat = s0[:, :, 0].reshape(-1)
    xe = _gather(s0_flat, pos, h, cap, cap_pad)
    y = _experts(xe, wg, wu, wd, cap)
    return _combine(s0_flat, pos.T, aff_t.T, y, x, gate_row, cap)


QK_SCALE = (DA_HEAD_DIM ** -0.5) * math.log2(math.e)


def _group_rms(u, gsum_ref, gain, eps):
    sq = u * u
    hi = sq.astype(BF16)
    lo = (sq - hi.astype(F32)).astype(BF16)
    ss = _dot(hi, gsum_ref[...]) + _dot(lo, gsum_ref[...])
    return u * lax.rsqrt(ss * (1.0 / DA_HEAD_DIM) + eps) * gain


def _rope(u, cos, sin_signed):
    d = u.shape[1]
    half = ROPE_AXIS_DIM // 2
    lane = lax.broadcasted_iota(I32, u.shape, 1)
    first = (lane & half) == 0
    swapped = jnp.where(first, pltpu.roll(u, d - half, 1), pltpu.roll(u, half, 1))
    return u * cos + swapped * sin_signed


def _qkv_kernel(*refs, rope):
    if rope:
        x_ref, mod_ref, w_ref, gsum_ref, qn_ref, kn_ref, cos_ref, sin_ref, q_ref, k_ref, v_ref = refs
    else:
        x_ref, mod_ref, w_ref, gsum_ref, qn_ref, kn_ref, q_ref, k_ref, v_ref = refs
    d = D_MODEL
    h = _norm_mod(x_ref[...], mod_ref[...]).astype(BF16)
    if rope:
        reps = d // cos_ref.shape[1]
        cos = _lane_tile(cos_ref[...], reps)
        sin = _lane_tile(sin_ref[...], reps)
    for part, (o_ref, gain_ref) in enumerate(((q_ref, qn_ref), (k_ref, kn_ref))):
        u = _dot(h, w_ref[:, part * d:(part + 1) * d])
        u = _group_rms(u, gsum_ref, gain_ref[...], NORM_EPS)
        if rope:
            u = _rope(u, cos, sin)
        if part == 0:
            u = u * QK_SCALE
        o_ref[...] = u.astype(BF16)
    v_ref[...] = _dot(h, w_ref[:, 2 * d:]).astype(BF16)


def _qkv(x, mod, w_bf, gsum, qn_row, kn_row, cos=None, sin=None):
    n, d = x.shape
    tm = min(512, n)
    rope = cos is not None
    row = pl.BlockSpec((tm, d), lambda i: (i, 0))
    in_specs = [row, _const_spec((8, d)), _const_spec((d, 3 * d)), _const_spec((d, d)),
                _const_spec((1, d)), _const_spec((1, d))]
    args = [x, mod, w_bf, gsum, qn_row, kn_row]
    if rope:
        tw = cos.shape[1]
        in_specs += [pl.BlockSpec((tm, tw), lambda i: (i, 0)), pl.BlockSpec((tm, tw), lambda i: (i, 0))]
        args += [cos, sin]
    out = jax.ShapeDtypeStruct((n, d), BF16)
    return pl.pallas_call(
        functools.partial(_qkv_kernel, rope=rope),
        grid=(n // tm,),
        in_specs=in_specs,
        out_specs=[row, row, row],
        out_shape=[out, out, out],
        compiler_params=_cp("parallel"),
        name="attn_qkv",
    )(*args)


NEG_BIG = -1e30


def _attn_kernel(q_ref, k_ref, v_ref, lam_ref, sub_ref, o_ref, m_scr, acc_scr, sa_scr, sb_scr, ma_scr, mb_scr,
                 p_scr, *,
                 lam_init):
    j = pl.program_id(1)
    nj = pl.num_programs(1)
    hd, vd = DA_HEAD_DIM, DA_V_DIM
    tk = k_ref.shape[0]

    @pl.when(j == 0)
    def _():
        m_scr[...] = jnp.full_like(m_scr, NEG_BIG)
        acc_scr[...] = jnp.zeros_like(acc_scr)

    lane = lax.broadcasted_iota(I32, (q_ref.shape[0], vd), 1)
    ones_col = jnp.where(lax.broadcasted_iota(I32, (tk, vd), 1) == 0, 1.0, 0.0).astype(BF16)

    def head_cols(h):
        return pl.ds(pl.multiple_of(h * vd, vd), vd)

    def scores(h, comp, s_ref, mx_ref):
        qb = q_ref[:, head_cols(h)]
        in_comp = (lane >= comp * hd) & (lane < (comp + 1) * hd)
        qm = jnp.where(in_comp, qb, jnp.zeros_like(qb))
        s = _dot_nt(qm, k_ref[:, head_cols(h)])
        s_ref[...] = s
        mx_ref[...] = jnp.broadcast_to(jnp.max(s, axis=1, keepdims=True), mx_ref.shape)

    def v_aug(h):
        return jnp.concatenate([v_ref[:, head_cols(h)], ones_col], axis=1)

    def update(h, comp, s_ref, mx_ref, defer):
        idx = 2 * h + comp
        m_prev = m_scr[idx]
        m_new = jnp.maximum(m_prev, mx_ref[...])
        alpha = _lane_tile(jnp.exp2(m_prev - m_new), 2)
        p = jnp.exp2(s_ref[...] - m_new[:, 0:1]).astype(BF16)
        m_scr[idx] = m_new
        if defer:
            acc_scr[idx] = alpha * acc_scr[idx]
            p_scr[...] = p
        else:
            acc_scr[idx] = alpha * acc_scr[idx] + _dot(p, v_aug(h))

    def deferred_pv(h):
        idx = 2 * h + 1
        acc_scr[idx] = acc_scr[idx] + _dot(p_scr[...], v_aug(h))

    p_scr[...] = jnp.zeros_like(p_scr)
    scores(0, 0, sa_scr, ma_scr)

    def head(h, carry):
        scores(h, 1, sb_scr, mb_scr)
        deferred_pv(jnp.maximum(h - 1, 0))
        update(h, 0, sa_scr, ma_scr, False)
        scores(jnp.minimum(h + 1, DA_HEADS - 1), 0, sa_scr, ma_scr)
        update(h, 1, sb_scr, mb_scr, True)
        return carry

    lax.fori_loop(0, DA_HEADS, head, 0)
    deferred_pv(DA_HEADS - 1)

    @pl.when(j == nj - 1)
    def _():
        lp = lam_ref[...]
        lam = (jnp.exp(jnp.sum(lp[0:1] * lp[1:2], axis=1, keepdims=True))
               - jnp.exp(jnp.sum(lp[2:3] * lp[3:4], axis=1, keepdims=True)) + lam_init)
        for h in range(DA_HEADS):
            a0 = acc_scr[2 * h]
            a1 = acc_scr[2 * h + 1]
            o = a0[:, :vd] / a0[:, vd:vd + 1] - lam * (a1[:, :vd] / a1[:, vd:vd + 1])
            ms = jnp.mean(o * o, axis=1, keepdims=True)
            o = o * lax.rsqrt(ms + SUBLN_EPS) * (sub_ref[...] * (1.0 - lam_init))
            o_ref[:, h * vd:(h + 1) * vd] = o.astype(BF16)


def _attention(q, k_all, v_all, lam_rows, subln_row, lam_init):
    n, d = q.shape
    nk = k_all.shape[0]
    tq = min(512, n)
    tk = 1280 if nk % 1280 == 0 else 256
    assert nk % tk == 0
    nc = 2 * DA_HEADS
    return pl.pallas_call(
        functools.partial(_attn_kernel, lam_init=lam_init),
        grid=(n // tq, nk // tk),
        in_specs=[pl.BlockSpec((tq, d), lambda i, j: (i, 0)),
                  pl.BlockSpec((tk, d), lambda i, j: (j, 0)),
                  pl.BlockSpec((tk, d), lambda i, j: (j, 0)),
                  _const_spec((8, DA_HEAD_DIM)), _const_spec((1, DA_V_DIM))],
        out_specs=pl.BlockSpec((tq, d), lambda i, j: (i, 0)),
        out_shape=jax.ShapeDtypeStruct((n, d), BF16),
        scratch_shapes=[pltpu.VMEM((nc, tq, DA_V_DIM), F32), pltpu.VMEM((nc, tq, 2 * DA_V_DIM), F32),
                        pltpu.VMEM((tq, tk), F32), pltpu.VMEM((tq, tk), F32),
                        pltpu.VMEM((tq, DA_V_DIM), F32), pltpu.VMEM((tq, DA_V_DIM), F32),
                        pltpu.VMEM((tq, tk), BF16)],
        compiler_params=_cp("parallel", "arbitrary"),
        name="diff_attention",
    )(q, k_all, v_all, lam_rows, subln_row)


def _rope_tables(n):
    t = jnp.arange(n, dtype=I32)
    row = (t // GRID_W).astype(F32)
    col = (t % GRID_W).astype(F32)
    inv = ROPE_THETA ** (-jnp.arange(0, ROPE_AXIS_DIM, 2, dtype=F32) / ROPE_AXIS_DIM)
    ar = row[:, None] * inv[None]
    ac = col[:, None] * inv[None]
    cos = jnp.concatenate([jnp.cos(ar), jnp.cos(ar), jnp.cos(ac), jnp.cos(ac)], axis=1)
    sin = jnp.concatenate([-jnp.sin(ar), jnp.sin(ar), -jnp.sin(ac), jnp.sin(ac)], axis=1)
    return jnp.concatenate([cos, cos], axis=1), jnp.concatenate([sin, sin], axis=1)


def _hyena_layer(x, mod, gate_row, kspec, inv_norm, w_in_bf, b_in, conv_w, conv_b, skip, w_out_bf, b_out):
    seq = x.shape[0]
    v, x1, x2 = _hy_in(x, mod, w_in_bf, b_in, conv_w, conv_b)
    z = _long_conv_gate(v, x1, skip[0:1], kspec, inv_norm, 0, seq)
    z = _long_conv_gate(z, x2, skip[1:2], kspec, inv_norm, 1, seq)
    return _proj_res(z, w_out_bf, b_out.reshape(1, -1), gate_row, x)


def kernel(x, c, ctx, c_ctx, ada_w, ada_b, norm_mix, norm_ffn, hy_w_in, hy_b_in, hy_conv_w, hy_conv_b, hy_f_w1, hy_f_b1, hy_f_w2, hy_f_b2, hy_f_w3, hy_f_b3, hy_f_w4, hy_f_freq, hy_skip, hy_w_out, hy_b_out, da_w_qkv, da_q_norm, da_k_norm, da_lam_q1, da_lam_k1, da_lam_q2, da_lam_k2, da_subln, da_w_out, moe_router, moe_w_gate, moe_w_up, moe_w_down):
    d = D_MODEL
    depth = ada_w.shape[0]
    assert x.shape[0] == 1 and x.shape[2] == d
    xs = x[0]
    cs = ctx[0]
    cond8 = jnp.concatenate([c[0:1], c_ctx[None], jnp.zeros((6, d), F32)], axis=0)
    mods = _adaln(cond8, ada_w, ada_b)

    def mod_slice(i, row, k):
        return mods[i, row, k * d:(k + 1) * d][None]

    for i in range(depth):
        last = i == depth - 1
        j = i // 2
        mix_x = _mod_rows(norm_mix[i], mods[i], 0, 0)
        mix_c = _mod_rows(norm_mix[i], mods[i], 1, 0)
        if i % 2 == 0:
            fparams = (hy_f_w1[j], hy_f_b1[j], hy_f_w2[j], hy_f_b2[j], hy_f_w3[j], hy_f_b3[j], hy_f_w4[j],
                       hy_f_freq[j])
            shared = (hy_w_in[j].astype(BF16), hy_b_in[j], hy_conv_w[j], hy_conv_b[j], hy_skip[j],
                      hy_w_out[j].astype(BF16), hy_b_out[j])
            kspec, asum = _filter_spectrum(xs.shape[0], fparams)
            inv_norm = 1.0 / (asum[0:1] + HY_FILTER_EPS)
            new_x = _hyena_layer(xs, mix_x, mod_slice(i, 0, 2), kspec, inv_norm, *shared)
            if not last:
                kspec_c, asum_c = _filter_spectrum(cs.shape[0], fparams)
                inv_c = 1.0 / (asum_c[0:1] + HY_FILTER_EPS)
                cs = _hyena_layer(cs, mix_c, mod_slice(i, 1, 2), kspec_c, inv_c, *shared)
            xs = new_x
        else:
            lam_init = 0.8 - 0.6 * math.exp(-0.3 * i)
            w_qkv = da_w_qkv[j].astype(BF16)
            gidx = np.arange(d) // DA_HEAD_DIM
            gsum = jnp.asarray((gidx[:, None] == gidx[None]).astype(np.float32)).astype(BF16)
            qn = jnp.tile(da_q_norm[j], 2 * DA_HEADS)[None]
            kn = jnp.tile(da_k_norm[j], 2 * DA_HEADS)[None]
            cos, sin = _rope_tables(xs.shape[0])
            qx, kx, vx = _qkv(xs, mix_x, w_qkv, gsum, qn, kn, cos, sin)
            qc, kc, vc = _qkv(cs, mix_c, w_qkv, gsum, qn, kn)
            k_all = jnp.concatenate([kc, kx], axis=0)
            v_all = jnp.concatenate([vc, vx], axis=0)
            lam_rows = jnp.concatenate([da_lam_q1[j][None], da_lam_k1[j][None], da_lam_q2[j][None],
                                        da_lam_k2[j][None], jnp.zeros((4, DA_HEAD_DIM), F32)], axis=0)
            w_out = da_w_out[j].astype(BF16)
            zero_b = jnp.zeros((1, d), F32)
            ox = _attention(qx, k_all, v_all, lam_rows, da_subln[j][None], lam_init)
            new_x = _proj_res(ox, w_out, zero_b, mod_slice(i, 0, 2), xs)
            if not last:
                oc = _attention(qc, kc, vc, lam_rows, da_subln[j][None], lam_init)
                cs = _proj_res(oc, w_out, zero_b, mod_slice(i, 1, 2), cs)
            xs = new_x
        wg = moe_w_gate[i].astype(BF16)
        wu = moe_w_up[i].astype(BF16)
        wd = moe_w_down[i].astype(BF16)
        if not last:
            cs = _moe_block(cs, _mod_rows(norm_ffn[i], mods[i], 1, 3), mod_slice(i, 1, 5), moe_router[i], wg, wu, wd)
        xs = _moe_block(xs, _mod_rows(norm_ffn[i], mods[i], 0, 3), mod_slice(i, 0, 5), moe_router[i], wg, wu, wd)
    return xs[None]
```

```python
import functools
import math

import jax
import jax.numpy as jnp
import numpy as np
from jax import lax
from jax.experimental import pallas as pl
from jax.experimental.pallas import tpu as pltpu

F32 = jnp.float32
BF16 = jnp.bfloat16
I32 = jnp.int32
HIGHEST = lax.Precision.HIGHEST

D_MODEL = 1024
N_MOD = 6
NORM_EPS = 1e-6
GRID_W = 64
HY_ORDER = 2
HY_SHORT = 3
HY_EMB_BANDS = 16
HY_EMB_DIM = 1 + 2 * HY_EMB_BANDS
HY_FILTER_HIDDEN = 64
HY_DECAY_FAST = 0.3
HY_DECAY_SLOW = 1.5
HY_DECAY_TARGET = 1e-2
HY_FILTER_EPS = 1e-6
Z_WIDTH = 40
Z_SIGN_COL = 33
DA_HEADS = 8
DA_HEAD_DIM = 64
DA_V_DIM = 128
ROPE_AXIS_DIM = 32
ROPE_THETA = 10000.0
SUBLN_EPS = 1e-5
N_EXPERTS = 16
EC_CAPACITY = 2
D_EXPERT = 1024
TOK_BLK = 256
ROW_ALIGN = 16
WIN_SHIFT = 7
WIN = 1 << WIN_SHIFT

VMEM_LIMIT = 56 * 1024 * 1024


def _cp(*sem):
    return pltpu.CompilerParams(dimension_semantics=sem, vmem_limit_bytes=VMEM_LIMIT)


def _const_spec(shape):
    nd = len(shape)
    return pl.BlockSpec(shape, lambda *_: (0,) * nd)


def _dot(a, b):
    return jnp.dot(a, b, preferred_element_type=F32)


def _dot_nt(a, b):
    return lax.dot_general(a, b, (((1,), (1,)), ((), ())), preferred_element_type=F32)


def _norm_mod(x, mod, eps=NORM_EPS):
    ms = jnp.mean(x * x, axis=-1, keepdims=True)
    return x * lax.rsqrt(ms + eps) * (mod[0:1] * (1.0 + mod[1:2])) + mod[2:3]


def _lane_tile(x, reps):
    return jnp.concatenate([x] * reps, axis=1) if reps > 1 else x


def _adaln_kernel(c_ref, w_ref, b_ref, o_ref):
    c = c_ref[...]
    s = c / (1.0 + jnp.exp(-c))
    o_ref[0] = jnp.dot(s, w_ref[0], precision=HIGHEST, preferred_element_type=F32) + b_ref[0]


def _adaln(cond8, ada_w, ada_b):
    depth, d, nout = ada_w.shape
    tn = 1536
    return pl.pallas_call(
        _adaln_kernel,
        grid=(depth, nout // tn),
        in_specs=[_const_spec((8, d)),
                  pl.BlockSpec((1, d, tn), lambda l, j: (l, 0, j)),
                  pl.BlockSpec((1, 1, tn), lambda l, j: (l, 0, j))],
        out_specs=pl.BlockSpec((1, 8, tn), lambda l, j: (l, 0, j)),
        out_shape=jax.ShapeDtypeStruct((depth, 8, nout), F32),
        compiler_params=_cp("parallel", "parallel"),
        name="adaln",
    )(cond8, ada_w, ada_b.reshape(depth, 1, nout))


def _mod_rows(norm_g, mods, row, k0):
    d = D_MODEL
    shift = mods[row, k0 * d:(k0 + 1) * d]
    scale = mods[row, (k0 + 1) * d:(k0 + 2) * d]
    z = jnp.zeros((5, d), F32)
    return jnp.concatenate([norm_g[None], scale[None], shift[None], z], axis=0)


HALO = 16


def _hy_in_kernel(x_ref, xp_ref, xn_ref, mod_ref, w_ref, b_ref, cw_ref, v_ref, x1_ref, x2_ref, *, tm, n_rows):
    i = pl.program_id(0)
    mod = mod_ref[...]
    hm = _norm_mod(x_ref[...], mod).astype(BF16)
    hp = _norm_mod(xp_ref[...], mod).astype(BF16)
    hn = _norm_mod(xn_ref[...], mod).astype(BF16)
    hcat = jnp.concatenate([hp, hm, hn], axis=0)
    row = lax.broadcasted_iota(I32, (tm + 2 * HALO, 1), 0) + (i * tm - HALO)
    valid = jnp.logical_and(row >= 0, row < n_rows)
    d = D_MODEL
    for c, o_ref in enumerate((v_ref, x1_ref, x2_ref)):
        u = _dot(hcat, w_ref[:, c * d:(c + 1) * d]) + b_ref[:, c * d:(c + 1) * d]
        u = jnp.where(valid, u, 0.0)
        cw = cw_ref[:, c * d:(c + 1) * d]
        y = (cw[3:4] + cw[0:1] * u[HALO - 1:HALO - 1 + tm] + cw[1:2] * u[HALO:HALO + tm]
             + cw[2:3] * u[HALO + 1:HALO + 1 + tm])
        o_ref[...] = y.astype(BF16)


def _hy_in(x, mod, w_bf, b_in, conv_w, conv_b):
    n, d = x.shape
    tm = min(512, n)
    nh = n // HALO
    cw = jnp.concatenate([conv_w, conv_b[None], jnp.zeros((4, 3 * d), F32)], axis=0)
    out = jax.ShapeDtypeStruct((n, d), BF16)
    row_spec = pl.BlockSpec((tm, d), lambda i: (i, 0))
    return pl.pallas_call(
        functools.partial(_hy_in_kernel, tm=tm, n_rows=n),
        grid=(n // tm,),
        in_specs=[row_spec,
                  pl.BlockSpec((HALO, d), lambda i: (jnp.maximum(i * (tm // HALO) - 1, 0), 0)),
                  pl.BlockSpec((HALO, d), lambda i: (jnp.minimum((i + 1) * (tm // HALO), nh - 1), 0)),
                  _const_spec((8, d)), _const_spec((d, 3 * d)), _const_spec((1, 3 * d)),
                  _const_spec((8, 3 * d))],
        out_specs=[row_spec, row_spec, row_spec],
        out_shape=[out, out, out],
        compiler_params=_cp("parallel"),
        name="hyena_in",
    )(x, x, x, mod, w_bf, b_in.reshape(1, 3 * d), cw)


def _filter_kernel(z_ref, w1_ref, b1_ref, w2_ref, b2_ref, w3_ref, b3_ref, fr_ref, w4_ref, dl_ref,
                   k_ref, asum_ref):
    i = pl.program_id(0)
    z = z_ref[...]
    fr = fr_ref[...]

    def lin(a, w_ref, b_ref):
        return jnp.dot(a, w_ref[...], precision=HIGHEST, preferred_element_type=F32) + b_ref[...]

    hid = jnp.sin(fr * lin(z, w1_ref, b1_ref))
    hid = jnp.sin(fr * lin(hid, w2_ref, b2_ref))
    hid = jnp.sin(fr * lin(hid, w3_ref, b3_ref))
    h = _dot(hid.astype(BF16), w4_ref[0])
    t = z[:, 0:1]
    sgn = z[:, Z_SIGN_COL:Z_SIGN_COL + 1]
    k = h * jnp.exp(-t * dl_ref[...]) * sgn

    @pl.when(i == 0)
    def _():
        asum_ref[...] = jnp.zeros_like(asum_ref)

    asum_ref[0:1, :] += jnp.sum(jnp.abs(k), axis=0, keepdims=True)
    k_ref[...] = k.astype(BF16)


def _filter_positions(seq):
    r = np.arange(2 * seq)
    pos = np.where(r < seq, r, 2 * seq - r).astype(np.float64)
    pos = np.minimum(pos, seq - 1)
    t = pos / (seq - 1)
    w = 2.0 * np.pi * pos / seq
    bands = np.linspace(1e-4, HY_EMB_BANDS - 1, HY_EMB_BANDS)[None]
    z = np.zeros((2 * seq, Z_WIDTH), np.float64)
    z[:, 0] = t
    z[:, 1:1 + HY_EMB_BANDS] = np.cos(w[:, None] * bands)
    z[:, 1 + HY_EMB_BANDS:HY_EMB_DIM] = -np.sin(w[:, None] * bands)
    z[:, Z_SIGN_COL] = np.where(r < seq, 1.0, np.where(r == seq, 0.0, -1.0))
    return jnp.asarray(z.astype(np.float32))


def _hyena_filter_time(seq, f_w1, f_b1, f_w2, f_b2, f_w3, f_b3, f_w4, f_freq):
    d, hid = D_MODEL, HY_FILTER_HIDDEN
    od = HY_ORDER * d
    tr = min(512, seq)
    z = _filter_positions(seq)
    w1p = jnp.concatenate([f_w1, jnp.zeros((Z_WIDTH - HY_EMB_DIM, hid), F32)], axis=0)
    w4d = f_w4.reshape(hid, HY_ORDER, 2, d).transpose(2, 0, 1, 3).reshape(2, hid, od).astype(BF16)
    max_decay = math.log(HY_DECAY_TARGET) / HY_DECAY_FAST
    min_decay = math.log(HY_DECAY_TARGET) / HY_DECAY_SLOW
    deltas = np.abs(np.linspace(min_decay, max_decay, d, dtype=np.float32))
    dl = jnp.asarray(np.tile(deltas, HY_ORDER)[None])
    tiles_per_dir = seq // tr
    k_t, asum = pl.pallas_call(
        _filter_kernel,
        grid=(2 * seq // tr,),
        in_specs=[pl.BlockSpec((tr, Z_WIDTH), lambda i: (i, 0)),
                  _const_spec((Z_WIDTH, hid)), _const_spec((1, hid)),
                  _const_spec((hid, hid)), _const_spec((1, hid)),
                  _const_spec((hid, hid)), _const_spec((1, hid)),
                  _const_spec((1, hid)),
                  pl.BlockSpec((1, hid, od), lambda i: (i // tiles_per_dir, 0, 0)),
                  _const_spec((1, od))],
        out_specs=[pl.BlockSpec((tr, od), lambda i: (i, 0)), _const_spec((8, od))],
        out_shape=[jax.ShapeDtypeStruct((2 * seq, od), BF16), jax.ShapeDtypeStruct((8, od), F32)],
        compiler_params=_cp("arbitrary"),
        name="hyena_filter",
    )(z, w1p, f_b1.reshape(1, hid), f_w2, f_b2.reshape(1, hid), f_w3, f_b3.reshape(1, hid),
      f_freq.reshape(1, hid), w4d, dl)
    return k_t, asum


def _fft_factors(seq):
    n = 2 * seq
    n1 = 256 if n >= 32768 else 32
    n2 = n // n1
    assert n1 * n2 == n and n2 % 16 == 0 and n1 % 32 == 0
    return n1, n2


@functools.lru_cache(maxsize=None)
def _fft_consts(seq):
    n = 2 * seq
    n1, n2 = _fft_factors(seq)
    k1 = np.arange(n1, dtype=np.float64)[:, None] + 0.5
    th1 = 2.0 * np.pi * k1 * np.arange(n1, dtype=np.float64)[None] / n1
    f1_full = np.concatenate([np.cos(th1), -np.sin(th1)], axis=0)
    f1_half = f1_full[:, :n1 // 2]
    tw = 2.0 * np.pi * k1 * np.arange(n2, dtype=np.float64)[None] / n
    tre, tim = np.cos(tw), -np.sin(tw)
    h2 = n2 // 2
    th2 = 2.0 * np.pi * np.arange(h2, dtype=np.float64)[:, None] * np.arange(n2, dtype=np.float64)[None] / n2
    c2, s2 = np.cos(th2), np.sin(th2)
    m2f = np.block([[c2, s2], [-s2, c2]])
    m2i = np.block([[c2.T, -s2.T], [s2.T, c2.T]])
    thb = th1[:, :n1 // 2].T
    gi = (2.0 / n) * np.concatenate([np.cos(thb), -np.sin(thb)], axis=1)
    bf = lambda a: jnp.asarray(a.astype(np.float32)).astype(BF16)
    f32 = lambda a: jnp.asarray(a.astype(np.float32))
    return dict(f1_full=bf(f1_full), f1_half=bf(f1_half), tre=f32(tre), tim=f32(tim),
                treT=f32(tre.T), timT=f32(tim.T), m2f=bf(m2f), m2i=bf(m2i), gi=bf(gi))


def _pick_col(tbl, idx):
    lane = lax.broadcasted_iota(I32, tbl.shape, 1)
    return jnp.sum(jnp.where(lane == idx, tbl, 0.0), axis=1, keepdims=True)


def _fft_s1_kernel(f1_ref, x_ref, tre_ref, tim_ref, o_ref, *, n1):
    j = pl.program_id(0)
    a = _dot(f1_ref[...], x_ref[...])
    are, aim = a[:n1], a[n1:]
    tre = _pick_col(tre_ref[...], j)
    tim = _pick_col(tim_ref[...], j)
    o_ref[0] = (are * tre - aim * tim).astype(BF16)
    o_ref[1] = (are * tim + aim * tre).astype(BF16)


def _fft_stage1(x2d, f1, tre, tim, n1, n2, chans):
    r = x2d.shape[0]
    tc = D_MODEL
    g = chans // tc
    return pl.pallas_call(
        functools.partial(_fft_s1_kernel, n1=n1),
        grid=(n2, g),
        in_specs=[_const_spec((2 * n1, r)),
                  pl.BlockSpec((r, tc), lambda j, c: (0, j * g + c)),
                  _const_spec((n1, n2)), _const_spec((n1, n2))],
        out_specs=pl.BlockSpec((2, n1, tc), lambda j, c: (0, 0, j * g + c)),
        out_shape=jax.ShapeDtypeStruct((2, n1, n2 * chans), BF16),
        compiler_params=_cp("parallel", "parallel"),
        name="fft_stage1",
    )(f1, x2d, tre, tim)


def _fft_s2_kernel(m2f_ref, b_ref, o_ref):
    b = jnp.concatenate([b_ref[0, 0], b_ref[1, 0]], axis=0)
    o_ref[0] = _dot(m2f_ref[...], b)


def _fft_stage2(b4, m2f, n1, n2, chans):
    tc = D_MODEL
    return pl.pallas_call(
        _fft_s2_kernel,
        grid=(n1, chans // tc),
        in_specs=[_const_spec((n2, 2 * n2)),
                  pl.BlockSpec((2, 1, n2, tc), lambda k, c: (0, k, 0, c))],
        out_specs=pl.BlockSpec((1, n2, tc), lambda k, c: (k, 0, c)),
        out_shape=jax.ShapeDtypeStruct((n1, n2, chans), F32),
        compiler_params=_cp("parallel", "parallel"),
        name="fft_stage2",
    )(m2f, b4)


def _fft_mid_kernel(m2f_ref, m2i_ref, b_ref, k_ref, inv_ref, treT_ref, timT_ref, o_ref, *, n2):
    k1 = pl.program_id(0)
    h2 = n2 // 2
    b = jnp.concatenate([b_ref[0, 0], b_ref[1, 0]], axis=0)
    x = _dot(m2f_ref[...], b)
    kk = k_ref[0] * inv_ref[...]
    xre, xim = x[:h2], x[h2:]
    kre, kim = kk[:h2], kk[h2:]
    y = jnp.concatenate([xre * kre - xim * kim, xre * kim + xim * kre], axis=0).astype(BF16)
    c = _dot(m2i_ref[...], y)
    cre, cim = c[:n2], c[n2:]
    tre = _pick_col(treT_ref[...], k1)
    tim = _pick_col(timT_ref[...], k1)
    o_ref[0, 0] = (cre * tre + cim * tim).astype(BF16)
    o_ref[1, 0] = (cim * tre - cre * tim).astype(BF16)


def _fft_mid(b4, kspec, inv_norm, order, cst, n1, n2):
    d = D_MODEL
    return pl.pallas_call(
        functools.partial(_fft_mid_kernel, n2=n2),
        grid=(n1,),
        in_specs=[_const_spec((n2, 2 * n2)), _const_spec((2 * n2, n2)),
                  pl.BlockSpec((2, 1, n2, d), lambda k: (0, k, 0, 0)),
                  pl.BlockSpec((1, n2, d), lambda k: (k, 0, order)),
                  pl.BlockSpec((1, d), lambda k: (0, order)),
                  _const_spec((n2, n1)), _const_spec((n2, n1))],
        out_specs=pl.BlockSpec((2, 1, n2, d), lambda k: (0, k, 0, 0)),
        out_shape=jax.ShapeDtypeStruct((2, n1, n2, d), BF16),
        compiler_params=_cp("parallel"),
        name="fft_mid",
    )(cst["m2f"], cst["m2i"], b4, kspec, inv_norm, cst["treT"], cst["timT"])


def _fft_last_kernel(gi_ref, c_ref, gate_ref, z_ref, skip_ref, o_ref):
    c = jnp.concatenate([c_ref[0], c_ref[1]], axis=0)
    y = _dot(gi_ref[...], c)
    z = z_ref[...].astype(F32)
    o_ref[...] = (gate_ref[...].astype(F32) * (y + skip_ref[...] * z)).astype(BF16)


def _fft_last(c3, gate2d, z2d, skip_row, gi, n1, n2):
    d = D_MODEL
    r = n1 // 2
    col = pl.BlockSpec((r, d), lambda j: (0, j))
    return pl.pallas_call(
        _fft_last_kernel,
        grid=(n2,),
        in_specs=[_const_spec((r, 2 * n1)),
                  pl.BlockSpec((2, n1, d), lambda j: (0, 0, j)),
                  col, col, _const_spec((1, d))],
        out_specs=col,
        out_shape=jax.ShapeDtypeStruct((r, n2 * d), BF16),
        compiler_params=_cp("parallel"),
        name="fft_last",
    )(gi, c3, gate2d, z2d, skip_row)


def _long_conv_gate(z_in, gate, skip_row, kspec, inv_norm, order, seq):
    d = D_MODEL
    n1, n2 = _fft_factors(seq)
    cst = _fft_consts(seq)
    z2d = z_in.reshape(n1 // 2, n2 * d)
    b = _fft_stage1(z2d, cst["f1_half"], cst["tre"], cst["tim"], n1, n2, d)
    c = _fft_mid(b.reshape(2, n1, n2, d), kspec, inv_norm, order, cst, n1, n2)
    out = _fft_last(c.reshape(2, n1, n2 * d), gate.reshape(n1 // 2, n2 * d), z2d, skip_row, cst["gi"], n1, n2)
    return out.reshape(seq, d)


def _filter_spectrum(seq, fparams):
    d = D_MODEL
    od = HY_ORDER * d
    n1, n2 = _fft_factors(seq)
    cst = _fft_consts(seq)
    k_t, asum = _hyena_filter_time(seq, *fparams)
    b = _fft_stage1(k_t.reshape(n1, n2 * od), cst["f1_full"], cst["tre"], cst["tim"], n1, n2, od)
    kspec = _fft_stage2(b.reshape(2, n1, n2, od), cst["m2f"], n1, n2, od)
    return kspec, asum


def _proj_res_kernel(a_ref, w_ref, b_ref, g_ref, x_ref, o_ref):
    y = _dot(a_ref[...], w_ref[...]) + b_ref[...]
    o_ref[...] = x_ref[...] + g_ref[...] * y


def _proj_res(a, w_bf, b_row, gate_row, xres):
    n, d = xres.shape
    tm = min(512, n)
    row = pl.BlockSpec((tm, d), lambda i: (i, 0))
    return pl.pallas_call(
        _proj_res_kernel,
        grid=(n // tm,),
        in_specs=[row, _const_spec((d, d)), _const_spec((1, d)), _const_spec((1, d)), row],
        out_specs=row,
        out_shape=jax.ShapeDtypeStruct((n, d), F32),
        compiler_params=_cp("parallel"),
        name="proj_residual",
    )(a, w_bf, b_row, gate_row, xres)


def _ffn_in_kernel(x_ref, mod_ref, wt_ref, h_ref, aff_ref):
    h = _norm_mod(x_ref[...], mod_ref[...])
    hi = h.astype(BF16)
    lo = (h - hi.astype(F32)).astype(BF16)
    wt = wt_ref[...]
    whi = wt.astype(BF16)
    wlo = (wt - whi.astype(F32)).astype(BF16)
    logits = _dot_nt(whi, hi) + (_dot_nt(whi, lo) + _dot_nt(wlo, hi))
    m = jnp.max(logits, axis=0, keepdims=True)
    p = jnp.exp(logits - m)
    aff_ref[...] = p / jnp.sum(p, axis=0, keepdims=True)
    h_ref[...] = hi


def _ffn_in(x, mod, w_router):
    n, d = x.shape
    e = N_EXPERTS
    tm = min(512, n)
    return pl.pallas_call(
        _ffn_in_kernel,
        grid=(n // tm,),
        in_specs=[pl.BlockSpec((tm, d), lambda i: (i, 0)), _const_spec((8, d)), _const_spec((e, d))],
        out_specs=[pl.BlockSpec((tm, d), lambda i: (i, 0)), pl.BlockSpec((e, tm), lambda i: (0, i))],
        out_shape=[jax.ShapeDtypeStruct((n, d), BF16), jax.ShapeDtypeStruct((e, n), F32)],
        compiler_params=_cp("parallel"),
        name="moe_router",
    )(x, mod, w_router.T)


def _select_kernel(a_ref, pos_ref, s0_ref, sel_ref, *, cap, nblk):
    e = N_EXPERTS
    bits = pltpu.bitcast(a_ref[...], I32)

    def bisect(i, thr):
        cand = thr | jnp.left_shift(jnp.int32(1), 30 - i)
        cnt = jnp.sum(jnp.where(bits >= cand, 1.0, 0.0), axis=1, keepdims=True)
        return jnp.where(cnt >= cap, cand, thr)

    thr = lax.fori_loop(0, 31, bisect, jnp.zeros((e, 1), I32))
    n_gt = jnp.sum(jnp.where(bits > thr, 1.0, 0.0), axis=1, keepdims=True)
    need = cap - n_gt
    r = lax.broadcasted_iota(I32, (TOK_BLK, TOK_BLK), 0)
    c = lax.broadcasted_iota(I32, (TOK_BLK, TOK_BLK), 1)
    upper = jnp.where(r < c, 1.0, 0.0).astype(BF16)

    def pass1(j, carry):
        sl = pl.ds(pl.multiple_of(j * TOK_BLK, TOK_BLK), TOK_BLK)
        bj = pltpu.bitcast(a_ref[:, sl], I32)
        eq = jnp.where(bj == thr, 1.0, 0.0)
        rank = _dot(eq.astype(BF16), upper) + carry
        keep = jnp.logical_or(bj > thr, jnp.logical_and(bj == thr, rank < need))
        sel_ref[:, sl] = jnp.where(keep, 1.0, 0.0)
        return carry + jnp.sum(eq, axis=1, keepdims=True)

    lax.fori_loop(0, nblk, pass1, jnp.zeros((e, 1), F32))

    def pass2(j, carry):
        sl = pl.ds(pl.multiple_of(j * TOK_BLK, TOK_BLK), TOK_BLK)
        s = sel_ref[:, sl]
        slot = _dot(s.astype(BF16), upper) + carry
        pos_ref[:, sl] = jnp.where(s > 0.5, slot, -1.0).astype(I32)
        s0_ref[j] = jnp.broadcast_to(carry, (e, 128)).astype(I32)
        return carry + jnp.sum(s, axis=1, keepdims=True)

    total = lax.fori_loop(0, nblk, pass2, jnp.zeros((e, 1), F32))
    s0_ref[nblk] = jnp.broadcast_to(total, (e, 128)).astype(I32)


def _select(aff_t, cap):
    e, n = aff_t.shape
    nblk = n // TOK_BLK
    return pl.pallas_call(
        functools.partial(_select_kernel, cap=cap, nblk=nblk),
        out_shape=[jax.ShapeDtypeStruct((e, n), I32), jax.ShapeDtypeStruct((nblk + 1, e, 128), I32)],
        scratch_shapes=[pltpu.VMEM((e, n), F32)],
        compiler_params=pltpu.CompilerParams(vmem_limit_bytes=VMEM_LIMIT),
        name="moe_select",
    )(aff_t)


def _block_windows(s0_ref, blk, e_idx):
    s0 = s0_ref[blk * N_EXPERTS + e_idx]
    s1 = s0_ref[(blk + 1) * N_EXPERTS + e_idx]
    start = lax.shift_left(lax.shift_right_logical(s0, 4), 4)
    nwin = jnp.where(s1 > s0, lax.shift_right_logical(s1 - start + (WIN - 1), WIN_SHIFT), 0)
    return start, nwin


def _gather_kernel(s0_ref, pos_ref, h_ref, xe_ref, *, sub):
    e_idx = pl.program_id(0)
    c = pl.program_id(1)

    @pl.when(c == 0)
    def _():
        xe_ref[...] = jnp.zeros_like(xe_ref)

    rows = lax.broadcasted_iota(I32, (WIN, TOK_BLK), 0)

    def body(i, carry):
        start, nwin = _block_windows(s0_ref, c * sub + i, e_idx)
        tok = pl.ds(pl.multiple_of(i * TOK_BLK, TOK_BLK), TOK_BLK)
        prow = pos_ref[0, :, tok]

        def window(w, carry2):
            base = pl.multiple_of(start + w * WIN, ROW_ALIGN)
            onehot = jnp.where(rows == prow - base, 1.0, 0.0).astype(BF16)
            got = _dot(onehot, h_ref[tok, :]).astype(BF16)
            win = pl.ds(base, WIN)
            xe_ref[0, win, :] = xe_ref[0, win, :] + got
            return carry2

        return lax.fori_loop(0, nwin, window, carry)

    lax.fori_loop(0, sub, body, 0)


def _gather(s0_flat, pos, h, cap_pad):
    e, n = pos.shape
    d = h.shape[1]
    chunk = min(2048, n)
    sub = chunk // TOK_BLK
    return pl.pallas_call(
        functools.partial(_gather_kernel, sub=sub),
        grid_spec=pltpu.PrefetchScalarGridSpec(
            num_scalar_prefetch=1,
            grid=(e, n // chunk),
            in_specs=[pl.BlockSpec((1, 1, chunk), lambda ei, c, s0: (ei, 0, c)),
                      pl.BlockSpec((chunk, d), lambda ei, c, s0: (c, 0))],
            out_specs=pl.BlockSpec((1, cap_pad, d), lambda ei, c, s0: (ei, 0, 0)),
        ),
        out_shape=jax.ShapeDtypeStruct((e, cap_pad, d), BF16),
        compiler_params=_cp("parallel", "arbitrary"),
        name="moe_gather",
    )(s0_flat, pos.reshape(e, 1, n), h)


def _expert_kernel(x_ref, wg_ref, wu_ref, wd_ref, y_ref, *, n_real):
    j = pl.program_id(1)

    @pl.when(j < n_real)
    def _():
        x = x_ref[0]
        g = _dot(x, wg_ref[0])
        u = _dot(x, wu_ref[0])
        a = (g / (1.0 + jnp.exp(-g))) * u
        y_ref[0] = _dot(a.astype(BF16), wd_ref[0]).astype(BF16)

    @pl.when(j >= n_real)
    def _():
        y_ref[0] = jnp.zeros_like(y_ref[0])


def _experts(xe, wg, wu, wd, cap):
    e, cap_pad, d = xe.shape
    f = wg.shape[2]
    tm = min(256, cap)
    tile = pl.BlockSpec((1, tm, d), lambda ei, j: (ei, j, 0))
    return pl.pallas_call(
        functools.partial(_expert_kernel, n_real=cap // tm),
        grid=(e, cap_pad // tm),
        in_specs=[tile,
                  pl.BlockSpec((1, d, f), lambda ei, j: (ei, 0, 0)),
                  pl.BlockSpec((1, d, f), lambda ei, j: (ei, 0, 0)),
                  pl.BlockSpec((1, f, d), lambda ei, j: (ei, 0, 0))],
        out_specs=tile,
        out_shape=jax.ShapeDtypeStruct((e, cap_pad, d), BF16),
        compiler_params=_cp("parallel", "arbitrary"),
        name="moe_experts",
    )(xe, wg, wu, wd)


def _combine_kernel(s0_ref, post_ref, gt_ref, y_ref, x_ref, gate_ref, o_ref, *, sub):
    c = pl.program_id(0)
    e_idx = pl.program_id(1)

    @pl.when(e_idx == 0)
    def _():
        o_ref[...] = jnp.zeros_like(o_ref)

    lanes = lax.broadcasted_iota(I32, (TOK_BLK, WIN), 1)
    elane = lax.broadcasted_iota(I32, (TOK_BLK, N_EXPERTS), 1) == e_idx

    def body(i, carry):
        start, nwin = _block_windows(s0_ref, c * sub + i, e_idx)
        tok = pl.ds(pl.multiple_of(i * TOK_BLK, TOK_BLK), TOK_BLK)

        def window(w, carry2):
            base = pl.multiple_of(start + w * WIN, ROW_ALIGN)
            pcol = jnp.sum(jnp.where(elane, post_ref[tok, :].astype(F32), 0.0), axis=1, keepdims=True)
            gcol = jnp.sum(jnp.where(elane, gt_ref[tok, :], 0.0), axis=1, keepdims=True)
            rel = pcol.astype(I32) - base
            onehot = jnp.where(lanes == rel, 1.0, 0.0).astype(BF16)
            strip = y_ref[0, pl.ds(base, WIN), :]
            o_ref[tok, :] = o_ref[tok, :] + gcol * _dot(onehot, strip)
            return carry2

        return lax.fori_loop(0, nwin, window, carry)

    lax.fori_loop(0, sub, body, 0)

    @pl.when(e_idx == N_EXPERTS - 1)
    def _():
        o_ref[...] = x_ref[...] + gate_ref[...] * o_ref[...]


def _combine(s0_flat, pos_t, g_t, y, xres, gate_row):
    n, d = xres.shape
    e, cap_pad, _ = y.shape
    chunk = min(2048, n)
    sub = chunk // TOK_BLK
    return pl.pallas_call(
        functools.partial(_combine_kernel, sub=sub),
        grid_spec=pltpu.PrefetchScalarGridSpec(
            num_scalar_prefetch=1,
            grid=(n // chunk, e),
            in_specs=[pl.BlockSpec((chunk, e), lambda c, ei, s0: (c, 0)),
                      pl.BlockSpec((chunk, e), lambda c, ei, s0: (c, 0)),
                      pl.BlockSpec((1, cap_pad, d), lambda c, ei, s0: (ei, 0, 0)),
                      pl.BlockSpec((chunk, d), lambda c, ei, s0: (c, 0)),
                      pl.BlockSpec((1, d), lambda c, ei, s0: (0, 0))],
            out_specs=pl.BlockSpec((chunk, d), lambda c, ei, s0: (c, 0)),
        ),
        out_shape=jax.ShapeDtypeStruct((n, d), F32),
        compiler_params=_cp("parallel", "arbitrary"),
        name="moe_combine",
    )(s0_flat, pos_t, g_t, y, xres, gate_row)


def _moe_block(x, mod, gate_row, w_router, wg, wu, wd):
    n = x.shape[0]
    cap = EC_CAPACITY * n // N_EXPERTS
    cap_pad = cap + TOK_BLK
    h, aff_t = _ffn_in(x, mod, w_router)
    pos, s0 = _select(aff_t, cap)
    s0_flat = s0[:, :, 0].reshape(-1)
    xe = _gather(s0_flat, pos, h, cap_pad)
    y = _experts(xe, wg, wu, wd, cap)
    return _combine(s0_flat, pos.T, aff_t.T, y, x, gate_row)


QK_SCALE = (DA_HEAD_DIM ** -0.5) * math.log2(math.e)


def _group_rms(u, gsum_ref, gain, eps):
    sq = u * u
    hi = sq.astype(BF16)
    lo = (sq - hi.astype(F32)).astype(BF16)
    ss = _dot(hi, gsum_ref[...]) + _dot(lo, gsum_ref[...])
    return u * lax.rsqrt(ss * (1.0 / DA_HEAD_DIM) + eps) * gain


def _rope(u, cos, sin_signed):
    d = u.shape[1]
    half = ROPE_AXIS_DIM // 2
    lane = lax.broadcasted_iota(I32, u.shape, 1)
    first = (lane & half) == 0
    swapped = jnp.where(first, pltpu.roll(u, d - half, 1), pltpu.roll(u, half, 1))
    return u * cos + swapped * sin_signed


def _qkv_kernel(*refs, rope):
    if rope:
        x_ref, mod_ref, w_ref, gsum_ref, qn_ref, kn_ref, cos_ref, sin_ref, q_ref, k_ref, v_ref = refs
    else:
        x_ref, mod_ref, w_ref, gsum_ref, qn_ref, kn_ref, q_ref, k_ref, v_ref = refs
    d = D_MODEL
    h = _norm_mod(x_ref[...], mod_ref[...]).astype(BF16)
    if rope:
        reps = d // cos_ref.shape[1]
        cos = _lane_tile(cos_ref[...], reps)
        sin = _lane_tile(sin_ref[...], reps)
    for part, (o_ref, gain_ref) in enumerate(((q_ref, qn_ref), (k_ref, kn_ref))):
        u = _dot(h, w_ref[:, part * d:(part + 1) * d])
        u = _group_rms(u, gsum_ref, gain_ref[...], NORM_EPS)
        if rope:
            u = _rope(u, cos, sin)
        if part == 0:
            u = u * QK_SCALE
        o_ref[...] = u.astype(BF16)
    v_ref[...] = _dot(h, w_ref[:, 2 * d:]).astype(BF16)


def _qkv(x, mod, w_bf, gsum, qn_row, kn_row, cos=None, sin=None):
    n, d = x.shape
    tm = min(512, n)
    rope = cos is not None
    row = pl.BlockSpec((tm, d), lambda i: (i, 0))
    in_specs = [row, _const_spec((8, d)), _const_spec((d, 3 * d)), _const_spec((d, d)),
                _const_spec((1, d)), _const_spec((1, d))]
    args = [x, mod, w_bf, gsum, qn_row, kn_row]
    if rope:
        tw = cos.shape[1]
        in_specs += [pl.BlockSpec((tm, tw), lambda i: (i, 0)), pl.BlockSpec((tm, tw), lambda i: (i, 0))]
        args += [cos, sin]
    out = jax.ShapeDtypeStruct((n, d), BF16)
    return pl.pallas_call(
        functools.partial(_qkv_kernel, rope=rope),
        grid=(n // tm,),
        in_specs=in_specs,
        out_specs=[row, row, row],
        out_shape=[out, out, out],
        compiler_params=_cp("parallel"),
        name="attn_qkv",
    )(*args)


NEG_BIG = -1e30


def _attn_kernel(q_ref, k_ref, v_ref, lam_ref, sub_ref, o_ref, m_scr, acc_scr, sa_scr, sb_scr, ma_scr, mb_scr,
                 pa_scr, pb_scr, *,
                 lam_init):
    j = pl.program_id(1)
    nj = pl.num_programs(1)
    hd, vd = DA_HEAD_DIM, DA_V_DIM
    tk = k_ref.shape[0]

    @pl.when(j == 0)
    def _():
        m_scr[...] = jnp.full_like(m_scr, NEG_BIG)
        acc_scr[...] = jnp.zeros_like(acc_scr)

    lane = lax.broadcasted_iota(I32, (q_ref.shape[0], vd), 1)
    ones_col = jnp.where(lax.broadcasted_iota(I32, (tk, vd), 1) == 0, 1.0, 0.0).astype(BF16)

    def head_cols(h):
        return pl.ds(pl.multiple_of(h * vd, vd), vd)

    def scores(h, comp, s_ref, mx_ref):
        qb = q_ref[:, head_cols(h)]
        in_comp = (lane >= comp * hd) & (lane < (comp + 1) * hd)
        qm = jnp.where(in_comp, qb, jnp.zeros_like(qb))
        s = _dot_nt(qm, k_ref[:, head_cols(h)])
        s_ref[...] = s
        mx_ref[...] = jnp.broadcast_to(jnp.max(s, axis=1, keepdims=True), mx_ref.shape)

    def v_aug(h):
        return jnp.concatenate([v_ref[:, head_cols(h)], ones_col], axis=1)

    def softmax_step(h, comp, s_ref, mx_ref, p_ref):
        idx = 2 * h + comp
        m_prev = m_scr[idx]
        m_new = jnp.maximum(m_prev, mx_ref[...])
        alpha = _lane_tile(jnp.exp2(m_prev - m_new), 2)
        p_ref[...] = jnp.exp2(s_ref[...] - m_new[:, 0:1]).astype(BF16)
        m_scr[idx] = m_new
        acc_scr[idx] = alpha * acc_scr[idx]

    def pv_step(h, comp, p_ref):
        idx = 2 * h + comp
        acc_scr[idx] = acc_scr[idx] + _dot(p_ref[...], v_aug(h))

    pb_scr[...] = jnp.zeros_like(pb_scr)
    scores(0, 0, sa_scr, ma_scr)

    def head(h, carry):
        scores(h, 1, sb_scr, mb_scr)
        pv_step(jnp.maximum(h - 1, 0), 1, pb_scr)
        softmax_step(h, 0, sa_scr, ma_scr, pa_scr)
        scores(jnp.minimum(h + 1, DA_HEADS - 1), 0, sa_scr, ma_scr)
        pv_step(h, 0, pa_scr)
        softmax_step(h, 1, sb_scr, mb_scr, pb_scr)
        return carry

    def head_pair(g, carry):
        return head(2 * g + 1, head(2 * g, carry))

    lax.fori_loop(0, DA_HEADS // 2, head_pair, 0)
    pv_step(DA_HEADS - 1, 1, pb_scr)

    @pl.when(j == nj - 1)
    def _():
        lp = lam_ref[...]
        lam = (jnp.exp(jnp.sum(lp[0:1] * lp[1:2], axis=1, keepdims=True))
               - jnp.exp(jnp.sum(lp[2:3] * lp[3:4], axis=1, keepdims=True)) + lam_init)
        for h in range(DA_HEADS):
            a0 = acc_scr[2 * h]
            a1 = acc_scr[2 * h + 1]
            o = a0[:, :vd] / a0[:, vd:vd + 1] - lam * (a1[:, :vd] / a1[:, vd:vd + 1])
            ms = jnp.mean(o * o, axis=1, keepdims=True)
            o = o * lax.rsqrt(ms + SUBLN_EPS) * (sub_ref[...] * (1.0 - lam_init))
            o_ref[:, h * vd:(h + 1) * vd] = o.astype(BF16)


def _attention(q, k_all, v_all, lam_rows, subln_row, lam_init):
    n, d = q.shape
    nk = k_all.shape[0]
    tq = min(512, n)
    tk = 1280 if nk % 1280 == 0 else 256
    assert nk % tk == 0
    nc = 2 * DA_HEADS
    return pl.pallas_call(
        functools.partial(_attn_kernel, lam_init=lam_init),
        grid=(n // tq, nk // tk),
        in_specs=[pl.BlockSpec((tq, d), lambda i, j: (i, 0)),
                  pl.BlockSpec((tk, d), lambda i, j: (j, 0)),
                  pl.BlockSpec((tk, d), lambda i, j: (j, 0)),
                  _const_spec((8, DA_HEAD_DIM)), _const_spec((1, DA_V_DIM))],
        out_specs=pl.BlockSpec((tq, d), lambda i, j: (i, 0)),
        out_shape=jax.ShapeDtypeStruct((n, d), BF16),
        scratch_shapes=[pltpu.VMEM((nc, tq, DA_V_DIM), F32), pltpu.VMEM((nc, tq, 2 * DA_V_DIM), F32),
                        pltpu.VMEM((tq, tk), F32), pltpu.VMEM((tq, tk), F32),
                        pltpu.VMEM((tq, DA_V_DIM), F32), pltpu.VMEM((tq, DA_V_DIM), F32),
                        pltpu.VMEM((tq, tk), BF16), pltpu.VMEM((tq, tk), BF16)],
        compiler_params=_cp("parallel", "arbitrary"),
        name="diff_attention",
    )(q, k_all, v_all, lam_rows, subln_row)


def _rope_tables(n):
    t = jnp.arange(n, dtype=I32)
    row = (t // GRID_W).astype(F32)
    col = (t % GRID_W).astype(F32)
    inv = ROPE_THETA ** (-jnp.arange(0, ROPE_AXIS_DIM, 2, dtype=F32) / ROPE_AXIS_DIM)
    ar = row[:, None] * inv[None]
    ac = col[:, None] * inv[None]
    cos = jnp.concatenate([jnp.cos(ar), jnp.cos(ar), jnp.cos(ac), jnp.cos(ac)], axis=1)
    sin = jnp.concatenate([-jnp.sin(ar), jnp.sin(ar), -jnp.sin(ac), jnp.sin(ac)], axis=1)
    return jnp.concatenate([cos, cos], axis=1), jnp.concatenate([sin, sin], axis=1)


def _hyena_layer(x, mod, gate_row, kspec, inv_norm, w_in_bf, b_in, conv_w, conv_b, skip, w_out_bf, b_out):
    seq = x.shape[0]
    v, x1, x2 = _hy_in(x, mod, w_in_bf, b_in, conv_w, conv_b)
    z = _long_conv_gate(v, x1, skip[0:1], kspec, inv_norm, 0, seq)
    z = _long_conv_gate(z, x2, skip[1:2], kspec, inv_norm, 1, seq)
    return _proj_res(z, w_out_bf, b_out.reshape(1, -1), gate_row, x)


def kernel(x, c, ctx, c_ctx, ada_w, ada_b, norm_mix, norm_ffn, hy_w_in, hy_b_in, hy_conv_w, hy_conv_b, hy_f_w1, hy_f_b1, hy_f_w2, hy_f_b2, hy_f_w3, hy_f_b3, hy_f_w4, hy_f_freq, hy_skip, hy_w_out, hy_b_out, da_w_qkv, da_q_norm, da_k_norm, da_lam_q1, da_lam_k1, da_lam_q2, da_lam_k2, da_subln, da_w_out, moe_router, moe_w_gate, moe_w_up, moe_w_down):
    d = D_MODEL
    depth = ada_w.shape[0]
    assert x.shape[0] == 1 and x.shape[2] == d
    xs = x[0]
    cs = ctx[0]
    cond8 = jnp.concatenate([c[0:1], c_ctx[None], jnp.zeros((6, d), F32)], axis=0)
    mods = _adaln(cond8, ada_w, ada_b)

    def mod_slice(i, row, k):
        return mods[i, row, k * d:(k + 1) * d][None]

    for i in range(depth):
        last = i == depth - 1
        j = i // 2
        mix_x = _mod_rows(norm_mix[i], mods[i], 0, 0)
        mix_c = _mod_rows(norm_mix[i], mods[i], 1, 0)
        if i % 2 == 0:
            fparams = (hy_f_w1[j], hy_f_b1[j], hy_f_w2[j], hy_f_b2[j], hy_f_w3[j], hy_f_b3[j], hy_f_w4[j],
                       hy_f_freq[j])
            shared = (hy_w_in[j].astype(BF16), hy_b_in[j], hy_conv_w[j], hy_conv_b[j], hy_skip[j],
                      hy_w_out[j].astype(BF16), hy_b_out[j])
            kspec, asum = _filter_spectrum(xs.shape[0], fparams)
            inv_norm = 1.0 / (asum[0:1] + HY_FILTER_EPS)
            new_x = _hyena_layer(xs, mix_x, mod_slice(i, 0, 2), kspec, inv_norm, *shared)
            if not last:
                kspec_c, asum_c = _filter_spectrum(cs.shape[0], fparams)
                inv_c = 1.0 / (asum_c[0:1] + HY_FILTER_EPS)
                cs = _hyena_layer(cs, mix_c, mod_slice(i, 1, 2), kspec_c, inv_c, *shared)
            xs = new_x
        else:
            lam_init = 0.8 - 0.6 * math.exp(-0.3 * i)
            w_qkv = da_w_qkv[j].astype(BF16)
            gidx = np.arange(d) // DA_HEAD_DIM
            gsum = jnp.asarray((gidx[:, None] == gidx[None]).astype(np.float32)).astype(BF16)
            qn = jnp.tile(da_q_norm[j], 2 * DA_HEADS)[None]
            kn = jnp.tile(da_k_norm[j], 2 * DA_HEADS)[None]
            cos, sin = _rope_tables(xs.shape[0])
            qx, kx, vx = _qkv(xs, mix_x, w_qkv, gsum, qn, kn, cos, sin)
            qc, kc, vc = _qkv(cs, mix_c, w_qkv, gsum, qn, kn)
            k_all = jnp.concatenate([kc, kx], axis=0)
            v_all = jnp.concatenate([vc, vx], axis=0)
            lam_rows = jnp.concatenate([da_lam_q1[j][None], da_lam_k1[j][None], da_lam_q2[j][None],
                                        da_lam_k2[j][None], jnp.zeros((4, DA_HEAD_DIM), F32)], axis=0)
            w_out = da_w_out[j].astype(BF16)
            zero_b = jnp.zeros((1, d), F32)
            ox = _attention(qx, k_all, v_all, lam_rows, da_subln[j][None], lam_init)
            new_x = _proj_res(ox, w_out, zero_b, mod_slice(i, 0, 2), xs)
            if not last:
                oc = _attention(qc, kc, vc, lam_rows, da_subln[j][None], lam_init)
                cs = _proj_res(oc, w_out, zero_b, mod_slice(i, 1, 2), cs)
            xs = new_x
        wg = moe_w_gate[i].astype(BF16)
        wu = moe_w_up[i].astype(BF16)
        wd = moe_w_down[i].astype(BF16)
        if not last:
            cs = _moe_block(cs, _mod_rows(norm_ffn[i], mods[i], 1, 3), mod_slice(i, 1, 5), moe_router[i], wg, wu, wd)
        xs = _moe_block(xs, _mod_rows(norm_ffn[i], mods[i], 0, 3), mod_slice(i, 0, 5), moe_router[i], wg, wu, wd)
    return xs[None]
```

```python
import functools
import math

import jax
import jax.numpy as jnp
import numpy as np
from jax import lax
from jax.experimental import pallas as pl
from jax.experimental.pallas import tpu as pltpu

F32 = jnp.float32
BF16 = jnp.bfloat16
I32 = jnp.int32
HIGHEST = lax.Precision.HIGHEST

D_MODEL = 1024
N_MOD = 6
NORM_EPS = 1e-6
GRID_W = 64
HY_ORDER = 2
HY_SHORT = 3
HY_EMB_BANDS = 16
HY_EMB_DIM = 1 + 2 * HY_EMB_BANDS
HY_FILTER_HIDDEN = 64
HY_DECAY_FAST = 0.3
HY_DECAY_SLOW = 1.5
HY_DECAY_TARGET = 1e-2
HY_FILTER_EPS = 1e-6
Z_WIDTH = 40
Z_SIGN_COL = 33
DA_HEADS = 8
DA_HEAD_DIM = 64
DA_V_DIM = 128
ROPE_AXIS_DIM = 32
ROPE_THETA = 10000.0
SUBLN_EPS = 1e-5
N_EXPERTS = 16
EC_CAPACITY = 2
D_EXPERT = 1024
TOK_BLK = 256
ROW_ALIGN = 16
WIN_SHIFT = 7
WIN = 1 << WIN_SHIFT

VMEM_LIMIT = 56 * 1024 * 1024


def _cp(*sem):
    return pltpu.CompilerParams(dimension_semantics=sem, vmem_limit_bytes=VMEM_LIMIT)


def _const_spec(shape):
    nd = len(shape)
    return pl.BlockSpec(shape, lambda *_: (0,) * nd)


def _dot(a, b):
    return jnp.dot(a, b, preferred_element_type=F32)


def _dot_nt(a, b):
    return lax.dot_general(a, b, (((1,), (1,)), ((), ())), preferred_element_type=F32)


def _norm_mod(x, mod, eps=NORM_EPS):
    ms = jnp.mean(x * x, axis=-1, keepdims=True)
    return x * lax.rsqrt(ms + eps) * (mod[0:1] * (1.0 + mod[1:2])) + mod[2:3]


def _lane_tile(x, reps):
    return jnp.concatenate([x] * reps, axis=1) if reps > 1 else x


def _adaln_kernel(c_ref, w_ref, b_ref, o_ref):
    c = c_ref[...]
    s = c / (1.0 + jnp.exp(-c))
    o_ref[0] = jnp.dot(s, w_ref[0], precision=HIGHEST, preferred_element_type=F32) + b_ref[0]


def _adaln(cond8, ada_w, ada_b):
    depth, d, nout = ada_w.shape
    tn = 1536
    return pl.pallas_call(
        _adaln_kernel,
        grid=(depth, nout // tn),
        in_specs=[_const_spec((8, d)),
                  pl.BlockSpec((1, d, tn), lambda l, j: (l, 0, j)),
                  pl.BlockSpec((1, 1, tn), lambda l, j: (l, 0, j))],
        out_specs=pl.BlockSpec((1, 8, tn), lambda l, j: (l, 0, j)),
        out_shape=jax.ShapeDtypeStruct((depth, 8, nout), F32),
        compiler_params=_cp("parallel", "parallel"),
        name="adaln",
    )(cond8, ada_w, ada_b.reshape(depth, 1, nout))


def _mod_rows(norm_g, mods, row, k0):
    d = D_MODEL
    shift = mods[row, k0 * d:(k0 + 1) * d]
    scale = mods[row, (k0 + 1) * d:(k0 + 2) * d]
    z = jnp.zeros((5, d), F32)
    return jnp.concatenate([norm_g[None], scale[None], shift[None], z], axis=0)


HALO = 16


def _hy_in_kernel(x_ref, xp_ref, xn_ref, mod_ref, w_ref, b_ref, cw_ref, v_ref, x1_ref, x2_ref, *, tm, n_rows):
    i = pl.program_id(0)
    mod = mod_ref[...]
    hm = _norm_mod(x_ref[...], mod).astype(BF16)
    hp = _norm_mod(xp_ref[...], mod).astype(BF16)
    hn = _norm_mod(xn_ref[...], mod).astype(BF16)
    hcat = jnp.concatenate([hp, hm, hn], axis=0)
    row = lax.broadcasted_iota(I32, (tm + 2 * HALO, 1), 0) + (i * tm - HALO)
    valid = jnp.logical_and(row >= 0, row < n_rows)
    d = D_MODEL
    for c, o_ref in enumerate((v_ref, x1_ref, x2_ref)):
        u = _dot(hcat, w_ref[:, c * d:(c + 1) * d]) + b_ref[:, c * d:(c + 1) * d]
        u = jnp.where(valid, u, 0.0)
        cw = cw_ref[:, c * d:(c + 1) * d]
        y = (cw[3:4] + cw[0:1] * u[HALO - 1:HALO - 1 + tm] + cw[1:2] * u[HALO:HALO + tm]
             + cw[2:3] * u[HALO + 1:HALO + 1 + tm])
        o_ref[...] = y.astype(BF16)


def _hy_in(x, mod, w_bf, b_in, conv_w, conv_b):
    n, d = x.shape
    tm = min(512, n)
    nh = n // HALO
    cw = jnp.concatenate([conv_w, conv_b[None], jnp.zeros((4, 3 * d), F32)], axis=0)
    out = jax.ShapeDtypeStruct((n, d), BF16)
    row_spec = pl.BlockSpec((tm, d), lambda i: (i, 0))
    return pl.pallas_call(
        functools.partial(_hy_in_kernel, tm=tm, n_rows=n),
        grid=(n // tm,),
        in_specs=[row_spec,
                  pl.BlockSpec((HALO, d), lambda i: (jnp.maximum(i * (tm // HALO) - 1, 0), 0)),
                  pl.BlockSpec((HALO, d), lambda i: (jnp.minimum((i + 1) * (tm // HALO), nh - 1), 0)),
                  _const_spec((8, d)), _const_spec((d, 3 * d)), _const_spec((1, 3 * d)),
                  _const_spec((8, 3 * d))],
        out_specs=[row_spec, row_spec, row_spec],
        out_shape=[out, out, out],
        compiler_params=_cp("parallel"),
        name="hyena_in",
    )(x, x, x, mod, w_bf, b_in.reshape(1, 3 * d), cw)


def _filter_kernel(z_ref, w1_ref, b1_ref, w2_ref, b2_ref, w3_ref, b3_ref, fr_ref, w4_ref, dl_ref,
                   k_ref, asum_ref):
    i = pl.program_id(0)
    z = z_ref[...]
    fr = fr_ref[...]

    def lin(a, w_ref, b_ref):
        return jnp.dot(a, w_ref[...], precision=HIGHEST, preferred_element_type=F32) + b_ref[...]

    hid = jnp.sin(fr * lin(z, w1_ref, b1_ref))
    hid = jnp.sin(fr * lin(hid, w2_ref, b2_ref))
    hid = jnp.sin(fr * lin(hid, w3_ref, b3_ref))
    h = _dot(hid.astype(BF16), w4_ref[0])
    t = z[:, 0:1]
    sgn = z[:, Z_SIGN_COL:Z_SIGN_COL + 1]
    k = h * jnp.exp(-t * dl_ref[...]) * sgn

    @pl.when(i == 0)
    def _():
        asum_ref[...] = jnp.zeros_like(asum_ref)

    asum_ref[0:1, :] += jnp.sum(jnp.abs(k), axis=0, keepdims=True)
    k_ref[...] = k.astype(BF16)


def _filter_positions(seq):
    r = np.arange(2 * seq)
    pos = np.where(r < seq, r, 2 * seq - r).astype(np.float64)
    pos = np.minimum(pos, seq - 1)
    t = pos / (seq - 1)
    w = 2.0 * np.pi * pos / seq
    bands = np.linspace(1e-4, HY_EMB_BANDS - 1, HY_EMB_BANDS)[None]
    z = np.zeros((2 * seq, Z_WIDTH), np.float64)
    z[:, 0] = t
    z[:, 1:1 + HY_EMB_BANDS] = np.cos(w[:, None] * bands)
    z[:, 1 + HY_EMB_BANDS:HY_EMB_DIM] = -np.sin(w[:, None] * bands)
    z[:, Z_SIGN_COL] = np.where(r < seq, 1.0, np.where(r == seq, 0.0, -1.0))
    return jnp.asarray(z.astype(np.float32))


def _hyena_filter_time(seq, f_w1, f_b1, f_w2, f_b2, f_w3, f_b3, f_w4, f_freq):
    d, hid = D_MODEL, HY_FILTER_HIDDEN
    od = HY_ORDER * d
    tr = min(512, seq)
    z = _filter_positions(seq)
    w1p = jnp.concatenate([f_w1, jnp.zeros((Z_WIDTH - HY_EMB_DIM, hid), F32)], axis=0)
    w4d = f_w4.reshape(hid, HY_ORDER, 2, d).transpose(2, 0, 1, 3).reshape(2, hid, od).astype(BF16)
    max_decay = math.log(HY_DECAY_TARGET) / HY_DECAY_FAST
    min_decay = math.log(HY_DECAY_TARGET) / HY_DECAY_SLOW
    deltas = np.abs(np.linspace(min_decay, max_decay, d, dtype=np.float32))
    dl = jnp.asarray(np.tile(deltas, HY_ORDER)[None])
    tiles_per_dir = seq // tr
    k_t, asum = pl.pallas_call(
        _filter_kernel,
        grid=(2 * seq // tr,),
        in_specs=[pl.BlockSpec((tr, Z_WIDTH), lambda i: (i, 0)),
                  _const_spec((Z_WIDTH, hid)), _const_spec((1, hid)),
                  _const_spec((hid, hid)), _const_spec((1, hid)),
                  _const_spec((hid, hid)), _const_spec((1, hid)),
                  _const_spec((1, hid)),
                  pl.BlockSpec((1, hid, od), lambda i: (i // tiles_per_dir, 0, 0)),
                  _const_spec((1, od))],
        out_specs=[pl.BlockSpec((tr, od), lambda i: (i, 0)), _const_spec((8, od))],
        out_shape=[jax.ShapeDtypeStruct((2 * seq, od), BF16), jax.ShapeDtypeStruct((8, od), F32)],
        compiler_params=_cp("arbitrary"),
        name="hyena_filter",
    )(z, w1p, f_b1.reshape(1, hid), f_w2, f_b2.reshape(1, hid), f_w3, f_b3.reshape(1, hid),
      f_freq.reshape(1, hid), w4d, dl)
    return k_t, asum


def _fft_factors(seq):
    n = 2 * seq
    n1 = 256 if n >= 32768 else 32
    n2 = n // n1
    assert n1 * n2 == n and n2 % 16 == 0 and n1 % 32 == 0
    return n1, n2


@functools.lru_cache(maxsize=None)
def _fft_consts(seq):
    n = 2 * seq
    n1, n2 = _fft_factors(seq)
    k1 = np.arange(n1, dtype=np.float64)[:, None] + 0.5
    th1 = 2.0 * np.pi * k1 * np.arange(n1, dtype=np.float64)[None] / n1
    perm = np.concatenate([np.arange(0, n1, 2), np.arange(1, n1, 2)])
    f1_full = np.concatenate([np.cos(th1)[perm], -np.sin(th1)[perm]], axis=0)
    f1_half = f1_full[:, :n1 // 2]
    tw = 2.0 * np.pi * k1 * np.arange(n2, dtype=np.float64)[None] / n
    tre, tim = np.cos(tw), -np.sin(tw)
    h2 = n2 // 2
    th2 = 2.0 * np.pi * np.arange(h2, dtype=np.float64)[:, None] * np.arange(n2, dtype=np.float64)[None] / n2
    c2, s2 = np.cos(th2), np.sin(th2)
    m2f = np.block([[c2, s2], [-s2, c2]])
    m2i = np.block([[c2.T, -s2.T], [s2.T, c2.T]])
    thb = th1[:, :n1 // 2].T
    gi = (2.0 / n) * np.concatenate([np.cos(thb)[:, perm], -np.sin(thb)[:, perm]], axis=1)
    bf = lambda a: jnp.asarray(a.astype(np.float32)).astype(BF16)
    f32 = lambda a: jnp.asarray(a.astype(np.float32))
    grp = lambda a: a.T.reshape(n2, n1 // K1_GRP, K1_GRP).transpose(1, 0, 2)
    return dict(f1_full=bf(f1_full), f1_half=bf(f1_half), tre=f32(tre[perm]), tim=f32(tim[perm]),
                tre_grp=f32(grp(tre)), tim_grp=f32(grp(tim)), m2f=bf(m2f), m2i=bf(m2i), gi=bf(gi))


def _pick_col(tbl, idx):
    lane = lax.broadcasted_iota(I32, tbl.shape, 1)
    return jnp.sum(jnp.where(lane == idx, tbl, 0.0), axis=1, keepdims=True)


K1_GRP = 16
PAIRS = K1_GRP // 2
LANES = 128
MID_TC = 512


def _unpack_pair(w, half):
    bits = lax.shift_left(w, jnp.uint32(16)) if half == 0 else (w & jnp.uint32(0xFFFF0000))
    return pltpu.bitcast(bits, F32).astype(BF16)


def _pack_pair(even, odd):
    ue = pltpu.bitcast(even.astype(BF16).astype(F32), jnp.uint32)
    uo = pltpu.bitcast(odd.astype(BF16).astype(F32), jnp.uint32)
    return lax.shift_right_logical(ue, jnp.uint32(16)) | uo


def _fft_s1_kernel(f1_ref, x_ref, tre_ref, tim_ref, o_ref, *, n1):
    j = pl.program_id(0)
    a = _dot(f1_ref[...], x_ref[...])
    are, aim = a[:n1], a[n1:]
    tre = _pick_col(tre_ref[...], j)
    tim = _pick_col(tim_ref[...], j)
    re = are * tre - aim * tim
    im = are * tim + aim * tre
    h = n1 // 2
    shape = (n1 // K1_GRP, PAIRS, a.shape[1])
    for part, val in enumerate((re, im)):
        words = _pack_pair(val[:h], val[h:]).reshape(shape)
        for ct in range(a.shape[1] // LANES):
            o_ref[part, :, ct] = words[:, :, ct * LANES:(ct + 1) * LANES]


def _fft_stage1(x2d, f1, tre, tim, n1, n2, chans):
    r = x2d.shape[0]
    tc = D_MODEL
    g = chans // tc
    return pl.pallas_call(
        functools.partial(_fft_s1_kernel, n1=n1),
        grid=(n2, g),
        in_specs=[_const_spec((2 * n1, r)),
                  pl.BlockSpec((r, tc), lambda j, c: (0, j * g + c)),
                  _const_spec((n1, n2)), _const_spec((n1, n2))],
        out_specs=pl.BlockSpec((2, n1 // K1_GRP, tc // LANES, PAIRS, LANES), lambda j, c: (0, 0, c, j, 0)),
        out_shape=jax.ShapeDtypeStruct((2, n1 // K1_GRP, chans // LANES, n2 * PAIRS, LANES), jnp.uint32),
        compiler_params=_cp("parallel", "parallel"),
        name="fft_stage1",
    )(f1, x2d, tre, tim)


def _load_pair(b_ref, part, s, n2):
    return jnp.concatenate([b_ref[part, 0, ct, pl.ds(s, n2, stride=PAIRS), :] for ct in range(b_ref.shape[2])],
                           axis=1)


def _store_pair(o_ref, part, s, n2, words):
    for ct in range(o_ref.shape[2]):
        o_ref[part, 0, ct, pl.ds(s, n2, stride=PAIRS), :] = words[:, ct * LANES:(ct + 1) * LANES]


def _fft_s2_kernel(m2f_ref, b_ref, o_ref):
    n2 = m2f_ref.shape[0]
    for s in range(PAIRS):
        wre = _load_pair(b_ref, 0, s, n2)
        wim = _load_pair(b_ref, 1, s, n2)
        for half in range(2):
            b = jnp.concatenate([_unpack_pair(wre, half), _unpack_pair(wim, half)], axis=0)
            o_ref[2 * s + half] = _dot(m2f_ref[...], b).astype(BF16)


def _fft_stage2(b5, m2f, n1, n2, chans):
    return pl.pallas_call(
        _fft_s2_kernel,
        grid=(n1 // K1_GRP, chans // MID_TC),
        in_specs=[_const_spec((n2, 2 * n2)),
                  pl.BlockSpec((2, 1, MID_TC // LANES, n2 * PAIRS, LANES), lambda g, c: (0, g, c, 0, 0))],
        out_specs=pl.BlockSpec((K1_GRP, n2, MID_TC), lambda g, c: (g, 0, c)),
        out_shape=jax.ShapeDtypeStruct((n1, n2, chans), BF16),
        compiler_params=_cp("parallel", "parallel"),
        name="fft_stage2",
    )(m2f, b5)


def _fft_mid_kernel(m2f_ref, m2i_ref, b_ref, k_ref, inv_ref, tre_ref, tim_ref, o_ref, *, n2):
    h2 = n2 // 2
    inv = inv_ref[...]
    tre_g = tre_ref[0]
    tim_g = tim_ref[0]
    for s in range(PAIRS):
        wre = _load_pair(b_ref, 0, s, n2)
        wim = _load_pair(b_ref, 1, s, n2)
        res = []
        for half in range(2):
            kl = 2 * s + half
            b = jnp.concatenate([_unpack_pair(wre, half), _unpack_pair(wim, half)], axis=0)
            x = _dot(m2f_ref[...], b)
            kk = k_ref[kl].astype(F32) * inv
            xre, xim = x[:h2], x[h2:]
            kre, kim = kk[:h2], kk[h2:]
            y = jnp.concatenate([xre * kre - xim * kim, xre * kim + xim * kre], axis=0).astype(BF16)
            c = _dot(m2i_ref[...], y)
            cre, cim = c[:n2], c[n2:]
            tre = tre_g[:, kl:kl + 1]
            tim = tim_g[:, kl:kl + 1]
            res.append((cre * tre + cim * tim, cim * tre - cre * tim))
        _store_pair(o_ref, 0, s, n2, _pack_pair(res[0][0], res[1][0]))
        _store_pair(o_ref, 1, s, n2, _pack_pair(res[0][1], res[1][1]))


def _fft_mid(b5, kspec, inv_norm, order, cst, n1, n2):
    d = D_MODEL
    nc = d // MID_TC
    blk = pl.BlockSpec((2, 1, MID_TC // LANES, n2 * PAIRS, LANES), lambda g, c: (0, g, c, 0, 0))
    tw = pl.BlockSpec((1, n2, K1_GRP), lambda g, c: (g, 0, 0))
    return pl.pallas_call(
        functools.partial(_fft_mid_kernel, n2=n2),
        grid=(n1 // K1_GRP, nc),
        in_specs=[_const_spec((n2, 2 * n2)), _const_spec((2 * n2, n2)), blk,
                  pl.BlockSpec((K1_GRP, n2, MID_TC), lambda g, c: (g, 0, order * nc + c)),
                  pl.BlockSpec((1, MID_TC), lambda g, c: (0, order * nc + c)),
                  tw, tw],
        out_specs=blk,
        out_shape=jax.ShapeDtypeStruct((2, n1 // K1_GRP, d // LANES, n2 * PAIRS, LANES), jnp.uint32),
        compiler_params=_cp("parallel", "parallel"),
        name="fft_mid",
    )(cst["m2f"], cst["m2i"], b5, kspec, inv_norm, cst["tre_grp"], cst["tim_grp"])


def _fft_last_kernel(gi_ref, c_ref, gate_ref, z_ref, skip_ref, o_ref):
    h = c_ref.shape[1] * PAIRS

    def words(part):
        return jnp.concatenate([c_ref[part, :, ct].reshape(h, LANES) for ct in range(c_ref.shape[2])], axis=1)

    wre = words(0)
    wim = words(1)
    c = jnp.concatenate([_unpack_pair(wre, 0), _unpack_pair(wre, 1), _unpack_pair(wim, 0),
                         _unpack_pair(wim, 1)], axis=0)
    y = _dot(gi_ref[...], c)
    z = z_ref[...].astype(F32)
    o_ref[...] = (gate_ref[...].astype(F32) * (y + skip_ref[...] * z)).astype(BF16)


def _fft_last(c5, gate2d, z2d, skip_row, gi, n1, n2):
    d = D_MODEL
    r = n1 // 2
    col = pl.BlockSpec((r, d), lambda j: (0, j))
    return pl.pallas_call(
        _fft_last_kernel,
        grid=(n2,),
        in_specs=[_const_spec((r, 2 * n1)),
                  pl.BlockSpec((2, n1 // K1_GRP, d // LANES, PAIRS, LANES), lambda j: (0, 0, 0, j, 0)),
                  col, col, _const_spec((1, d))],
        out_specs=col,
        out_shape=jax.ShapeDtypeStruct((r, n2 * d), BF16),
        compiler_params=_cp("parallel"),
        name="fft_last",
    )(gi, c5, gate2d, z2d, skip_row)


def _long_conv_gate(z_in, gate, skip_row, kspec, inv_norm, order, seq):
    d = D_MODEL
    n1, n2 = _fft_factors(seq)
    cst = _fft_consts(seq)
    z2d = z_in.reshape(n1 // 2, n2 * d)
    b = _fft_stage1(z2d, cst["f1_half"], cst["tre"], cst["tim"], n1, n2, d)
    c = _fft_mid(b, kspec, inv_norm, order, cst, n1, n2)
    out = _fft_last(c, gate.reshape(n1 // 2, n2 * d), z2d, skip_row, cst["gi"], n1, n2)
    return out.reshape(seq, d)


def _filter_spectrum(seq, fparams):
    d = D_MODEL
    od = HY_ORDER * d
    n1, n2 = _fft_factors(seq)
    cst = _fft_consts(seq)
    k_t, asum = _hyena_filter_time(seq, *fparams)
    b = _fft_stage1(k_t.reshape(n1, n2 * od), cst["f1_full"], cst["tre"], cst["tim"], n1, n2, od)
    kspec = _fft_stage2(b, cst["m2f"], n1, n2, od)
    return kspec, asum


def _proj_res_kernel(a_ref, w_ref, b_ref, g_ref, x_ref, o_ref):
    y = _dot(a_ref[...], w_ref[...]) + b_ref[...]
    o_ref[...] = x_ref[...] + g_ref[...] * y


def _proj_res(a, w_bf, b_row, gate_row, xres):
    n, d = xres.shape
    tm = min(512, n)
    row = pl.BlockSpec((tm, d), lambda i: (i, 0))
    return pl.pallas_call(
        _proj_res_kernel,
        grid=(n // tm,),
        in_specs=[row, _const_spec((d, d)), _const_spec((1, d)), _const_spec((1, d)), row],
        out_specs=row,
        out_shape=jax.ShapeDtypeStruct((n, d), F32),
        compiler_params=_cp("parallel"),
        name="proj_residual",
    )(a, w_bf, b_row, gate_row, xres)


def _ffn_in_kernel(x_ref, mod_ref, wt_ref, h_ref, aff_ref):
    h = _norm_mod(x_ref[...], mod_ref[...])
    hi = h.astype(BF16)
    lo = (h - hi.astype(F32)).astype(BF16)
    wt = wt_ref[...]
    whi = wt.astype(BF16)
    wlo = (wt - whi.astype(F32)).astype(BF16)
    logits = _dot_nt(whi, hi) + (_dot_nt(whi, lo) + _dot_nt(wlo, hi))
    m = jnp.max(logits, axis=0, keepdims=True)
    p = jnp.exp(logits - m)
    aff_ref[...] = p / jnp.sum(p, axis=0, keepdims=True)
    h_ref[...] = hi


def _ffn_in(x, mod, w_router):
    n, d = x.shape
    e = N_EXPERTS
    tm = min(512, n)
    return pl.pallas_call(
        _ffn_in_kernel,
        grid=(n // tm,),
        in_specs=[pl.BlockSpec((tm, d), lambda i: (i, 0)), _const_spec((8, d)), _const_spec((e, d))],
        out_specs=[pl.BlockSpec((tm, d), lambda i: (i, 0)), pl.BlockSpec((e, tm), lambda i: (0, i))],
        out_shape=[jax.ShapeDtypeStruct((n, d), BF16), jax.ShapeDtypeStruct((e, n), F32)],
        compiler_params=_cp("parallel"),
        name="moe_router",
    )(x, mod, w_router.T)


def _select_kernel(a_ref, pos_ref, s0_ref, sel_ref, *, cap, nblk):
    e = N_EXPERTS
    bits = pltpu.bitcast(a_ref[...], I32)

    def bisect(i, thr):
        cand = thr | jnp.left_shift(jnp.int32(1), 30 - i)
        cnt = jnp.sum(jnp.where(bits >= cand, 1.0, 0.0), axis=1, keepdims=True)
        return jnp.where(cnt >= cap, cand, thr)

    thr = lax.fori_loop(0, 31, bisect, jnp.zeros((e, 1), I32))
    n_gt = jnp.sum(jnp.where(bits > thr, 1.0, 0.0), axis=1, keepdims=True)
    need = cap - n_gt
    r = lax.broadcasted_iota(I32, (TOK_BLK, TOK_BLK), 0)
    c = lax.broadcasted_iota(I32, (TOK_BLK, TOK_BLK), 1)
    upper = jnp.where(r < c, 1.0, 0.0).astype(BF16)

    def pass1(j, carry):
        sl = pl.ds(pl.multiple_of(j * TOK_BLK, TOK_BLK), TOK_BLK)
        bj = pltpu.bitcast(a_ref[:, sl], I32)
        eq = jnp.where(bj == thr, 1.0, 0.0)
        rank = _dot(eq.astype(BF16), upper) + carry
        keep = jnp.logical_or(bj > thr, jnp.logical_and(bj == thr, rank < need))
        sel_ref[:, sl] = jnp.where(keep, 1.0, 0.0)
        return carry + jnp.sum(eq, axis=1, keepdims=True)

    lax.fori_loop(0, nblk, pass1, jnp.zeros((e, 1), F32))

    def pass2(j, carry):
        sl = pl.ds(pl.multiple_of(j * TOK_BLK, TOK_BLK), TOK_BLK)
        s = sel_ref[:, sl]
        slot = _dot(s.astype(BF16), upper) + carry
        pos_ref[:, sl] = jnp.where(s > 0.5, slot, -1.0).astype(I32)
        s0_ref[j] = jnp.broadcast_to(carry, (e, 128)).astype(I32)
        return carry + jnp.sum(s, axis=1, keepdims=True)

    total = lax.fori_loop(0, nblk, pass2, jnp.zeros((e, 1), F32))
    s0_ref[nblk] = jnp.broadcast_to(total, (e, 128)).astype(I32)


def _select(aff_t, cap):
    e, n = aff_t.shape
    nblk = n // TOK_BLK
    return pl.pallas_call(
        functools.partial(_select_kernel, cap=cap, nblk=nblk),
        out_shape=[jax.ShapeDtypeStruct((e, n), I32), jax.ShapeDtypeStruct((nblk + 1, e, 128), I32)],
        scratch_shapes=[pltpu.VMEM((e, n), F32)],
        compiler_params=pltpu.CompilerParams(vmem_limit_bytes=VMEM_LIMIT),
        name="moe_select",
    )(aff_t)


def _block_windows(s0_ref, blk, e_idx):
    s0 = s0_ref[blk * N_EXPERTS + e_idx]
    s1 = s0_ref[(blk + 1) * N_EXPERTS + e_idx]
    start = lax.shift_left(lax.shift_right_logical(s0, 4), 4)
    nwin = jnp.where(s1 > s0, lax.shift_right_logical(s1 - start + (WIN - 1), WIN_SHIFT), 0)
    return start, nwin


def _gather_kernel(s0_ref, pos_ref, h_ref, xe_ref, *, sub):
    e_idx = pl.program_id(0)
    c = pl.program_id(1)

    @pl.when(c == 0)
    def _():
        xe_ref[...] = jnp.zeros_like(xe_ref)

    rows = lax.broadcasted_iota(I32, (WIN, TOK_BLK), 0)

    def body(i, carry):
        start, nwin = _block_windows(s0_ref, c * sub + i, e_idx)
        tok = pl.ds(pl.multiple_of(i * TOK_BLK, TOK_BLK), TOK_BLK)
        prow = pos_ref[0, :, tok]

        def window(w, carry2):
            base = pl.multiple_of(start + w * WIN, ROW_ALIGN)
            onehot = jnp.where(rows == prow - base, 1.0, 0.0).astype(BF16)
            got = _dot(onehot, h_ref[tok, :]).astype(BF16)
            win = pl.ds(base, WIN)
            xe_ref[0, win, :] = xe_ref[0, win, :] + got
            return carry2

        return lax.fori_loop(0, nwin, window, carry)

    lax.fori_loop(0, sub, body, 0)


def _gather(s0_flat, pos, h, cap_pad):
    e, n = pos.shape
    d = h.shape[1]
    chunk = min(2048, n)
    sub = chunk // TOK_BLK
    return pl.pallas_call(
        functools.partial(_gather_kernel, sub=sub),
        grid_spec=pltpu.PrefetchScalarGridSpec(
            num_scalar_prefetch=1,
            grid=(e, n // chunk),
            in_specs=[pl.BlockSpec((1, 1, chunk), lambda ei, c, s0: (ei, 0, c)),
                      pl.BlockSpec((chunk, d), lambda ei, c, s0: (c, 0))],
            out_specs=pl.BlockSpec((1, cap_pad, d), lambda ei, c, s0: (ei, 0, 0)),
        ),
        out_shape=jax.ShapeDtypeStruct((e, cap_pad, d), BF16),
        compiler_params=_cp("parallel", "arbitrary"),
        name="moe_gather",
    )(s0_flat, pos.reshape(e, 1, n), h)


def _expert_kernel(x_ref, wg_ref, wu_ref, wd_ref, y_ref, wg_scr, wu_scr, wd_scr, *, n_real):
    j = pl.program_id(1)

    @pl.when(j == 0)
    def _():
        wg_scr[...] = wg_ref[0].astype(BF16)
        wu_scr[...] = wu_ref[0].astype(BF16)
        wd_scr[...] = wd_ref[0].astype(BF16)

    @pl.when(j < n_real)
    def _():
        x = x_ref[0]
        g = _dot(x, wg_scr[...])
        u = _dot(x, wu_scr[...])
        a = (g / (1.0 + jnp.exp(-g))) * u
        y_ref[0] = _dot(a.astype(BF16), wd_scr[...]).astype(BF16)

    @pl.when(j >= n_real)
    def _():
        y_ref[0] = jnp.zeros_like(y_ref[0])


def _experts(xe, wg, wu, wd, layer, cap):
    e, cap_pad, d = xe.shape
    f = wg.shape[3]
    tm = min(256, cap)
    tile = pl.BlockSpec((1, tm, d), lambda ei, j: (ei, j, 0))
    return pl.pallas_call(
        functools.partial(_expert_kernel, n_real=cap // tm),
        grid=(e, cap_pad // tm),
        in_specs=[tile,
                  pl.BlockSpec((None, 1, d, f), lambda ei, j: (layer, ei, 0, 0)),
                  pl.BlockSpec((None, 1, d, f), lambda ei, j: (layer, ei, 0, 0)),
                  pl.BlockSpec((None, 1, f, d), lambda ei, j: (layer, ei, 0, 0))],
        out_specs=tile,
        out_shape=jax.ShapeDtypeStruct((e, cap_pad, d), BF16),
        scratch_shapes=[pltpu.VMEM((d, f), BF16), pltpu.VMEM((d, f), BF16), pltpu.VMEM((f, d), BF16)],
        compiler_params=_cp("parallel", "arbitrary"),
        name="moe_experts",
    )(xe, wg, wu, wd)


def _combine_kernel(s0_ref, post_ref, gt_ref, y_ref, x_ref, gate_ref, o_ref, *, sub):
    c = pl.program_id(0)
    e_idx = pl.program_id(1)

    @pl.when(e_idx == 0)
    def _():
        o_ref[...] = jnp.zeros_like(o_ref)

    lanes = lax.broadcasted_iota(I32, (TOK_BLK, WIN), 1)
    elane = lax.broadcasted_iota(I32, (TOK_BLK, N_EXPERTS), 1) == e_idx

    def body(i, carry):
        start, nwin = _block_windows(s0_ref, c * sub + i, e_idx)
        tok = pl.ds(pl.multiple_of(i * TOK_BLK, TOK_BLK), TOK_BLK)

        def window(w, carry2):
            base = pl.multiple_of(start + w * WIN, ROW_ALIGN)
            pcol = jnp.sum(jnp.where(elane, post_ref[tok, :].astype(F32), 0.0), axis=1, keepdims=True)
            gcol = jnp.sum(jnp.where(elane, gt_ref[tok, :], 0.0), axis=1, keepdims=True)
            rel = pcol.astype(I32) - base
            onehot = jnp.where(lanes == rel, 1.0, 0.0).astype(BF16)
            strip = y_ref[0, pl.ds(base, WIN), :]
            o_ref[tok, :] = o_ref[tok, :] + gcol * _dot(onehot, strip)
            return carry2

        return lax.fori_loop(0, nwin, window, carry)

    lax.fori_loop(0, sub, body, 0)

    @pl.when(e_idx == N_EXPERTS - 1)
    def _():
        o_ref[...] = x_ref[...] + gate_ref[...] * o_ref[...]


def _combine(s0_flat, pos_t, g_t, y, xres, gate_row):
    n, d = xres.shape
    e, cap_pad, _ = y.shape
    chunk = min(2048, n)
    sub = chunk // TOK_BLK
    return pl.pallas_call(
        functools.partial(_combine_kernel, sub=sub),
        grid_spec=pltpu.PrefetchScalarGridSpec(
            num_scalar_prefetch=1,
            grid=(n // chunk, e),
            in_specs=[pl.BlockSpec((chunk, e), lambda c, ei, s0: (c, 0)),
                      pl.BlockSpec((chunk, e), lambda c, ei, s0: (c, 0)),
                      pl.BlockSpec((1, cap_pad, d), lambda c, ei, s0: (ei, 0, 0)),
                      pl.BlockSpec((chunk, d), lambda c, ei, s0: (c, 0)),
                      pl.BlockSpec((1, d), lambda c, ei, s0: (0, 0))],
            out_specs=pl.BlockSpec((chunk, d), lambda c, ei, s0: (c, 0)),
        ),
        out_shape=jax.ShapeDtypeStruct((n, d), F32),
        compiler_params=_cp("parallel", "arbitrary"),
        name="moe_combine",
    )(s0_flat, pos_t, g_t, y, xres, gate_row)


def _moe_block(x, mod, gate_row, w_router, wg, wu, wd, layer):
    n = x.shape[0]
    cap = EC_CAPACITY * n // N_EXPERTS
    cap_pad = cap + TOK_BLK
    h, aff_t = _ffn_in(x, mod, w_router)
    pos, s0 = _select(aff_t, cap)
    s0_flat = s0[:, :, 0].reshape(-1)
    xe = _gather(s0_flat, pos, h, cap_pad)
    y = _experts(xe, wg, wu, wd, layer, cap)
    return _combine(s0_flat, pos.T, aff_t.T, y, x, gate_row)


QK_SCALE = (DA_HEAD_DIM ** -0.5) * math.log2(math.e)


def _group_rms(u, gsum_ref, gain, eps):
    sq = u * u
    hi = sq.astype(BF16)
    lo = (sq - hi.astype(F32)).astype(BF16)
    ss = _dot(hi, gsum_ref[...]) + _dot(lo, gsum_ref[...])
    return u * lax.rsqrt(ss * (1.0 / DA_HEAD_DIM) + eps) * gain


def _rope(u, cos, sin_signed):
    d = u.shape[1]
    half = ROPE_AXIS_DIM // 2
    lane = lax.broadcasted_iota(I32, u.shape, 1)
    first = (lane & half) == 0
    swapped = jnp.where(first, pltpu.roll(u, d - half, 1), pltpu.roll(u, half, 1))
    return u * cos + swapped * sin_signed


def _qkv_kernel(*refs, rope):
    if rope:
        x_ref, mod_ref, w_ref, gsum_ref, qn_ref, kn_ref, cos_ref, sin_ref, q_ref, k_ref, v_ref = refs
    else:
        x_ref, mod_ref, w_ref, gsum_ref, qn_ref, kn_ref, q_ref, k_ref, v_ref = refs
    d = D_MODEL
    h = _norm_mod(x_ref[...], mod_ref[...]).astype(BF16)
    if rope:
        reps = d // cos_ref.shape[1]
        cos = _lane_tile(cos_ref[...], reps)
        sin = _lane_tile(sin_ref[...], reps)
    for part, (o_ref, gain_ref) in enumerate(((q_ref, qn_ref), (k_ref, kn_ref))):
        u = _dot(h, w_ref[:, part * d:(part + 1) * d])
        u = _group_rms(u, gsum_ref, gain_ref[...], NORM_EPS)
        if rope:
            u = _rope(u, cos, sin)
        if part == 0:
            u = u * QK_SCALE
        o_ref[...] = u.astype(BF16)
    v_ref[...] = _dot(h, w_ref[:, 2 * d:]).astype(BF16)


def _qkv(x, mod, w_bf, gsum, qn_row, kn_row, cos=None, sin=None):
    n, d = x.shape
    tm = min(512, n)
    rope = cos is not None
    row = pl.BlockSpec((tm, d), lambda i: (i, 0))
    in_specs = [row, _const_spec((8, d)), _const_spec((d, 3 * d)), _const_spec((d, d)),
                _const_spec((1, d)), _const_spec((1, d))]
    args = [x, mod, w_bf, gsum, qn_row, kn_row]
    if rope:
        tw = cos.shape[1]
        in_specs += [pl.BlockSpec((tm, tw), lambda i: (i, 0)), pl.BlockSpec((tm, tw), lambda i: (i, 0))]
        args += [cos, sin]
    out = jax.ShapeDtypeStruct((n, d), BF16)
    return pl.pallas_call(
        functools.partial(_qkv_kernel, rope=rope),
        grid=(n // tm,),
        in_specs=in_specs,
        out_specs=[row, row, row],
        out_shape=[out, out, out],
        compiler_params=_cp("parallel"),
        name="attn_qkv",
    )(*args)


NEG_BIG = -1e30


def _attn_kernel(q_ref, k_ref, v_ref, lam_ref, sub_ref, o_ref, m_scr, acc_scr, sa_scr, sb_scr, ma_scr, mb_scr,
                 pa_scr, pb_scr, *,
                 lam_init):
    j = pl.program_id(1)
    nj = pl.num_programs(1)
    hd, vd = DA_HEAD_DIM, DA_V_DIM
    tk = k_ref.shape[0]

    @pl.when(j == 0)
    def _():
        m_scr[...] = jnp.full_like(m_scr, NEG_BIG)
        acc_scr[...] = jnp.zeros_like(acc_scr)

    lane = lax.broadcasted_iota(I32, (q_ref.shape[0], vd), 1)
    ones_col = jnp.where(lax.broadcasted_iota(I32, (tk, vd), 1) == 0, 1.0, 0.0).astype(BF16)

    def head_cols(h):
        return pl.ds(pl.multiple_of(h * vd, vd), vd)

    def scores(h, comp, s_ref, mx_ref):
        qb = q_ref[:, head_cols(h)]
        in_comp = (lane >= comp * hd) & (lane < (comp + 1) * hd)
        qm = jnp.where(in_comp, qb, jnp.zeros_like(qb))
        s = _dot_nt(qm, k_ref[:, head_cols(h)])
        s_ref[...] = s
        mx_ref[...] = jnp.broadcast_to(jnp.max(s, axis=1, keepdims=True), mx_ref.shape)

    def v_aug(h):
        return jnp.concatenate([v_ref[:, head_cols(h)], ones_col], axis=1)

    def softmax_step(h, comp, s_ref, mx_ref, p_ref):
        idx = 2 * h + comp
        m_prev = m_scr[idx]
        m_new = jnp.maximum(m_prev, mx_ref[...])
        alpha = _lane_tile(jnp.exp2(m_prev - m_new), 2)
        p_ref[...] = jnp.exp2(s_ref[...] - m_new[:, 0:1]).astype(BF16)
        m_scr[idx] = m_new
        acc_scr[idx] = alpha * acc_scr[idx]

    def pv_step(h, comp, p_ref):
        idx = 2 * h + comp
        acc_scr[idx] = acc_scr[idx] + _dot(p_ref[...], v_aug(h))

    pb_scr[...] = jnp.zeros_like(pb_scr)
    scores(0, 0, sa_scr, ma_scr)

    def head(h, carry):
        scores(h, 1, sb_scr, mb_scr)
        pv_step(jnp.maximum(h - 1, 0), 1, pb_scr)
        softmax_step(h, 0, sa_scr, ma_scr, pa_scr)
        scores(jnp.minimum(h + 1, DA_HEADS - 1), 0, sa_scr, ma_scr)
        pv_step(h, 0, pa_scr)
        softmax_step(h, 1, sb_scr, mb_scr, pb_scr)
        return carry

    def head_pair(g, carry):
        return head(2 * g + 1, head(2 * g, carry))

    lax.fori_loop(0, DA_HEADS // 2, head_pair, 0)
    pv_step(DA_HEADS - 1, 1, pb_scr)

    @pl.when(j == nj - 1)
    def _():
        lp = lam_ref[...]
        lam = (jnp.exp(jnp.sum(lp[0:1] * lp[1:2], axis=1, keepdims=True))
               - jnp.exp(jnp.sum(lp[2:3] * lp[3:4], axis=1, keepdims=True)) + lam_init)
        for h in range(DA_HEADS):
            a0 = acc_scr[2 * h]
            a1 = acc_scr[2 * h + 1]
            o = a0[:, :vd] / a0[:, vd:vd + 1] - lam * (a1[:, :vd] / a1[:, vd:vd + 1])
            ms = jnp.mean(o * o, axis=1, keepdims=True)
            o = o * lax.rsqrt(ms + SUBLN_EPS) * (sub_ref[...] * (1.0 - lam_init))
            o_ref[:, h * vd:(h + 1) * vd] = o.astype(BF16)


def _attention(q, k_all, v_all, lam_rows, subln_row, lam_init):
    n, d = q.shape
    nk = k_all.shape[0]
    tq = min(512, n)
    tk = 1280 if nk % 1280 == 0 else 256
    assert nk % tk == 0
    nc = 2 * DA_HEADS
    return pl.pallas_call(
        functools.partial(_attn_kernel, lam_init=lam_init),
        grid=(n // tq, nk // tk),
        in_specs=[pl.BlockSpec((tq, d), lambda i, j: (i, 0)),
                  pl.BlockSpec((tk, d), lambda i, j: (j, 0)),
                  pl.BlockSpec((tk, d), lambda i, j: (j, 0)),
                  _const_spec((8, DA_HEAD_DIM)), _const_spec((1, DA_V_DIM))],
        out_specs=pl.BlockSpec((tq, d), lambda i, j: (i, 0)),
        out_shape=jax.ShapeDtypeStruct((n, d), BF16),
        scratch_shapes=[pltpu.VMEM((nc, tq, DA_V_DIM), F32), pltpu.VMEM((nc, tq, 2 * DA_V_DIM), F32),
                        pltpu.VMEM((tq, tk), F32), pltpu.VMEM((tq, tk), F32),
                        pltpu.VMEM((tq, DA_V_DIM), F32), pltpu.VMEM((tq, DA_V_DIM), F32),
                        pltpu.VMEM((tq, tk), BF16), pltpu.VMEM((tq, tk), BF16)],
        compiler_params=_cp("parallel", "arbitrary"),
        name="diff_attention",
    )(q, k_all, v_all, lam_rows, subln_row)


def _rope_tables(n):
    t = jnp.arange(n, dtype=I32)
    row = (t // GRID_W).astype(F32)
    col = (t % GRID_W).astype(F32)
    inv = ROPE_THETA ** (-jnp.arange(0, ROPE_AXIS_DIM, 2, dtype=F32) / ROPE_AXIS_DIM)
    ar = row[:, None] * inv[None]
    ac = col[:, None] * inv[None]
    cos = jnp.concatenate([jnp.cos(ar), jnp.cos(ar), jnp.cos(ac), jnp.cos(ac)], axis=1)
    sin = jnp.concatenate([-jnp.sin(ar), jnp.sin(ar), -jnp.sin(ac), jnp.sin(ac)], axis=1)
    return jnp.concatenate([cos, cos], axis=1), jnp.concatenate([sin, sin], axis=1)


def _hyena_layer(x, mod, gate_row, kspec, inv_norm, w_in_bf, b_in, conv_w, conv_b, skip, w_out_bf, b_out):
    seq = x.shape[0]
    v, x1, x2 = _hy_in(x, mod, w_in_bf, b_in, conv_w, conv_b)
    z = _long_conv_gate(v, x1, skip[0:1], kspec, inv_norm, 0, seq)
    z = _long_conv_gate(z, x2, skip[1:2], kspec, inv_norm, 1, seq)
    return _proj_res(z, w_out_bf, b_out.reshape(1, -1), gate_row, x)


def kernel(x, c, ctx, c_ctx, ada_w, ada_b, norm_mix, norm_ffn, hy_w_in, hy_b_in, hy_conv_w, hy_conv_b, hy_f_w1, hy_f_b1, hy_f_w2, hy_f_b2, hy_f_w3, hy_f_b3, hy_f_w4, hy_f_freq, hy_skip, hy_w_out, hy_b_out, da_w_qkv, da_q_norm, da_k_norm, da_lam_q1, da_lam_k1, da_lam_q2, da_lam_k2, da_subln, da_w_out, moe_router, moe_w_gate, moe_w_up, moe_w_down):
    d = D_MODEL
    depth = ada_w.shape[0]
    assert x.shape[0] == 1 and x.shape[2] == d
    xs = x[0]
    cs = ctx[0]
    cond8 = jnp.concatenate([c[0:1], c_ctx[None], jnp.zeros((6, d), F32)], axis=0)
    mods = _adaln(cond8, ada_w, ada_b)

    def mod_slice(i, row, k):
        return mods[i, row, k * d:(k + 1) * d][None]

    for i in range(depth):
        last = i == depth - 1
        j = i // 2
        mix_x = _mod_rows(norm_mix[i], mods[i], 0, 0)
        mix_c = _mod_rows(norm_mix[i], mods[i], 1, 0)
        if i % 2 == 0:
            fparams = (hy_f_w1[j], hy_f_b1[j], hy_f_w2[j], hy_f_b2[j], hy_f_w3[j], hy_f_b3[j], hy_f_w4[j],
                       hy_f_freq[j])
            shared = (hy_w_in[j].astype(BF16), hy_b_in[j], hy_conv_w[j], hy_conv_b[j], hy_skip[j],
                      hy_w_out[j].astype(BF16), hy_b_out[j])
            kspec, asum = _filter_spectrum(xs.shape[0], fparams)
            inv_norm = 1.0 / (asum[0:1] + HY_FILTER_EPS)
            new_x = _hyena_layer(xs, mix_x, mod_slice(i, 0, 2), kspec, inv_norm, *shared)
            if not last:
                kspec_c, asum_c = _filter_spectrum(cs.shape[0], fparams)
                inv_c = 1.0 / (asum_c[0:1] + HY_FILTER_EPS)
                cs = _hyena_layer(cs, mix_c, mod_slice(i, 1, 2), kspec_c, inv_c, *shared)
            xs = new_x
        else:
            lam_init = 0.8 - 0.6 * math.exp(-0.3 * i)
            w_qkv = da_w_qkv[j].astype(BF16)
            gidx = np.arange(d) // DA_HEAD_DIM
            gsum = jnp.asarray((gidx[:, None] == gidx[None]).astype(np.float32)).astype(BF16)
            qn = jnp.tile(da_q_norm[j], 2 * DA_HEADS)[None]
            kn = jnp.tile(da_k_norm[j], 2 * DA_HEADS)[None]
            cos, sin = _rope_tables(xs.shape[0])
            qx, kx, vx = _qkv(xs, mix_x, w_qkv, gsum, qn, kn, cos, sin)
            qc, kc, vc = _qkv(cs, mix_c, w_qkv, gsum, qn, kn)
            k_all = jnp.concatenate([kc, kx], axis=0)
            v_all = jnp.concatenate([vc, vx], axis=0)
            lam_rows = jnp.concatenate([da_lam_q1[j][None], da_lam_k1[j][None], da_lam_q2[j][None],
                                        da_lam_k2[j][None], jnp.zeros((4, DA_HEAD_DIM), F32)], axis=0)
            w_out = da_w_out[j].astype(BF16)
            zero_b = jnp.zeros((1, d), F32)
            ox = _attention(qx, k_all, v_all, lam_rows, da_subln[j][None], lam_init)
            new_x = _proj_res(ox, w_out, zero_b, mod_slice(i, 0, 2), xs)
            if not last:
                oc = _attention(qc, kc, vc, lam_rows, da_subln[j][None], lam_init)
                cs = _proj_res(oc, w_out, zero_b, mod_slice(i, 1, 2), cs)
            xs = new_x
        experts = (moe_w_gate, moe_w_up, moe_w_down, i)
        if not last:
            cs = _moe_block(cs, _mod_rows(norm_ffn[i], mods[i], 1, 3), mod_slice(i, 1, 5), moe_router[i], *experts)
        xs = _moe_block(xs, _mod_rows(norm_ffn[i], mods[i], 0, 3), mod_slice(i, 0, 5), moe_router[i], *experts)
    return xs[None]
```

```python
import functools
import math

import jax
import jax.numpy as jnp
import numpy as np
from jax import lax
from jax.experimental import pallas as pl
from jax.experimental.pallas import tpu as pltpu

F32 = jnp.float32
BF16 = jnp.bfloat16
I32 = jnp.int32
HIGHEST = lax.Precision.HIGHEST

D_MODEL = 1024
N_MOD = 6
NORM_EPS = 1e-6
GRID_W = 64
HY_ORDER = 2
HY_SHORT = 3
HY_EMB_BANDS = 16
HY_EMB_DIM = 1 + 2 * HY_EMB_BANDS
HY_FILTER_HIDDEN = 64
HY_DECAY_FAST = 0.3
HY_DECAY_SLOW = 1.5
HY_DECAY_TARGET = 1e-2
HY_FILTER_EPS = 1e-6
Z_SIGN_COL = 33
DA_HEADS = 8
DA_HEAD_DIM = 64
DA_V_DIM = 128
ROPE_AXIS_DIM = 32
ROPE_THETA = 10000.0
SUBLN_EPS = 1e-5
N_EXPERTS = 16
EC_CAPACITY = 2
D_EXPERT = 1024
TOK_BLK = 256
ROW_ALIGN = 16
WIN_SHIFT = 7
WIN = 1 << WIN_SHIFT

VMEM_LIMIT = 56 * 1024 * 1024


def _cp(*sem):
    return pltpu.CompilerParams(dimension_semantics=sem, vmem_limit_bytes=VMEM_LIMIT)


def _const_spec(shape):
    nd = len(shape)
    return pl.BlockSpec(shape, lambda *_: (0,) * nd)


def _dot(a, b):
    return jnp.dot(a, b, preferred_element_type=F32)


def _dot_nt(a, b):
    return lax.dot_general(a, b, (((1,), (1,)), ((), ())), preferred_element_type=F32)


def _norm_mod(x, mod, eps=NORM_EPS):
    ms = jnp.mean(x * x, axis=-1, keepdims=True)
    return x * lax.rsqrt(ms + eps) * (mod[0:1] * (1.0 + mod[1:2])) + mod[2:3]


def _lane_tile(x, reps):
    return jnp.concatenate([x] * reps, axis=1) if reps > 1 else x


def _adaln_kernel(c_ref, w_ref, b_ref, o_ref):
    c = c_ref[...]
    s = c / (1.0 + jnp.exp(-c))
    o_ref[0] = jnp.dot(s, w_ref[0], precision=HIGHEST, preferred_element_type=F32) + b_ref[0]


def _adaln(cond8, ada_w, ada_b):
    depth, d, nout = ada_w.shape
    tn = 1536
    return pl.pallas_call(
        _adaln_kernel,
        grid=(depth, nout // tn),
        in_specs=[_const_spec((8, d)),
                  pl.BlockSpec((1, d, tn), lambda l, j: (l, 0, j)),
                  pl.BlockSpec((1, 1, tn), lambda l, j: (l, 0, j))],
        out_specs=pl.BlockSpec((1, 8, tn), lambda l, j: (l, 0, j)),
        out_shape=jax.ShapeDtypeStruct((depth, 8, nout), F32),
        compiler_params=_cp("parallel", "parallel"),
        name="adaln",
    )(cond8, ada_w, ada_b.reshape(depth, 1, nout))


def _mod_rows(norm_g, mods, row, k0):
    d = D_MODEL
    shift = mods[row, k0 * d:(k0 + 1) * d]
    scale = mods[row, (k0 + 1) * d:(k0 + 2) * d]
    z = jnp.zeros((5, d), F32)
    return jnp.concatenate([norm_g[None], scale[None], shift[None], z], axis=0)


HALO = 16


def _hy_in_kernel(x_ref, xp_ref, xn_ref, mod_ref, w_ref, b_ref, cw_ref, v_ref, x1_ref, x2_ref, *, tm, n_rows):
    i = pl.program_id(0)
    mod = mod_ref[...]
    hm = _norm_mod(x_ref[...], mod).astype(BF16)
    hp = _norm_mod(xp_ref[...], mod).astype(BF16)
    hn = _norm_mod(xn_ref[...], mod).astype(BF16)
    hcat = jnp.concatenate([hp, hm, hn], axis=0)
    row = lax.broadcasted_iota(I32, (tm + 2 * HALO, 1), 0) + (i * tm - HALO)
    valid = jnp.logical_and(row >= 0, row < n_rows)
    d = D_MODEL
    for c, o_ref in enumerate((v_ref, x1_ref, x2_ref)):
        u = _dot(hcat, w_ref[:, c * d:(c + 1) * d]) + b_ref[:, c * d:(c + 1) * d]
        u = jnp.where(valid, u, 0.0)
        cw = cw_ref[:, c * d:(c + 1) * d]
        y = (cw[3:4] + cw[0:1] * u[HALO - 1:HALO - 1 + tm] + cw[1:2] * u[HALO:HALO + tm]
             + cw[2:3] * u[HALO + 1:HALO + 1 + tm])
        o_ref[...] = y.astype(BF16)


def _hy_in(x, mod, w_bf, b_in, conv_w, conv_b):
    n, d = x.shape
    tm = min(512, n)
    nh = n // HALO
    cw = jnp.concatenate([conv_w, conv_b[None], jnp.zeros((4, 3 * d), F32)], axis=0)
    out = jax.ShapeDtypeStruct((n, d), BF16)
    row_spec = pl.BlockSpec((tm, d), lambda i: (i, 0))
    return pl.pallas_call(
        functools.partial(_hy_in_kernel, tm=tm, n_rows=n),
        grid=(n // tm,),
        in_specs=[row_spec,
                  pl.BlockSpec((HALO, d), lambda i: (jnp.maximum(i * (tm // HALO) - 1, 0), 0)),
                  pl.BlockSpec((HALO, d), lambda i: (jnp.minimum((i + 1) * (tm // HALO), nh - 1), 0)),
                  _const_spec((8, d)), _const_spec((d, 3 * d)), _const_spec((1, 3 * d)),
                  _const_spec((8, 3 * d))],
        out_specs=[row_spec, row_spec, row_spec],
        out_shape=[out, out, out],
        compiler_params=_cp("parallel"),
        name="hyena_in",
    )(x, x, x, mod, w_bf, b_in.reshape(1, 3 * d), cw)


Z_HALF = 64


def _filter_s1_kernel(z_ref, w1_ref, b1_ref, w2_ref, b2_ref, w3_ref, b3_ref, fr_ref, w4f_ref, w4b_ref, dl_ref,
                      f1_ref, tre_ref, tim_ref, o_ref, asum_ref, *, n1):
    j = pl.program_id(0)
    z = z_ref[...]
    fr = fr_ref[...]

    def lin(a, w_ref, b_ref):
        return jnp.dot(a, w_ref[...], precision=HIGHEST, preferred_element_type=F32) + b_ref[...]

    hid = jnp.sin(fr * lin(z, w1_ref, b1_ref))
    hid = jnp.sin(fr * lin(hid, w2_ref, b2_ref))
    hid = jnp.sin(fr * lin(hid, w3_ref, b3_ref)).astype(BF16)
    dl = dl_ref[...]

    def taps(w4_ref, col):
        t = z[:, col:col + 1]
        sgn = z[:, col + Z_SIGN_COL:col + Z_SIGN_COL + 1]
        return _dot(hid, w4_ref[...]) * jnp.exp(-t * dl) * sgn

    k = jnp.concatenate([taps(w4f_ref, 0), taps(w4b_ref, Z_HALF)], axis=0)

    @pl.when(j == 0)
    def _():
        asum_ref[...] = jnp.zeros_like(asum_ref)

    asum_ref[0:1, :] += jnp.sum(jnp.abs(k), axis=0, keepdims=True)
    _s1_store(_dot(f1_ref[...], k.astype(BF16)), tre_ref, tim_ref, j, o_ref, n1)


def _filter_positions(seq, n1, n2):
    i = np.arange(n1 // 2)[None, :]
    j = np.arange(n2)[:, None]
    bands = np.linspace(1e-4, HY_EMB_BANDS - 1, HY_EMB_BANDS)
    z = np.zeros((n2, n1 // 2, 2 * Z_HALF), np.float64)
    for col, r in ((0, i * n2 + j), (Z_HALF, i * n2 + j + seq)):
        pos = np.minimum(np.where(r < seq, r, 2 * seq - r), seq - 1).astype(np.float64)
        w = 2.0 * np.pi * pos / seq
        z[:, :, col] = pos / (seq - 1)
        z[:, :, col + 1:col + 1 + HY_EMB_BANDS] = np.cos(w[..., None] * bands)
        z[:, :, col + 1 + HY_EMB_BANDS:col + HY_EMB_DIM] = -np.sin(w[..., None] * bands)
        z[:, :, col + Z_SIGN_COL] = np.where(r < seq, 1.0, np.where(r == seq, 0.0, -1.0))
    return jnp.asarray(z.reshape(n2 * (n1 // 2), 2 * Z_HALF).astype(np.float32))


def _filter_stage1(seq, f_w1, f_b1, f_w2, f_b2, f_w3, f_b3, f_w4, f_freq):
    d, hid = D_MODEL, HY_FILTER_HIDDEN
    od = HY_ORDER * d
    n1, n2 = _fft_factors(seq)
    cst = _fft_consts(seq)
    z = _filter_positions(seq, n1, n2)
    zw = 2 * Z_HALF
    zero = jnp.zeros((hid, hid), F32)
    pad = jnp.zeros((Z_HALF - HY_EMB_DIM, hid), F32)
    w1 = jnp.concatenate([jnp.concatenate([f_w1, pad], axis=0), jnp.zeros((Z_HALF, hid), F32)], axis=0)
    w1 = jnp.concatenate([w1, jnp.roll(w1, Z_HALF, axis=0)], axis=1)
    blockdiag = lambda w: jnp.concatenate([jnp.concatenate([w, zero], axis=1),
                                           jnp.concatenate([zero, w], axis=1)], axis=0)
    twice = lambda v: jnp.tile(v, 2).reshape(1, 2 * hid)
    w4d = f_w4.reshape(hid, HY_ORDER, 2, d).transpose(2, 0, 1, 3).reshape(2, hid, od)
    zrows = jnp.zeros((hid, od), F32)
    w4f = jnp.concatenate([w4d[0], zrows], axis=0).astype(BF16)
    w4b = jnp.concatenate([zrows, w4d[1]], axis=0).astype(BF16)
    max_decay = math.log(HY_DECAY_TARGET) / HY_DECAY_FAST
    min_decay = math.log(HY_DECAY_TARGET) / HY_DECAY_SLOW
    deltas = np.abs(np.linspace(min_decay, max_decay, d, dtype=np.float32))
    dl = jnp.asarray(np.tile(deltas, HY_ORDER)[None])
    r = n1 // 2
    return pl.pallas_call(
        functools.partial(_filter_s1_kernel, n1=n1),
        grid=(n2,),
        in_specs=[pl.BlockSpec((r, zw), lambda j: (j, 0)),
                  _const_spec((zw, 2 * hid)), _const_spec((1, 2 * hid)),
                  _const_spec((2 * hid, 2 * hid)), _const_spec((1, 2 * hid)),
                  _const_spec((2 * hid, 2 * hid)), _const_spec((1, 2 * hid)),
                  _const_spec((1, 2 * hid)),
                  _const_spec((2 * hid, od)), _const_spec((2 * hid, od)), _const_spec((1, od)),
                  _const_spec((2 * n1, n1)), _const_spec((n1, n2)), _const_spec((n1, n2))],
        out_specs=[pl.BlockSpec((2, n1 // K1_GRP, od // LANES, PAIRS, LANES), lambda j: (0, 0, 0, j, 0)),
                   _const_spec((8, od))],
        out_shape=[jax.ShapeDtypeStruct((2, n1 // K1_GRP, od // LANES, n2 * PAIRS, LANES), jnp.uint32),
                   jax.ShapeDtypeStruct((8, od), F32)],
        compiler_params=_cp("arbitrary"),
        name="hyena_filter",
    )(z, w1, twice(f_b1), blockdiag(f_w2), twice(f_b2), blockdiag(f_w3), twice(f_b3), twice(f_freq),
      w4f, w4b, dl, cst["f1_full"], cst["tre"], cst["tim"])


def _fft_factors(seq):
    n = 2 * seq
    n1 = 256 if n >= 32768 else 32
    n2 = n // n1
    assert n1 * n2 == n and n2 % 16 == 0 and n1 % 32 == 0
    return n1, n2


@functools.lru_cache(maxsize=None)
def _fft_consts(seq):
    n = 2 * seq
    n1, n2 = _fft_factors(seq)
    k1 = np.arange(n1, dtype=np.float64)[:, None] + 0.5
    th1 = 2.0 * np.pi * k1 * np.arange(n1, dtype=np.float64)[None] / n1
    perm = np.concatenate([np.arange(0, n1, 2), np.arange(1, n1, 2)])
    f1_full = np.concatenate([np.cos(th1)[perm], -np.sin(th1)[perm]], axis=0)
    f1_half = f1_full[:, :n1 // 2]
    tw = 2.0 * np.pi * k1 * np.arange(n2, dtype=np.float64)[None] / n
    tre, tim = np.cos(tw), -np.sin(tw)
    h2 = n2 // 2
    th2 = 2.0 * np.pi * np.arange(h2, dtype=np.float64)[:, None] * np.arange(n2, dtype=np.float64)[None] / n2
    c2, s2 = np.cos(th2), np.sin(th2)
    m2f = np.block([[c2, s2], [-s2, c2]])
    m2i = np.block([[c2.T, -s2.T], [s2.T, c2.T]])
    thb = th1[:, :n1 // 2].T
    gi = (2.0 / n) * np.concatenate([np.cos(thb)[:, perm], -np.sin(thb)[:, perm]], axis=1)
    bf = lambda a: jnp.asarray(a.astype(np.float32)).astype(BF16)
    f32 = lambda a: jnp.asarray(a.astype(np.float32))
    grp = lambda a: a.T.reshape(n2, n1 // K1_GRP, K1_GRP).transpose(1, 0, 2)
    return dict(f1_full=bf(f1_full), f1_half=bf(f1_half), tre=f32(tre[perm]), tim=f32(tim[perm]),
                tre_grp=f32(grp(tre)), tim_grp=f32(grp(tim)), m2f=bf(m2f), m2i=bf(m2i), gi=bf(gi))


def _pick_col(tbl, idx):
    lane = lax.broadcasted_iota(I32, tbl.shape, 1)
    return jnp.sum(jnp.where(lane == idx, tbl, 0.0), axis=1, keepdims=True)


K1_GRP = 16
PAIRS = K1_GRP // 2
LANES = 128
MID_TC = 512


def _unpack_pair(w, half):
    bits = lax.shift_left(w, jnp.uint32(16)) if half == 0 else (w & jnp.uint32(0xFFFF0000))
    return pltpu.bitcast(bits, F32).astype(BF16)


def _pack_pair(even, odd):
    ue = pltpu.bitcast(even.astype(BF16).astype(F32), jnp.uint32)
    uo = pltpu.bitcast(odd.astype(BF16).astype(F32), jnp.uint32)
    return lax.shift_right_logical(ue, jnp.uint32(16)) | uo


def _s1_store(a, tre_ref, tim_ref, j, o_ref, n1):
    are, aim = a[:n1], a[n1:]
    tre = _pick_col(tre_ref[...], j)
    tim = _pick_col(tim_ref[...], j)
    re = are * tre - aim * tim
    im = are * tim + aim * tre
    h = n1 // 2
    shape = (n1 // K1_GRP, PAIRS, a.shape[1])
    for part, val in enumerate((re, im)):
        words = _pack_pair(val[:h], val[h:]).reshape(shape)
        for ct in range(a.shape[1] // LANES):
            o_ref[part, :, ct] = words[:, :, ct * LANES:(ct + 1) * LANES]


def _fft_s1_kernel(f1_ref, x_ref, tre_ref, tim_ref, o_ref, *, n1):
    _s1_store(_dot(f1_ref[...], x_ref[...]), tre_ref, tim_ref, pl.program_id(0), o_ref, n1)


def _fft_stage1(x2d, f1, tre, tim, n1, n2, chans):
    r = x2d.shape[0]
    tc = D_MODEL
    g = chans // tc
    return pl.pallas_call(
        functools.partial(_fft_s1_kernel, n1=n1),
        grid=(n2, g),
        in_specs=[_const_spec((2 * n1, r)),
                  pl.BlockSpec((r, tc), lambda j, c: (0, j * g + c)),
                  _const_spec((n1, n2)), _const_spec((n1, n2))],
        out_specs=pl.BlockSpec((2, n1 // K1_GRP, tc // LANES, PAIRS, LANES), lambda j, c: (0, 0, c, j, 0)),
        out_shape=jax.ShapeDtypeStruct((2, n1 // K1_GRP, chans // LANES, n2 * PAIRS, LANES), jnp.uint32),
        compiler_params=_cp("parallel", "parallel"),
        name="fft_stage1",
    )(f1, x2d, tre, tim)


def _load_pair(b_ref, part, s, n2):
    return jnp.concatenate([b_ref[part, 0, ct, pl.ds(s, n2, stride=PAIRS), :] for ct in range(b_ref.shape[2])],
                           axis=1)


def _store_pair(o_ref, part, s, n2, words):
    for ct in range(o_ref.shape[2]):
        o_ref[part, 0, ct, pl.ds(s, n2, stride=PAIRS), :] = words[:, ct * LANES:(ct + 1) * LANES]


def _fft_s2_kernel(m2f_ref, b_ref, o_ref):
    n2 = m2f_ref.shape[0]
    for s in range(PAIRS):
        wre = _load_pair(b_ref, 0, s, n2)
        wim = _load_pair(b_ref, 1, s, n2)
        for half in range(2):
            b = jnp.concatenate([_unpack_pair(wre, half), _unpack_pair(wim, half)], axis=0)
            o_ref[2 * s + half] = _dot(m2f_ref[...], b).astype(BF16)


def _fft_stage2(b5, m2f, n1, n2, chans):
    return pl.pallas_call(
        _fft_s2_kernel,
        grid=(n1 // K1_GRP, chans // MID_TC),
        in_specs=[_const_spec((n2, 2 * n2)),
                  pl.BlockSpec((2, 1, MID_TC // LANES, n2 * PAIRS, LANES), lambda g, c: (0, g, c, 0, 0))],
        out_specs=pl.BlockSpec((K1_GRP, n2, MID_TC), lambda g, c: (g, 0, c)),
        out_shape=jax.ShapeDtypeStruct((n1, n2, chans), BF16),
        compiler_params=_cp("parallel", "parallel"),
        name="fft_stage2",
    )(m2f, b5)


def _fft_mid_kernel(m2f_ref, m2i_ref, b_ref, k_ref, inv_ref, tre_ref, tim_ref, o_ref, *, n2):
    h2 = n2 // 2
    inv = inv_ref[...]
    tre_g = tre_ref[0]
    tim_g = tim_ref[0]
    for s in range(PAIRS):
        wre = _load_pair(b_ref, 0, s, n2)
        wim = _load_pair(b_ref, 1, s, n2)
        res = []
        for half in range(2):
            kl = 2 * s + half
            b = jnp.concatenate([_unpack_pair(wre, half), _unpack_pair(wim, half)], axis=0)
            x = _dot(m2f_ref[...], b)
            kk = k_ref[kl].astype(F32) * inv
            xre, xim = x[:h2], x[h2:]
            kre, kim = kk[:h2], kk[h2:]
            y = jnp.concatenate([xre * kre - xim * kim, xre * kim + xim * kre], axis=0).astype(BF16)
            c = _dot(m2i_ref[...], y)
            cre, cim = c[:n2], c[n2:]
            tre = tre_g[:, kl:kl + 1]
            tim = tim_g[:, kl:kl + 1]
            res.append((cre * tre + cim * tim, cim * tre - cre * tim))
        _store_pair(o_ref, 0, s, n2, _pack_pair(res[0][0], res[1][0]))
        _store_pair(o_ref, 1, s, n2, _pack_pair(res[0][1], res[1][1]))


def _fft_mid(b5, kspec, inv_norm, order, cst, n1, n2):
    d = D_MODEL
    nc = d // MID_TC
    blk = pl.BlockSpec((2, 1, MID_TC // LANES, n2 * PAIRS, LANES), lambda g, c: (0, g, c, 0, 0))
    tw = pl.BlockSpec((1, n2, K1_GRP), lambda g, c: (g, 0, 0))
    return pl.pallas_call(
        functools.partial(_fft_mid_kernel, n2=n2),
        grid=(n1 // K1_GRP, nc),
        in_specs=[_const_spec((n2, 2 * n2)), _const_spec((2 * n2, n2)), blk,
                  pl.BlockSpec((K1_GRP, n2, MID_TC), lambda g, c: (g, 0, order * nc + c)),
                  pl.BlockSpec((1, MID_TC), lambda g, c: (0, order * nc + c)),
                  tw, tw],
        out_specs=blk,
        out_shape=jax.ShapeDtypeStruct((2, n1 // K1_GRP, d // LANES, n2 * PAIRS, LANES), jnp.uint32),
        compiler_params=_cp("parallel", "parallel"),
        name="fft_mid",
    )(cst["m2f"], cst["m2i"], b5, kspec, inv_norm, cst["tre_grp"], cst["tim_grp"])


def _fft_last_kernel(gi_ref, c_ref, gate_ref, z_ref, skip_ref, o_ref):
    h = c_ref.shape[1] * PAIRS

    def words(part):
        return jnp.concatenate([c_ref[part, :, ct].reshape(h, LANES) for ct in range(c_ref.shape[2])], axis=1)

    wre = words(0)
    wim = words(1)
    c = jnp.concatenate([_unpack_pair(wre, 0), _unpack_pair(wre, 1), _unpack_pair(wim, 0),
                         _unpack_pair(wim, 1)], axis=0)
    y = _dot(gi_ref[...], c)
    z = z_ref[...].astype(F32)
    o_ref[...] = (gate_ref[...].astype(F32) * (y + skip_ref[...] * z)).astype(BF16)


def _fft_last(c5, gate2d, z2d, skip_row, gi, n1, n2):
    d = D_MODEL
    r = n1 // 2
    col = pl.BlockSpec((r, d), lambda j: (0, j))
    return pl.pallas_call(
        _fft_last_kernel,
        grid=(n2,),
        in_specs=[_const_spec((r, 2 * n1)),
                  pl.BlockSpec((2, n1 // K1_GRP, d // LANES, PAIRS, LANES), lambda j: (0, 0, 0, j, 0)),
                  col, col, _const_spec((1, d))],
        out_specs=col,
        out_shape=jax.ShapeDtypeStruct((r, n2 * d), BF16),
        compiler_params=_cp("parallel"),
        name="fft_last",
    )(gi, c5, gate2d, z2d, skip_row)


def _long_conv_gate(z_in, gate, skip_row, kspec, inv_norm, order, seq):
    d = D_MODEL
    n1, n2 = _fft_factors(seq)
    cst = _fft_consts(seq)
    z2d = z_in.reshape(n1 // 2, n2 * d)
    b = _fft_stage1(z2d, cst["f1_half"], cst["tre"], cst["tim"], n1, n2, d)
    c = _fft_mid(b, kspec, inv_norm, order, cst, n1, n2)
    out = _fft_last(c, gate.reshape(n1 // 2, n2 * d), z2d, skip_row, cst["gi"], n1, n2)
    return out.reshape(seq, d)


def _filter_spectrum(seq, fparams):
    d = D_MODEL
    od = HY_ORDER * d
    n1, n2 = _fft_factors(seq)
    cst = _fft_consts(seq)
    b, asum = _filter_stage1(seq, *fparams)
    kspec = _fft_stage2(b, cst["m2f"], n1, n2, od)
    return kspec, asum


def _proj_res_kernel(a_ref, w_ref, b_ref, g_ref, x_ref, o_ref):
    y = _dot(a_ref[...], w_ref[...]) + b_ref[...]
    o_ref[...] = x_ref[...] + g_ref[...] * y


def _proj_res(a, w_bf, b_row, gate_row, xres):
    n, d = xres.shape
    tm = min(512, n)
    row = pl.BlockSpec((tm, d), lambda i: (i, 0))
    return pl.pallas_call(
        _proj_res_kernel,
        grid=(n // tm,),
        in_specs=[row, _const_spec((d, d)), _const_spec((1, d)), _const_spec((1, d)), row],
        out_specs=row,
        out_shape=jax.ShapeDtypeStruct((n, d), F32),
        compiler_params=_cp("parallel"),
        name="proj_residual",
    )(a, w_bf, b_row, gate_row, xres)


def _ffn_in_kernel(x_ref, mod_ref, wt_ref, h_ref, aff_ref):
    h = _norm_mod(x_ref[...], mod_ref[...])
    hi = h.astype(BF16)
    lo = (h - hi.astype(F32)).astype(BF16)
    wt = wt_ref[...]
    whi = wt.astype(BF16)
    wlo = (wt - whi.astype(F32)).astype(BF16)
    logits = _dot_nt(whi, hi) + (_dot_nt(whi, lo) + _dot_nt(wlo, hi))
    m = jnp.max(logits, axis=0, keepdims=True)
    p = jnp.exp(logits - m)
    aff_ref[...] = p / jnp.sum(p, axis=0, keepdims=True)
    h_ref[...] = hi


def _ffn_in(x, mod, w_router):
    n, d = x.shape
    e = N_EXPERTS
    tm = min(512, n)
    return pl.pallas_call(
        _ffn_in_kernel,
        grid=(n // tm,),
        in_specs=[pl.BlockSpec((tm, d), lambda i: (i, 0)), _const_spec((8, d)), _const_spec((e, d))],
        out_specs=[pl.BlockSpec((tm, d), lambda i: (i, 0)), pl.BlockSpec((e, tm), lambda i: (0, i))],
        out_shape=[jax.ShapeDtypeStruct((n, d), BF16), jax.ShapeDtypeStruct((e, n), F32)],
        compiler_params=_cp("parallel"),
        name="moe_router",
    )(x, mod, w_router.T)


def _select_kernel(a_ref, pos_ref, s0_ref, sel_ref, *, cap, nblk):
    e = N_EXPERTS
    bits = pltpu.bitcast(a_ref[...], I32)

    def bisect(i, thr):
        cand = thr | jnp.left_shift(jnp.int32(1), 30 - i)
        cnt = jnp.sum(jnp.where(bits >= cand, 1.0, 0.0), axis=1, keepdims=True)
        return jnp.where(cnt >= cap, cand, thr)

    thr = lax.fori_loop(0, 31, bisect, jnp.zeros((e, 1), I32))
    n_gt = jnp.sum(jnp.where(bits > thr, 1.0, 0.0), axis=1, keepdims=True)
    need = cap - n_gt
    r = lax.broadcasted_iota(I32, (TOK_BLK, TOK_BLK), 0)
    c = lax.broadcasted_iota(I32, (TOK_BLK, TOK_BLK), 1)
    upper = jnp.where(r < c, 1.0, 0.0).astype(BF16)

    def pass1(j, carry):
        sl = pl.ds(pl.multiple_of(j * TOK_BLK, TOK_BLK), TOK_BLK)
        bj = pltpu.bitcast(a_ref[:, sl], I32)
        eq = jnp.where(bj == thr, 1.0, 0.0)
        rank = _dot(eq.astype(BF16), upper) + carry
        keep = jnp.logical_or(bj > thr, jnp.logical_and(bj == thr, rank < need))
        sel_ref[:, sl] = jnp.where(keep, 1.0, 0.0)
        return carry + jnp.sum(eq, axis=1, keepdims=True)

    lax.fori_loop(0, nblk, pass1, jnp.zeros((e, 1), F32))

    def pass2(j, carry):
        sl = pl.ds(pl.multiple_of(j * TOK_BLK, TOK_BLK), TOK_BLK)
        s = sel_ref[:, sl]
        slot = _dot(s.astype(BF16), upper) + carry
        pos_ref[:, sl] = jnp.where(s > 0.5, slot, -1.0).astype(I32)
        s0_ref[j] = jnp.broadcast_to(carry, (e, 128)).astype(I32)
        return carry + jnp.sum(s, axis=1, keepdims=True)

    total = lax.fori_loop(0, nblk, pass2, jnp.zeros((e, 1), F32))
    s0_ref[nblk] = jnp.broadcast_to(total, (e, 128)).astype(I32)


def _select(aff_t, cap):
    e, n = aff_t.shape
    nblk = n // TOK_BLK
    return pl.pallas_call(
        functools.partial(_select_kernel, cap=cap, nblk=nblk),
        out_shape=[jax.ShapeDtypeStruct((e, n), I32), jax.ShapeDtypeStruct((nblk + 1, e, 128), I32)],
        scratch_shapes=[pltpu.VMEM((e, n), F32)],
        compiler_params=pltpu.CompilerParams(vmem_limit_bytes=VMEM_LIMIT),
        name="moe_select",
    )(aff_t)


def _block_windows(s0_ref, blk, e_idx):
    s0 = s0_ref[blk * N_EXPERTS + e_idx]
    s1 = s0_ref[(blk + 1) * N_EXPERTS + e_idx]
    start = lax.shift_left(lax.shift_right_logical(s0, 4), 4)
    nwin = jnp.where(s1 > s0, lax.shift_right_logical(s1 - start + (WIN - 1), WIN_SHIFT), 0)
    return start, nwin


def _gather_kernel(s0_ref, pos_ref, h_ref, xe_ref, *, sub):
    e_idx = pl.program_id(0)
    c = pl.program_id(1)

    @pl.when(c == 0)
    def _():
        xe_ref[...] = jnp.zeros_like(xe_ref)

    rows = lax.broadcasted_iota(I32, (WIN, TOK_BLK), 0)

    def body(i, carry):
        start, nwin = _block_windows(s0_ref, c * sub + i, e_idx)
        tok = pl.ds(pl.multiple_of(i * TOK_BLK, TOK_BLK), TOK_BLK)
        prow = pos_ref[0, :, tok]

        def window(w, carry2):
            base = pl.multiple_of(start + w * WIN, ROW_ALIGN)
            onehot = jnp.where(rows == prow - base, 1.0, 0.0).astype(BF16)
            got = _dot(onehot, h_ref[tok, :]).astype(BF16)
            win = pl.ds(base, WIN)
            xe_ref[0, win, :] = xe_ref[0, win, :] + got
            return carry2

        return lax.fori_loop(0, nwin, window, carry)

    lax.fori_loop(0, sub, body, 0)


def _gather(s0_flat, pos, h, cap_pad):
    e, n = pos.shape
    d = h.shape[1]
    chunk = min(2048, n)
    sub = chunk // TOK_BLK
    return pl.pallas_call(
        functools.partial(_gather_kernel, sub=sub),
        grid_spec=pltpu.PrefetchScalarGridSpec(
            num_scalar_prefetch=1,
            grid=(e, n // chunk),
            in_specs=[pl.BlockSpec((1, 1, chunk), lambda ei, c, s0: (ei, 0, c)),
                      pl.BlockSpec((chunk, d), lambda ei, c, s0: (c, 0))],
            out_specs=pl.BlockSpec((1, cap_pad, d), lambda ei, c, s0: (ei, 0, 0)),
        ),
        out_shape=jax.ShapeDtypeStruct((e, cap_pad, d), BF16),
        compiler_params=_cp("parallel", "arbitrary"),
        name="moe_gather",
    )(s0_flat, pos.reshape(e, 1, n), h)


def _expert_kernel(x_ref, wg_ref, wu_ref, wd_ref, y_ref, wg_scr, wu_scr, wd_scr, *, n_real):
    j = pl.program_id(1)

    @pl.when(j == 0)
    def _():
        wg_scr[...] = wg_ref[0].astype(BF16)
        wu_scr[...] = wu_ref[0].astype(BF16)
        wd_scr[...] = wd_ref[0].astype(BF16)

    @pl.when(j < n_real)
    def _():
        x = x_ref[0]
        g = _dot(x, wg_scr[...])
        u = _dot(x, wu_scr[...])
        a = (g / (1.0 + jnp.exp(-g))) * u
        y_ref[0] = _dot(a.astype(BF16), wd_scr[...]).astype(BF16)

    @pl.when(j >= n_real)
    def _():
        y_ref[0] = jnp.zeros_like(y_ref[0])


def _experts(xe, wg, wu, wd, layer, cap):
    e, cap_pad, d = xe.shape
    f = wg.shape[3]
    tm = min(256, cap)
    tile = pl.BlockSpec((1, tm, d), lambda ei, j: (ei, j, 0))
    return pl.pallas_call(
        functools.partial(_expert_kernel, n_real=cap // tm),
        grid=(e, cap_pad // tm),
        in_specs=[tile,
                  pl.BlockSpec((None, 1, d, f), lambda ei, j: (layer, ei, 0, 0)),
                  pl.BlockSpec((None, 1, d, f), lambda ei, j: (layer, ei, 0, 0)),
                  pl.BlockSpec((None, 1, f, d), lambda ei, j: (layer, ei, 0, 0))],
        out_specs=tile,
        out_shape=jax.ShapeDtypeStruct((e, cap_pad, d), BF16),
        scratch_shapes=[pltpu.VMEM((d, f), BF16), pltpu.VMEM((d, f), BF16), pltpu.VMEM((f, d), BF16)],
        compiler_params=_cp("parallel", "arbitrary"),
        name="moe_experts",
    )(xe, wg, wu, wd)


def _combine_kernel(s0_ref, post_ref, gt_ref, y_ref, x_ref, gate_ref, o_ref, *, sub):
    c = pl.program_id(0)
    e_idx = pl.program_id(1)

    @pl.when(e_idx == 0)
    def _():
        o_ref[...] = jnp.zeros_like(o_ref)

    lanes = lax.broadcasted_iota(I32, (TOK_BLK, WIN), 1)
    elane = lax.broadcasted_iota(I32, (TOK_BLK, N_EXPERTS), 1) == e_idx

    def body(i, carry):
        start, nwin = _block_windows(s0_ref, c * sub + i, e_idx)
        tok = pl.ds(pl.multiple_of(i * TOK_BLK, TOK_BLK), TOK_BLK)

        def window(w, carry2):
            base = pl.multiple_of(start + w * WIN, ROW_ALIGN)
            pcol = jnp.sum(jnp.where(elane, post_ref[tok, :].astype(F32), 0.0), axis=1, keepdims=True)
            gcol = jnp.sum(jnp.where(elane, gt_ref[tok, :], 0.0), axis=1, keepdims=True)
            rel = pcol.astype(I32) - base
            onehot = jnp.where(lanes == rel, 1.0, 0.0).astype(BF16)
            strip = y_ref[0, pl.ds(base, WIN), :]
            o_ref[tok, :] = o_ref[tok, :] + gcol * _dot(onehot, strip)
            return carry2

        return lax.fori_loop(0, nwin, window, carry)

    lax.fori_loop(0, sub, body, 0)

    @pl.when(e_idx == N_EXPERTS - 1)
    def _():
        o_ref[...] = x_ref[...] + gate_ref[...] * o_ref[...]


def _combine(s0_flat, pos_t, g_t, y, xres, gate_row):
    n, d = xres.shape
    e, cap_pad, _ = y.shape
    chunk = min(2048, n)
    sub = chunk // TOK_BLK
    return pl.pallas_call(
        functools.partial(_combine_kernel, sub=sub),
        grid_spec=pltpu.PrefetchScalarGridSpec(
            num_scalar_prefetch=1,
            grid=(n // chunk, e),
            in_specs=[pl.BlockSpec((chunk, e), lambda c, ei, s0: (c, 0)),
                      pl.BlockSpec((chunk, e), lambda c, ei, s0: (c, 0)),
                      pl.BlockSpec((1, cap_pad, d), lambda c, ei, s0: (ei, 0, 0)),
                      pl.BlockSpec((chunk, d), lambda c, ei, s0: (c, 0)),
                      pl.BlockSpec((1, d), lambda c, ei, s0: (0, 0))],
            out_specs=pl.BlockSpec((chunk, d), lambda c, ei, s0: (c, 0)),
        ),
        out_shape=jax.ShapeDtypeStruct((n, d), F32),
        compiler_params=_cp("parallel", "arbitrary"),
        name="moe_combine",
    )(s0_flat, pos_t, g_t, y, xres, gate_row)


def _moe_block(x, mod, gate_row, w_router, wg, wu, wd, layer):
    n = x.shape[0]
    cap = EC_CAPACITY * n // N_EXPERTS
    cap_pad = cap + TOK_BLK
    h, aff_t = _ffn_in(x, mod, w_router)
    pos, s0 = _select(aff_t, cap)
    s0_flat = s0[:, :, 0].reshape(-1)
    xe = _gather(s0_flat, pos, h, cap_pad)
    y = _experts(xe, wg, wu, wd, layer, cap)
    return _combine(s0_flat, pos.T, aff_t.T, y, x, gate_row)


QK_SCALE = (DA_HEAD_DIM ** -0.5) * math.log2(math.e)


def _group_rms(u, gsum_ref, gain, eps):
    sq = u * u
    hi = sq.astype(BF16)
    lo = (sq - hi.astype(F32)).astype(BF16)
    ss = _dot(hi, gsum_ref[...]) + _dot(lo, gsum_ref[...])
    return u * lax.rsqrt(ss * (1.0 / DA_HEAD_DIM) + eps) * gain


def _rope(u, cos, sin_signed):
    d = u.shape[1]
    half = ROPE_AXIS_DIM // 2
    lane = lax.broadcasted_iota(I32, u.shape, 1)
    first = (lane & half) == 0
    swapped = jnp.where(first, pltpu.roll(u, d - half, 1), pltpu.roll(u, half, 1))
    return u * cos + swapped * sin_signed


def _qkv_kernel(*refs, rope):
    if rope:
        x_ref, mod_ref, w_ref, gsum_ref, qn_ref, kn_ref, cos_ref, sin_ref, q_ref, k_ref, v_ref = refs
    else:
        x_ref, mod_ref, w_ref, gsum_ref, qn_ref, kn_ref, q_ref, k_ref, v_ref = refs
    d = D_MODEL
    h = _norm_mod(x_ref[...], mod_ref[...]).astype(BF16)
    if rope:
        reps = d // cos_ref.shape[1]
        cos = _lane_tile(cos_ref[...], reps)
        sin = _lane_tile(sin_ref[...], reps)
    for part, (o_ref, gain_ref) in enumerate(((q_ref, qn_ref), (k_ref, kn_ref))):
        u = _dot(h, w_ref[:, part * d:(part + 1) * d])
        u = _group_rms(u, gsum_ref, gain_ref[...], NORM_EPS)
        if rope:
            u = _rope(u, cos, sin)
        if part == 0:
            u = u * QK_SCALE
        o_ref[...] = u.astype(BF16)
    v_ref[...] = _dot(h, w_ref[:, 2 * d:]).astype(BF16)


def _qkv(x, mod, w_bf, gsum, qn_row, kn_row, cos=None, sin=None):
    n, d = x.shape
    tm = min(512, n)
    rope = cos is not None
    row = pl.BlockSpec((tm, d), lambda i: (i, 0))
    in_specs = [row, _const_spec((8, d)), _const_spec((d, 3 * d)), _const_spec((d, d)),
                _const_spec((1, d)), _const_spec((1, d))]
    args = [x, mod, w_bf, gsum, qn_row, kn_row]
    if rope:
        tw = cos.shape[1]
        in_specs += [pl.BlockSpec((tm, tw), lambda i: (i, 0)), pl.BlockSpec((tm, tw), lambda i: (i, 0))]
        args += [cos, sin]
    out = jax.ShapeDtypeStruct((n, d), BF16)
    return pl.pallas_call(
        functools.partial(_qkv_kernel, rope=rope),
        grid=(n // tm,),
        in_specs=in_specs,
        out_specs=[row, row, row],
        out_shape=[out, out, out],
        compiler_params=_cp("parallel"),
        name="attn_qkv",
    )(*args)


NEG_BIG = -1e30


def _attn_kernel(q_ref, k_ref, v_ref, lam_ref, sub_ref, o_ref, m_scr, acc_scr, sa_scr, sb_scr, ma_scr, mb_scr,
                 pa_scr, pb_scr, vlast_scr, *,
                 lam_init):
    j = pl.program_id(1)
    nj = pl.num_programs(1)
    hd, vd = DA_HEAD_DIM, DA_V_DIM
    tk = k_ref.shape[0]

    last = DA_HEADS - 1

    @pl.when(j == 0)
    def _():
        m_scr[...] = jnp.full_like(m_scr, NEG_BIG)
        acc_scr[...] = jnp.zeros_like(acc_scr)
        sb_scr[...] = jnp.full_like(sb_scr, 2.0 * NEG_BIG)
        mb_scr[...] = jnp.full_like(mb_scr, 2.0 * NEG_BIG)
        vlast_scr[...] = jnp.zeros_like(vlast_scr)

    lane = lax.broadcasted_iota(I32, (q_ref.shape[0], vd), 1)
    ones_col = jnp.where(lax.broadcasted_iota(I32, (tk, vd), 1) == 0, 1.0, 0.0).astype(BF16)

    def head_cols(h):
        return pl.ds(pl.multiple_of(h * vd, vd), vd)

    def scores(h, comp, s_ref, mx_ref):
        qb = q_ref[:, head_cols(h)]
        in_comp = (lane >= comp * hd) & (lane < (comp + 1) * hd)
        qm = jnp.where(in_comp, qb, jnp.zeros_like(qb))
        s = _dot_nt(qm, k_ref[:, head_cols(h)])
        s_ref[...] = s
        mx_ref[...] = jnp.broadcast_to(jnp.max(s, axis=1, keepdims=True), mx_ref.shape)

    def softmax_step(h, comp, s_ref, mx_ref, p_ref):
        idx = 2 * h + comp
        m_prev = m_scr[idx]
        m_new = jnp.maximum(m_prev, mx_ref[...])
        alpha = _lane_tile(jnp.exp2(m_prev - m_new), 2)
        p_ref[...] = jnp.exp2(s_ref[...] - m_new[:, 0:1]).astype(BF16)
        m_scr[idx] = m_new
        acc_scr[idx] = alpha * acc_scr[idx]

    def pv_step(h, comp, p_ref, v_block=None):
        idx = 2 * h + comp
        if v_block is None:
            v_block = v_ref[:, head_cols(h)]
        acc_scr[idx] = acc_scr[idx] + _dot(p_ref[...], jnp.concatenate([v_block, ones_col], axis=1))

    def head(h, carry):
        scores(h, 1, sb_scr, mb_scr)
        pv_step(h - 1, 1, pb_scr)
        softmax_step(h, 0, sa_scr, ma_scr, pa_scr)
        scores(h + 1, 0, sa_scr, ma_scr)
        pv_step(h, 0, pa_scr)
        softmax_step(h, 1, sb_scr, mb_scr, pb_scr)
        return carry

    def head_pair(g, carry):
        return head(2 * g + 2, head(2 * g + 1, carry))

    scores(0, 0, sa_scr, ma_scr)
    softmax_step(last, 1, sb_scr, mb_scr, pb_scr)
    scores(0, 1, sb_scr, mb_scr)
    pv_step(last, 1, pb_scr, vlast_scr[...])
    softmax_step(0, 0, sa_scr, ma_scr, pa_scr)
    scores(1, 0, sa_scr, ma_scr)
    pv_step(0, 0, pa_scr)
    softmax_step(0, 1, sb_scr, mb_scr, pb_scr)
    lax.fori_loop(0, (DA_HEADS - 2) // 2, head_pair, 0)
    scores(last, 1, sb_scr, mb_scr)
    pv_step(last - 1, 1, pb_scr)
    softmax_step(last, 0, sa_scr, ma_scr, pa_scr)
    pv_step(last, 0, pa_scr)
    vlast_scr[...] = v_ref[:, last * vd:(last + 1) * vd]

    @pl.when(j == nj - 1)
    def _():
        softmax_step(last, 1, sb_scr, mb_scr, pb_scr)
        pv_step(last, 1, pb_scr)
        lp = lam_ref[...]
        lam = (jnp.exp(jnp.sum(lp[0:1] * lp[1:2], axis=1, keepdims=True))
               - jnp.exp(jnp.sum(lp[2:3] * lp[3:4], axis=1, keepdims=True)) + lam_init)
        for h in range(DA_HEADS):
            a0 = acc_scr[2 * h]
            a1 = acc_scr[2 * h + 1]
            o = a0[:, :vd] / a0[:, vd:vd + 1] - lam * (a1[:, :vd] / a1[:, vd:vd + 1])
            ms = jnp.mean(o * o, axis=1, keepdims=True)
            o = o * lax.rsqrt(ms + SUBLN_EPS) * (sub_ref[...] * (1.0 - lam_init))
            o_ref[:, h * vd:(h + 1) * vd] = o.astype(BF16)


def _attention(q, k_all, v_all, lam_rows, subln_row, lam_init):
    n, d = q.shape
    nk = k_all.shape[0]
    tq = min(512, n)
    tk = 1280 if nk % 1280 == 0 else 256
    assert nk % tk == 0
    nc = 2 * DA_HEADS
    return pl.pallas_call(
        functools.partial(_attn_kernel, lam_init=lam_init),
        grid=(n // tq, nk // tk),
        in_specs=[pl.BlockSpec((tq, d), lambda i, j: (i, 0)),
                  pl.BlockSpec((tk, d), lambda i, j: (j, 0)),
                  pl.BlockSpec((tk, d), lambda i, j: (j, 0)),
                  _const_spec((8, DA_HEAD_DIM)), _const_spec((1, DA_V_DIM))],
        out_specs=pl.BlockSpec((tq, d), lambda i, j: (i, 0)),
        out_shape=jax.ShapeDtypeStruct((n, d), BF16),
        scratch_shapes=[pltpu.VMEM((nc, tq, DA_V_DIM), F32), pltpu.VMEM((nc, tq, 2 * DA_V_DIM), F32),
                        pltpu.VMEM((tq, tk), F32), pltpu.VMEM((tq, tk), F32),
                        pltpu.VMEM((tq, DA_V_DIM), F32), pltpu.VMEM((tq, DA_V_DIM), F32),
                        pltpu.VMEM((tq, tk), BF16), pltpu.VMEM((tq, tk), BF16),
                        pltpu.VMEM((tk, DA_V_DIM), BF16)],
        compiler_params=_cp("parallel", "arbitrary"),
        name="diff_attention",
    )(q, k_all, v_all, lam_rows, subln_row)


def _rope_tables(n):
    t = jnp.arange(n, dtype=I32)
    row = (t // GRID_W).astype(F32)
    col = (t % GRID_W).astype(F32)
    inv = ROPE_THETA ** (-jnp.arange(0, ROPE_AXIS_DIM, 2, dtype=F32) / ROPE_AXIS_DIM)
    ar = row[:, None] * inv[None]
    ac = col[:, None] * inv[None]
    cos = jnp.concatenate([jnp.cos(ar), jnp.cos(ar), jnp.cos(ac), jnp.cos(ac)], axis=1)
    sin = jnp.concatenate([-jnp.sin(ar), jnp.sin(ar), -jnp.sin(ac), jnp.sin(ac)], axis=1)
    return jnp.concatenate([cos, cos], axis=1), jnp.concatenate([sin, sin], axis=1)


def _hyena_layer(x, mod, gate_row, kspec, inv_norm, w_in_bf, b_in, conv_w, conv_b, skip, w_out_bf, b_out):
    seq = x.shape[0]
    v, x1, x2 = _hy_in(x, mod, w_in_bf, b_in, conv_w, conv_b)
    z = _long_conv_gate(v, x1, skip[0:1], kspec, inv_norm, 0, seq)
    z = _long_conv_gate(z, x2, skip[1:2], kspec, inv_norm, 1, seq)
    return _proj_res(z, w_out_bf, b_out.reshape(1, -1), gate_row, x)


def kernel(x, c, ctx, c_ctx, ada_w, ada_b, norm_mix, norm_ffn, hy_w_in, hy_b_in, hy_conv_w, hy_conv_b, hy_f_w1, hy_f_b1, hy_f_w2, hy_f_b2, hy_f_w3, hy_f_b3, hy_f_w4, hy_f_freq, hy_skip, hy_w_out, hy_b_out, da_w_qkv, da_q_norm, da_k_norm, da_lam_q1, da_lam_k1, da_lam_q2, da_lam_k2, da_subln, da_w_out, moe_router, moe_w_gate, moe_w_up, moe_w_down):
    d = D_MODEL
    depth = ada_w.shape[0]
    assert x.shape[0] == 1 and x.shape[2] == d
    xs = x[0]
    cs = ctx[0]
    cond8 = jnp.concatenate([c[0:1], c_ctx[None], jnp.zeros((6, d), F32)], axis=0)
    mods = _adaln(cond8, ada_w, ada_b)

    def mod_slice(i, row, k):
        return mods[i, row, k * d:(k + 1) * d][None]

    for i in range(depth):
        last = i == depth - 1
        j = i // 2
        mix_x = _mod_rows(norm_mix[i], mods[i], 0, 0)
        mix_c = _mod_rows(norm_mix[i], mods[i], 1, 0)
        if i % 2 == 0:
            fparams = (hy_f_w1[j], hy_f_b1[j], hy_f_w2[j], hy_f_b2[j], hy_f_w3[j], hy_f_b3[j], hy_f_w4[j],
                       hy_f_freq[j])
            shared = (hy_w_in[j].astype(BF16), hy_b_in[j], hy_conv_w[j], hy_conv_b[j], hy_skip[j],
                      hy_w_out[j].astype(BF16), hy_b_out[j])
            kspec, asum = _filter_spectrum(xs.shape[0], fparams)
            inv_norm = 1.0 / (asum[0:1] + HY_FILTER_EPS)
            new_x = _hyena_layer(xs, mix_x, mod_slice(i, 0, 2), kspec, inv_norm, *shared)
            if not last:
                kspec_c, asum_c = _filter_spectrum(cs.shape[0], fparams)
                inv_c = 1.0 / (asum_c[0:1] + HY_FILTER_EPS)
                cs = _hyena_layer(cs, mix_c, mod_slice(i, 1, 2), kspec_c, inv_c, *shared)
            xs = new_x
        else:
            lam_init = 0.8 - 0.6 * math.exp(-0.3 * i)
            w_qkv = da_w_qkv[j].astype(BF16)
            gidx = np.arange(d) // DA_HEAD_DIM
            gsum = jnp.asarray((gidx[:, None] == gidx[None]).astype(np.float32)).astype(BF16)
            qn = jnp.tile(da_q_norm[j], 2 * DA_HEADS)[None]
            kn = jnp.tile(da_k_norm[j], 2 * DA_HEADS)[None]
            cos, sin = _rope_tables(xs.shape[0])
            qx, kx, vx = _qkv(xs, mix_x, w_qkv, gsum, qn, kn, cos, sin)
            qc, kc, vc = _qkv(cs, mix_c, w_qkv, gsum, qn, kn)
            k_all = jnp.concatenate([kc, kx], axis=0)
            v_all = jnp.concatenate([vc, vx], axis=0)
            lam_rows = jnp.concatenate([da_lam_q1[j][None], da_lam_k1[j][None], da_lam_q2[j][None],
                                        da_lam_k2[j][None], jnp.zeros((4, DA_HEAD_DIM), F32)], axis=0)
            w_out = da_w_out[j].astype(BF16)
            zero_b = jnp.zeros((1, d), F32)
            ox = _attention(qx, k_all, v_all, lam_rows, da_subln[j][None], lam_init)
            new_x = _proj_res(ox, w_out, zero_b, mod_slice(i, 0, 2), xs)
            if not last:
                oc = _attention(qc, kc, vc, lam_rows, da_subln[j][None], lam_init)
                cs = _proj_res(oc, w_out, zero_b, mod_slice(i, 1, 2), cs)
            xs = new_x
        experts = (moe_w_gate, moe_w_up, moe_w_down, i)
        if not last:
            cs = _moe_block(cs, _mod_rows(norm_ffn[i], mods[i], 1, 3), mod_slice(i, 1, 5), moe_router[i], *experts)
        xs = _moe_block(xs, _mod_rows(norm_ffn[i], mods[i], 0, 3), mod_slice(i, 0, 5), moe_router[i], *experts)
    return xs[None]
```

```python
import functools
import math

import jax
import jax.numpy as jnp
import numpy as np
from jax import lax
from jax.experimental import pallas as pl
from jax.experimental.pallas import tpu as pltpu

F32 = jnp.float32
BF16 = jnp.bfloat16
I32 = jnp.int32
HIGHEST = lax.Precision.HIGHEST

D_MODEL = 1024
N_MOD = 6
NORM_EPS = 1e-6
GRID_W = 64
HY_ORDER = 2
HY_SHORT = 3
HY_EMB_BANDS = 16
HY_EMB_DIM = 1 + 2 * HY_EMB_BANDS
HY_FILTER_HIDDEN = 64
HY_DECAY_FAST = 0.3
HY_DECAY_SLOW = 1.5
HY_DECAY_TARGET = 1e-2
HY_FILTER_EPS = 1e-6
Z_SIGN_COL = 33
DA_HEADS = 8
DA_HEAD_DIM = 64
DA_V_DIM = 128
ROPE_AXIS_DIM = 32
ROPE_THETA = 10000.0
SUBLN_EPS = 1e-5
N_EXPERTS = 16
EC_CAPACITY = 2
D_EXPERT = 1024
TOK_BLK = 256
ROW_ALIGN = 16
WIN_SHIFT = 7
WIN = 1 << WIN_SHIFT

VMEM_LIMIT = 56 * 1024 * 1024


def _cp(*sem):
    return pltpu.CompilerParams(dimension_semantics=sem, vmem_limit_bytes=VMEM_LIMIT)


def _const_spec(shape):
    nd = len(shape)
    return pl.BlockSpec(shape, lambda *_: (0,) * nd)


def _dot(a, b):
    return jnp.dot(a, b, preferred_element_type=F32)


def _dot_nt(a, b):
    return lax.dot_general(a, b, (((1,), (1,)), ((), ())), preferred_element_type=F32)


def _norm_mod(x, mod, eps=NORM_EPS):
    ms = jnp.mean(x * x, axis=-1, keepdims=True)
    return x * lax.rsqrt(ms + eps) * (mod[0:1] * (1.0 + mod[1:2])) + mod[2:3]


def _lane_tile(x, reps):
    return jnp.concatenate([x] * reps, axis=1) if reps > 1 else x


def _adaln_kernel(c_ref, w_ref, b_ref, o_ref):
    c = c_ref[...]
    s = c / (1.0 + jnp.exp(-c))
    o_ref[0] = jnp.dot(s, w_ref[0], precision=HIGHEST, preferred_element_type=F32) + b_ref[0]


def _adaln(cond8, ada_w, ada_b):
    depth, d, nout = ada_w.shape
    tn = 1536
    return pl.pallas_call(
        _adaln_kernel,
        grid=(depth, nout // tn),
        in_specs=[_const_spec((8, d)),
                  pl.BlockSpec((1, d, tn), lambda l, j: (l, 0, j)),
                  pl.BlockSpec((1, 1, tn), lambda l, j: (l, 0, j))],
        out_specs=pl.BlockSpec((1, 8, tn), lambda l, j: (l, 0, j)),
        out_shape=jax.ShapeDtypeStruct((depth, 8, nout), F32),
        compiler_params=_cp("parallel", "parallel"),
        name="adaln",
    )(cond8, ada_w, ada_b.reshape(depth, 1, nout))


def _mod_rows(norm_g, mods, row, k0):
    d = D_MODEL
    shift = mods[row, k0 * d:(k0 + 1) * d]
    scale = mods[row, (k0 + 1) * d:(k0 + 2) * d]
    z = jnp.zeros((5, d), F32)
    return jnp.concatenate([norm_g[None], scale[None], shift[None], z], axis=0)


HALO = 16


def _hy_in_kernel(x_ref, xp_ref, xn_ref, mod_ref, w_ref, b_ref, cw_ref, v_ref, x1_ref, x2_ref, *, tm, n_rows):
    i = pl.program_id(0)
    mod = mod_ref[...]
    hm = _norm_mod(x_ref[...], mod).astype(BF16)
    hp = _norm_mod(xp_ref[...], mod).astype(BF16)
    hn = _norm_mod(xn_ref[...], mod).astype(BF16)
    hcat = jnp.concatenate([hp, hm, hn], axis=0)
    row = lax.broadcasted_iota(I32, (tm + 2 * HALO, 1), 0) + (i * tm - HALO)
    valid = jnp.logical_and(row >= 0, row < n_rows)
    d = D_MODEL
    for c, o_ref in enumerate((v_ref, x1_ref, x2_ref)):
        u = _dot(hcat, w_ref[:, c * d:(c + 1) * d]) + b_ref[:, c * d:(c + 1) * d]
        u = jnp.where(valid, u, 0.0)
        cw = cw_ref[:, c * d:(c + 1) * d]
        y = (cw[3:4] + cw[0:1] * u[HALO - 1:HALO - 1 + tm] + cw[1:2] * u[HALO:HALO + tm]
             + cw[2:3] * u[HALO + 1:HALO + 1 + tm])
        o_ref[...] = y.astype(BF16)


def _hy_in(x, mod, w_bf, b_in, conv_w, conv_b):
    n, d = x.shape
    tm = min(512, n)
    nh = n // HALO
    cw = jnp.concatenate([conv_w, conv_b[None], jnp.zeros((4, 3 * d), F32)], axis=0)
    out = jax.ShapeDtypeStruct((n, d), BF16)
    row_spec = pl.BlockSpec((tm, d), lambda i: (i, 0))
    return pl.pallas_call(
        functools.partial(_hy_in_kernel, tm=tm, n_rows=n),
        grid=(n // tm,),
        in_specs=[row_spec,
                  pl.BlockSpec((HALO, d), lambda i: (jnp.maximum(i * (tm // HALO) - 1, 0), 0)),
                  pl.BlockSpec((HALO, d), lambda i: (jnp.minimum((i + 1) * (tm // HALO), nh - 1), 0)),
                  _const_spec((8, d)), _const_spec((d, 3 * d)), _const_spec((1, 3 * d)),
                  _const_spec((8, 3 * d))],
        out_specs=[row_spec, row_spec, row_spec],
        out_shape=[out, out, out],
        compiler_params=_cp("parallel"),
        name="hyena_in",
    )(x, x, x, mod, w_bf, b_in.reshape(1, 3 * d), cw)


Z_HALF = 64


def _filter_s1_kernel(z_ref, w1_ref, b1_ref, w2_ref, b2_ref, w3_ref, b3_ref, fr_ref, w4f_ref, w4b_ref, dl_ref,
                      f1_ref, tre_ref, tim_ref, o_ref, asum_ref, *, n1):
    j = pl.program_id(0)
    z = z_ref[...]
    fr = fr_ref[...]

    def lin(a, w_ref, b_ref):
        return jnp.dot(a, w_ref[...], precision=HIGHEST, preferred_element_type=F32) + b_ref[...]

    hid = jnp.sin(fr * lin(z, w1_ref, b1_ref))
    hid = jnp.sin(fr * lin(hid, w2_ref, b2_ref))
    hid = jnp.sin(fr * lin(hid, w3_ref, b3_ref)).astype(BF16)
    dl = dl_ref[...]

    def taps(w4_ref, col):
        t = z[:, col:col + 1]
        sgn = z[:, col + Z_SIGN_COL:col + Z_SIGN_COL + 1]
        return _dot(hid, w4_ref[...]) * jnp.exp(-t * dl) * sgn

    k = jnp.concatenate([taps(w4f_ref, 0), taps(w4b_ref, Z_HALF)], axis=0)

    @pl.when(j == 0)
    def _():
        asum_ref[...] = jnp.zeros_like(asum_ref)

    asum_ref[0:1, :] += jnp.sum(jnp.abs(k), axis=0, keepdims=True)
    _s1_store(_dot(f1_ref[...], k.astype(BF16)), tre_ref, tim_ref, j, o_ref, n1)


def _filter_positions(seq, n1, n2):
    i = np.arange(n1 // 2)[None, :]
    j = np.arange(n2)[:, None]
    bands = np.linspace(1e-4, HY_EMB_BANDS - 1, HY_EMB_BANDS)
    z = np.zeros((n2, n1 // 2, 2 * Z_HALF), np.float64)
    for col, r in ((0, i * n2 + j), (Z_HALF, i * n2 + j + seq)):
        pos = np.minimum(np.where(r < seq, r, 2 * seq - r), seq - 1).astype(np.float64)
        w = 2.0 * np.pi * pos / seq
        z[:, :, col] = pos / (seq - 1)
        z[:, :, col + 1:col + 1 + HY_EMB_BANDS] = np.cos(w[..., None] * bands)
        z[:, :, col + 1 + HY_EMB_BANDS:col + HY_EMB_DIM] = -np.sin(w[..., None] * bands)
        z[:, :, col + Z_SIGN_COL] = np.where(r < seq, 1.0, np.where(r == seq, 0.0, -1.0))
    return jnp.asarray(z.reshape(n2 * (n1 // 2), 2 * Z_HALF).astype(np.float32))


def _filter_stage1(seq, f_w1, f_b1, f_w2, f_b2, f_w3, f_b3, f_w4, f_freq):
    d, hid = D_MODEL, HY_FILTER_HIDDEN
    od = HY_ORDER * d
    n1, n2 = _fft_factors(seq)
    cst = _fft_consts(seq)
    z = _filter_positions(seq, n1, n2)
    zw = 2 * Z_HALF
    zero = jnp.zeros((hid, hid), F32)
    pad = jnp.zeros((Z_HALF - HY_EMB_DIM, hid), F32)
    w1 = jnp.concatenate([jnp.concatenate([f_w1, pad], axis=0), jnp.zeros((Z_HALF, hid), F32)], axis=0)
    w1 = jnp.concatenate([w1, jnp.roll(w1, Z_HALF, axis=0)], axis=1)
    blockdiag = lambda w: jnp.concatenate([jnp.concatenate([w, zero], axis=1),
                                           jnp.concatenate([zero, w], axis=1)], axis=0)
    twice = lambda v: jnp.tile(v, 2).reshape(1, 2 * hid)
    w4d = f_w4.reshape(hid, HY_ORDER, 2, d).transpose(2, 0, 1, 3).reshape(2, hid, od)
    zrows = jnp.zeros((hid, od), F32)
    w4f = jnp.concatenate([w4d[0], zrows], axis=0).astype(BF16)
    w4b = jnp.concatenate([zrows, w4d[1]], axis=0).astype(BF16)
    max_decay = math.log(HY_DECAY_TARGET) / HY_DECAY_FAST
    min_decay = math.log(HY_DECAY_TARGET) / HY_DECAY_SLOW
    deltas = np.abs(np.linspace(min_decay, max_decay, d, dtype=np.float32))
    dl = jnp.asarray(np.tile(deltas, HY_ORDER)[None])
    r = n1 // 2
    return pl.pallas_call(
        functools.partial(_filter_s1_kernel, n1=n1),
        grid=(n2,),
        in_specs=[pl.BlockSpec((r, zw), lambda j: (j, 0)),
                  _const_spec((zw, 2 * hid)), _const_spec((1, 2 * hid)),
                  _const_spec((2 * hid, 2 * hid)), _const_spec((1, 2 * hid)),
                  _const_spec((2 * hid, 2 * hid)), _const_spec((1, 2 * hid)),
                  _const_spec((1, 2 * hid)),
                  _const_spec((2 * hid, od)), _const_spec((2 * hid, od)), _const_spec((1, od)),
                  _const_spec((2 * n1, n1)), _const_spec((n1, n2)), _const_spec((n1, n2))],
        out_specs=[pl.BlockSpec((2, n1 // K1_GRP, od // LANES, PAIRS, LANES), lambda j: (0, 0, 0, j, 0)),
                   _const_spec((8, od))],
        out_shape=[jax.ShapeDtypeStruct((2, n1 // K1_GRP, od // LANES, n2 * PAIRS, LANES), jnp.uint32),
                   jax.ShapeDtypeStruct((8, od), F32)],
        compiler_params=_cp("arbitrary"),
        name="hyena_filter",
    )(z, w1, twice(f_b1), blockdiag(f_w2), twice(f_b2), blockdiag(f_w3), twice(f_b3), twice(f_freq),
      w4f, w4b, dl, cst["f1_full"], cst["tre"], cst["tim"])


def _fft_factors(seq):
    n = 2 * seq
    n1 = 256 if n >= 32768 else 32
    n2 = n // n1
    assert n1 * n2 == n and n2 % 16 == 0 and n1 % 32 == 0
    return n1, n2


@functools.lru_cache(maxsize=None)
def _fft_consts(seq):
    n = 2 * seq
    n1, n2 = _fft_factors(seq)
    k1 = np.arange(n1, dtype=np.float64)[:, None] + 0.5
    th1 = 2.0 * np.pi * k1 * np.arange(n1, dtype=np.float64)[None] / n1
    perm = np.concatenate([np.arange(0, n1, 2), np.arange(1, n1, 2)])
    f1_full = np.concatenate([np.cos(th1)[perm], -np.sin(th1)[perm]], axis=0)
    f1_half = f1_full[:, :n1 // 2]
    tw = 2.0 * np.pi * k1 * np.arange(n2, dtype=np.float64)[None] / n
    tre, tim = np.cos(tw), -np.sin(tw)
    h2 = n2 // 2
    th2 = 2.0 * np.pi * np.arange(h2, dtype=np.float64)[:, None] * np.arange(n2, dtype=np.float64)[None] / n2
    c2, s2 = np.cos(th2), np.sin(th2)
    m2f = np.block([[c2, s2], [-s2, c2]])
    m2i = np.block([[c2.T, -s2.T], [s2.T, c2.T]])
    thb = th1[:, :n1 // 2].T
    gi = (2.0 / n) * np.concatenate([np.cos(thb)[:, perm], -np.sin(thb)[:, perm]], axis=1)
    bf = lambda a: jnp.asarray(a.astype(np.float32)).astype(BF16)
    f32 = lambda a: jnp.asarray(a.astype(np.float32))
    grp = lambda a: a.T.reshape(n2, n1 // K1_GRP, K1_GRP).transpose(1, 0, 2)
    return dict(f1_full=bf(f1_full), f1_half=bf(f1_half), tre=f32(tre[perm]), tim=f32(tim[perm]),
                tre_grp=f32(grp(tre)), tim_grp=f32(grp(tim)), m2f=bf(m2f), m2i=bf(m2i), gi=bf(gi))


def _pick_col(tbl, idx):
    lane = lax.broadcasted_iota(I32, tbl.shape, 1)
    return jnp.sum(jnp.where(lane == idx, tbl, 0.0), axis=1, keepdims=True)


K1_GRP = 16
PAIRS = K1_GRP // 2
LANES = 128
MID_TC = 512


def _unpack_pair(w, half):
    bits = lax.shift_left(w, jnp.uint32(16)) if half == 0 else (w & jnp.uint32(0xFFFF0000))
    return pltpu.bitcast(bits, F32).astype(BF16)


def _pack_pair(even, odd):
    ue = pltpu.bitcast(even.astype(BF16).astype(F32), jnp.uint32)
    uo = pltpu.bitcast(odd.astype(BF16).astype(F32), jnp.uint32)
    return lax.shift_right_logical(ue, jnp.uint32(16)) | uo


def _s1_store(a, tre_ref, tim_ref, j, o_ref, n1):
    are, aim = a[:n1], a[n1:]
    tre = _pick_col(tre_ref[...], j)
    tim = _pick_col(tim_ref[...], j)
    re = are * tre - aim * tim
    im = are * tim + aim * tre
    h = n1 // 2
    shape = (n1 // K1_GRP, PAIRS, a.shape[1])
    for part, val in enumerate((re, im)):
        words = _pack_pair(val[:h], val[h:]).reshape(shape)
        for ct in range(a.shape[1] // LANES):
            o_ref[part, :, ct] = words[:, :, ct * LANES:(ct + 1) * LANES]


def _fft_s1_kernel(f1_ref, x_ref, tre_ref, tim_ref, o_ref, *, n1):
    _s1_store(_dot(f1_ref[...], x_ref[...]), tre_ref, tim_ref, pl.program_id(0), o_ref, n1)


def _fft_stage1(x2d, f1, tre, tim, n1, n2, chans):
    r = x2d.shape[0]
    tc = D_MODEL
    g = chans // tc
    return pl.pallas_call(
        functools.partial(_fft_s1_kernel, n1=n1),
        grid=(n2, g),
        in_specs=[_const_spec((2 * n1, r)),
                  pl.BlockSpec((r, tc), lambda j, c: (0, j * g + c)),
                  _const_spec((n1, n2)), _const_spec((n1, n2))],
        out_specs=pl.BlockSpec((2, n1 // K1_GRP, tc // LANES, PAIRS, LANES), lambda j, c: (0, 0, c, j, 0)),
        out_shape=jax.ShapeDtypeStruct((2, n1 // K1_GRP, chans // LANES, n2 * PAIRS, LANES), jnp.uint32),
        compiler_params=_cp("parallel", "parallel"),
        name="fft_stage1",
    )(f1, x2d, tre, tim)


def _load_pair(b_ref, part, s, n2):
    return jnp.concatenate([b_ref[part, 0, ct, pl.ds(s, n2, stride=PAIRS), :] for ct in range(b_ref.shape[2])],
                           axis=1)


def _store_pair(o_ref, part, s, n2, words):
    for ct in range(o_ref.shape[2]):
        o_ref[part, 0, ct, pl.ds(s, n2, stride=PAIRS), :] = words[:, ct * LANES:(ct + 1) * LANES]


def _fft_s2_kernel(m2f_ref, b_ref, o_ref):
    n2 = m2f_ref.shape[0]
    for s in range(PAIRS):
        wre = _load_pair(b_ref, 0, s, n2)
        wim = _load_pair(b_ref, 1, s, n2)
        for half in range(2):
            b = jnp.concatenate([_unpack_pair(wre, half), _unpack_pair(wim, half)], axis=0)
            o_ref[2 * s + half] = _dot(m2f_ref[...], b).astype(BF16)


def _fft_stage2(b5, m2f, n1, n2, chans):
    return pl.pallas_call(
        _fft_s2_kernel,
        grid=(n1 // K1_GRP, chans // MID_TC),
        in_specs=[_const_spec((n2, 2 * n2)),
                  pl.BlockSpec((2, 1, MID_TC // LANES, n2 * PAIRS, LANES), lambda g, c: (0, g, c, 0, 0))],
        out_specs=pl.BlockSpec((K1_GRP, n2, MID_TC), lambda g, c: (g, 0, c)),
        out_shape=jax.ShapeDtypeStruct((n1, n2, chans), BF16),
        compiler_params=_cp("parallel", "parallel"),
        name="fft_stage2",
    )(m2f, b5)


def _fft_mid_kernel(m2f_ref, m2i_ref, b_ref, k_ref, inv_ref, tre_ref, tim_ref, o_ref, *, n2):
    h2 = n2 // 2
    inv = inv_ref[...]
    tre_g = tre_ref[0]
    tim_g = tim_ref[0]
    for s in range(PAIRS):
        wre = _load_pair(b_ref, 0, s, n2)
        wim = _load_pair(b_ref, 1, s, n2)
        res = []
        for half in range(2):
            kl = 2 * s + half
            b = jnp.concatenate([_unpack_pair(wre, half), _unpack_pair(wim, half)], axis=0)
            x = _dot(m2f_ref[...], b)
            kk = k_ref[kl].astype(F32) * inv
            xre, xim = x[:h2], x[h2:]
            kre, kim = kk[:h2], kk[h2:]
            y = jnp.concatenate([xre * kre - xim * kim, xre * kim + xim * kre], axis=0).astype(BF16)
            c = _dot(m2i_ref[...], y)
            cre, cim = c[:n2], c[n2:]
            tre = tre_g[:, kl:kl + 1]
            tim = tim_g[:, kl:kl + 1]
            res.append((cre * tre + cim * tim, cim * tre - cre * tim))
        _store_pair(o_ref, 0, s, n2, _pack_pair(res[0][0], res[1][0]))
        _store_pair(o_ref, 1, s, n2, _pack_pair(res[0][1], res[1][1]))


def _fft_mid(b5, kspec, inv_norm, order, cst, n1, n2):
    d = D_MODEL
    nc = d // MID_TC
    blk = pl.BlockSpec((2, 1, MID_TC // LANES, n2 * PAIRS, LANES), lambda g, c: (0, g, c, 0, 0))
    tw = pl.BlockSpec((1, n2, K1_GRP), lambda g, c: (g, 0, 0))
    return pl.pallas_call(
        functools.partial(_fft_mid_kernel, n2=n2),
        grid=(n1 // K1_GRP, nc),
        in_specs=[_const_spec((n2, 2 * n2)), _const_spec((2 * n2, n2)), blk,
                  pl.BlockSpec((K1_GRP, n2, MID_TC), lambda g, c: (g, 0, order * nc + c)),
                  pl.BlockSpec((1, MID_TC), lambda g, c: (0, order * nc + c)),
                  tw, tw],
        out_specs=blk,
        out_shape=jax.ShapeDtypeStruct((2, n1 // K1_GRP, d // LANES, n2 * PAIRS, LANES), jnp.uint32),
        compiler_params=_cp("parallel", "parallel"),
        name="fft_mid",
    )(cst["m2f"], cst["m2i"], b5, kspec, inv_norm, cst["tre_grp"], cst["tim_grp"])


def _fft_last_kernel(gi_ref, c_ref, gate_ref, z_ref, skip_ref, o_ref):
    h = c_ref.shape[1] * PAIRS

    def words(part):
        return jnp.concatenate([c_ref[part, :, ct].reshape(h, LANES) for ct in range(c_ref.shape[2])], axis=1)

    wre = words(0)
    wim = words(1)
    c = jnp.concatenate([_unpack_pair(wre, 0), _unpack_pair(wre, 1), _unpack_pair(wim, 0),
                         _unpack_pair(wim, 1)], axis=0)
    y = _dot(gi_ref[...], c)
    z = z_ref[...].astype(F32)
    o_ref[...] = (gate_ref[...].astype(F32) * (y + skip_ref[...] * z)).astype(BF16)


def _fft_last(c5, gate2d, z2d, skip_row, gi, n1, n2):
    d = D_MODEL
    r = n1 // 2
    col = pl.BlockSpec((r, d), lambda j: (0, j))
    return pl.pallas_call(
        _fft_last_kernel,
        grid=(n2,),
        in_specs=[_const_spec((r, 2 * n1)),
                  pl.BlockSpec((2, n1 // K1_GRP, d // LANES, PAIRS, LANES), lambda j: (0, 0, 0, j, 0)),
                  col, col, _const_spec((1, d))],
        out_specs=col,
        out_shape=jax.ShapeDtypeStruct((r, n2 * d), BF16),
        compiler_params=_cp("parallel"),
        name="fft_last",
    )(gi, c5, gate2d, z2d, skip_row)


def _long_conv_gate(z_in, gate, skip_row, kspec, inv_norm, order, seq):
    d = D_MODEL
    n1, n2 = _fft_factors(seq)
    cst = _fft_consts(seq)
    z2d = z_in.reshape(n1 // 2, n2 * d)
    b = _fft_stage1(z2d, cst["f1_half"], cst["tre"], cst["tim"], n1, n2, d)
    c = _fft_mid(b, kspec, inv_norm, order, cst, n1, n2)
    out = _fft_last(c, gate.reshape(n1 // 2, n2 * d), z2d, skip_row, cst["gi"], n1, n2)
    return out.reshape(seq, d)


def _filter_spectrum(seq, fparams):
    d = D_MODEL
    od = HY_ORDER * d
    n1, n2 = _fft_factors(seq)
    cst = _fft_consts(seq)
    b, asum = _filter_stage1(seq, *fparams)
    kspec = _fft_stage2(b, cst["m2f"], n1, n2, od)
    return kspec, asum


def _proj_res_kernel(a_ref, w_ref, b_ref, g_ref, x_ref, o_ref):
    y = _dot(a_ref[...], w_ref[...]) + b_ref[...]
    o_ref[...] = x_ref[...] + g_ref[...] * y


def _proj_res(a, w_bf, b_row, gate_row, xres):
    n, d = xres.shape
    tm = min(512, n)
    row = pl.BlockSpec((tm, d), lambda i: (i, 0))
    return pl.pallas_call(
        _proj_res_kernel,
        grid=(n // tm,),
        in_specs=[row, _const_spec((d, d)), _const_spec((1, d)), _const_spec((1, d)), row],
        out_specs=row,
        out_shape=jax.ShapeDtypeStruct((n, d), F32),
        compiler_params=_cp("parallel"),
        name="proj_residual",
    )(a, w_bf, b_row, gate_row, xres)


def _ffn_in_kernel(x_ref, mod_ref, wt_ref, h_ref, aff_ref):
    h = _norm_mod(x_ref[...], mod_ref[...])
    hi = h.astype(BF16)
    lo = (h - hi.astype(F32)).astype(BF16)
    wt = wt_ref[...]
    whi = wt.astype(BF16)
    wlo = (wt - whi.astype(F32)).astype(BF16)
    logits = _dot_nt(whi, hi) + (_dot_nt(whi, lo) + _dot_nt(wlo, hi))
    m = jnp.max(logits, axis=0, keepdims=True)
    p = jnp.exp(logits - m)
    aff_ref[...] = p / jnp.sum(p, axis=0, keepdims=True)
    h_ref[...] = hi


def _ffn_in(x, mod, w_router):
    n, d = x.shape
    e = N_EXPERTS
    tm = min(512, n)
    return pl.pallas_call(
        _ffn_in_kernel,
        grid=(n // tm,),
        in_specs=[pl.BlockSpec((tm, d), lambda i: (i, 0)), _const_spec((8, d)), _const_spec((e, d))],
        out_specs=[pl.BlockSpec((tm, d), lambda i: (i, 0)), pl.BlockSpec((e, tm), lambda i: (0, i))],
        out_shape=[jax.ShapeDtypeStruct((n, d), BF16), jax.ShapeDtypeStruct((e, n), F32)],
        compiler_params=_cp("parallel"),
        name="moe_router",
    )(x, mod, w_router.T)


def _select_kernel(a_ref, pos_ref, s0_ref, sel_ref, *, cap, nblk):
    e = N_EXPERTS
    bits = pltpu.bitcast(a_ref[...], I32)

    def bisect(i, thr):
        cand = thr | jnp.left_shift(jnp.int32(1), 30 - i)
        cnt = jnp.sum(jnp.where(bits >= cand, 1.0, 0.0), axis=1, keepdims=True)
        return jnp.where(cnt >= cap, cand, thr)

    thr = lax.fori_loop(0, 31, bisect, jnp.zeros((e, 1), I32))
    n_gt = jnp.sum(jnp.where(bits > thr, 1.0, 0.0), axis=1, keepdims=True)
    need = cap - n_gt
    r = lax.broadcasted_iota(I32, (TOK_BLK, TOK_BLK), 0)
    c = lax.broadcasted_iota(I32, (TOK_BLK, TOK_BLK), 1)
    upper = jnp.where(r < c, 1.0, 0.0).astype(BF16)

    def pass1(j, carry):
        sl = pl.ds(pl.multiple_of(j * TOK_BLK, TOK_BLK), TOK_BLK)
        bj = pltpu.bitcast(a_ref[:, sl], I32)
        eq = jnp.where(bj == thr, 1.0, 0.0)
        rank = _dot(eq.astype(BF16), upper) + carry
        keep = jnp.logical_or(bj > thr, jnp.logical_and(bj == thr, rank < need))
        sel_ref[:, sl] = jnp.where(keep, 1.0, 0.0)
        return carry + jnp.sum(eq, axis=1, keepdims=True)

    lax.fori_loop(0, nblk, pass1, jnp.zeros((e, 1), F32))

    def pass2(j, carry):
        sl = pl.ds(pl.multiple_of(j * TOK_BLK, TOK_BLK), TOK_BLK)
        s = sel_ref[:, sl]
        slot = _dot(s.astype(BF16), upper) + carry
        pos_ref[:, sl] = jnp.where(s > 0.5, slot, -1.0).astype(I32)
        s0_ref[j] = jnp.broadcast_to(carry, (e, 128)).astype(I32)
        return carry + jnp.sum(s, axis=1, keepdims=True)

    total = lax.fori_loop(0, nblk, pass2, jnp.zeros((e, 1), F32))
    s0_ref[nblk] = jnp.broadcast_to(total, (e, 128)).astype(I32)


def _select(aff_t, cap):
    e, n = aff_t.shape
    nblk = n // TOK_BLK
    return pl.pallas_call(
        functools.partial(_select_kernel, cap=cap, nblk=nblk),
        out_shape=[jax.ShapeDtypeStruct((e, n), I32), jax.ShapeDtypeStruct((nblk + 1, e, 128), I32)],
        scratch_shapes=[pltpu.VMEM((e, n), F32)],
        compiler_params=pltpu.CompilerParams(vmem_limit_bytes=VMEM_LIMIT),
        name="moe_select",
    )(aff_t)


def _block_windows(s0_ref, blk, e_idx):
    s0 = s0_ref[blk * N_EXPERTS + e_idx]
    s1 = s0_ref[(blk + 1) * N_EXPERTS + e_idx]
    start = lax.shift_left(lax.shift_right_logical(s0, 4), 4)
    nwin = jnp.where(s1 > s0, lax.shift_right_logical(s1 - start + (WIN - 1), WIN_SHIFT), 0)
    return start, nwin


def _token_block(i):
    start = i * TOK_BLK
    return pl.ds(start if isinstance(i, int) else pl.multiple_of(start, TOK_BLK), TOK_BLK)


def _gather_kernel(s0_ref, pos_ref, h_ref, xe_ref, *, sub):
    e_idx = pl.program_id(0)
    c = pl.program_id(1)

    @pl.when(c == 0)
    def _():
        xe_ref[...] = jnp.zeros_like(xe_ref)

    rows = lax.broadcasted_iota(I32, (WIN, TOK_BLK), 0)

    def window(i, base):
        tok = _token_block(i)
        base = pl.multiple_of(base, ROW_ALIGN)
        onehot = jnp.where(rows == pos_ref[0, :, tok] - base, 1.0, 0.0).astype(BF16)
        got = _dot(onehot, h_ref[tok, :]).astype(BF16)
        win = pl.ds(base, WIN)
        xe_ref[0, win, :] = xe_ref[0, win, :] + got

    plan = [_block_windows(s0_ref, c * sub + i, e_idx) for i in range(sub)]
    single = functools.reduce(jnp.logical_and, [nwin <= 1 for _, nwin in plan])

    @pl.when(single)
    def _():
        for i, (start, _) in enumerate(plan):
            window(i, start)

    @pl.when(jnp.logical_not(single))
    def _():
        def body(i, carry):
            start, nwin = _block_windows(s0_ref, c * sub + i, e_idx)
            return lax.fori_loop(0, nwin, lambda w, cc: (window(i, start + w * WIN), cc)[1], carry)

        lax.fori_loop(0, sub, body, 0)


def _gather(s0_flat, pos, h, cap_pad):
    e, n = pos.shape
    d = h.shape[1]
    chunk = min(2048, n)
    sub = chunk // TOK_BLK
    return pl.pallas_call(
        functools.partial(_gather_kernel, sub=sub),
        grid_spec=pltpu.PrefetchScalarGridSpec(
            num_scalar_prefetch=1,
            grid=(e, n // chunk),
            in_specs=[pl.BlockSpec((1, 1, chunk), lambda ei, c, s0: (ei, 0, c)),
                      pl.BlockSpec((chunk, d), lambda ei, c, s0: (c, 0))],
            out_specs=pl.BlockSpec((1, cap_pad, d), lambda ei, c, s0: (ei, 0, 0)),
        ),
        out_shape=jax.ShapeDtypeStruct((e, cap_pad, d), BF16),
        compiler_params=_cp("parallel", "arbitrary"),
        name="moe_gather",
    )(s0_flat, pos.reshape(e, 1, n), h)


def _expert_kernel(x_ref, wg_ref, wu_ref, wd_ref, y_ref, wg_scr, wu_scr, wd_scr, *, n_real):
    j = pl.program_id(1)

    @pl.when(j == 0)
    def _():
        wg_scr[...] = wg_ref[0].astype(BF16)
        wu_scr[...] = wu_ref[0].astype(BF16)
        wd_scr[...] = wd_ref[0].astype(BF16)

    @pl.when(j < n_real)
    def _():
        x = x_ref[0]
        g = _dot(x, wg_scr[...])
        u = _dot(x, wu_scr[...])
        a = (g / (1.0 + jnp.exp(-g))) * u
        y_ref[0] = _dot(a.astype(BF16), wd_scr[...]).astype(BF16)

    @pl.when(j >= n_real)
    def _():
        y_ref[0] = jnp.zeros_like(y_ref[0])


def _experts(xe, wg, wu, wd, layer, cap):
    e, cap_pad, d = xe.shape
    f = wg.shape[3]
    tm = min(256, cap)
    tile = pl.BlockSpec((1, tm, d), lambda ei, j: (ei, j, 0))
    return pl.pallas_call(
        functools.partial(_expert_kernel, n_real=cap // tm),
        grid=(e, cap_pad // tm),
        in_specs=[tile,
                  pl.BlockSpec((None, 1, d, f), lambda ei, j: (layer, ei, 0, 0)),
                  pl.BlockSpec((None, 1, d, f), lambda ei, j: (layer, ei, 0, 0)),
                  pl.BlockSpec((None, 1, f, d), lambda ei, j: (layer, ei, 0, 0))],
        out_specs=tile,
        out_shape=jax.ShapeDtypeStruct((e, cap_pad, d), BF16),
        scratch_shapes=[pltpu.VMEM((d, f), BF16), pltpu.VMEM((d, f), BF16), pltpu.VMEM((f, d), BF16)],
        compiler_params=_cp("parallel", "arbitrary"),
        name="moe_experts",
    )(xe, wg, wu, wd)


def _combine_kernel(s0_ref, post_ref, gt_ref, y_ref, x_ref, gate_ref, o_ref, *, sub):
    c = pl.program_id(0)
    e_idx = pl.program_id(1)

    @pl.when(e_idx == 0)
    def _():
        o_ref[...] = jnp.zeros_like(o_ref)

    lanes = lax.broadcasted_iota(I32, (TOK_BLK, WIN), 1)
    elane = lax.broadcasted_iota(I32, (TOK_BLK, N_EXPERTS), 1) == e_idx

    def window(i, base):
        tok = _token_block(i)
        base = pl.multiple_of(base, ROW_ALIGN)
        pcol = jnp.sum(jnp.where(elane, post_ref[tok, :].astype(F32), 0.0), axis=1, keepdims=True)
        gcol = jnp.sum(jnp.where(elane, gt_ref[tok, :], 0.0), axis=1, keepdims=True)
        rel = pcol.astype(I32) - base
        onehot = jnp.where(lanes == rel, 1.0, 0.0).astype(BF16)
        strip = y_ref[0, pl.ds(base, WIN), :]
        o_ref[tok, :] = o_ref[tok, :] + gcol * _dot(onehot, strip)

    plan = [_block_windows(s0_ref, c * sub + i, e_idx) for i in range(sub)]
    single = functools.reduce(jnp.logical_and, [nwin <= 1 for _, nwin in plan])

    @pl.when(single)
    def _():
        for i, (start, _) in enumerate(plan):
            window(i, start)

    @pl.when(jnp.logical_not(single))
    def _():
        def body(i, carry):
            start, nwin = _block_windows(s0_ref, c * sub + i, e_idx)
            return lax.fori_loop(0, nwin, lambda w, cc: (window(i, start + w * WIN), cc)[1], carry)

        lax.fori_loop(0, sub, body, 0)

    @pl.when(e_idx == N_EXPERTS - 1)
    def _():
        o_ref[...] = x_ref[...] + gate_ref[...] * o_ref[...]


def _combine(s0_flat, pos_t, g_t, y, xres, gate_row):
    n, d = xres.shape
    e, cap_pad, _ = y.shape
    chunk = min(2048, n)
    sub = chunk // TOK_BLK
    return pl.pallas_call(
        functools.partial(_combine_kernel, sub=sub),
        grid_spec=pltpu.PrefetchScalarGridSpec(
            num_scalar_prefetch=1,
            grid=(n // chunk, e),
            in_specs=[pl.BlockSpec((chunk, e), lambda c, ei, s0: (c, 0)),
                      pl.BlockSpec((chunk, e), lambda c, ei, s0: (c, 0)),
                      pl.BlockSpec((1, cap_pad, d), lambda c, ei, s0: (ei, 0, 0)),
                      pl.BlockSpec((chunk, d), lambda c, ei, s0: (c, 0)),
                      pl.BlockSpec((1, d), lambda c, ei, s0: (0, 0))],
            out_specs=pl.BlockSpec((chunk, d), lambda c, ei, s0: (c, 0)),
        ),
        out_shape=jax.ShapeDtypeStruct((n, d), F32),
        compiler_params=_cp("parallel", "arbitrary"),
        name="moe_combine",
    )(s0_flat, pos_t, g_t, y, xres, gate_row)


def _moe_block(x, mod, gate_row, w_router, wg, wu, wd, layer):
    n = x.shape[0]
    cap = EC_CAPACITY * n // N_EXPERTS
    cap_pad = cap + TOK_BLK
    h, aff_t = _ffn_in(x, mod, w_router)
    pos, s0 = _select(aff_t, cap)
    s0_flat = s0[:, :, 0].reshape(-1)
    xe = _gather(s0_flat, pos, h, cap_pad)
    y = _experts(xe, wg, wu, wd, layer, cap)
    return _combine(s0_flat, pos.T, aff_t.T, y, x, gate_row)


QK_SCALE = (DA_HEAD_DIM ** -0.5) * math.log2(math.e)


def _split_bf16(x):
    hi = x.astype(BF16)
    return hi, (x - hi.astype(F32)).astype(BF16)


def _group_rms(u, gsel_ref, gain, eps):
    g = gsel_ref[...]
    hi, lo = _split_bf16(u * u)
    r = lax.rsqrt((_dot(hi, g) + _dot(lo, g)) * (1.0 / DA_HEAD_DIM) + eps)
    rhi, rlo = _split_bf16(r)
    return u * (_dot_nt(rhi, g) + _dot_nt(rlo, g)) * gain


def _rope(u, cos, sin_signed):
    d = u.shape[1]
    half = ROPE_AXIS_DIM // 2
    lane = lax.broadcasted_iota(I32, u.shape, 1)
    first = (lane & half) == 0
    swapped = jnp.where(first, pltpu.roll(u, d - half, 1), pltpu.roll(u, half, 1))
    return u * cos + swapped * sin_signed


def _qkv_kernel(*refs, rope):
    if rope:
        x_ref, mod_ref, w_ref, gsum_ref, qn_ref, kn_ref, cos_ref, sin_ref, q_ref, k_ref, v_ref = refs
    else:
        x_ref, mod_ref, w_ref, gsum_ref, qn_ref, kn_ref, q_ref, k_ref, v_ref = refs
    d = D_MODEL
    h = _norm_mod(x_ref[...], mod_ref[...]).astype(BF16)
    if rope:
        reps = d // cos_ref.shape[1]
        cos = _lane_tile(cos_ref[...], reps)
        sin = _lane_tile(sin_ref[...], reps)
    for part, (o_ref, gain_ref) in enumerate(((q_ref, qn_ref), (k_ref, kn_ref))):
        u = _dot(h, w_ref[:, part * d:(part + 1) * d])
        u = _group_rms(u, gsum_ref, gain_ref[...], NORM_EPS)
        if rope:
            u = _rope(u, cos, sin)
        if part == 0:
            u = u * QK_SCALE
        o_ref[...] = u.astype(BF16)
    v_ref[...] = _dot(h, w_ref[:, 2 * d:]).astype(BF16)


def _qkv(x, mod, w_bf, gsum, qn_row, kn_row, cos=None, sin=None):
    n, d = x.shape
    tm = min(512, n)
    rope = cos is not None
    row = pl.BlockSpec((tm, d), lambda i: (i, 0))
    in_specs = [row, _const_spec((8, d)), _const_spec((d, 3 * d)), _const_spec(gsum.shape),
                _const_spec((1, d)), _const_spec((1, d))]
    args = [x, mod, w_bf, gsum, qn_row, kn_row]
    if rope:
        tw = cos.shape[1]
        in_specs += [pl.BlockSpec((tm, tw), lambda i: (i, 0)), pl.BlockSpec((tm, tw), lambda i: (i, 0))]
        args += [cos, sin]
    out = jax.ShapeDtypeStruct((n, d), BF16)
    return pl.pallas_call(
        functools.partial(_qkv_kernel, rope=rope),
        grid=(n // tm,),
        in_specs=in_specs,
        out_specs=[row, row, row],
        out_shape=[out, out, out],
        compiler_params=_cp("parallel"),
        name="attn_qkv",
    )(*args)


NEG_BIG = -1e30


def _attn_kernel(q_ref, k_ref, v_ref, lam_ref, sub_ref, o_ref, m_scr, acc_scr, sa_scr, sb_scr, ma_scr, mb_scr,
                 pa_scr, pb_scr, vlast_scr, *,
                 lam_init):
    j = pl.program_id(1)
    nj = pl.num_programs(1)
    hd, vd = DA_HEAD_DIM, DA_V_DIM
    tk = k_ref.shape[0]

    last = DA_HEADS - 1

    @pl.when(j == 0)
    def _():
        m_scr[...] = jnp.full_like(m_scr, NEG_BIG)
        acc_scr[...] = jnp.zeros_like(acc_scr)
        sb_scr[...] = jnp.full_like(sb_scr, 2.0 * NEG_BIG)
        mb_scr[...] = jnp.full_like(mb_scr, 2.0 * NEG_BIG)
        vlast_scr[...] = jnp.zeros_like(vlast_scr)

    lane = lax.broadcasted_iota(I32, (q_ref.shape[0], vd), 1)
    ones_col = jnp.where(lax.broadcasted_iota(I32, (tk, vd), 1) == 0, 1.0, 0.0).astype(BF16)

    def head_cols(h):
        return pl.ds(pl.multiple_of(h * vd, vd), vd)

    def scores(h, comp, s_ref, mx_ref):
        qb = q_ref[:, head_cols(h)]
        in_comp = (lane >= comp * hd) & (lane < (comp + 1) * hd)
        qm = jnp.where(in_comp, qb, jnp.zeros_like(qb))
        s = _dot_nt(qm, k_ref[:, head_cols(h)])
        s_ref[...] = s
        mx_ref[...] = jnp.broadcast_to(jnp.max(s, axis=1, keepdims=True), mx_ref.shape)

    def softmax_step(h, comp, s_ref, mx_ref, p_ref):
        idx = 2 * h + comp
        m_prev = m_scr[idx]
        m_new = jnp.maximum(m_prev, mx_ref[...])
        alpha = _lane_tile(jnp.exp2(m_prev - m_new), 2)
        p_ref[...] = jnp.exp2(s_ref[...] - m_new[:, 0:1]).astype(BF16)
        m_scr[idx] = m_new
        acc_scr[idx] = alpha * acc_scr[idx]

    def pv_step(h, comp, p_ref, v_block=None):
        idx = 2 * h + comp
        if v_block is None:
            v_block = v_ref[:, head_cols(h)]
        acc_scr[idx] = acc_scr[idx] + _dot(p_ref[...], jnp.concatenate([v_block, ones_col], axis=1))

    def head(h, carry):
        scores(h, 1, sb_scr, mb_scr)
        pv_step(h - 1, 1, pb_scr)
        softmax_step(h, 0, sa_scr, ma_scr, pa_scr)
        scores(h + 1, 0, sa_scr, ma_scr)
        pv_step(h, 0, pa_scr)
        softmax_step(h, 1, sb_scr, mb_scr, pb_scr)
        return carry

    def head_pair(g, carry):
        return head(2 * g + 2, head(2 * g + 1, carry))

    scores(0, 0, sa_scr, ma_scr)
    softmax_step(last, 1, sb_scr, mb_scr, pb_scr)
    scores(0, 1, sb_scr, mb_scr)
    pv_step(last, 1, pb_scr, vlast_scr[...])
    softmax_step(0, 0, sa_scr, ma_scr, pa_scr)
    scores(1, 0, sa_scr, ma_scr)
    pv_step(0, 0, pa_scr)
    softmax_step(0, 1, sb_scr, mb_scr, pb_scr)
    lax.fori_loop(0, (DA_HEADS - 2) // 2, head_pair, 0)
    scores(last, 1, sb_scr, mb_scr)
    pv_step(last - 1, 1, pb_scr)
    softmax_step(last, 0, sa_scr, ma_scr, pa_scr)
    pv_step(last, 0, pa_scr)
    vlast_scr[...] = v_ref[:, last * vd:(last + 1) * vd]

    @pl.when(j == nj - 1)
    def _():
        softmax_step(last, 1, sb_scr, mb_scr, pb_scr)
        pv_step(last, 1, pb_scr)
        lp = lam_ref[...]
        lam = (jnp.exp(jnp.sum(lp[0:1] * lp[1:2], axis=1, keepdims=True))
               - jnp.exp(jnp.sum(lp[2:3] * lp[3:4], axis=1, keepdims=True)) + lam_init)
        for h in range(DA_HEADS):
            a0 = acc_scr[2 * h]
            a1 = acc_scr[2 * h + 1]
            o = a0[:, :vd] / a0[:, vd:vd + 1] - lam * (a1[:, :vd] / a1[:, vd:vd + 1])
            ms = jnp.mean(o * o, axis=1, keepdims=True)
            o = o * lax.rsqrt(ms + SUBLN_EPS) * (sub_ref[...] * (1.0 - lam_init))
            o_ref[:, h * vd:(h + 1) * vd] = o.astype(BF16)


def _attention(q, k_all, v_all, lam_rows, subln_row, lam_init):
    n, d = q.shape
    nk = k_all.shape[0]
    tq = min(512, n)
    tk = 1280 if nk % 1280 == 0 else 256
    assert nk % tk == 0
    nc = 2 * DA_HEADS
    return pl.pallas_call(
        functools.partial(_attn_kernel, lam_init=lam_init),
        grid=(n // tq, nk // tk),
        in_specs=[pl.BlockSpec((tq, d), lambda i, j: (i, 0)),
                  pl.BlockSpec((tk, d), lambda i, j: (j, 0)),
                  pl.BlockSpec((tk, d), lambda i, j: (j, 0)),
                  _const_spec((8, DA_HEAD_DIM)), _const_spec((1, DA_V_DIM))],
        out_specs=pl.BlockSpec((tq, d), lambda i, j: (i, 0)),
        out_shape=jax.ShapeDtypeStruct((n, d), BF16),
        scratch_shapes=[pltpu.VMEM((nc, tq, DA_V_DIM), F32), pltpu.VMEM((nc, tq, 2 * DA_V_DIM), F32),
                        pltpu.VMEM((tq, tk), F32), pltpu.VMEM((tq, tk), F32),
                        pltpu.VMEM((tq, DA_V_DIM), F32), pltpu.VMEM((tq, DA_V_DIM), F32),
                        pltpu.VMEM((tq, tk), BF16), pltpu.VMEM((tq, tk), BF16),
                        pltpu.VMEM((tk, DA_V_DIM), BF16)],
        compiler_params=_cp("parallel", "arbitrary"),
        name="diff_attention",
    )(q, k_all, v_all, lam_rows, subln_row)


def _rope_tables(n):
    rows = n // GRID_W
    inv = ROPE_THETA ** (-jnp.arange(0, ROPE_AXIS_DIM, 2, dtype=F32) / ROPE_AXIS_DIM)
    ar = jnp.arange(rows, dtype=F32)[:, None] * inv[None]
    ac = jnp.arange(GRID_W, dtype=F32)[:, None] * inv[None]
    nf = inv.shape[0]
    by_row = lambda a: jnp.broadcast_to(a[:, None, :], (rows, GRID_W, nf))
    by_col = lambda a: jnp.broadcast_to(a[None, :, :], (rows, GRID_W, nf))
    cr, sr, cc, sc = by_row(jnp.cos(ar)), by_row(jnp.sin(ar)), by_col(jnp.cos(ac)), by_col(jnp.sin(ac))
    cos = jnp.concatenate([cr, cr, cc, cc] * 2, axis=2).reshape(n, 8 * nf)
    sin = jnp.concatenate([-sr, sr, -sc, sc] * 2, axis=2).reshape(n, 8 * nf)
    return cos, sin


def _hyena_layer(x, mod, gate_row, kspec, inv_norm, w_in_bf, b_in, conv_w, conv_b, skip, w_out_bf, b_out):
    seq = x.shape[0]
    v, x1, x2 = _hy_in(x, mod, w_in_bf, b_in, conv_w, conv_b)
    z = _long_conv_gate(v, x1, skip[0:1], kspec, inv_norm, 0, seq)
    z = _long_conv_gate(z, x2, skip[1:2], kspec, inv_norm, 1, seq)
    return _proj_res(z, w_out_bf, b_out.reshape(1, -1), gate_row, x)


def kernel(x, c, ctx, c_ctx, ada_w, ada_b, norm_mix, norm_ffn, hy_w_in, hy_b_in, hy_conv_w, hy_conv_b, hy_f_w1, hy_f_b1, hy_f_w2, hy_f_b2, hy_f_w3, hy_f_b3, hy_f_w4, hy_f_freq, hy_skip, hy_w_out, hy_b_out, da_w_qkv, da_q_norm, da_k_norm, da_lam_q1, da_lam_k1, da_lam_q2, da_lam_k2, da_subln, da_w_out, moe_router, moe_w_gate, moe_w_up, moe_w_down):
    d = D_MODEL
    depth = ada_w.shape[0]
    assert x.shape[0] == 1 and x.shape[2] == d
    xs = x[0]
    cs = ctx[0]
    cond8 = jnp.concatenate([c[0:1], c_ctx[None], jnp.zeros((6, d), F32)], axis=0)
    mods = _adaln(cond8, ada_w, ada_b)

    def mod_slice(i, row, k):
        return mods[i, row, k * d:(k + 1) * d][None]

    for i in range(depth):
        last = i == depth - 1
        j = i // 2
        mix_x = _mod_rows(norm_mix[i], mods[i], 0, 0)
        mix_c = _mod_rows(norm_mix[i], mods[i], 1, 0)
        if i % 2 == 0:
            fparams = (hy_f_w1[j], hy_f_b1[j], hy_f_w2[j], hy_f_b2[j], hy_f_w3[j], hy_f_b3[j], hy_f_w4[j],
                       hy_f_freq[j])
            shared = (hy_w_in[j].astype(BF16), hy_b_in[j], hy_conv_w[j], hy_conv_b[j], hy_skip[j],
                      hy_w_out[j].astype(BF16), hy_b_out[j])
            kspec, asum = _filter_spectrum(xs.shape[0], fparams)
            inv_norm = 1.0 / (asum[0:1] + HY_FILTER_EPS)
            new_x = _hyena_layer(xs, mix_x, mod_slice(i, 0, 2), kspec, inv_norm, *shared)
            if not last:
                kspec_c, asum_c = _filter_spectrum(cs.shape[0], fparams)
                inv_c = 1.0 / (asum_c[0:1] + HY_FILTER_EPS)
                cs = _hyena_layer(cs, mix_c, mod_slice(i, 1, 2), kspec_c, inv_c, *shared)
            xs = new_x
        else:
            lam_init = 0.8 - 0.6 * math.exp(-0.3 * i)
            w_qkv = da_w_qkv[j].astype(BF16)
            gidx = np.arange(d) // DA_HEAD_DIM
            gsum = jnp.asarray((gidx[:, None] == np.arange(128)[None]).astype(np.float32)).astype(BF16)
            qn = jnp.tile(da_q_norm[j], 2 * DA_HEADS)[None]
            kn = jnp.tile(da_k_norm[j], 2 * DA_HEADS)[None]
            cos, sin = _rope_tables(xs.shape[0])
            qx, kx, vx = _qkv(xs, mix_x, w_qkv, gsum, qn, kn, cos, sin)
            qc, kc, vc = _qkv(cs, mix_c, w_qkv, gsum, qn, kn)
            k_all = jnp.concatenate([kc, kx], axis=0)
            v_all = jnp.concatenate([vc, vx], axis=0)
            lam_rows = jnp.concatenate([da_lam_q1[j][None], da_lam_k1[j][None], da_lam_q2[j][None],
                                        da_lam_k2[j][None], jnp.zeros((4, DA_HEAD_DIM), F32)], axis=0)
            w_out = da_w_out[j].astype(BF16)
            zero_b = jnp.zeros((1, d), F32)
            ox = _attention(qx, k_all, v_all, lam_rows, da_subln[j][None], lam_init)
            new_x = _proj_res(ox, w_out, zero_b, mod_slice(i, 0, 2), xs)
            if not last:
                oc = _attention(qc, kc, vc, lam_rows, da_subln[j][None], lam_init)
                cs = _proj_res(oc, w_out, zero_b, mod_slice(i, 1, 2), cs)
            xs = new_x
        experts = (moe_w_gate, moe_w_up, moe_w_down, i)
        if not last:
            cs = _moe_block(cs, _mod_rows(norm_ffn[i], mods[i], 1, 3), mod_slice(i, 1, 5), moe_router[i], *experts)
        xs = _moe_block(xs, _mod_rows(norm_ffn[i], mods[i], 0, 3), mod_slice(i, 0, 5), moe_router[i], *experts)
    return xs[None]
```

```python
import functools
import math

import jax
import jax.numpy as jnp
import numpy as np
from jax import lax
from jax.experimental import pallas as pl
from jax.experimental.pallas import tpu as pltpu

F32 = jnp.float32
BF16 = jnp.bfloat16
I32 = jnp.int32
HIGHEST = lax.Precision.HIGHEST

D_MODEL = 1024
N_MOD = 6
NORM_EPS = 1e-6
GRID_W = 64
HY_ORDER = 2
HY_SHORT = 3
HY_EMB_BANDS = 16
HY_EMB_DIM = 1 + 2 * HY_EMB_BANDS
HY_FILTER_HIDDEN = 64
HY_DECAY_FAST = 0.3
HY_DECAY_SLOW = 1.5
HY_DECAY_TARGET = 1e-2
HY_FILTER_EPS = 1e-6
Z_SIGN_COL = 33
DA_HEADS = 8
DA_HEAD_DIM = 64
DA_V_DIM = 128
ROPE_AXIS_DIM = 32
ROPE_THETA = 10000.0
SUBLN_EPS = 1e-5
N_EXPERTS = 16
EC_CAPACITY = 2
D_EXPERT = 1024
TOK_BLK = 256
ROW_ALIGN = 16
WIN_SHIFT = 7
WIN = 1 << WIN_SHIFT

VMEM_LIMIT = 56 * 1024 * 1024


def _cp(*sem):
    return pltpu.CompilerParams(dimension_semantics=sem, vmem_limit_bytes=VMEM_LIMIT)


def _const_spec(shape):
    nd = len(shape)
    return pl.BlockSpec(shape, lambda *_: (0,) * nd)


def _dot(a, b):
    return jnp.dot(a, b, preferred_element_type=F32)


def _dot_nt(a, b):
    return lax.dot_general(a, b, (((1,), (1,)), ((), ())), preferred_element_type=F32)


def _norm_mod(x, mod, eps=NORM_EPS):
    ms = jnp.mean(x * x, axis=-1, keepdims=True)
    return x * lax.rsqrt(ms + eps) * (mod[0:1] * (1.0 + mod[1:2])) + mod[2:3]


def _lane_tile(x, reps):
    return jnp.concatenate([x] * reps, axis=1) if reps > 1 else x


def _adaln_kernel(c_ref, w_ref, b_ref, o_ref):
    c = c_ref[...]
    s = c / (1.0 + jnp.exp(-c))
    o_ref[0] = jnp.dot(s, w_ref[0], precision=HIGHEST, preferred_element_type=F32) + b_ref[0]


def _adaln(cond8, ada_w, ada_b):
    depth, d, nout = ada_w.shape
    tn = 1536
    return pl.pallas_call(
        _adaln_kernel,
        grid=(depth, nout // tn),
        in_specs=[_const_spec((8, d)),
                  pl.BlockSpec((1, d, tn), lambda l, j: (l, 0, j)),
                  pl.BlockSpec((1, 1, tn), lambda l, j: (l, 0, j))],
        out_specs=pl.BlockSpec((1, 8, tn), lambda l, j: (l, 0, j)),
        out_shape=jax.ShapeDtypeStruct((depth, 8, nout), F32),
        compiler_params=_cp("parallel", "parallel"),
        name="adaln",
    )(cond8, ada_w, ada_b.reshape(depth, 1, nout))


def _mod_rows(norm_g, mods, row, k0):
    d = D_MODEL
    shift = mods[row, k0 * d:(k0 + 1) * d]
    scale = mods[row, (k0 + 1) * d:(k0 + 2) * d]
    z = jnp.zeros((5, d), F32)
    return jnp.concatenate([norm_g[None], scale[None], shift[None], z], axis=0)


HALO = 16


def _hy_in_kernel(x_ref, xp_ref, xn_ref, mod_ref, w_ref, b_ref, cw_ref, v_ref, x1_ref, x2_ref, *, tm, n_rows):
    i = pl.program_id(0)
    mod = mod_ref[...]
    hm = _norm_mod(x_ref[...], mod).astype(BF16)
    hp = _norm_mod(xp_ref[...], mod).astype(BF16)
    hn = _norm_mod(xn_ref[...], mod).astype(BF16)
    hcat = jnp.concatenate([hp, hm, hn], axis=0)
    row = lax.broadcasted_iota(I32, (tm + 2 * HALO, 1), 0) + (i * tm - HALO)
    valid = jnp.logical_and(row >= 0, row < n_rows)
    d = D_MODEL
    for c, o_ref in enumerate((v_ref, x1_ref, x2_ref)):
        u = _dot(hcat, w_ref[:, c * d:(c + 1) * d]) + b_ref[:, c * d:(c + 1) * d]
        u = jnp.where(valid, u, 0.0)
        cw = cw_ref[:, c * d:(c + 1) * d]
        y = (cw[3:4] + cw[0:1] * u[HALO - 1:HALO - 1 + tm] + cw[1:2] * u[HALO:HALO + tm]
             + cw[2:3] * u[HALO + 1:HALO + 1 + tm])
        o_ref[...] = y.astype(BF16)


def _hy_in(x, mod, w_bf, b_in, conv_w, conv_b):
    n, d = x.shape
    tm = min(512, n)
    nh = n // HALO
    cw = jnp.concatenate([conv_w, conv_b[None], jnp.zeros((4, 3 * d), F32)], axis=0)
    out = jax.ShapeDtypeStruct((n, d), BF16)
    row_spec = pl.BlockSpec((tm, d), lambda i: (i, 0))
    return pl.pallas_call(
        functools.partial(_hy_in_kernel, tm=tm, n_rows=n),
        grid=(n // tm,),
        in_specs=[row_spec,
                  pl.BlockSpec((HALO, d), lambda i: (jnp.maximum(i * (tm // HALO) - 1, 0), 0)),
                  pl.BlockSpec((HALO, d), lambda i: (jnp.minimum((i + 1) * (tm // HALO), nh - 1), 0)),
                  _const_spec((8, d)), _const_spec((d, 3 * d)), _const_spec((1, 3 * d)),
                  _const_spec((8, 3 * d))],
        out_specs=[row_spec, row_spec, row_spec],
        out_shape=[out, out, out],
        compiler_params=_cp("parallel"),
        name="hyena_in",
    )(x, x, x, mod, w_bf, b_in.reshape(1, 3 * d), cw)


Z_HALF = 64


def _filter_s1_kernel(z_ref, w1_ref, b1_ref, w2_ref, b2_ref, w3_ref, b3_ref, fr_ref, w4f_ref, w4b_ref, dl_ref,
                      f1_ref, tre_ref, tim_ref, o_ref, asum_ref, *, n1):
    j = pl.program_id(0)
    z = z_ref[...]
    fr = fr_ref[...]

    def lin(a, w_ref, b_ref):
        return jnp.dot(a, w_ref[...], precision=HIGHEST, preferred_element_type=F32) + b_ref[...]

    hid = jnp.sin(fr * lin(z, w1_ref, b1_ref))
    hid = jnp.sin(fr * lin(hid, w2_ref, b2_ref))
    hid = jnp.sin(fr * lin(hid, w3_ref, b3_ref)).astype(BF16)
    dl = dl_ref[...]

    def taps(w4_ref, col):
        t = z[:, col:col + 1]
        sgn = z[:, col + Z_SIGN_COL:col + Z_SIGN_COL + 1]
        return _dot(hid, w4_ref[...]) * jnp.exp(-t * dl) * sgn

    k = jnp.concatenate([taps(w4f_ref, 0), taps(w4b_ref, Z_HALF)], axis=0)

    @pl.when(j == 0)
    def _():
        asum_ref[...] = jnp.zeros_like(asum_ref)

    asum_ref[0:1, :] += jnp.sum(jnp.abs(k), axis=0, keepdims=True)
    _s1_store(_dot(f1_ref[...], k.astype(BF16)), tre_ref, tim_ref, j, o_ref, n1)


def _filter_positions(seq, n1, n2):
    i = np.arange(n1 // 2)[None, :]
    j = np.arange(n2)[:, None]
    bands = np.linspace(1e-4, HY_EMB_BANDS - 1, HY_EMB_BANDS)
    z = np.zeros((n2, n1 // 2, 2 * Z_HALF), np.float64)
    for col, r in ((0, i * n2 + j), (Z_HALF, i * n2 + j + seq)):
        pos = np.minimum(np.where(r < seq, r, 2 * seq - r), seq - 1).astype(np.float64)
        w = 2.0 * np.pi * pos / seq
        z[:, :, col] = pos / (seq - 1)
        z[:, :, col + 1:col + 1 + HY_EMB_BANDS] = np.cos(w[..., None] * bands)
        z[:, :, col + 1 + HY_EMB_BANDS:col + HY_EMB_DIM] = -np.sin(w[..., None] * bands)
        z[:, :, col + Z_SIGN_COL] = np.where(r < seq, 1.0, np.where(r == seq, 0.0, -1.0))
    return jnp.asarray(z.reshape(n2 * (n1 // 2), 2 * Z_HALF).astype(np.float32))


def _filter_stage1(seq, f_w1, f_b1, f_w2, f_b2, f_w3, f_b3, f_w4, f_freq):
    d, hid = D_MODEL, HY_FILTER_HIDDEN
    od = HY_ORDER * d
    n1, n2 = _fft_factors(seq)
    cst = _fft_consts(seq)
    z = _filter_positions(seq, n1, n2)
    zw = 2 * Z_HALF
    zero = jnp.zeros((hid, hid), F32)
    pad = jnp.zeros((Z_HALF - HY_EMB_DIM, hid), F32)
    w1 = jnp.concatenate([jnp.concatenate([f_w1, pad], axis=0), jnp.zeros((Z_HALF, hid), F32)], axis=0)
    w1 = jnp.concatenate([w1, jnp.roll(w1, Z_HALF, axis=0)], axis=1)
    blockdiag = lambda w: jnp.concatenate([jnp.concatenate([w, zero], axis=1),
                                           jnp.concatenate([zero, w], axis=1)], axis=0)
    twice = lambda v: jnp.tile(v, 2).reshape(1, 2 * hid)
    w4d = f_w4.reshape(hid, HY_ORDER, 2, d).transpose(2, 0, 1, 3).reshape(2, hid, od)
    zrows = jnp.zeros((hid, od), F32)
    w4f = jnp.concatenate([w4d[0], zrows], axis=0).astype(BF16)
    w4b = jnp.concatenate([zrows, w4d[1]], axis=0).astype(BF16)
    max_decay = math.log(HY_DECAY_TARGET) / HY_DECAY_FAST
    min_decay = math.log(HY_DECAY_TARGET) / HY_DECAY_SLOW
    deltas = np.abs(np.linspace(min_decay, max_decay, d, dtype=np.float32))
    dl = jnp.asarray(np.tile(deltas, HY_ORDER)[None])
    r = n1 // 2
    return pl.pallas_call(
        functools.partial(_filter_s1_kernel, n1=n1),
        grid=(n2,),
        in_specs=[pl.BlockSpec((r, zw), lambda j: (j, 0)),
                  _const_spec((zw, 2 * hid)), _const_spec((1, 2 * hid)),
                  _const_spec((2 * hid, 2 * hid)), _const_spec((1, 2 * hid)),
                  _const_spec((2 * hid, 2 * hid)), _const_spec((1, 2 * hid)),
                  _const_spec((1, 2 * hid)),
                  _const_spec((2 * hid, od)), _const_spec((2 * hid, od)), _const_spec((1, od)),
                  _const_spec((2 * n1, n1)), _const_spec((n1, n2)), _const_spec((n1, n2))],
        out_specs=[pl.BlockSpec((2, n1 // K1_GRP, od // LANES, PAIRS, LANES), lambda j: (0, 0, 0, j, 0)),
                   _const_spec((8, od))],
        out_shape=[jax.ShapeDtypeStruct((2, n1 // K1_GRP, od // LANES, n2 * PAIRS, LANES), jnp.uint32),
                   jax.ShapeDtypeStruct((8, od), F32)],
        compiler_params=_cp("arbitrary"),
        name="hyena_filter",
    )(z, w1, twice(f_b1), blockdiag(f_w2), twice(f_b2), blockdiag(f_w3), twice(f_b3), twice(f_freq),
      w4f, w4b, dl, cst["f1_full"], cst["tre"], cst["tim"])


def _fft_factors(seq):
    n = 2 * seq
    n1 = 256 if n >= 32768 else 32
    n2 = n // n1
    assert n1 * n2 == n and n2 % 16 == 0 and n1 % 32 == 0
    return n1, n2


@functools.lru_cache(maxsize=None)
def _fft_consts(seq):
    n = 2 * seq
    n1, n2 = _fft_factors(seq)
    k1 = np.arange(n1, dtype=np.float64)[:, None] + 0.5
    th1 = 2.0 * np.pi * k1 * np.arange(n1, dtype=np.float64)[None] / n1
    perm = np.concatenate([np.arange(0, n1, 2), np.arange(1, n1, 2)])
    f1_full = np.concatenate([np.cos(th1)[perm], -np.sin(th1)[perm]], axis=0)
    f1_half = f1_full[:, :n1 // 2]
    tw = 2.0 * np.pi * k1 * np.arange(n2, dtype=np.float64)[None] / n
    tre, tim = np.cos(tw), -np.sin(tw)
    h2 = n2 // 2
    th2 = 2.0 * np.pi * np.arange(h2, dtype=np.float64)[:, None] * np.arange(n2, dtype=np.float64)[None] / n2
    c2, s2 = np.cos(th2), np.sin(th2)
    m2f = np.block([[c2, s2], [-s2, c2]])
    m2i = np.block([[c2.T, -s2.T], [s2.T, c2.T]])
    thb = th1[:, :n1 // 2].T
    gi = (2.0 / n) * np.concatenate([np.cos(thb)[:, perm], -np.sin(thb)[:, perm]], axis=1)
    bf = lambda a: jnp.asarray(a.astype(np.float32)).astype(BF16)
    f32 = lambda a: jnp.asarray(a.astype(np.float32))
    grp = lambda a: a.T.reshape(n2, n1 // K1_GRP, K1_GRP).transpose(1, 0, 2)
    return dict(f1_full=bf(f1_full), f1_half=bf(f1_half), tre=f32(tre[perm]), tim=f32(tim[perm]),
                tre_grp=f32(grp(tre)), tim_grp=f32(grp(tim)), m2f=bf(m2f), m2i=bf(m2i), gi=bf(gi))


def _pick_col(tbl, idx):
    lane = lax.broadcasted_iota(I32, tbl.shape, 1)
    return jnp.sum(jnp.where(lane == idx, tbl, 0.0), axis=1, keepdims=True)


K1_GRP = 16
PAIRS = K1_GRP // 2
LANES = 128
MID_TC = 512


def _unpack_pair(w, half):
    bits = lax.shift_left(w, jnp.uint32(16)) if half == 0 else (w & jnp.uint32(0xFFFF0000))
    return pltpu.bitcast(bits, F32).astype(BF16)


def _pack_pair(even, odd):
    ue = pltpu.bitcast(even.astype(BF16).astype(F32), jnp.uint32)
    uo = pltpu.bitcast(odd.astype(BF16).astype(F32), jnp.uint32)
    return lax.shift_right_logical(ue, jnp.uint32(16)) | uo


def _s1_store(a, tre_ref, tim_ref, j, o_ref, n1):
    are, aim = a[:n1], a[n1:]
    tre = _pick_col(tre_ref[...], j)
    tim = _pick_col(tim_ref[...], j)
    re = are * tre - aim * tim
    im = are * tim + aim * tre
    h = n1 // 2
    shape = (n1 // K1_GRP, PAIRS, a.shape[1])
    for part, val in enumerate((re, im)):
        words = _pack_pair(val[:h], val[h:]).reshape(shape)
        for ct in range(a.shape[1] // LANES):
            o_ref[part, :, ct] = words[:, :, ct * LANES:(ct + 1) * LANES]


def _fft_s1_kernel(f1_ref, x_ref, tre_ref, tim_ref, o_ref, *, n1):
    _s1_store(_dot(f1_ref[...], x_ref[...]), tre_ref, tim_ref, pl.program_id(0), o_ref, n1)


def _fft_stage1(x2d, f1, tre, tim, n1, n2, chans):
    r = x2d.shape[0]
    tc = D_MODEL
    g = chans // tc
    return pl.pallas_call(
        functools.partial(_fft_s1_kernel, n1=n1),
        grid=(n2, g),
        in_specs=[_const_spec((2 * n1, r)),
                  pl.BlockSpec((r, tc), lambda j, c: (0, j * g + c)),
                  _const_spec((n1, n2)), _const_spec((n1, n2))],
        out_specs=pl.BlockSpec((2, n1 // K1_GRP, tc // LANES, PAIRS, LANES), lambda j, c: (0, 0, c, j, 0)),
        out_shape=jax.ShapeDtypeStruct((2, n1 // K1_GRP, chans // LANES, n2 * PAIRS, LANES), jnp.uint32),
        compiler_params=_cp("parallel", "parallel"),
        name="fft_stage1",
    )(f1, x2d, tre, tim)


def _load_pair(b_ref, part, s, n2):
    return jnp.concatenate([b_ref[part, 0, ct, pl.ds(s, n2, stride=PAIRS), :] for ct in range(b_ref.shape[2])],
                           axis=1)


def _store_pair(o_ref, part, s, n2, words):
    for ct in range(o_ref.shape[2]):
        o_ref[part, 0, ct, pl.ds(s, n2, stride=PAIRS), :] = words[:, ct * LANES:(ct + 1) * LANES]


def _fft_s2_kernel(m2f_ref, b_ref, o_ref):
    n2 = m2f_ref.shape[0]
    for s in range(PAIRS):
        wre = _load_pair(b_ref, 0, s, n2)
        wim = _load_pair(b_ref, 1, s, n2)
        for half in range(2):
            b = jnp.concatenate([_unpack_pair(wre, half), _unpack_pair(wim, half)], axis=0)
            o_ref[2 * s + half] = _dot(m2f_ref[...], b).astype(BF16)


def _fft_stage2(b5, m2f, n1, n2, chans):
    return pl.pallas_call(
        _fft_s2_kernel,
        grid=(n1 // K1_GRP, chans // MID_TC),
        in_specs=[_const_spec((n2, 2 * n2)),
                  pl.BlockSpec((2, 1, MID_TC // LANES, n2 * PAIRS, LANES), lambda g, c: (0, g, c, 0, 0))],
        out_specs=pl.BlockSpec((K1_GRP, n2, MID_TC), lambda g, c: (g, 0, c)),
        out_shape=jax.ShapeDtypeStruct((n1, n2, chans), BF16),
        compiler_params=_cp("parallel", "parallel"),
        name="fft_stage2",
    )(m2f, b5)


def _fft_mid_kernel(m2f_ref, m2i_ref, b_ref, k_ref, inv_ref, tre_ref, tim_ref, o_ref, *, n2):
    h2 = n2 // 2
    inv = inv_ref[...]
    tre_g = tre_ref[0]
    tim_g = tim_ref[0]
    for s in range(PAIRS):
        wre = _load_pair(b_ref, 0, s, n2)
        wim = _load_pair(b_ref, 1, s, n2)
        res = []
        for half in range(2):
            kl = 2 * s + half
            b = jnp.concatenate([_unpack_pair(wre, half), _unpack_pair(wim, half)], axis=0)
            x = _dot(m2f_ref[...], b)
            kk = k_ref[kl].astype(F32) * inv
            xre, xim = x[:h2], x[h2:]
            kre, kim = kk[:h2], kk[h2:]
            y = jnp.concatenate([xre * kre - xim * kim, xre * kim + xim * kre], axis=0).astype(BF16)
            c = _dot(m2i_ref[...], y)
            cre, cim = c[:n2], c[n2:]
            tre = tre_g[:, kl:kl + 1]
            tim = tim_g[:, kl:kl + 1]
            res.append((cre * tre + cim * tim, cim * tre - cre * tim))
        _store_pair(o_ref, 0, s, n2, _pack_pair(res[0][0], res[1][0]))
        _store_pair(o_ref, 1, s, n2, _pack_pair(res[0][1], res[1][1]))


def _fft_mid(b5, kspec, inv_norm, order, cst, n1, n2):
    d = D_MODEL
    nc = d // MID_TC
    blk = pl.BlockSpec((2, 1, MID_TC // LANES, n2 * PAIRS, LANES), lambda g, c: (0, g, c, 0, 0))
    tw = pl.BlockSpec((1, n2, K1_GRP), lambda g, c: (g, 0, 0))
    return pl.pallas_call(
        functools.partial(_fft_mid_kernel, n2=n2),
        grid=(n1 // K1_GRP, nc),
        in_specs=[_const_spec((n2, 2 * n2)), _const_spec((2 * n2, n2)), blk,
                  pl.BlockSpec((K1_GRP, n2, MID_TC), lambda g, c: (g, 0, order * nc + c)),
                  pl.BlockSpec((1, MID_TC), lambda g, c: (0, order * nc + c)),
                  tw, tw],
        out_specs=blk,
        out_shape=jax.ShapeDtypeStruct((2, n1 // K1_GRP, d // LANES, n2 * PAIRS, LANES), jnp.uint32),
        compiler_params=_cp("parallel", "parallel"),
        name="fft_mid",
    )(cst["m2f"], cst["m2i"], b5, kspec, inv_norm, cst["tre_grp"], cst["tim_grp"])


def _fft_last_kernel(gi_ref, c_ref, gate_ref, z_ref, skip_ref, o_ref):
    h = c_ref.shape[1] * PAIRS

    def words(part):
        return jnp.concatenate([c_ref[part, :, ct].reshape(h, LANES) for ct in range(c_ref.shape[2])], axis=1)

    wre = words(0)
    wim = words(1)
    c = jnp.concatenate([_unpack_pair(wre, 0), _unpack_pair(wre, 1), _unpack_pair(wim, 0),
                         _unpack_pair(wim, 1)], axis=0)
    y = _dot(gi_ref[...], c)
    z = z_ref[...].astype(F32)
    o_ref[...] = (gate_ref[...].astype(F32) * (y + skip_ref[...] * z)).astype(BF16)


def _fft_last(c5, gate2d, z2d, skip_row, gi, n1, n2):
    d = D_MODEL
    r = n1 // 2
    col = pl.BlockSpec((r, d), lambda j: (0, j))
    return pl.pallas_call(
        _fft_last_kernel,
        grid=(n2,),
        in_specs=[_const_spec((r, 2 * n1)),
                  pl.BlockSpec((2, n1 // K1_GRP, d // LANES, PAIRS, LANES), lambda j: (0, 0, 0, j, 0)),
                  col, col, _const_spec((1, d))],
        out_specs=col,
        out_shape=jax.ShapeDtypeStruct((r, n2 * d), BF16),
        compiler_params=_cp("parallel"),
        name="fft_last",
    )(gi, c5, gate2d, z2d, skip_row)


def _long_conv_gate(z_in, gate, skip_row, kspec, inv_norm, order, seq):
    d = D_MODEL
    n1, n2 = _fft_factors(seq)
    cst = _fft_consts(seq)
    z2d = z_in.reshape(n1 // 2, n2 * d)
    b = _fft_stage1(z2d, cst["f1_half"], cst["tre"], cst["tim"], n1, n2, d)
    c = _fft_mid(b, kspec, inv_norm, order, cst, n1, n2)
    out = _fft_last(c, gate.reshape(n1 // 2, n2 * d), z2d, skip_row, cst["gi"], n1, n2)
    return out.reshape(seq, d)


def _filter_spectrum(seq, fparams):
    d = D_MODEL
    od = HY_ORDER * d
    n1, n2 = _fft_factors(seq)
    cst = _fft_consts(seq)
    b, asum = _filter_stage1(seq, *fparams)
    kspec = _fft_stage2(b, cst["m2f"], n1, n2, od)
    return kspec, asum


def _proj_res_kernel(a_ref, w_ref, b_ref, g_ref, x_ref, o_ref):
    y = _dot(a_ref[...], w_ref[...]) + b_ref[...]
    o_ref[...] = x_ref[...] + g_ref[...] * y


def _proj_res(a, w_bf, b_row, gate_row, xres):
    n, d = xres.shape
    tm = min(512, n)
    row = pl.BlockSpec((tm, d), lambda i: (i, 0))
    return pl.pallas_call(
        _proj_res_kernel,
        grid=(n // tm,),
        in_specs=[row, _const_spec((d, d)), _const_spec((1, d)), _const_spec((1, d)), row],
        out_specs=row,
        out_shape=jax.ShapeDtypeStruct((n, d), F32),
        compiler_params=_cp("parallel"),
        name="proj_residual",
    )(a, w_bf, b_row, gate_row, xres)


def _ffn_in_kernel(x_ref, mod_ref, wt_ref, h_ref, aff_ref):
    h = _norm_mod(x_ref[...], mod_ref[...])
    hi = h.astype(BF16)
    lo = (h - hi.astype(F32)).astype(BF16)
    wt = wt_ref[...]
    whi = wt.astype(BF16)
    wlo = (wt - whi.astype(F32)).astype(BF16)
    logits = _dot_nt(whi, hi) + (_dot_nt(whi, lo) + _dot_nt(wlo, hi))
    m = jnp.max(logits, axis=0, keepdims=True)
    p = jnp.exp(logits - m)
    aff_ref[...] = p / jnp.sum(p, axis=0, keepdims=True)
    h_ref[...] = hi


def _ffn_in(x, mod, w_router):
    n, d = x.shape
    e = N_EXPERTS
    tm = min(512, n)
    return pl.pallas_call(
        _ffn_in_kernel,
        grid=(n // tm,),
        in_specs=[pl.BlockSpec((tm, d), lambda i: (i, 0)), _const_spec((8, d)), _const_spec((e, d))],
        out_specs=[pl.BlockSpec((tm, d), lambda i: (i, 0)), pl.BlockSpec((e, tm), lambda i: (0, i))],
        out_shape=[jax.ShapeDtypeStruct((n, d), BF16), jax.ShapeDtypeStruct((e, n), F32)],
        compiler_params=_cp("parallel"),
        name="moe_router",
    )(x, mod, w_router.T)


def _select_kernel(a_ref, pos_ref, s0_ref, sel_ref, *, cap, nblk):
    e = N_EXPERTS
    bits = pltpu.bitcast(a_ref[...], I32)

    def bisect(i, thr):
        cand = thr | jnp.left_shift(jnp.int32(1), 30 - i)
        cnt = jnp.sum(jnp.where(bits >= cand, 1.0, 0.0), axis=1, keepdims=True)
        return jnp.where(cnt >= cap, cand, thr)

    thr = lax.fori_loop(0, 31, bisect, jnp.zeros((e, 1), I32))
    n_gt = jnp.sum(jnp.where(bits > thr, 1.0, 0.0), axis=1, keepdims=True)
    need = cap - n_gt
    r = lax.broadcasted_iota(I32, (TOK_BLK, TOK_BLK), 0)
    c = lax.broadcasted_iota(I32, (TOK_BLK, TOK_BLK), 1)
    upper = jnp.where(r < c, 1.0, 0.0).astype(BF16)

    def pass1(j, carry):
        sl = pl.ds(pl.multiple_of(j * TOK_BLK, TOK_BLK), TOK_BLK)
        bj = pltpu.bitcast(a_ref[:, sl], I32)
        eq = jnp.where(bj == thr, 1.0, 0.0)
        rank = _dot(eq.astype(BF16), upper) + carry
        keep = jnp.logical_or(bj > thr, jnp.logical_and(bj == thr, rank < need))
        sel_ref[:, sl] = jnp.where(keep, 1.0, 0.0)
        return carry + jnp.sum(eq, axis=1, keepdims=True)

    lax.fori_loop(0, nblk, pass1, jnp.zeros((e, 1), F32))

    def pass2(j, carry):
        sl = pl.ds(pl.multiple_of(j * TOK_BLK, TOK_BLK), TOK_BLK)
        s = sel_ref[:, sl]
        slot = _dot(s.astype(BF16), upper) + carry
        pos_ref[:, sl] = jnp.where(s > 0.5, slot, -1.0).astype(I32)
        s0_ref[j] = jnp.broadcast_to(carry, (e, 128)).astype(I32)
        return carry + jnp.sum(s, axis=1, keepdims=True)

    total = lax.fori_loop(0, nblk, pass2, jnp.zeros((e, 1), F32))
    s0_ref[nblk] = jnp.broadcast_to(total, (e, 128)).astype(I32)


def _select(aff_t, cap):
    e, n = aff_t.shape
    nblk = n // TOK_BLK
    return pl.pallas_call(
        functools.partial(_select_kernel, cap=cap, nblk=nblk),
        out_shape=[jax.ShapeDtypeStruct((e, n), I32), jax.ShapeDtypeStruct((nblk + 1, e, 128), I32)],
        scratch_shapes=[pltpu.VMEM((e, n), F32)],
        compiler_params=pltpu.CompilerParams(vmem_limit_bytes=VMEM_LIMIT),
        name="moe_select",
    )(aff_t)


def _block_windows(s0_ref, blk, e_idx):
    s0 = s0_ref[blk * N_EXPERTS + e_idx]
    s1 = s0_ref[(blk + 1) * N_EXPERTS + e_idx]
    start = lax.shift_left(lax.shift_right_logical(s0, 4), 4)
    nwin = jnp.where(s1 > s0, lax.shift_right_logical(s1 - start + (WIN - 1), WIN_SHIFT), 0)
    return start, nwin


def _token_block(i):
    start = i * TOK_BLK
    return pl.ds(start if isinstance(i, int) else pl.multiple_of(start, TOK_BLK), TOK_BLK)


def _gather_kernel(s0_ref, pos_ref, h_ref, xe_ref, *, sub):
    e_idx = pl.program_id(0)
    c = pl.program_id(1)

    @pl.when(c == 0)
    def _():
        xe_ref[...] = jnp.zeros_like(xe_ref)

    rows = lax.broadcasted_iota(I32, (WIN, TOK_BLK), 0)

    def window(i, base):
        tok = _token_block(i)
        base = pl.multiple_of(base, ROW_ALIGN)
        onehot = jnp.where(rows == pos_ref[0, :, tok] - base, 1.0, 0.0).astype(BF16)
        got = _dot(onehot, h_ref[tok, :]).astype(BF16)
        win = pl.ds(base, WIN)
        xe_ref[0, win, :] = xe_ref[0, win, :] + got

    plan = [_block_windows(s0_ref, c * sub + i, e_idx) for i in range(sub)]
    single = functools.reduce(jnp.logical_and, [nwin <= 1 for _, nwin in plan])

    @pl.when(single)
    def _():
        for i, (start, _) in enumerate(plan):
            window(i, start)

    @pl.when(jnp.logical_not(single))
    def _():
        def body(i, carry):
            start, nwin = _block_windows(s0_ref, c * sub + i, e_idx)
            return lax.fori_loop(0, nwin, lambda w, cc: (window(i, start + w * WIN), cc)[1], carry)

        lax.fori_loop(0, sub, body, 0)


def _gather(s0_flat, pos, h, cap_pad):
    e, n = pos.shape
    d = h.shape[1]
    chunk = min(2048, n)
    sub = chunk // TOK_BLK
    return pl.pallas_call(
        functools.partial(_gather_kernel, sub=sub),
        grid_spec=pltpu.PrefetchScalarGridSpec(
            num_scalar_prefetch=1,
            grid=(e, n // chunk),
            in_specs=[pl.BlockSpec((1, 1, chunk), lambda ei, c, s0: (ei, 0, c)),
                      pl.BlockSpec((chunk, d), lambda ei, c, s0: (c, 0))],
            out_specs=pl.BlockSpec((1, cap_pad, d), lambda ei, c, s0: (ei, 0, 0)),
        ),
        out_shape=jax.ShapeDtypeStruct((e, cap_pad, d), BF16),
        compiler_params=_cp("parallel", "arbitrary"),
        name="moe_gather",
    )(s0_flat, pos.reshape(e, 1, n), h)


def _expert_kernel(x_ref, wg_ref, wu_ref, wd_ref, y_ref, wg_scr, wu_scr, wd_scr, *, n_real):
    j = pl.program_id(1)

    @pl.when(j == 0)
    def _():
        wg_scr[...] = wg_ref[0].astype(BF16)
        wu_scr[...] = wu_ref[0].astype(BF16)
        wd_scr[...] = wd_ref[0].astype(BF16)

    @pl.when(j < n_real)
    def _():
        x = x_ref[0]
        g = _dot(x, wg_scr[...])
        u = _dot(x, wu_scr[...])
        a = (g / (1.0 + jnp.exp(-g))) * u
        y_ref[0] = _dot(a.astype(BF16), wd_scr[...]).astype(BF16)

    @pl.when(j >= n_real)
    def _():
        y_ref[0] = jnp.zeros_like(y_ref[0])


def _experts(xe, wg, wu, wd, layer, cap):
    e, cap_pad, d = xe.shape
    f = wg.shape[3]
    tm = 256 if cap % 256 == 0 else cap_pad
    tile = pl.BlockSpec((1, tm, d), lambda ei, j: (ei, j, 0))
    return pl.pallas_call(
        functools.partial(_expert_kernel, n_real=pl.cdiv(cap, tm)),
        grid=(e, cap_pad // tm),
        in_specs=[tile,
                  pl.BlockSpec((None, 1, d, f), lambda ei, j: (layer, ei, 0, 0)),
                  pl.BlockSpec((None, 1, d, f), lambda ei, j: (layer, ei, 0, 0)),
                  pl.BlockSpec((None, 1, f, d), lambda ei, j: (layer, ei, 0, 0))],
        out_specs=tile,
        out_shape=jax.ShapeDtypeStruct((e, cap_pad, d), BF16),
        scratch_shapes=[pltpu.VMEM((d, f), BF16), pltpu.VMEM((d, f), BF16), pltpu.VMEM((f, d), BF16)],
        compiler_params=_cp("parallel", "arbitrary"),
        name="moe_experts",
    )(xe, wg, wu, wd)


def _combine_kernel(s0_ref, post_ref, gt_ref, y_ref, x_ref, gate_ref, o_ref, *, sub):
    c = pl.program_id(0)
    e_idx = pl.program_id(1)

    @pl.when(e_idx == 0)
    def _():
        o_ref[...] = jnp.zeros_like(o_ref)

    lanes = lax.broadcasted_iota(I32, (TOK_BLK, WIN), 1)
    elane = lax.broadcasted_iota(I32, (TOK_BLK, N_EXPERTS), 1) == e_idx

    def window(i, base):
        tok = _token_block(i)
        base = pl.multiple_of(base, ROW_ALIGN)
        pcol = jnp.sum(jnp.where(elane, post_ref[tok, :].astype(F32), 0.0), axis=1, keepdims=True)
        gcol = jnp.sum(jnp.where(elane, gt_ref[tok, :], 0.0), axis=1, keepdims=True)
        rel = pcol.astype(I32) - base
        onehot = jnp.where(lanes == rel, 1.0, 0.0).astype(BF16)
        strip = y_ref[0, pl.ds(base, WIN), :]
        o_ref[tok, :] = o_ref[tok, :] + gcol * _dot(onehot, strip)

    plan = [_block_windows(s0_ref, c * sub + i, e_idx) for i in range(sub)]
    single = functools.reduce(jnp.logical_and, [nwin <= 1 for _, nwin in plan])

    @pl.when(single)
    def _():
        for i, (start, _) in enumerate(plan):
            window(i, start)

    @pl.when(jnp.logical_not(single))
    def _():
        def body(i, carry):
            start, nwin = _block_windows(s0_ref, c * sub + i, e_idx)
            return lax.fori_loop(0, nwin, lambda w, cc: (window(i, start + w * WIN), cc)[1], carry)

        lax.fori_loop(0, sub, body, 0)

    @pl.when(e_idx == N_EXPERTS - 1)
    def _():
        o_ref[...] = x_ref[...] + gate_ref[...] * o_ref[...]


def _combine(s0_flat, pos_t, g_t, y, xres, gate_row):
    n, d = xres.shape
    e, cap_pad, _ = y.shape
    chunk = min(2048, n)
    sub = chunk // TOK_BLK
    return pl.pallas_call(
        functools.partial(_combine_kernel, sub=sub),
        grid_spec=pltpu.PrefetchScalarGridSpec(
            num_scalar_prefetch=1,
            grid=(n // chunk, e),
            in_specs=[pl.BlockSpec((chunk, e), lambda c, ei, s0: (c, 0)),
                      pl.BlockSpec((chunk, e), lambda c, ei, s0: (c, 0)),
                      pl.BlockSpec((1, cap_pad, d), lambda c, ei, s0: (ei, 0, 0)),
                      pl.BlockSpec((chunk, d), lambda c, ei, s0: (c, 0)),
                      pl.BlockSpec((1, d), lambda c, ei, s0: (0, 0))],
            out_specs=pl.BlockSpec((chunk, d), lambda c, ei, s0: (c, 0)),
        ),
        out_shape=jax.ShapeDtypeStruct((n, d), F32),
        compiler_params=_cp("parallel", "arbitrary"),
        name="moe_combine",
    )(s0_flat, pos_t, g_t, y, xres, gate_row)


def _moe_block(x, mod, gate_row, w_router, wg, wu, wd, layer):
    n = x.shape[0]
    cap = EC_CAPACITY * n // N_EXPERTS
    cap_pad = cap + TOK_BLK
    h, aff_t = _ffn_in(x, mod, w_router)
    pos, s0 = _select(aff_t, cap)
    s0_flat = s0[:, :, 0].reshape(-1)
    xe = _gather(s0_flat, pos, h, cap_pad)
    y = _experts(xe, wg, wu, wd, layer, cap)
    return _combine(s0_flat, pos.T, aff_t.T, y, x, gate_row)


QK_SCALE = (DA_HEAD_DIM ** -0.5) * math.log2(math.e)


def _split_bf16(x):
    hi = x.astype(BF16)
    return hi, (x - hi.astype(F32)).astype(BF16)


def _group_rms(u, gsel_ref, gain, eps):
    g = gsel_ref[...]
    hi, lo = _split_bf16(u * u)
    r = lax.rsqrt((_dot(hi, g) + _dot(lo, g)) * (1.0 / DA_HEAD_DIM) + eps)
    rhi, rlo = _split_bf16(r)
    return u * (_dot_nt(rhi, g) + _dot_nt(rlo, g)) * gain


def _rope(u, cos, sin_signed):
    d = u.shape[1]
    half = ROPE_AXIS_DIM // 2
    lane = lax.broadcasted_iota(I32, u.shape, 1)
    first = (lane & half) == 0
    swapped = jnp.where(first, pltpu.roll(u, d - half, 1), pltpu.roll(u, half, 1))
    return u * cos + swapped * sin_signed


def _qkv_kernel(*refs, rope):
    if rope:
        x_ref, mod_ref, w_ref, gsum_ref, qn_ref, kn_ref, cos_ref, sin_ref, q_ref, k_ref, v_ref = refs
    else:
        x_ref, mod_ref, w_ref, gsum_ref, qn_ref, kn_ref, q_ref, k_ref, v_ref = refs
    d = D_MODEL
    h = _norm_mod(x_ref[...], mod_ref[...]).astype(BF16)
    if rope:
        reps = d // cos_ref.shape[1]
        cos = _lane_tile(cos_ref[...], reps)
        sin = _lane_tile(sin_ref[...], reps)
    for part, (o_ref, gain_ref) in enumerate(((q_ref, qn_ref), (k_ref, kn_ref))):
        u = _dot(h, w_ref[:, part * d:(part + 1) * d])
        u = _group_rms(u, gsum_ref, gain_ref[...], NORM_EPS)
        if rope:
            u = _rope(u, cos, sin)
        if part == 0:
            u = u * QK_SCALE
        o_ref[...] = u.astype(BF16)
    v_ref[...] = _dot(h, w_ref[:, 2 * d:]).astype(BF16)


def _qkv(x, mod, w_bf, gsum, qn_row, kn_row, cos=None, sin=None):
    n, d = x.shape
    tm = min(512, n)
    rope = cos is not None
    row = pl.BlockSpec((tm, d), lambda i: (i, 0))
    in_specs = [row, _const_spec((8, d)), _const_spec((d, 3 * d)), _const_spec(gsum.shape),
                _const_spec((1, d)), _const_spec((1, d))]
    args = [x, mod, w_bf, gsum, qn_row, kn_row]
    if rope:
        tw = cos.shape[1]
        in_specs += [pl.BlockSpec((tm, tw), lambda i: (i, 0)), pl.BlockSpec((tm, tw), lambda i: (i, 0))]
        args += [cos, sin]
    out = jax.ShapeDtypeStruct((n, d), BF16)
    return pl.pallas_call(
        functools.partial(_qkv_kernel, rope=rope),
        grid=(n // tm,),
        in_specs=in_specs,
        out_specs=[row, row, row],
        out_shape=[out, out, out],
        compiler_params=_cp("parallel"),
        name="attn_qkv",
    )(*args)


NEG_BIG = -1e30
HEAD_UNROLL = 8


def _attn_kernel(q_ref, k_ref, v_ref, lam_ref, sub_ref, o_ref, m_scr, acc_scr, sa_scr, sb_scr, ma_scr, mb_scr,
                 pa_scr, pb_scr, vprev_scr, *,
                 lam_init):
    j = pl.program_id(1)
    nj = pl.num_programs(1)
    hd, vd = DA_HEAD_DIM, DA_V_DIM
    tk = k_ref.shape[0]

    last = DA_HEADS - 1

    @pl.when(j == 0)
    def _():
        m_scr[...] = jnp.full_like(m_scr, NEG_BIG)
        acc_scr[...] = jnp.zeros_like(acc_scr)
        pa_scr[...] = jnp.zeros_like(pa_scr)
        sb_scr[...] = jnp.full_like(sb_scr, 2.0 * NEG_BIG)
        mb_scr[...] = jnp.full_like(mb_scr, 2.0 * NEG_BIG)
        vprev_scr[...] = jnp.zeros_like(vprev_scr)

    lane = lax.broadcasted_iota(I32, (q_ref.shape[0], vd), 1)
    ones_col = jnp.where(lax.broadcasted_iota(I32, (tk, vd), 1) == 0, 1.0, 0.0).astype(BF16)

    def head_cols(h):
        return pl.ds(pl.multiple_of(h * vd, vd), vd)

    def scores(h, comp, s_ref, mx_ref):
        qb = q_ref[:, head_cols(h)]
        in_comp = (lane >= comp * hd) & (lane < (comp + 1) * hd)
        qm = jnp.where(in_comp, qb, jnp.zeros_like(qb))
        s = _dot_nt(qm, k_ref[:, head_cols(h)])
        s_ref[...] = s
        mx_ref[...] = jnp.broadcast_to(jnp.max(s, axis=1, keepdims=True), mx_ref.shape)

    def softmax_step(h, comp, s_ref, mx_ref, p_ref):
        idx = 2 * h + comp
        m_prev = m_scr[idx]
        m_new = jnp.maximum(m_prev, mx_ref[...])
        alpha = _lane_tile(jnp.exp2(m_prev - m_new), 2)
        p_ref[...] = jnp.exp2(s_ref[...] - m_new[:, 0:1]).astype(BF16)
        m_scr[idx] = m_new
        acc_scr[idx] = alpha * acc_scr[idx]

    def pv_step(h, comp, p_ref):
        idx = 2 * h + comp
        acc_scr[idx] = acc_scr[idx] + _dot(p_ref[...], jnp.concatenate([vprev_scr[...], ones_col], axis=1))

    def head(h, carry):
        hp = (h + last) & last
        scores(h, 0, sa_scr, ma_scr)
        pv_step(hp, 0, pa_scr)
        softmax_step(hp, 1, sb_scr, mb_scr, pb_scr)
        scores(h, 1, sb_scr, mb_scr)
        pv_step(hp, 1, pb_scr)
        softmax_step(h, 0, sa_scr, ma_scr, pa_scr)
        vprev_scr[...] = v_ref[:, head_cols(h)]
        return carry

    def head_group(g, carry):
        for u in range(HEAD_UNROLL):
            carry = head(HEAD_UNROLL * g + u, carry)
        return carry

    lax.fori_loop(0, DA_HEADS // HEAD_UNROLL, head_group, 0)

    @pl.when(j == nj - 1)
    def _():
        pv_step(last, 0, pa_scr)
        softmax_step(last, 1, sb_scr, mb_scr, pb_scr)
        pv_step(last, 1, pb_scr)
        lp = lam_ref[...]
        lam = (jnp.exp(jnp.sum(lp[0:1] * lp[1:2], axis=1, keepdims=True))
               - jnp.exp(jnp.sum(lp[2:3] * lp[3:4], axis=1, keepdims=True)) + lam_init)
        for h in range(DA_HEADS):
            a0 = acc_scr[2 * h]
            a1 = acc_scr[2 * h + 1]
            o = a0[:, :vd] / a0[:, vd:vd + 1] - lam * (a1[:, :vd] / a1[:, vd:vd + 1])
            ms = jnp.mean(o * o, axis=1, keepdims=True)
            o = o * lax.rsqrt(ms + SUBLN_EPS) * (sub_ref[...] * (1.0 - lam_init))
            o_ref[:, h * vd:(h + 1) * vd] = o.astype(BF16)


def _attention(q, k_all, v_all, lam_rows, subln_row, lam_init):
    n, d = q.shape
    nk = k_all.shape[0]
    tq = min(512, n)
    tk = 1280 if nk % 1280 == 0 else 256
    assert nk % tk == 0
    nc = 2 * DA_HEADS
    return pl.pallas_call(
        functools.partial(_attn_kernel, lam_init=lam_init),
        grid=(n // tq, nk // tk),
        in_specs=[pl.BlockSpec((tq, d), lambda i, j: (i, 0)),
                  pl.BlockSpec((tk, d), lambda i, j: (j, 0)),
                  pl.BlockSpec((tk, d), lambda i, j: (j, 0)),
                  _const_spec((8, DA_HEAD_DIM)), _const_spec((1, DA_V_DIM))],
        out_specs=pl.BlockSpec((tq, d), lambda i, j: (i, 0)),
        out_shape=jax.ShapeDtypeStruct((n, d), BF16),
        scratch_shapes=[pltpu.VMEM((nc, tq, DA_V_DIM), F32), pltpu.VMEM((nc, tq, 2 * DA_V_DIM), F32),
                        pltpu.VMEM((tq, tk), F32), pltpu.VMEM((tq, tk), F32),
                        pltpu.VMEM((tq, DA_V_DIM), F32), pltpu.VMEM((tq, DA_V_DIM), F32),
                        pltpu.VMEM((tq, tk), BF16), pltpu.VMEM((tq, tk), BF16),
                        pltpu.VMEM((tk, DA_V_DIM), BF16)],
        compiler_params=_cp("parallel", "arbitrary"),
        name="diff_attention",
    )(q, k_all, v_all, lam_rows, subln_row)


def _rope_tables(n):
    rows = n // GRID_W
    inv = ROPE_THETA ** (-jnp.arange(0, ROPE_AXIS_DIM, 2, dtype=F32) / ROPE_AXIS_DIM)
    ar = jnp.arange(rows, dtype=F32)[:, None] * inv[None]
    ac = jnp.arange(GRID_W, dtype=F32)[:, None] * inv[None]
    nf = inv.shape[0]
    by_row = lambda a: jnp.broadcast_to(a[:, None, :], (rows, GRID_W, nf))
    by_col = lambda a: jnp.broadcast_to(a[None, :, :], (rows, GRID_W, nf))
    cr, sr, cc, sc = by_row(jnp.cos(ar)), by_row(jnp.sin(ar)), by_col(jnp.cos(ac)), by_col(jnp.sin(ac))
    cos = jnp.concatenate([cr, cr, cc, cc] * 2, axis=2).reshape(n, 8 * nf)
    sin = jnp.concatenate([-sr, sr, -sc, sc] * 2, axis=2).reshape(n, 8 * nf)
    return cos, sin


def _hyena_layer(x, mod, gate_row, kspec, inv_norm, w_in_bf, b_in, conv_w, conv_b, skip, w_out_bf, b_out):
    seq = x.shape[0]
    v, x1, x2 = _hy_in(x, mod, w_in_bf, b_in, conv_w, conv_b)
    z = _long_conv_gate(v, x1, skip[0:1], kspec, inv_norm, 0, seq)
    z = _long_conv_gate(z, x2, skip[1:2], kspec, inv_norm, 1, seq)
    return _proj_res(z, w_out_bf, b_out.reshape(1, -1), gate_row, x)


def kernel(x, c, ctx, c_ctx, ada_w, ada_b, norm_mix, norm_ffn, hy_w_in, hy_b_in, hy_conv_w, hy_conv_b, hy_f_w1, hy_f_b1, hy_f_w2, hy_f_b2, hy_f_w3, hy_f_b3, hy_f_w4, hy_f_freq, hy_skip, hy_w_out, hy_b_out, da_w_qkv, da_q_norm, da_k_norm, da_lam_q1, da_lam_k1, da_lam_q2, da_lam_k2, da_subln, da_w_out, moe_router, moe_w_gate, moe_w_up, moe_w_down):
    d = D_MODEL
    depth = ada_w.shape[0]
    assert x.shape[0] == 1 and x.shape[2] == d
    xs = x[0]
    cs = ctx[0]
    cond8 = jnp.concatenate([c[0:1], c_ctx[None], jnp.zeros((6, d), F32)], axis=0)
    mods = _adaln(cond8, ada_w, ada_b)

    def mod_slice(i, row, k):
        return mods[i, row, k * d:(k + 1) * d][None]

    for i in range(depth):
        last = i == depth - 1
        j = i // 2
        mix_x = _mod_rows(norm_mix[i], mods[i], 0, 0)
        mix_c = _mod_rows(norm_mix[i], mods[i], 1, 0)
        if i % 2 == 0:
            fparams = (hy_f_w1[j], hy_f_b1[j], hy_f_w2[j], hy_f_b2[j], hy_f_w3[j], hy_f_b3[j], hy_f_w4[j],
                       hy_f_freq[j])
            shared = (hy_w_in[j].astype(BF16), hy_b_in[j], hy_conv_w[j], hy_conv_b[j], hy_skip[j],
                      hy_w_out[j].astype(BF16), hy_b_out[j])
            kspec, asum = _filter_spectrum(xs.shape[0], fparams)
            inv_norm = 1.0 / (asum[0:1] + HY_FILTER_EPS)
            new_x = _hyena_layer(xs, mix_x, mod_slice(i, 0, 2), kspec, inv_norm, *shared)
            if not last:
                kspec_c, asum_c = _filter_spectrum(cs.shape[0], fparams)
                inv_c = 1.0 / (asum_c[0:1] + HY_FILTER_EPS)
                cs = _hyena_layer(cs, mix_c, mod_slice(i, 1, 2), kspec_c, inv_c, *shared)
            xs = new_x
        else:
            lam_init = 0.8 - 0.6 * math.exp(-0.3 * i)
            w_qkv = da_w_qkv[j].astype(BF16)
            gidx = np.arange(d) // DA_HEAD_DIM
            gsum = jnp.asarray((gidx[:, None] == np.arange(128)[None]).astype(np.float32)).astype(BF16)
            qn = jnp.tile(da_q_norm[j], 2 * DA_HEADS)[None]
            kn = jnp.tile(da_k_norm[j], 2 * DA_HEADS)[None]
            cos, sin = _rope_tables(xs.shape[0])
            qx, kx, vx = _qkv(xs, mix_x, w_qkv, gsum, qn, kn, cos, sin)
            qc, kc, vc = _qkv(cs, mix_c, w_qkv, gsum, qn, kn)
            k_all = jnp.concatenate([kc, kx], axis=0)
            v_all = jnp.concatenate([vc, vx], axis=0)
            lam_rows = jnp.concatenate([da_lam_q1[j][None], da_lam_k1[j][None], da_lam_q2[j][None],
                                        da_lam_k2[j][None], jnp.zeros((4, DA_HEAD_DIM), F32)], axis=0)
            w_out = da_w_out[j].astype(BF16)
            zero_b = jnp.zeros((1, d), F32)
            ox = _attention(qx, k_all, v_all, lam_rows, da_subln[j][None], lam_init)
            new_x = _proj_res(ox, w_out, zero_b, mod_slice(i, 0, 2), xs)
            if not last:
                oc = _attention(qc, kc, vc, lam_rows, da_subln[j][None], lam_init)
                cs = _proj_res(oc, w_out, zero_b, mod_slice(i, 1, 2), cs)
            xs = new_x
        experts = (moe_w_gate, moe_w_up, moe_w_down, i)
        if not last:
            cs = _moe_block(cs, _mod_rows(norm_ffn[i], mods[i], 1, 3), mod_slice(i, 1, 5), moe_router[i], *experts)
        xs = _moe_block(xs, _mod_rows(norm_ffn[i], mods[i], 0, 3), mod_slice(i, 0, 5), moe_router[i], *experts)
    return xs[None]
```

```python
import functools
import math

import jax
import jax.numpy as jnp
import numpy as np
from jax import lax
from jax.experimental import pallas as pl
from jax.experimental.pallas import tpu as pltpu

F32 = jnp.float32
BF16 = jnp.bfloat16
I32 = jnp.int32
HIGHEST = lax.Precision.HIGHEST

D_MODEL = 1024
N_MOD = 6
NORM_EPS = 1e-6
GRID_W = 64
HY_ORDER = 2
HY_SHORT = 3
HY_EMB_BANDS = 16
HY_EMB_DIM = 1 + 2 * HY_EMB_BANDS
HY_FILTER_HIDDEN = 64
HY_DECAY_FAST = 0.3
HY_DECAY_SLOW = 1.5
HY_DECAY_TARGET = 1e-2
HY_FILTER_EPS = 1e-6
Z_SIGN_COL = 33
DA_HEADS = 8
DA_HEAD_DIM = 64
DA_V_DIM = 128
ROPE_AXIS_DIM = 32
ROPE_THETA = 10000.0
SUBLN_EPS = 1e-5
N_EXPERTS = 16
EC_CAPACITY = 2
D_EXPERT = 1024
TOK_BLK = 256
ROW_ALIGN = 16
WIN_SHIFT = 7
WIN = 1 << WIN_SHIFT

VMEM_LIMIT = 56 * 1024 * 1024


def _cp(*sem):
    return pltpu.CompilerParams(dimension_semantics=sem, vmem_limit_bytes=VMEM_LIMIT)


def _const_spec(shape):
    nd = len(shape)
    return pl.BlockSpec(shape, lambda *_: (0,) * nd)


def _dot(a, b):
    return jnp.dot(a, b, preferred_element_type=F32)


def _dot_nt(a, b):
    return lax.dot_general(a, b, (((1,), (1,)), ((), ())), preferred_element_type=F32)


def _norm_mod(x, mod, eps=NORM_EPS):
    ms = jnp.mean(x * x, axis=-1, keepdims=True)
    return x * lax.rsqrt(ms + eps) * (mod[0:1] * (1.0 + mod[1:2])) + mod[2:3]


def _lane_tile(x, reps):
    return jnp.concatenate([x] * reps, axis=1) if reps > 1 else x


def _adaln_kernel(c_ref, w_ref, b_ref, o_ref):
    c = c_ref[...]
    s = c / (1.0 + jnp.exp(-c))
    o_ref[0] = jnp.dot(s, w_ref[0], precision=HIGHEST, preferred_element_type=F32) + b_ref[0]


def _adaln(cond8, ada_w, ada_b):
    depth, d, nout = ada_w.shape
    tn = 1536
    return pl.pallas_call(
        _adaln_kernel,
        grid=(depth, nout // tn),
        in_specs=[_const_spec((8, d)),
                  pl.BlockSpec((1, d, tn), lambda l, j: (l, 0, j)),
                  pl.BlockSpec((1, 1, tn), lambda l, j: (l, 0, j))],
        out_specs=pl.BlockSpec((1, 8, tn), lambda l, j: (l, 0, j)),
        out_shape=jax.ShapeDtypeStruct((depth, 8, nout), F32),
        compiler_params=_cp("parallel", "parallel"),
        name="adaln",
    )(cond8, ada_w, ada_b.reshape(depth, 1, nout))


def _mod_rows(norm_g, mods, row, k0):
    d = D_MODEL
    shift = mods[row, k0 * d:(k0 + 1) * d]
    scale = mods[row, (k0 + 1) * d:(k0 + 2) * d]
    z = jnp.zeros((5, d), F32)
    return jnp.concatenate([norm_g[None], scale[None], shift[None], z], axis=0)


HALO = 16


def _hy_in_kernel(x_ref, xp_ref, xn_ref, mod_ref, w_ref, b_ref, cw_ref, v_ref, x1_ref, x2_ref, *, tm, n_rows):
    i = pl.program_id(0)
    mod = mod_ref[...]
    hm = _norm_mod(x_ref[...], mod).astype(BF16)
    hp = _norm_mod(xp_ref[...], mod).astype(BF16)
    hn = _norm_mod(xn_ref[...], mod).astype(BF16)
    hcat = jnp.concatenate([hp, hm, hn], axis=0)
    row = lax.broadcasted_iota(I32, (tm + 2 * HALO, 1), 0) + (i * tm - HALO)
    valid = jnp.logical_and(row >= 0, row < n_rows)
    d = D_MODEL
    for c, o_ref in enumerate((v_ref, x1_ref, x2_ref)):
        u = _dot(hcat, w_ref[:, c * d:(c + 1) * d]) + b_ref[:, c * d:(c + 1) * d]
        u = jnp.where(valid, u, 0.0)
        cw = cw_ref[:, c * d:(c + 1) * d]
        y = (cw[3:4] + cw[0:1] * u[HALO - 1:HALO - 1 + tm] + cw[1:2] * u[HALO:HALO + tm]
             + cw[2:3] * u[HALO + 1:HALO + 1 + tm])
        o_ref[...] = y.astype(BF16)


def _hy_in(x, mod, w_bf, b_in, conv_w, conv_b):
    n, d = x.shape
    tm = min(512, n)
    nh = n // HALO
    cw = jnp.concatenate([conv_w, conv_b[None], jnp.zeros((4, 3 * d), F32)], axis=0)
    out = jax.ShapeDtypeStruct((n, d), BF16)
    row_spec = pl.BlockSpec((tm, d), lambda i: (i, 0))
    return pl.pallas_call(
        functools.partial(_hy_in_kernel, tm=tm, n_rows=n),
        grid=(n // tm,),
        in_specs=[row_spec,
                  pl.BlockSpec((HALO, d), lambda i: (jnp.maximum(i * (tm // HALO) - 1, 0), 0)),
                  pl.BlockSpec((HALO, d), lambda i: (jnp.minimum((i + 1) * (tm // HALO), nh - 1), 0)),
                  _const_spec((8, d)), _const_spec((d, 3 * d)), _const_spec((1, 3 * d)),
                  _const_spec((8, 3 * d))],
        out_specs=[row_spec, row_spec, row_spec],
        out_shape=[out, out, out],
        compiler_params=_cp("parallel"),
        name="hyena_in",
    )(x, x, x, mod, w_bf, b_in.reshape(1, 3 * d), cw)


Z_HALF = 64


def _filter_s1_kernel(z_ref, w1_ref, b1_ref, w2_ref, b2_ref, w3_ref, b3_ref, fr_ref, w4f_ref, w4b_ref, dl_ref,
                      f1_ref, tre_ref, tim_ref, o_ref, asum_ref, *, n1):
    j0 = pl.program_id(0) * N2_GRP
    fr = fr_ref[...]
    dl = dl_ref[...]
    r = n1 // 2

    def lin(a, w_ref, b_ref):
        return jnp.dot(a, w_ref[...], precision=HIGHEST, preferred_element_type=F32) + b_ref[...]

    @pl.when(j0 == 0)
    def _():
        asum_ref[...] = jnp.zeros_like(asum_ref)

    for slot in range(N2_GRP):
        z = z_ref[slot * r:(slot + 1) * r, :]
        hid = jnp.sin(fr * lin(z, w1_ref, b1_ref))
        hid = jnp.sin(fr * lin(hid, w2_ref, b2_ref))
        hid = jnp.sin(fr * lin(hid, w3_ref, b3_ref)).astype(BF16)

        def taps(w4_ref, col):
            t = z[:, col:col + 1]
            sgn = z[:, col + Z_SIGN_COL:col + Z_SIGN_COL + 1]
            return _dot(hid, w4_ref[...]) * jnp.exp(-t * dl) * sgn

        k = jnp.concatenate([taps(w4f_ref, 0), taps(w4b_ref, Z_HALF)], axis=0)
        asum_ref[0:1, :] += jnp.sum(jnp.abs(k), axis=0, keepdims=True)
        _s1_store(_dot(f1_ref[...], k.astype(BF16)), tre_ref, tim_ref, j0 + slot, o_ref, n1, slot)


def _filter_positions(seq, n1, n2):
    i = np.arange(n1 // 2)[None, :]
    j = np.arange(n2)[:, None]
    bands = np.linspace(1e-4, HY_EMB_BANDS - 1, HY_EMB_BANDS)
    z = np.zeros((n2, n1 // 2, 2 * Z_HALF), np.float64)
    for col, r in ((0, i * n2 + j), (Z_HALF, i * n2 + j + seq)):
        pos = np.minimum(np.where(r < seq, r, 2 * seq - r), seq - 1).astype(np.float64)
        w = 2.0 * np.pi * pos / seq
        z[:, :, col] = pos / (seq - 1)
        z[:, :, col + 1:col + 1 + HY_EMB_BANDS] = np.cos(w[..., None] * bands)
        z[:, :, col + 1 + HY_EMB_BANDS:col + HY_EMB_DIM] = -np.sin(w[..., None] * bands)
        z[:, :, col + Z_SIGN_COL] = np.where(r < seq, 1.0, np.where(r == seq, 0.0, -1.0))
    return jnp.asarray(z.reshape(n2 * (n1 // 2), 2 * Z_HALF).astype(np.float32))


def _filter_stage1(seq, f_w1, f_b1, f_w2, f_b2, f_w3, f_b3, f_w4, f_freq):
    d, hid = D_MODEL, HY_FILTER_HIDDEN
    od = HY_ORDER * d
    n1, n2 = _fft_factors(seq)
    cst = _fft_consts(seq)
    z = _filter_positions(seq, n1, n2)
    zw = 2 * Z_HALF
    zero = jnp.zeros((hid, hid), F32)
    pad = jnp.zeros((Z_HALF - HY_EMB_DIM, hid), F32)
    w1 = jnp.concatenate([jnp.concatenate([f_w1, pad], axis=0), jnp.zeros((Z_HALF, hid), F32)], axis=0)
    w1 = jnp.concatenate([w1, jnp.roll(w1, Z_HALF, axis=0)], axis=1)
    blockdiag = lambda w: jnp.concatenate([jnp.concatenate([w, zero], axis=1),
                                           jnp.concatenate([zero, w], axis=1)], axis=0)
    twice = lambda v: jnp.tile(v, 2).reshape(1, 2 * hid)
    w4d = f_w4.reshape(hid, HY_ORDER, 2, d).transpose(2, 0, 1, 3).reshape(2, hid, od)
    zrows = jnp.zeros((hid, od), F32)
    w4f = jnp.concatenate([w4d[0], zrows], axis=0).astype(BF16)
    w4b = jnp.concatenate([zrows, w4d[1]], axis=0).astype(BF16)
    max_decay = math.log(HY_DECAY_TARGET) / HY_DECAY_FAST
    min_decay = math.log(HY_DECAY_TARGET) / HY_DECAY_SLOW
    deltas = np.abs(np.linspace(min_decay, max_decay, d, dtype=np.float32))
    dl = jnp.asarray(np.tile(deltas, HY_ORDER)[None])
    r = n1 // 2
    return pl.pallas_call(
        functools.partial(_filter_s1_kernel, n1=n1),
        grid=(n2 // N2_GRP,),
        in_specs=[pl.BlockSpec((N2_GRP * r, zw), lambda j: (j, 0)),
                  _const_spec((zw, 2 * hid)), _const_spec((1, 2 * hid)),
                  _const_spec((2 * hid, 2 * hid)), _const_spec((1, 2 * hid)),
                  _const_spec((2 * hid, 2 * hid)), _const_spec((1, 2 * hid)),
                  _const_spec((1, 2 * hid)),
                  _const_spec((2 * hid, od)), _const_spec((2 * hid, od)), _const_spec((1, od)),
                  _const_spec((2 * n1, n1)), _const_spec((n1, n2)), _const_spec((n1, n2))],
        out_specs=[pl.BlockSpec((2, n1 // K1_GRP, od // LANES, N2_GRP * PAIRS, LANES), lambda j: (0, 0, 0, j, 0)),
                   _const_spec((8, od))],
        out_shape=[jax.ShapeDtypeStruct((2, n1 // K1_GRP, od // LANES, n2 * PAIRS, LANES), jnp.uint32),
                   jax.ShapeDtypeStruct((8, od), F32)],
        compiler_params=_cp("arbitrary"),
        name="hyena_filter",
    )(z, w1, twice(f_b1), blockdiag(f_w2), twice(f_b2), blockdiag(f_w3), twice(f_b3), twice(f_freq),
      w4f, w4b, dl, cst["f1_full"], cst["tre"], cst["tim"])


def _fft_factors(seq):
    n = 2 * seq
    n1 = 256 if n >= 32768 else 32
    n2 = n // n1
    assert n1 * n2 == n and n2 % 16 == 0 and n1 % 32 == 0
    return n1, n2


@functools.lru_cache(maxsize=None)
def _fft_consts(seq):
    n = 2 * seq
    n1, n2 = _fft_factors(seq)
    k1 = np.arange(n1, dtype=np.float64)[:, None] + 0.5
    th1 = 2.0 * np.pi * k1 * np.arange(n1, dtype=np.float64)[None] / n1
    perm = np.concatenate([np.arange(0, n1, 2), np.arange(1, n1, 2)])
    f1_full = np.concatenate([np.cos(th1)[perm], -np.sin(th1)[perm]], axis=0)
    f1_half = f1_full[:, :n1 // 2]
    tw = 2.0 * np.pi * k1 * np.arange(n2, dtype=np.float64)[None] / n
    tre, tim = np.cos(tw), -np.sin(tw)
    h2 = n2 // 2
    th2 = 2.0 * np.pi * np.arange(h2, dtype=np.float64)[:, None] * np.arange(n2, dtype=np.float64)[None] / n2
    c2, s2 = np.cos(th2), np.sin(th2)
    m2f = np.block([[c2, s2], [-s2, c2]])
    m2i = np.block([[c2.T, -s2.T], [s2.T, c2.T]])
    thb = th1[:, :n1 // 2].T
    gi = (2.0 / n) * np.concatenate([np.cos(thb)[:, perm], -np.sin(thb)[:, perm]], axis=1)
    bf = lambda a: jnp.asarray(a.astype(np.float32)).astype(BF16)
    f32 = lambda a: jnp.asarray(a.astype(np.float32))
    grp = lambda a: a.T.reshape(n2, n1 // K1_GRP, K1_GRP).transpose(1, 0, 2)
    return dict(f1_full=bf(f1_full), f1_half=bf(f1_half), tre=f32(tre[perm]), tim=f32(tim[perm]),
                tre_grp=f32(grp(tre)), tim_grp=f32(grp(tim)), m2f=bf(m2f), m2i=bf(m2i), gi=bf(gi))


def _pick_col(tbl, idx):
    lane = lax.broadcasted_iota(I32, tbl.shape, 1)
    return jnp.sum(jnp.where(lane == idx, tbl, 0.0), axis=1, keepdims=True)


K1_GRP = 16
PAIRS = K1_GRP // 2
LANES = 128
N2_GRP = 4
MID_TC = 512


def _unpack_pair(w, half):
    bits = lax.shift_left(w, jnp.uint32(16)) if half == 0 else (w & jnp.uint32(0xFFFF0000))
    return pltpu.bitcast(bits, F32).astype(BF16)


def _pack_pair(even, odd):
    ue = pltpu.bitcast(even.astype(BF16).astype(F32), jnp.uint32)
    uo = pltpu.bitcast(odd.astype(BF16).astype(F32), jnp.uint32)
    return lax.shift_right_logical(ue, jnp.uint32(16)) | uo


def _s1_store(a, tre_ref, tim_ref, j, o_ref, n1, slot):
    are, aim = a[:n1], a[n1:]
    tre = _pick_col(tre_ref[...], j)
    tim = _pick_col(tim_ref[...], j)
    re = are * tre - aim * tim
    im = are * tim + aim * tre
    h = n1 // 2
    shape = (n1 // K1_GRP, PAIRS, a.shape[1])
    for part, val in enumerate((re, im)):
        words = _pack_pair(val[:h], val[h:]).reshape(shape)
        for ct in range(a.shape[1] // LANES):
            o_ref[part, :, ct, slot * PAIRS:(slot + 1) * PAIRS, :] = words[:, :, ct * LANES:(ct + 1) * LANES]


def _fft_s1_kernel(f1_ref, x_ref, tre_ref, tim_ref, o_ref, *, n1, tc):
    j0 = pl.program_id(0) * N2_GRP
    for slot in range(N2_GRP):
        a = _dot(f1_ref[...], x_ref[:, slot * tc:(slot + 1) * tc])
        _s1_store(a, tre_ref, tim_ref, j0 + slot, o_ref, n1, slot)


def _fft_stage1(x2d, f1, tre, tim, n1, n2, chans):
    r = x2d.shape[0]
    return pl.pallas_call(
        functools.partial(_fft_s1_kernel, n1=n1, tc=chans),
        grid=(n2 // N2_GRP,),
        in_specs=[_const_spec((2 * n1, r)),
                  pl.BlockSpec((r, N2_GRP * chans), lambda j: (0, j)),
                  _const_spec((n1, n2)), _const_spec((n1, n2))],
        out_specs=pl.BlockSpec((2, n1 // K1_GRP, chans // LANES, N2_GRP * PAIRS, LANES),
                               lambda j: (0, 0, 0, j, 0)),
        out_shape=jax.ShapeDtypeStruct((2, n1 // K1_GRP, chans // LANES, n2 * PAIRS, LANES), jnp.uint32),
        compiler_params=_cp("parallel"),
        name="fft_stage1",
    )(f1, x2d, tre, tim)


def _load_pair(b_ref, part, s, n2):
    return jnp.concatenate([b_ref[part, 0, ct, pl.ds(s, n2, stride=PAIRS), :] for ct in range(b_ref.shape[2])],
                           axis=1)


def _store_pair(o_ref, part, s, n2, words):
    for ct in range(o_ref.shape[2]):
        o_ref[part, 0, ct, pl.ds(s, n2, stride=PAIRS), :] = words[:, ct * LANES:(ct + 1) * LANES]


def _fft_s2_kernel(m2f_ref, b_ref, o_ref):
    n2 = m2f_ref.shape[0]
    for s in range(PAIRS):
        wre = _load_pair(b_ref, 0, s, n2)
        wim = _load_pair(b_ref, 1, s, n2)
        for half in range(2):
            b = jnp.concatenate([_unpack_pair(wre, half), _unpack_pair(wim, half)], axis=0)
            o_ref[2 * s + half] = _dot(m2f_ref[...], b).astype(BF16)


def _fft_stage2(b5, m2f, n1, n2, chans):
    return pl.pallas_call(
        _fft_s2_kernel,
        grid=(n1 // K1_GRP, chans // MID_TC),
        in_specs=[_const_spec((n2, 2 * n2)),
                  pl.BlockSpec((2, 1, MID_TC // LANES, n2 * PAIRS, LANES), lambda g, c: (0, g, c, 0, 0))],
        out_specs=pl.BlockSpec((K1_GRP, n2, MID_TC), lambda g, c: (g, 0, c)),
        out_shape=jax.ShapeDtypeStruct((n1, n2, chans), BF16),
        compiler_params=_cp("parallel", "parallel"),
        name="fft_stage2",
    )(m2f, b5)


def _fft_mid_kernel(m2f_ref, m2i_ref, b_ref, k_ref, inv_ref, tre_ref, tim_ref, o_ref, *, n2):
    h2 = n2 // 2
    inv = inv_ref[...]
    tre_g = tre_ref[0]
    tim_g = tim_ref[0]
    for s in range(PAIRS):
        wre = _load_pair(b_ref, 0, s, n2)
        wim = _load_pair(b_ref, 1, s, n2)
        res = []
        for half in range(2):
            kl = 2 * s + half
            b = jnp.concatenate([_unpack_pair(wre, half), _unpack_pair(wim, half)], axis=0)
            x = _dot(m2f_ref[...], b)
            kk = k_ref[kl].astype(F32) * inv
            xre, xim = x[:h2], x[h2:]
            kre, kim = kk[:h2], kk[h2:]
            y = jnp.concatenate([xre * kre - xim * kim, xre * kim + xim * kre], axis=0).astype(BF16)
            c = _dot(m2i_ref[...], y)
            cre, cim = c[:n2], c[n2:]
            tre = tre_g[:, kl:kl + 1]
            tim = tim_g[:, kl:kl + 1]
            res.append((cre * tre + cim * tim, cim * tre - cre * tim))
        _store_pair(o_ref, 0, s, n2, _pack_pair(res[0][0], res[1][0]))
        _store_pair(o_ref, 1, s, n2, _pack_pair(res[0][1], res[1][1]))


def _fft_mid(b5, kspec, inv_norm, order, cst, n1, n2):
    d = D_MODEL
    nc = d // MID_TC
    blk = pl.BlockSpec((2, 1, MID_TC // LANES, n2 * PAIRS, LANES), lambda g, c: (0, g, c, 0, 0))
    tw = pl.BlockSpec((1, n2, K1_GRP), lambda g, c: (g, 0, 0))
    return pl.pallas_call(
        functools.partial(_fft_mid_kernel, n2=n2),
        grid=(n1 // K1_GRP, nc),
        in_specs=[_const_spec((n2, 2 * n2)), _const_spec((2 * n2, n2)), blk,
                  pl.BlockSpec((K1_GRP, n2, MID_TC), lambda g, c: (g, 0, order * nc + c)),
                  pl.BlockSpec((1, MID_TC), lambda g, c: (0, order * nc + c)),
                  tw, tw],
        out_specs=blk,
        out_shape=jax.ShapeDtypeStruct((2, n1 // K1_GRP, d // LANES, n2 * PAIRS, LANES), jnp.uint32),
        compiler_params=_cp("parallel", "parallel"),
        name="fft_mid",
    )(cst["m2f"], cst["m2i"], b5, kspec, inv_norm, cst["tre_grp"], cst["tim_grp"])


def _fft_last_kernel(gi_ref, c_ref, gate_ref, z_ref, skip_ref, o_ref):
    h = c_ref.shape[1] * PAIRS
    d = c_ref.shape[2] * LANES
    for slot in range(N2_GRP):
        rows = slice(slot * PAIRS, (slot + 1) * PAIRS)
        cols = slice(slot * d, (slot + 1) * d)

        def words(part):
            return jnp.concatenate([c_ref[part, :, ct, rows, :].reshape(h, LANES)
                                    for ct in range(c_ref.shape[2])], axis=1)

        wre = words(0)
        wim = words(1)
        c = jnp.concatenate([_unpack_pair(wre, 0), _unpack_pair(wre, 1), _unpack_pair(wim, 0),
                             _unpack_pair(wim, 1)], axis=0)
        y = _dot(gi_ref[...], c)
        z = z_ref[:, cols].astype(F32)
        o_ref[:, cols] = (gate_ref[:, cols].astype(F32) * (y + skip_ref[...] * z)).astype(BF16)


def _fft_last(c5, gate2d, z2d, skip_row, gi, n1, n2):
    d = D_MODEL
    r = n1 // 2
    col = pl.BlockSpec((r, N2_GRP * d), lambda j: (0, j))
    return pl.pallas_call(
        _fft_last_kernel,
        grid=(n2 // N2_GRP,),
        in_specs=[_const_spec((r, 2 * n1)),
                  pl.BlockSpec((2, n1 // K1_GRP, d // LANES, N2_GRP * PAIRS, LANES), lambda j: (0, 0, 0, j, 0)),
                  col, col, _const_spec((1, d))],
        out_specs=col,
        out_shape=jax.ShapeDtypeStruct((r, n2 * d), BF16),
        compiler_params=_cp("parallel"),
        name="fft_last",
    )(gi, c5, gate2d, z2d, skip_row)


def _long_conv_gate(z_in, gate, skip_row, kspec, inv_norm, order, seq):
    d = D_MODEL
    n1, n2 = _fft_factors(seq)
    cst = _fft_consts(seq)
    z2d = z_in.reshape(n1 // 2, n2 * d)
    b = _fft_stage1(z2d, cst["f1_half"], cst["tre"], cst["tim"], n1, n2, d)
    c = _fft_mid(b, kspec, inv_norm, order, cst, n1, n2)
    out = _fft_last(c, gate.reshape(n1 // 2, n2 * d), z2d, skip_row, cst["gi"], n1, n2)
    return out.reshape(seq, d)


def _filter_spectrum(seq, fparams):
    d = D_MODEL
    od = HY_ORDER * d
    n1, n2 = _fft_factors(seq)
    cst = _fft_consts(seq)
    b, asum = _filter_stage1(seq, *fparams)
    kspec = _fft_stage2(b, cst["m2f"], n1, n2, od)
    return kspec, asum


def _proj_res_kernel(a_ref, w_ref, b_ref, g_ref, x_ref, o_ref):
    y = _dot(a_ref[...], w_ref[...]) + b_ref[...]
    o_ref[...] = x_ref[...] + g_ref[...] * y


def _proj_res(a, w_bf, b_row, gate_row, xres):
    n, d = xres.shape
    tm = min(512, n)
    row = pl.BlockSpec((tm, d), lambda i: (i, 0))
    return pl.pallas_call(
        _proj_res_kernel,
        grid=(n // tm,),
        in_specs=[row, _const_spec((d, d)), _const_spec((1, d)), _const_spec((1, d)), row],
        out_specs=row,
        out_shape=jax.ShapeDtypeStruct((n, d), F32),
        compiler_params=_cp("parallel"),
        name="proj_residual",
    )(a, w_bf, b_row, gate_row, xres)


def _ffn_in_kernel(x_ref, mod_ref, wt_ref, h_ref, aff_ref):
    h = _norm_mod(x_ref[...], mod_ref[...])
    hi = h.astype(BF16)
    lo = (h - hi.astype(F32)).astype(BF16)
    wt = wt_ref[...]
    whi = wt.astype(BF16)
    wlo = (wt - whi.astype(F32)).astype(BF16)
    logits = _dot_nt(whi, hi) + (_dot_nt(whi, lo) + _dot_nt(wlo, hi))
    m = jnp.max(logits, axis=0, keepdims=True)
    p = jnp.exp(logits - m)
    aff_ref[...] = p / jnp.sum(p, axis=0, keepdims=True)
    h_ref[...] = hi


def _ffn_in(x, mod, w_router):
    n, d = x.shape
    e = N_EXPERTS
    tm = min(512, n)
    return pl.pallas_call(
        _ffn_in_kernel,
        grid=(n // tm,),
        in_specs=[pl.BlockSpec((tm, d), lambda i: (i, 0)), _const_spec((8, d)), _const_spec((e, d))],
        out_specs=[pl.BlockSpec((tm, d), lambda i: (i, 0)), pl.BlockSpec((e, tm), lambda i: (0, i))],
        out_shape=[jax.ShapeDtypeStruct((n, d), BF16), jax.ShapeDtypeStruct((e, n), F32)],
        compiler_params=_cp("parallel"),
        name="moe_router",
    )(x, mod, w_router.T)


def _select_kernel(a_ref, pos_ref, s0_ref, sel_ref, *, cap, nblk):
    e = N_EXPERTS
    bits = pltpu.bitcast(a_ref[...], I32)

    def bisect(i, thr):
        cand = thr | jnp.left_shift(jnp.int32(1), 30 - i)
        cnt = jnp.sum(jnp.where(bits >= cand, 1.0, 0.0), axis=1, keepdims=True)
        return jnp.where(cnt >= cap, cand, thr)

    thr = lax.fori_loop(0, 31, bisect, jnp.zeros((e, 1), I32))
    n_gt = jnp.sum(jnp.where(bits > thr, 1.0, 0.0), axis=1, keepdims=True)
    need = cap - n_gt
    r = lax.broadcasted_iota(I32, (TOK_BLK, TOK_BLK), 0)
    c = lax.broadcasted_iota(I32, (TOK_BLK, TOK_BLK), 1)
    upper = jnp.where(r < c, 1.0, 0.0).astype(BF16)

    def pass1(j, carry):
        sl = pl.ds(pl.multiple_of(j * TOK_BLK, TOK_BLK), TOK_BLK)
        bj = pltpu.bitcast(a_ref[:, sl], I32)
        eq = jnp.where(bj == thr, 1.0, 0.0)
        rank = _dot(eq.astype(BF16), upper) + carry
        keep = jnp.logical_or(bj > thr, jnp.logical_and(bj == thr, rank < need))
        sel_ref[:, sl] = jnp.where(keep, 1.0, 0.0)
        return carry + jnp.sum(eq, axis=1, keepdims=True)

    lax.fori_loop(0, nblk, pass1, jnp.zeros((e, 1), F32))

    def pass2(j, carry):
        sl = pl.ds(pl.multiple_of(j * TOK_BLK, TOK_BLK), TOK_BLK)
        s = sel_ref[:, sl]
        slot = _dot(s.astype(BF16), upper) + carry
        pos_ref[:, sl] = jnp.where(s > 0.5, slot, -1.0).astype(I32)
        s0_ref[j] = jnp.broadcast_to(carry, (e, 128)).astype(I32)
        return carry + jnp.sum(s, axis=1, keepdims=True)

    total = lax.fori_loop(0, nblk, pass2, jnp.zeros((e, 1), F32))
    s0_ref[nblk] = jnp.broadcast_to(total, (e, 128)).astype(I32)


def _select(aff_t, cap):
    e, n = aff_t.shape
    nblk = n // TOK_BLK
    return pl.pallas_call(
        functools.partial(_select_kernel, cap=cap, nblk=nblk),
        out_shape=[jax.ShapeDtypeStruct((e, n), I32), jax.ShapeDtypeStruct((nblk + 1, e, 128), I32)],
        scratch_shapes=[pltpu.VMEM((e, n), F32)],
        compiler_params=pltpu.CompilerParams(vmem_limit_bytes=VMEM_LIMIT),
        name="moe_select",
    )(aff_t)


def _block_windows(s0_ref, blk, e_idx):
    s0 = s0_ref[blk * N_EXPERTS + e_idx]
    s1 = s0_ref[(blk + 1) * N_EXPERTS + e_idx]
    start = lax.shift_left(lax.shift_right_logical(s0, 4), 4)
    nwin = jnp.where(s1 > s0, lax.shift_right_logical(s1 - start + (WIN - 1), WIN_SHIFT), 0)
    return start, nwin


def _token_block(i):
    start = i * TOK_BLK
    return pl.ds(start if isinstance(i, int) else pl.multiple_of(start, TOK_BLK), TOK_BLK)


def _gather_kernel(s0_ref, pos_ref, h_ref, xe_ref, *, sub):
    e_idx = pl.program_id(0)
    c = pl.program_id(1)

    @pl.when(c == 0)
    def _():
        xe_ref[...] = jnp.zeros_like(xe_ref)

    rows = lax.broadcasted_iota(I32, (WIN, TOK_BLK), 0)

    def window(i, base):
        tok = _token_block(i)
        base = pl.multiple_of(base, ROW_ALIGN)
        onehot = jnp.where(rows == pos_ref[0, :, tok] - base, 1.0, 0.0).astype(BF16)
        got = _dot(onehot, h_ref[tok, :]).astype(BF16)
        win = pl.ds(base, WIN)
        xe_ref[0, win, :] = xe_ref[0, win, :] + got

    plan = [_block_windows(s0_ref, c * sub + i, e_idx) for i in range(sub)]
    single = functools.reduce(jnp.logical_and, [nwin <= 1 for _, nwin in plan])

    @pl.when(single)
    def _():
        for i, (start, _) in enumerate(plan):
            window(i, start)

    @pl.when(jnp.logical_not(single))
    def _():
        def body(i, carry):
            start, nwin = _block_windows(s0_ref, c * sub + i, e_idx)
            return lax.fori_loop(0, nwin, lambda w, cc: (window(i, start + w * WIN), cc)[1], carry)

        lax.fori_loop(0, sub, body, 0)


def _gather(s0_flat, pos, h, cap_pad):
    e, n = pos.shape
    d = h.shape[1]
    chunk = min(2048, n)
    sub = chunk // TOK_BLK
    return pl.pallas_call(
        functools.partial(_gather_kernel, sub=sub),
        grid_spec=pltpu.PrefetchScalarGridSpec(
            num_scalar_prefetch=1,
            grid=(e, n // chunk),
            in_specs=[pl.BlockSpec((1, 1, chunk), lambda ei, c, s0: (ei, 0, c)),
                      pl.BlockSpec((chunk, d), lambda ei, c, s0: (c, 0))],
            out_specs=pl.BlockSpec((1, cap_pad, d), lambda ei, c, s0: (ei, 0, 0)),
        ),
        out_shape=jax.ShapeDtypeStruct((e, cap_pad, d), BF16),
        compiler_params=_cp("parallel", "arbitrary"),
        name="moe_gather",
    )(s0_flat, pos.reshape(e, 1, n), h)


def _expert_kernel(x_ref, wg_ref, wu_ref, wd_ref, y_ref, wg_scr, wu_scr, wd_scr, *, n_real):
    j = pl.program_id(1)

    @pl.when(j == 0)
    def _():
        wg_scr[...] = wg_ref[0].astype(BF16)
        wu_scr[...] = wu_ref[0].astype(BF16)
        wd_scr[...] = wd_ref[0].astype(BF16)

    @pl.when(j < n_real)
    def _():
        x = x_ref[0]
        g = _dot(x, wg_scr[...])
        u = _dot(x, wu_scr[...])
        a = (g / (1.0 + jnp.exp(-g))) * u
        y_ref[0] = _dot(a.astype(BF16), wd_scr[...]).astype(BF16)

    @pl.when(j >= n_real)
    def _():
        y_ref[0] = jnp.zeros_like(y_ref[0])


def _experts(xe, wg, wu, wd, layer, cap):
    e, cap_pad, d = xe.shape
    f = wg.shape[3]
    tm = 256 if cap % 256 == 0 else cap_pad
    tile = pl.BlockSpec((1, tm, d), lambda ei, j: (ei, j, 0))
    return pl.pallas_call(
        functools.partial(_expert_kernel, n_real=pl.cdiv(cap, tm)),
        grid=(e, cap_pad // tm),
        in_specs=[tile,
                  pl.BlockSpec((None, 1, d, f), lambda ei, j: (layer, ei, 0, 0)),
                  pl.BlockSpec((None, 1, d, f), lambda ei, j: (layer, ei, 0, 0)),
                  pl.BlockSpec((None, 1, f, d), lambda ei, j: (layer, ei, 0, 0))],
        out_specs=tile,
        out_shape=jax.ShapeDtypeStruct((e, cap_pad, d), BF16),
        scratch_shapes=[pltpu.VMEM((d, f), BF16), pltpu.VMEM((d, f), BF16), pltpu.VMEM((f, d), BF16)],
        compiler_params=_cp("parallel", "arbitrary"),
        name="moe_experts",
    )(xe, wg, wu, wd)


def _combine_kernel(s0_ref, post_ref, gt_ref, y_ref, x_ref, gate_ref, o_ref, *, sub):
    c = pl.program_id(0)
    e_idx = pl.program_id(1)

    @pl.when(e_idx == 0)
    def _():
        o_ref[...] = jnp.zeros_like(o_ref)

    lanes = lax.broadcasted_iota(I32, (TOK_BLK, WIN), 1)
    elane = lax.broadcasted_iota(I32, (TOK_BLK, N_EXPERTS), 1) == e_idx

    def window(i, base):
        tok = _token_block(i)
        base = pl.multiple_of(base, ROW_ALIGN)
        pcol = jnp.sum(jnp.where(elane, post_ref[tok, :].astype(F32), 0.0), axis=1, keepdims=True)
        gcol = jnp.sum(jnp.where(elane, gt_ref[tok, :], 0.0), axis=1, keepdims=True)
        rel = pcol.astype(I32) - base
        onehot = jnp.where(lanes == rel, 1.0, 0.0).astype(BF16)
        strip = y_ref[0, pl.ds(base, WIN), :]
        o_ref[tok, :] = o_ref[tok, :] + gcol * _dot(onehot, strip)

    plan = [_block_windows(s0_ref, c * sub + i, e_idx) for i in range(sub)]
    single = functools.reduce(jnp.logical_and, [nwin <= 1 for _, nwin in plan])

    @pl.when(single)
    def _():
        for i, (start, _) in enumerate(plan):
            window(i, start)

    @pl.when(jnp.logical_not(single))
    def _():
        def body(i, carry):
            start, nwin = _block_windows(s0_ref, c * sub + i, e_idx)
            return lax.fori_loop(0, nwin, lambda w, cc: (window(i, start + w * WIN), cc)[1], carry)

        lax.fori_loop(0, sub, body, 0)

    @pl.when(e_idx == N_EXPERTS - 1)
    def _():
        o_ref[...] = x_ref[...] + gate_ref[...] * o_ref[...]


def _combine(s0_flat, pos_t, g_t, y, xres, gate_row):
    n, d = xres.shape
    e, cap_pad, _ = y.shape
    chunk = min(2048, n)
    sub = chunk // TOK_BLK
    return pl.pallas_call(
        functools.partial(_combine_kernel, sub=sub),
        grid_spec=pltpu.PrefetchScalarGridSpec(
            num_scalar_prefetch=1,
            grid=(n // chunk, e),
            in_specs=[pl.BlockSpec((chunk, e), lambda c, ei, s0: (c, 0)),
                      pl.BlockSpec((chunk, e), lambda c, ei, s0: (c, 0)),
                      pl.BlockSpec((1, cap_pad, d), lambda c, ei, s0: (ei, 0, 0)),
                      pl.BlockSpec((chunk, d), lambda c, ei, s0: (c, 0)),
                      pl.BlockSpec((1, d), lambda c, ei, s0: (0, 0))],
            out_specs=pl.BlockSpec((chunk, d), lambda c, ei, s0: (c, 0)),
        ),
        out_shape=jax.ShapeDtypeStruct((n, d), F32),
        compiler_params=_cp("parallel", "arbitrary"),
        name="moe_combine",
    )(s0_flat, pos_t, g_t, y, xres, gate_row)


def _moe_block(x, mod, gate_row, w_router, wg, wu, wd, layer):
    n = x.shape[0]
    cap = EC_CAPACITY * n // N_EXPERTS
    cap_pad = cap + TOK_BLK
    h, aff_t = _ffn_in(x, mod, w_router)
    pos, s0 = _select(aff_t, cap)
    s0_flat = s0[:, :, 0].reshape(-1)
    xe = _gather(s0_flat, pos, h, cap_pad)
    y = _experts(xe, wg, wu, wd, layer, cap)
    return _combine(s0_flat, pos.T, aff_t.T, y, x, gate_row)


QK_SCALE = (DA_HEAD_DIM ** -0.5) * math.log2(math.e)


def _split_bf16(x):
    hi = x.astype(BF16)
    return hi, (x - hi.astype(F32)).astype(BF16)


def _group_rms(u, gsel_ref, gain, eps):
    g = gsel_ref[...]
    hi, lo = _split_bf16(u * u)
    r = lax.rsqrt((_dot(hi, g) + _dot(lo, g)) * (1.0 / DA_HEAD_DIM) + eps)
    rhi, rlo = _split_bf16(r)
    return u * (_dot_nt(rhi, g) + _dot_nt(rlo, g)) * gain


def _rope(u, cos, sin_signed):
    d = u.shape[1]
    half = ROPE_AXIS_DIM // 2
    lane = lax.broadcasted_iota(I32, u.shape, 1)
    first = (lane & half) == 0
    swapped = jnp.where(first, pltpu.roll(u, d - half, 1), pltpu.roll(u, half, 1))
    return u * cos + swapped * sin_signed


def _qkv_kernel(*refs, rope):
    x_ref, mod_ref, w_ref, gsum_ref, qn_ref, kn_ref = refs[:6]
    q_ref, k_ref, v_ref = refs[-3:]
    if rope:
        cos_ref, sin_ref = refs[6:8]
    d = D_MODEL
    h = _norm_mod(x_ref[...], mod_ref[...]).astype(BF16)
    if rope:
        reps = d // cos_ref.shape[1]
        cos = _lane_tile(cos_ref[...], reps)
        sin = _lane_tile(sin_ref[...], reps)
    for part, (o_ref, gain_ref) in enumerate(((q_ref, qn_ref), (k_ref, kn_ref))):
        u = _dot(h, w_ref[:, part * d:(part + 1) * d])
        u = _group_rms(u, gsum_ref, gain_ref[...], NORM_EPS)
        if rope:
            u = _rope(u, cos, sin)
        if part == 0:
            u = u * QK_SCALE
        o_ref[...] = u.astype(BF16)
    v_ref[...] = _dot(h, w_ref[:, 2 * d:]).astype(BF16)


def _qkv(x, mod, w_bf, gsum, qn_row, kn_row, cos=None, sin=None, kv_rows=None, kv_offset=0, kv_into=None):
    n, d = x.shape
    tm = min(256, n)
    kv_rows = n if kv_rows is None else kv_rows
    off = kv_offset // tm
    assert kv_offset % tm == 0
    rope = cos is not None
    row = pl.BlockSpec((tm, d), lambda i: (i, 0))
    kv_row = pl.BlockSpec((tm, d), lambda i: (i + off, 0))
    in_specs = [row, _const_spec((8, d)), _const_spec((d, 3 * d)), _const_spec(gsum.shape),
                _const_spec((1, d)), _const_spec((1, d))]
    args = [x, mod, w_bf, gsum, qn_row, kn_row]
    if rope:
        tw = cos.shape[1]
        in_specs += [pl.BlockSpec((tm, tw), lambda i: (i, 0)), pl.BlockSpec((tm, tw), lambda i: (i, 0))]
        args += [cos, sin]
    aliases = {}
    if kv_into is not None:
        aliases = {len(args): 1, len(args) + 1: 2}
        in_specs += [pl.BlockSpec(memory_space=pl.ANY), pl.BlockSpec(memory_space=pl.ANY)]
        args += list(kv_into)
    kv = jax.ShapeDtypeStruct((kv_rows, d), BF16)
    return pl.pallas_call(
        functools.partial(_qkv_kernel, rope=rope),
        grid=(n // tm,),
        in_specs=in_specs,
        out_specs=[row, kv_row, kv_row],
        out_shape=[jax.ShapeDtypeStruct((n, d), BF16), kv, kv],
        input_output_aliases=aliases,
        compiler_params=_cp("parallel"),
        name="attn_qkv",
    )(*args)


NEG_BIG = -1e30
HEAD_UNROLL = 8


def _attn_kernel(q_ref, k_ref, v_ref, lam_ref, sub_ref, o_ref, m_scr, acc_scr, sa_scr, sb_scr, ma_scr, mb_scr,
                 pa_scr, pb_scr, vprev_scr, *,
                 lam_init):
    j = pl.program_id(1)
    nj = pl.num_programs(1)
    hd, vd = DA_HEAD_DIM, DA_V_DIM
    tk = k_ref.shape[0]

    last = DA_HEADS - 1

    @pl.when(j == 0)
    def _():
        m_scr[...] = jnp.full_like(m_scr, NEG_BIG)
        acc_scr[...] = jnp.zeros_like(acc_scr)
        pa_scr[...] = jnp.zeros_like(pa_scr)
        sb_scr[...] = jnp.full_like(sb_scr, 2.0 * NEG_BIG)
        mb_scr[...] = jnp.full_like(mb_scr, 2.0 * NEG_BIG)
        vprev_scr[...] = jnp.zeros_like(vprev_scr)

    lane = lax.broadcasted_iota(I32, (q_ref.shape[0], vd), 1)
    ones_col = jnp.where(lax.broadcasted_iota(I32, (tk, vd), 1) == 0, 1.0, 0.0).astype(BF16)

    def head_cols(h):
        return pl.ds(pl.multiple_of(h * vd, vd), vd)

    def scores(h, comp, s_ref, mx_ref):
        qb = q_ref[:, head_cols(h)]
        in_comp = (lane >= comp * hd) & (lane < (comp + 1) * hd)
        qm = jnp.where(in_comp, qb, jnp.zeros_like(qb))
        s = _dot_nt(qm, k_ref[:, head_cols(h)])
        s_ref[...] = s
        mx_ref[...] = jnp.broadcast_to(jnp.max(s, axis=1, keepdims=True), mx_ref.shape)

    def softmax_step(h, comp, s_ref, mx_ref, p_ref):
        idx = 2 * h + comp
        m_prev = m_scr[idx]
        m_new = jnp.maximum(m_prev, mx_ref[...])
        alpha = _lane_tile(jnp.exp2(m_prev - m_new), 2)
        p_ref[...] = jnp.exp2(s_ref[...] - m_new[:, 0:1]).astype(BF16)
        m_scr[idx] = m_new
        acc_scr[idx] = alpha * acc_scr[idx]

    def pv_step(h, comp, p_ref):
        idx = 2 * h + comp
        acc_scr[idx] = acc_scr[idx] + _dot(p_ref[...], jnp.concatenate([vprev_scr[...], ones_col], axis=1))

    def head(h, carry):
        hp = (h + last) & last
        scores(h, 0, sa_scr, ma_scr)
        pv_step(hp, 0, pa_scr)
        softmax_step(hp, 1, sb_scr, mb_scr, pb_scr)
        scores(h, 1, sb_scr, mb_scr)
        pv_step(hp, 1, pb_scr)
        softmax_step(h, 0, sa_scr, ma_scr, pa_scr)
        vprev_scr[...] = v_ref[:, head_cols(h)]
        return carry

    def head_group(g, carry):
        for u in range(HEAD_UNROLL):
            carry = head(HEAD_UNROLL * g + u, carry)
        return carry

    lax.fori_loop(0, DA_HEADS // HEAD_UNROLL, head_group, 0)

    @pl.when(j == nj - 1)
    def _():
        pv_step(last, 0, pa_scr)
        softmax_step(last, 1, sb_scr, mb_scr, pb_scr)
        pv_step(last, 1, pb_scr)
        lp = lam_ref[...]
        lam = (jnp.exp(jnp.sum(lp[0:1] * lp[1:2], axis=1, keepdims=True))
               - jnp.exp(jnp.sum(lp[2:3] * lp[3:4], axis=1, keepdims=True)) + lam_init)
        for h in range(DA_HEADS):
            a0 = acc_scr[2 * h]
            a1 = acc_scr[2 * h + 1]
            o = a0[:, :vd] / a0[:, vd:vd + 1] - lam * (a1[:, :vd] / a1[:, vd:vd + 1])
            ms = jnp.mean(o * o, axis=1, keepdims=True)
            o = o * lax.rsqrt(ms + SUBLN_EPS) * (sub_ref[...] * (1.0 - lam_init))
            o_ref[:, h * vd:(h + 1) * vd] = o.astype(BF16)


def _attention(q, k_all, v_all, lam_rows, subln_row, lam_init):
    n, d = q.shape
    nk = k_all.shape[0]
    tq = min(512, n)
    tk = 1280 if nk % 1280 == 0 else 256
    assert nk % tk == 0
    nc = 2 * DA_HEADS
    return pl.pallas_call(
        functools.partial(_attn_kernel, lam_init=lam_init),
        grid=(n // tq, nk // tk),
        in_specs=[pl.BlockSpec((tq, d), lambda i, j: (i, 0)),
                  pl.BlockSpec((tk, d), lambda i, j: (j, 0)),
                  pl.BlockSpec((tk, d), lambda i, j: (j, 0)),
                  _const_spec((8, DA_HEAD_DIM)), _const_spec((1, DA_V_DIM))],
        out_specs=pl.BlockSpec((tq, d), lambda i, j: (i, 0)),
        out_shape=jax.ShapeDtypeStruct((n, d), BF16),
        scratch_shapes=[pltpu.VMEM((nc, tq, DA_V_DIM), F32), pltpu.VMEM((nc, tq, 2 * DA_V_DIM), F32),
                        pltpu.VMEM((tq, tk), F32), pltpu.VMEM((tq, tk), F32),
                        pltpu.VMEM((tq, DA_V_DIM), F32), pltpu.VMEM((tq, DA_V_DIM), F32),
                        pltpu.VMEM((tq, tk), BF16), pltpu.VMEM((tq, tk), BF16),
                        pltpu.VMEM((tk, DA_V_DIM), BF16)],
        compiler_params=_cp("parallel", "arbitrary"),
        name="diff_attention",
    )(q, k_all, v_all, lam_rows, subln_row)


def _rope_tables(n):
    rows = n // GRID_W
    inv = ROPE_THETA ** (-jnp.arange(0, ROPE_AXIS_DIM, 2, dtype=F32) / ROPE_AXIS_DIM)
    ar = jnp.arange(rows, dtype=F32)[:, None] * inv[None]
    ac = jnp.arange(GRID_W, dtype=F32)[:, None] * inv[None]
    nf = inv.shape[0]
    by_row = lambda a: jnp.broadcast_to(a[:, None, :], (rows, GRID_W, nf))
    by_col = lambda a: jnp.broadcast_to(a[None, :, :], (rows, GRID_W, nf))
    cr, sr, cc, sc = by_row(jnp.cos(ar)), by_row(jnp.sin(ar)), by_col(jnp.cos(ac)), by_col(jnp.sin(ac))
    cos = jnp.concatenate([cr, cr, cc, cc] * 2, axis=2).reshape(n, 8 * nf)
    sin = jnp.concatenate([-sr, sr, -sc, sc] * 2, axis=2).reshape(n, 8 * nf)
    return cos, sin


def _hyena_layer(x, mod, gate_row, kspec, inv_norm, w_in_bf, b_in, conv_w, conv_b, skip, w_out_bf, b_out):
    seq = x.shape[0]
    v, x1, x2 = _hy_in(x, mod, w_in_bf, b_in, conv_w, conv_b)
    z = _long_conv_gate(v, x1, skip[0:1], kspec, inv_norm, 0, seq)
    z = _long_conv_gate(z, x2, skip[1:2], kspec, inv_norm, 1, seq)
    return _proj_res(z, w_out_bf, b_out.reshape(1, -1), gate_row, x)


def kernel(x, c, ctx, c_ctx, ada_w, ada_b, norm_mix, norm_ffn, hy_w_in, hy_b_in, hy_conv_w, hy_conv_b, hy_f_w1, hy_f_b1, hy_f_w2, hy_f_b2, hy_f_w3, hy_f_b3, hy_f_w4, hy_f_freq, hy_skip, hy_w_out, hy_b_out, da_w_qkv, da_q_norm, da_k_norm, da_lam_q1, da_lam_k1, da_lam_q2, da_lam_k2, da_subln, da_w_out, moe_router, moe_w_gate, moe_w_up, moe_w_down):
    d = D_MODEL
    depth = ada_w.shape[0]
    assert x.shape[0] == 1 and x.shape[2] == d
    xs = x[0]
    cs = ctx[0]
    cond8 = jnp.concatenate([c[0:1], c_ctx[None], jnp.zeros((6, d), F32)], axis=0)
    mods = _adaln(cond8, ada_w, ada_b)

    def mod_slice(i, row, k):
        return mods[i, row, k * d:(k + 1) * d][None]

    for i in range(depth):
        last = i == depth - 1
        j = i // 2
        mix_x = _mod_rows(norm_mix[i], mods[i], 0, 0)
        mix_c = _mod_rows(norm_mix[i], mods[i], 1, 0)
        if i % 2 == 0:
            fparams = (hy_f_w1[j], hy_f_b1[j], hy_f_w2[j], hy_f_b2[j], hy_f_w3[j], hy_f_b3[j], hy_f_w4[j],
                       hy_f_freq[j])
            shared = (hy_w_in[j].astype(BF16), hy_b_in[j], hy_conv_w[j], hy_conv_b[j], hy_skip[j],
                      hy_w_out[j].astype(BF16), hy_b_out[j])
            kspec, asum = _filter_spectrum(xs.shape[0], fparams)
            inv_norm = 1.0 / (asum[0:1] + HY_FILTER_EPS)
            new_x = _hyena_layer(xs, mix_x, mod_slice(i, 0, 2), kspec, inv_norm, *shared)
            if not last:
                kspec_c, asum_c = _filter_spectrum(cs.shape[0], fparams)
                inv_c = 1.0 / (asum_c[0:1] + HY_FILTER_EPS)
                cs = _hyena_layer(cs, mix_c, mod_slice(i, 1, 2), kspec_c, inv_c, *shared)
            xs = new_x
        else:
            lam_init = 0.8 - 0.6 * math.exp(-0.3 * i)
            w_qkv = da_w_qkv[j].astype(BF16)
            gidx = np.arange(d) // DA_HEAD_DIM
            gsum = jnp.asarray((gidx[:, None] == np.arange(128)[None]).astype(np.float32)).astype(BF16)
            qn = jnp.tile(da_q_norm[j], 2 * DA_HEADS)[None]
            kn = jnp.tile(da_k_norm[j], 2 * DA_HEADS)[None]
            cos, sin = _rope_tables(xs.shape[0])
            n_ctx = cs.shape[0]
            nk = n_ctx + xs.shape[0]
            qx, k_all, v_all = _qkv(xs, mix_x, w_qkv, gsum, qn, kn, cos, sin, kv_rows=nk, kv_offset=n_ctx)
            qc, k_all, v_all = _qkv(cs, mix_c, w_qkv, gsum, qn, kn, kv_rows=nk, kv_into=(k_all, v_all))
            lam_rows = jnp.concatenate([da_lam_q1[j][None], da_lam_k1[j][None], da_lam_q2[j][None],
                                        da_lam_k2[j][None], jnp.zeros((4, DA_HEAD_DIM), F32)], axis=0)
            w_out = da_w_out[j].astype(BF16)
            zero_b = jnp.zeros((1, d), F32)
            ox = _attention(qx, k_all, v_all, lam_rows, da_subln[j][None], lam_init)
            new_x = _proj_res(ox, w_out, zero_b, mod_slice(i, 0, 2), xs)
            if not last:
                oc = _attention(qc, k_all[:n_ctx], v_all[:n_ctx], lam_rows, da_subln[j][None], lam_init)
                cs = _proj_res(oc, w_out, zero_b, mod_slice(i, 1, 2), cs)
            xs = new_x
        experts = (moe_w_gate, moe_w_up, moe_w_down, i)
        if not last:
            cs = _moe_block(cs, _mod_rows(norm_ffn[i], mods[i], 1, 3), mod_slice(i, 1, 5), moe_router[i], *experts)
        xs = _moe_block(xs, _mod_rows(norm_ffn[i], mods[i], 0, 3), mod_slice(i, 0, 5), moe_router[i], *experts)
    return xs[None]
```

```python
import functools
import math

import jax
import jax.numpy as jnp
import numpy as np
from jax import lax
from jax.experimental import pallas as pl
from jax.experimental.pallas import tpu as pltpu

F32 = jnp.float32
BF16 = jnp.bfloat16
I32 = jnp.int32
HIGHEST = lax.Precision.HIGHEST

D_MODEL = 1024
N_MOD = 6
NORM_EPS = 1e-6
GRID_W = 64
HY_ORDER = 2
HY_SHORT = 3
HY_EMB_BANDS = 16
HY_EMB_DIM = 1 + 2 * HY_EMB_BANDS
HY_FILTER_HIDDEN = 64
HY_DECAY_FAST = 0.3
HY_DECAY_SLOW = 1.5
HY_DECAY_TARGET = 1e-2
HY_FILTER_EPS = 1e-6
Z_SIGN_COL = 33
DA_HEADS = 8
DA_HEAD_DIM = 64
DA_V_DIM = 128
ROPE_AXIS_DIM = 32
ROPE_THETA = 10000.0
SUBLN_EPS = 1e-5
N_EXPERTS = 16
EC_CAPACITY = 2
D_EXPERT = 1024
TOK_BLK = 256
ROW_ALIGN = 16
WIN_SHIFT = 7
WIN = 1 << WIN_SHIFT

VMEM_LIMIT = 56 * 1024 * 1024


def _cp(*sem):
    return pltpu.CompilerParams(dimension_semantics=sem, vmem_limit_bytes=VMEM_LIMIT)


def _const_spec(shape):
    nd = len(shape)
    return pl.BlockSpec(shape, lambda *_: (0,) * nd)


def _dot(a, b):
    return jnp.dot(a, b, preferred_element_type=F32)


def _dot_nt(a, b):
    return lax.dot_general(a, b, (((1,), (1,)), ((), ())), preferred_element_type=F32)


def _norm_mod(x, mod, eps=NORM_EPS):
    ms = jnp.mean(x * x, axis=-1, keepdims=True)
    return x * lax.rsqrt(ms + eps) * (mod[0:1] * (1.0 + mod[1:2])) + mod[2:3]


def _lane_tile(x, reps):
    return jnp.concatenate([x] * reps, axis=1) if reps > 1 else x


def _adaln_kernel(c_ref, w_ref, b_ref, o_ref):
    c = c_ref[...]
    s = c / (1.0 + jnp.exp(-c))
    o_ref[0] = jnp.dot(s, w_ref[0], precision=HIGHEST, preferred_element_type=F32) + b_ref[0]


def _adaln(cond8, ada_w, ada_b):
    depth, d, nout = ada_w.shape
    tn = 1536
    return pl.pallas_call(
        _adaln_kernel,
        grid=(depth, nout // tn),
        in_specs=[_const_spec((8, d)),
                  pl.BlockSpec((1, d, tn), lambda l, j: (l, 0, j)),
                  pl.BlockSpec((1, 1, tn), lambda l, j: (l, 0, j))],
        out_specs=pl.BlockSpec((1, 8, tn), lambda l, j: (l, 0, j)),
        out_shape=jax.ShapeDtypeStruct((depth, 8, nout), F32),
        compiler_params=_cp("parallel", "parallel"),
        name="adaln",
    )(cond8, ada_w, ada_b.reshape(depth, 1, nout))


def _mod_rows(norm_g, mods, row, k0):
    d = D_MODEL
    shift = mods[row, k0 * d:(k0 + 1) * d]
    scale = mods[row, (k0 + 1) * d:(k0 + 2) * d]
    z = jnp.zeros((5, d), F32)
    return jnp.concatenate([norm_g[None], scale[None], shift[None], z], axis=0)


HALO = 16


def _hy_in_kernel(x_ref, xp_ref, xn_ref, mod_ref, w_ref, b_ref, cw_ref, v_ref, x1_ref, x2_ref, *, tm, n_rows):
    i = pl.program_id(0)
    mod = mod_ref[...]
    hm = _norm_mod(x_ref[...], mod).astype(BF16)
    hp = _norm_mod(xp_ref[...], mod).astype(BF16)
    hn = _norm_mod(xn_ref[...], mod).astype(BF16)
    hcat = jnp.concatenate([hp, hm, hn], axis=0)
    row = lax.broadcasted_iota(I32, (tm + 2 * HALO, 1), 0) + (i * tm - HALO)
    valid = jnp.logical_and(row >= 0, row < n_rows)
    d = D_MODEL
    for c, o_ref in enumerate((v_ref, x1_ref, x2_ref)):
        u = _dot(hcat, w_ref[:, c * d:(c + 1) * d]) + b_ref[:, c * d:(c + 1) * d]
        u = jnp.where(valid, u, 0.0)
        cw = cw_ref[:, c * d:(c + 1) * d]
        y = (cw[3:4] + cw[0:1] * u[HALO - 1:HALO - 1 + tm] + cw[1:2] * u[HALO:HALO + tm]
             + cw[2:3] * u[HALO + 1:HALO + 1 + tm])
        o_ref[...] = y.astype(BF16)


def _hy_in(x, mod, w_bf, b_in, conv_w, conv_b):
    n, d = x.shape
    tm = min(512, n)
    nh = n // HALO
    cw = jnp.concatenate([conv_w, conv_b[None], jnp.zeros((4, 3 * d), F32)], axis=0)
    out = jax.ShapeDtypeStruct((n, d), BF16)
    row_spec = pl.BlockSpec((tm, d), lambda i: (i, 0))
    return pl.pallas_call(
        functools.partial(_hy_in_kernel, tm=tm, n_rows=n),
        grid=(n // tm,),
        in_specs=[row_spec,
                  pl.BlockSpec((HALO, d), lambda i: (jnp.maximum(i * (tm // HALO) - 1, 0), 0)),
                  pl.BlockSpec((HALO, d), lambda i: (jnp.minimum((i + 1) * (tm // HALO), nh - 1), 0)),
                  _const_spec((8, d)), _const_spec((d, 3 * d)), _const_spec((1, 3 * d)),
                  _const_spec((8, 3 * d))],
        out_specs=[row_spec, row_spec, row_spec],
        out_shape=[out, out, out],
        compiler_params=_cp("parallel"),
        name="hyena_in",
    )(x, x, x, mod, w_bf, b_in.reshape(1, 3 * d), cw)


Z_HALF = 64


def _filter_s1_kernel(z_ref, w1_ref, b1_ref, w2_ref, b2_ref, w3_ref, b3_ref, fr_ref, w4f_ref, w4b_ref, dl_ref,
                      f1_ref, tre_ref, tim_ref, o_ref, asum_ref, *, n1):
    j0 = pl.program_id(0) * N2_GRP
    fr = fr_ref[...]
    dl = dl_ref[...]
    r = n1 // 2

    def lin(a, w_ref, b_ref):
        return jnp.dot(a, w_ref[...], precision=HIGHEST, preferred_element_type=F32) + b_ref[...]

    @pl.when(j0 == 0)
    def _():
        asum_ref[...] = jnp.zeros_like(asum_ref)

    for slot in range(N2_GRP):
        z = z_ref[slot * r:(slot + 1) * r, :]
        hid = jnp.sin(fr * lin(z, w1_ref, b1_ref))
        hid = jnp.sin(fr * lin(hid, w2_ref, b2_ref))
        hid = jnp.sin(fr * lin(hid, w3_ref, b3_ref)).astype(BF16)

        def taps(w4_ref, col):
            t = z[:, col:col + 1]
            sgn = z[:, col + Z_SIGN_COL:col + Z_SIGN_COL + 1]
            return _dot(hid, w4_ref[...]) * jnp.exp(-t * dl) * sgn

        k = jnp.concatenate([taps(w4f_ref, 0), taps(w4b_ref, Z_HALF)], axis=0)
        asum_ref[0:1, :] += jnp.sum(jnp.abs(k), axis=0, keepdims=True)
        _s1_store(_dot(f1_ref[...], k.astype(BF16)), tre_ref, tim_ref, j0 + slot, o_ref, n1, slot)


def _filter_positions(seq, n1, n2):
    i = np.arange(n1 // 2)[None, :]
    j = np.arange(n2)[:, None]
    bands = np.linspace(1e-4, HY_EMB_BANDS - 1, HY_EMB_BANDS)
    z = np.zeros((n2, n1 // 2, 2 * Z_HALF), np.float64)
    for col, r in ((0, i * n2 + j), (Z_HALF, i * n2 + j + seq)):
        pos = np.minimum(np.where(r < seq, r, 2 * seq - r), seq - 1).astype(np.float64)
        w = 2.0 * np.pi * pos / seq
        z[:, :, col] = pos / (seq - 1)
        z[:, :, col + 1:col + 1 + HY_EMB_BANDS] = np.cos(w[..., None] * bands)
        z[:, :, col + 1 + HY_EMB_BANDS:col + HY_EMB_DIM] = -np.sin(w[..., None] * bands)
        z[:, :, col + Z_SIGN_COL] = np.where(r < seq, 1.0, np.where(r == seq, 0.0, -1.0))
    return jnp.asarray(z.reshape(n2 * (n1 // 2), 2 * Z_HALF).astype(np.float32))


def _filter_stage1(seq, f_w1, f_b1, f_w2, f_b2, f_w3, f_b3, f_w4, f_freq):
    d, hid = D_MODEL, HY_FILTER_HIDDEN
    od = HY_ORDER * d
    n1, n2 = _fft_factors(seq)
    cst = _fft_consts(seq)
    z = _filter_positions(seq, n1, n2)
    zw = 2 * Z_HALF
    zero = jnp.zeros((hid, hid), F32)
    pad = jnp.zeros((Z_HALF - HY_EMB_DIM, hid), F32)
    w1 = jnp.concatenate([jnp.concatenate([f_w1, pad], axis=0), jnp.zeros((Z_HALF, hid), F32)], axis=0)
    w1 = jnp.concatenate([w1, jnp.roll(w1, Z_HALF, axis=0)], axis=1)
    blockdiag = lambda w: jnp.concatenate([jnp.concatenate([w, zero], axis=1),
                                           jnp.concatenate([zero, w], axis=1)], axis=0)
    twice = lambda v: jnp.tile(v, 2).reshape(1, 2 * hid)
    w4d = f_w4.reshape(hid, HY_ORDER, 2, d).transpose(2, 0, 1, 3).reshape(2, hid, od)
    zrows = jnp.zeros((hid, od), F32)
    w4f = jnp.concatenate([w4d[0], zrows], axis=0).astype(BF16)
    w4b = jnp.concatenate([zrows, w4d[1]], axis=0).astype(BF16)
    max_decay = math.log(HY_DECAY_TARGET) / HY_DECAY_FAST
    min_decay = math.log(HY_DECAY_TARGET) / HY_DECAY_SLOW
    deltas = np.abs(np.linspace(min_decay, max_decay, d, dtype=np.float32))
    dl = jnp.asarray(np.tile(deltas, HY_ORDER)[None])
    r = n1 // 2
    return pl.pallas_call(
        functools.partial(_filter_s1_kernel, n1=n1),
        grid=(n2 // N2_GRP,),
        in_specs=[pl.BlockSpec((N2_GRP * r, zw), lambda j: (j, 0)),
                  _const_spec((zw, 2 * hid)), _const_spec((1, 2 * hid)),
                  _const_spec((2 * hid, 2 * hid)), _const_spec((1, 2 * hid)),
                  _const_spec((2 * hid, 2 * hid)), _const_spec((1, 2 * hid)),
                  _const_spec((1, 2 * hid)),
                  _const_spec((2 * hid, od)), _const_spec((2 * hid, od)), _const_spec((1, od)),
                  _const_spec((2 * n1, n1)), _const_spec((n1, n2)), _const_spec((n1, n2))],
        out_specs=[pl.BlockSpec((2, n1 // K1_GRP, od // LANES, N2_GRP * PAIRS, LANES), lambda j: (0, 0, 0, j, 0)),
                   _const_spec((8, od))],
        out_shape=[jax.ShapeDtypeStruct((2, n1 // K1_GRP, od // LANES, n2 * PAIRS, LANES), jnp.uint32),
                   jax.ShapeDtypeStruct((8, od), F32)],
        compiler_params=_cp("arbitrary"),
        name="hyena_filter",
    )(z, w1, twice(f_b1), blockdiag(f_w2), twice(f_b2), blockdiag(f_w3), twice(f_b3), twice(f_freq),
      w4f, w4b, dl, cst["f1_full"], cst["tre"], cst["tim"])


def _fft_factors(seq):
    n = 2 * seq
    n1 = 256 if n >= 32768 else 32
    n2 = n // n1
    assert n1 * n2 == n and n2 % 16 == 0 and n1 % 32 == 0
    return n1, n2


@functools.lru_cache(maxsize=None)
def _fft_consts(seq):
    n = 2 * seq
    n1, n2 = _fft_factors(seq)
    k1 = np.arange(n1, dtype=np.float64)[:, None] + 0.5
    th1 = 2.0 * np.pi * k1 * np.arange(n1, dtype=np.float64)[None] / n1
    perm = np.concatenate([np.arange(0, n1, 2), np.arange(1, n1, 2)])
    f1_full = np.concatenate([np.cos(th1)[perm], -np.sin(th1)[perm]], axis=0)
    f1_half = f1_full[:, :n1 // 2]
    tw = 2.0 * np.pi * k1 * np.arange(n2, dtype=np.float64)[None] / n
    tre, tim = np.cos(tw), -np.sin(tw)
    h2 = n2 // 2
    th2 = 2.0 * np.pi * np.arange(h2, dtype=np.float64)[:, None] * np.arange(n2, dtype=np.float64)[None] / n2
    c2, s2 = np.cos(th2), np.sin(th2)
    m2f = np.block([[c2, s2], [-s2, c2]])
    m2i = np.block([[c2.T, -s2.T], [s2.T, c2.T]])
    thb = th1[:, :n1 // 2].T
    gi = (2.0 / n) * np.concatenate([np.cos(thb)[:, perm], -np.sin(thb)[:, perm]], axis=1)
    bf = lambda a: jnp.asarray(a.astype(np.float32)).astype(BF16)
    f32 = lambda a: jnp.asarray(a.astype(np.float32))
    grp = lambda a: a.T.reshape(n2, n1 // K1_GRP, K1_GRP).transpose(1, 0, 2)
    return dict(f1_full=bf(f1_full), f1_half=bf(f1_half), tre=f32(tre[perm]), tim=f32(tim[perm]),
                tre_grp=f32(grp(tre)), tim_grp=f32(grp(tim)), m2f=bf(m2f), m2i=bf(m2i), gi=bf(gi))


def _pick_col(tbl, idx):
    lane = lax.broadcasted_iota(I32, tbl.shape, 1)
    return jnp.sum(jnp.where(lane == idx, tbl, 0.0), axis=1, keepdims=True)


K1_GRP = 16
PAIRS = K1_GRP // 2
LANES = 128
N2_GRP = 4
MID_TC = 512


def _unpack_pair(w, half):
    bits = lax.shift_left(w, jnp.uint32(16)) if half == 0 else (w & jnp.uint32(0xFFFF0000))
    return pltpu.bitcast(bits, F32).astype(BF16)


def _pack_pair(even, odd):
    ue = pltpu.bitcast(even.astype(BF16).astype(F32), jnp.uint32)
    uo = pltpu.bitcast(odd.astype(BF16).astype(F32), jnp.uint32)
    return lax.shift_right_logical(ue, jnp.uint32(16)) | uo


def _s1_store(a, tre_ref, tim_ref, j, o_ref, n1, slot):
    are, aim = a[:n1], a[n1:]
    tre = _pick_col(tre_ref[...], j)
    tim = _pick_col(tim_ref[...], j)
    re = are * tre - aim * tim
    im = are * tim + aim * tre
    h = n1 // 2
    shape = (n1 // K1_GRP, PAIRS, a.shape[1])
    for part, val in enumerate((re, im)):
        words = _pack_pair(val[:h], val[h:]).reshape(shape)
        for ct in range(a.shape[1] // LANES):
            o_ref[part, :, ct, slot * PAIRS:(slot + 1) * PAIRS, :] = words[:, :, ct * LANES:(ct + 1) * LANES]


def _fft_s1_kernel(f1_ref, x_ref, tre_ref, tim_ref, o_ref, *, n1, tc):
    j0 = pl.program_id(0) * N2_GRP
    for slot in range(N2_GRP):
        a = _dot(f1_ref[...], x_ref[:, slot * tc:(slot + 1) * tc])
        _s1_store(a, tre_ref, tim_ref, j0 + slot, o_ref, n1, slot)


def _fft_stage1(x2d, f1, tre, tim, n1, n2, chans):
    r = x2d.shape[0]
    return pl.pallas_call(
        functools.partial(_fft_s1_kernel, n1=n1, tc=chans),
        grid=(n2 // N2_GRP,),
        in_specs=[_const_spec((2 * n1, r)),
                  pl.BlockSpec((r, N2_GRP * chans), lambda j: (0, j)),
                  _const_spec((n1, n2)), _const_spec((n1, n2))],
        out_specs=pl.BlockSpec((2, n1 // K1_GRP, chans // LANES, N2_GRP * PAIRS, LANES),
                               lambda j: (0, 0, 0, j, 0)),
        out_shape=jax.ShapeDtypeStruct((2, n1 // K1_GRP, chans // LANES, n2 * PAIRS, LANES), jnp.uint32),
        compiler_params=_cp("parallel"),
        name="fft_stage1",
    )(f1, x2d, tre, tim)


def _load_pair(b_ref, part, s, n2):
    return jnp.concatenate([b_ref[part, 0, ct, pl.ds(s, n2, stride=PAIRS), :] for ct in range(b_ref.shape[2])],
                           axis=1)


def _store_pair(o_ref, part, s, n2, words):
    for ct in range(o_ref.shape[2]):
        o_ref[part, 0, ct, pl.ds(s, n2, stride=PAIRS), :] = words[:, ct * LANES:(ct + 1) * LANES]


def _fft_s2_kernel(m2f_ref, b_ref, o_ref):
    n2 = m2f_ref.shape[0]
    for s in range(PAIRS):
        wre = _load_pair(b_ref, 0, s, n2)
        wim = _load_pair(b_ref, 1, s, n2)
        for half in range(2):
            b = jnp.concatenate([_unpack_pair(wre, half), _unpack_pair(wim, half)], axis=0)
            o_ref[2 * s + half] = _dot(m2f_ref[...], b).astype(BF16)


def _fft_stage2(b5, m2f, n1, n2, chans):
    return pl.pallas_call(
        _fft_s2_kernel,
        grid=(n1 // K1_GRP, chans // MID_TC),
        in_specs=[_const_spec((n2, 2 * n2)),
                  pl.BlockSpec((2, 1, MID_TC // LANES, n2 * PAIRS, LANES), lambda g, c: (0, g, c, 0, 0))],
        out_specs=pl.BlockSpec((K1_GRP, n2, MID_TC), lambda g, c: (g, 0, c)),
        out_shape=jax.ShapeDtypeStruct((n1, n2, chans), BF16),
        compiler_params=_cp("parallel", "parallel"),
        name="fft_stage2",
    )(m2f, b5)


def _fft_mid_kernel(m2f_ref, m2i_ref, b_ref, k_ref, inv_ref, tre_ref, tim_ref, o_ref, *, n2):
    h2 = n2 // 2
    inv = inv_ref[...]
    tre_g = tre_ref[0]
    tim_g = tim_ref[0]
    for s in range(PAIRS):
        wre = _load_pair(b_ref, 0, s, n2)
        wim = _load_pair(b_ref, 1, s, n2)
        res = []
        for half in range(2):
            kl = 2 * s + half
            b = jnp.concatenate([_unpack_pair(wre, half), _unpack_pair(wim, half)], axis=0)
            x = _dot(m2f_ref[...], b)
            kk = k_ref[kl].astype(F32) * inv
            xre, xim = x[:h2], x[h2:]
            kre, kim = kk[:h2], kk[h2:]
            y = jnp.concatenate([xre * kre - xim * kim, xre * kim + xim * kre], axis=0).astype(BF16)
            c = _dot(m2i_ref[...], y)
            cre, cim = c[:n2], c[n2:]
            tre = tre_g[:, kl:kl + 1]
            tim = tim_g[:, kl:kl + 1]
            res.append((cre * tre + cim * tim, cim * tre - cre * tim))
        _store_pair(o_ref, 0, s, n2, _pack_pair(res[0][0], res[1][0]))
        _store_pair(o_ref, 1, s, n2, _pack_pair(res[0][1], res[1][1]))


def _fft_mid(b5, kspec, inv_norm, order, cst, n1, n2):
    d = D_MODEL
    nc = d // MID_TC
    blk = pl.BlockSpec((2, 1, MID_TC // LANES, n2 * PAIRS, LANES), lambda g, c: (0, g, c, 0, 0))
    tw = pl.BlockSpec((1, n2, K1_GRP), lambda g, c: (g, 0, 0))
    return pl.pallas_call(
        functools.partial(_fft_mid_kernel, n2=n2),
        grid=(n1 // K1_GRP, nc),
        in_specs=[_const_spec((n2, 2 * n2)), _const_spec((2 * n2, n2)), blk,
                  pl.BlockSpec((K1_GRP, n2, MID_TC), lambda g, c: (g, 0, order * nc + c)),
                  pl.BlockSpec((1, MID_TC), lambda g, c: (0, order * nc + c)),
                  tw, tw],
        out_specs=blk,
        out_shape=jax.ShapeDtypeStruct((2, n1 // K1_GRP, d // LANES, n2 * PAIRS, LANES), jnp.uint32),
        compiler_params=_cp("parallel", "parallel"),
        name="fft_mid",
    )(cst["m2f"], cst["m2i"], b5, kspec, inv_norm, cst["tre_grp"], cst["tim_grp"])


def _fft_last_kernel(gi_ref, c_ref, gate_ref, z_ref, skip_ref, o_ref):
    h = c_ref.shape[1] * PAIRS
    d = c_ref.shape[2] * LANES
    for slot in range(N2_GRP):
        rows = slice(slot * PAIRS, (slot + 1) * PAIRS)
        cols = slice(slot * d, (slot + 1) * d)

        def words(part):
            return jnp.concatenate([c_ref[part, :, ct, rows, :].reshape(h, LANES)
                                    for ct in range(c_ref.shape[2])], axis=1)

        wre = words(0)
        wim = words(1)
        c = jnp.concatenate([_unpack_pair(wre, 0), _unpack_pair(wre, 1), _unpack_pair(wim, 0),
                             _unpack_pair(wim, 1)], axis=0)
        y = _dot(gi_ref[...], c)
        z = z_ref[:, cols].astype(F32)
        o_ref[:, cols] = (gate_ref[:, cols].astype(F32) * (y + skip_ref[...] * z)).astype(BF16)


def _fft_last(c5, gate2d, z2d, skip_row, gi, n1, n2):
    d = D_MODEL
    r = n1 // 2
    col = pl.BlockSpec((r, N2_GRP * d), lambda j: (0, j))
    return pl.pallas_call(
        _fft_last_kernel,
        grid=(n2 // N2_GRP,),
        in_specs=[_const_spec((r, 2 * n1)),
                  pl.BlockSpec((2, n1 // K1_GRP, d // LANES, N2_GRP * PAIRS, LANES), lambda j: (0, 0, 0, j, 0)),
                  col, col, _const_spec((1, d))],
        out_specs=col,
        out_shape=jax.ShapeDtypeStruct((r, n2 * d), BF16),
        compiler_params=_cp("parallel"),
        name="fft_last",
    )(gi, c5, gate2d, z2d, skip_row)


def _long_conv_gate(z_in, gate, skip_row, kspec, inv_norm, order, seq):
    d = D_MODEL
    n1, n2 = _fft_factors(seq)
    cst = _fft_consts(seq)
    z2d = z_in.reshape(n1 // 2, n2 * d)
    b = _fft_stage1(z2d, cst["f1_half"], cst["tre"], cst["tim"], n1, n2, d)
    c = _fft_mid(b, kspec, inv_norm, order, cst, n1, n2)
    out = _fft_last(c, gate.reshape(n1 // 2, n2 * d), z2d, skip_row, cst["gi"], n1, n2)
    return out.reshape(seq, d)


def _filter_spectrum(seq, fparams):
    d = D_MODEL
    od = HY_ORDER * d
    n1, n2 = _fft_factors(seq)
    cst = _fft_consts(seq)
    b, asum = _filter_stage1(seq, *fparams)
    kspec = _fft_stage2(b, cst["m2f"], n1, n2, od)
    return kspec, asum


def _proj_res_kernel(a_ref, w_ref, b_ref, g_ref, x_ref, o_ref):
    y = _dot(a_ref[...], w_ref[...]) + b_ref[...]
    o_ref[...] = x_ref[...] + g_ref[...] * y


def _proj_res(a, w_bf, b_row, gate_row, xres):
    n, d = xres.shape
    tm = min(512, n)
    row = pl.BlockSpec((tm, d), lambda i: (i, 0))
    return pl.pallas_call(
        _proj_res_kernel,
        grid=(n // tm,),
        in_specs=[row, _const_spec((d, d)), _const_spec((1, d)), _const_spec((1, d)), row],
        out_specs=row,
        out_shape=jax.ShapeDtypeStruct((n, d), F32),
        compiler_params=_cp("parallel"),
        name="proj_residual",
    )(a, w_bf, b_row, gate_row, xres)


def _ffn_in_kernel(x_ref, mod_ref, wt_ref, h_ref, aff_ref):
    h = _norm_mod(x_ref[...], mod_ref[...])
    hi = h.astype(BF16)
    lo = (h - hi.astype(F32)).astype(BF16)
    wt = wt_ref[...]
    whi = wt.astype(BF16)
    wlo = (wt - whi.astype(F32)).astype(BF16)
    logits = _dot_nt(whi, hi) + (_dot_nt(whi, lo) + _dot_nt(wlo, hi))
    m = jnp.max(logits, axis=0, keepdims=True)
    p = jnp.exp(logits - m)
    aff_ref[...] = p / jnp.sum(p, axis=0, keepdims=True)
    h_ref[...] = hi


def _ffn_in(x, mod, w_router):
    n, d = x.shape
    e = N_EXPERTS
    tm = min(512, n)
    return pl.pallas_call(
        _ffn_in_kernel,
        grid=(n // tm,),
        in_specs=[pl.BlockSpec((tm, d), lambda i: (i, 0)), _const_spec((8, d)), _const_spec((e, d))],
        out_specs=[pl.BlockSpec((tm, d), lambda i: (i, 0)), pl.BlockSpec((e, tm), lambda i: (0, i))],
        out_shape=[jax.ShapeDtypeStruct((n, d), BF16), jax.ShapeDtypeStruct((e, n), F32)],
        compiler_params=_cp("parallel"),
        name="moe_router",
    )(x, mod, w_router.T)


def _select_kernel(a_ref, pos_ref, s0_ref, sel_ref, *, cap, nblk):
    e = N_EXPERTS
    bits = pltpu.bitcast(a_ref[...], I32)

    def bisect(i, thr):
        cand = thr | jnp.left_shift(jnp.int32(1), 30 - i)
        cnt = jnp.sum(jnp.where(bits >= cand, 1.0, 0.0), axis=1, keepdims=True)
        return jnp.where(cnt >= cap, cand, thr)

    thr = lax.fori_loop(0, 31, bisect, jnp.zeros((e, 1), I32))
    n_gt = jnp.sum(jnp.where(bits > thr, 1.0, 0.0), axis=1, keepdims=True)
    need = cap - n_gt
    r = lax.broadcasted_iota(I32, (TOK_BLK, TOK_BLK), 0)
    c = lax.broadcasted_iota(I32, (TOK_BLK, TOK_BLK), 1)
    upper = jnp.where(r < c, 1.0, 0.0).astype(BF16)

    def pass1(j, carry):
        sl = pl.ds(pl.multiple_of(j * TOK_BLK, TOK_BLK), TOK_BLK)
        bj = pltpu.bitcast(a_ref[:, sl], I32)
        eq = jnp.where(bj == thr, 1.0, 0.0)
        rank = _dot(eq.astype(BF16), upper) + carry
        keep = jnp.logical_or(bj > thr, jnp.logical_and(bj == thr, rank < need))
        sel_ref[:, sl] = jnp.where(keep, 1.0, 0.0)
        return carry + jnp.sum(eq, axis=1, keepdims=True)

    lax.fori_loop(0, nblk, pass1, jnp.zeros((e, 1), F32))

    def pass2(j, carry):
        sl = pl.ds(pl.multiple_of(j * TOK_BLK, TOK_BLK), TOK_BLK)
        s = sel_ref[:, sl]
        slot = _dot(s.astype(BF16), upper) + carry
        pos_ref[:, sl] = jnp.where(s > 0.5, slot, -1.0).astype(I32)
        s0_ref[j] = jnp.broadcast_to(carry, (e, 128)).astype(I32)
        return carry + jnp.sum(s, axis=1, keepdims=True)

    total = lax.fori_loop(0, nblk, pass2, jnp.zeros((e, 1), F32))
    s0_ref[nblk] = jnp.broadcast_to(total, (e, 128)).astype(I32)


def _select(aff_t, cap):
    e, n = aff_t.shape
    nblk = n // TOK_BLK
    return pl.pallas_call(
        functools.partial(_select_kernel, cap=cap, nblk=nblk),
        out_shape=[jax.ShapeDtypeStruct((e, n), I32), jax.ShapeDtypeStruct((nblk + 1, e, 128), I32)],
        scratch_shapes=[pltpu.VMEM((e, n), F32)],
        compiler_params=pltpu.CompilerParams(vmem_limit_bytes=VMEM_LIMIT),
        name="moe_select",
    )(aff_t)


def _block_windows(s0_ref, blk, e_idx):
    s0 = s0_ref[blk * N_EXPERTS + e_idx]
    s1 = s0_ref[(blk + 1) * N_EXPERTS + e_idx]
    start = lax.shift_left(lax.shift_right_logical(s0, 4), 4)
    nwin = jnp.where(s1 > s0, lax.shift_right_logical(s1 - start + (WIN - 1), WIN_SHIFT), 0)
    return start, nwin


def _token_block(i):
    start = i * TOK_BLK
    return pl.ds(start if isinstance(i, int) else pl.multiple_of(start, TOK_BLK), TOK_BLK)


def _gather_kernel(s0_ref, pos_ref, h_ref, xe_ref, *, sub, nchunk):
    e_idx = pl.program_id(0)
    xe_ref[...] = jnp.zeros_like(xe_ref)
    rows = lax.broadcasted_iota(I32, (WIN, TOK_BLK), 0)

    def window(blk, base):
        tok = _token_block(blk)
        base = pl.multiple_of(base, ROW_ALIGN)
        onehot = jnp.where(rows == pos_ref[0, :, tok] - base, 1.0, 0.0).astype(BF16)
        got = _dot(onehot, h_ref[tok, :]).astype(BF16)
        win = pl.ds(base, WIN)
        xe_ref[0, win, :] = xe_ref[0, win, :] + got

    def chunk(c, carry):
        plan = [_block_windows(s0_ref, c * sub + i, e_idx) for i in range(sub)]
        single = functools.reduce(jnp.logical_and, [nwin <= 1 for _, nwin in plan])

        @pl.when(single)
        def _():
            for i, (start, _) in enumerate(plan):
                window(c * sub + i, start)

        @pl.when(jnp.logical_not(single))
        def _():
            def body(i, carry2):
                start, nwin = _block_windows(s0_ref, c * sub + i, e_idx)
                return lax.fori_loop(0, nwin, lambda w, cc: (window(c * sub + i, start + w * WIN), cc)[1], carry2)

            lax.fori_loop(0, sub, body, 0)

        return carry

    lax.fori_loop(0, nchunk, chunk, 0)


def _gather(s0_flat, pos, h, cap_pad):
    e, n = pos.shape
    d = h.shape[1]
    sub = min(8, n // TOK_BLK)
    return pl.pallas_call(
        functools.partial(_gather_kernel, sub=sub, nchunk=n // (sub * TOK_BLK)),
        grid_spec=pltpu.PrefetchScalarGridSpec(
            num_scalar_prefetch=1,
            grid=(e,),
            in_specs=[pl.BlockSpec((1, 1, n), lambda ei, s0: (ei, 0, 0)),
                      pl.BlockSpec((n, d), lambda ei, s0: (0, 0), pipeline_mode=pl.Buffered(1))],
            out_specs=pl.BlockSpec((1, cap_pad, d), lambda ei, s0: (ei, 0, 0)),
        ),
        out_shape=jax.ShapeDtypeStruct((e, cap_pad, d), BF16),
        compiler_params=_cp("parallel"),
        name="moe_gather",
    )(s0_flat, pos.reshape(e, 1, n), h)


def _expert_kernel(x_ref, wg_ref, wu_ref, wd_ref, y_ref, wg_scr, wu_scr, wd_scr, *, n_real):
    j = pl.program_id(1)

    @pl.when(j == 0)
    def _():
        wg_scr[...] = wg_ref[0].astype(BF16)
        wu_scr[...] = wu_ref[0].astype(BF16)
        wd_scr[...] = wd_ref[0].astype(BF16)

    @pl.when(j < n_real)
    def _():
        x = x_ref[0]
        g = _dot(x, wg_scr[...])
        u = _dot(x, wu_scr[...])
        a = (g / (1.0 + jnp.exp(-g))) * u
        y_ref[0] = _dot(a.astype(BF16), wd_scr[...]).astype(BF16)

    @pl.when(j >= n_real)
    def _():
        y_ref[0] = jnp.zeros_like(y_ref[0])


def _experts(xe, wg, wu, wd, layer, cap):
    e, cap_pad, d = xe.shape
    f = wg.shape[3]
    tm = 256 if cap % 256 == 0 else cap_pad
    tile = pl.BlockSpec((1, tm, d), lambda ei, j: (ei, j, 0))
    return pl.pallas_call(
        functools.partial(_expert_kernel, n_real=pl.cdiv(cap, tm)),
        grid=(e, cap_pad // tm),
        in_specs=[tile,
                  pl.BlockSpec((None, 1, d, f), lambda ei, j: (layer, ei, 0, 0)),
                  pl.BlockSpec((None, 1, d, f), lambda ei, j: (layer, ei, 0, 0)),
                  pl.BlockSpec((None, 1, f, d), lambda ei, j: (layer, ei, 0, 0))],
        out_specs=tile,
        out_shape=jax.ShapeDtypeStruct((e, cap_pad, d), BF16),
        scratch_shapes=[pltpu.VMEM((d, f), BF16), pltpu.VMEM((d, f), BF16), pltpu.VMEM((f, d), BF16)],
        compiler_params=_cp("parallel", "arbitrary"),
        name="moe_experts",
    )(xe, wg, wu, wd)


def _combine_kernel(s0_ref, post_ref, gt_ref, y_ref, x_ref, gate_ref, o_ref, *, sub):
    c = pl.program_id(0)
    e_idx = pl.program_id(1)

    @pl.when(e_idx == 0)
    def _():
        o_ref[...] = jnp.zeros_like(o_ref)

    lanes = lax.broadcasted_iota(I32, (TOK_BLK, WIN), 1)
    elane = lax.broadcasted_iota(I32, (TOK_BLK, N_EXPERTS), 1) == e_idx

    def window(i, base):
        tok = _token_block(i)
        base = pl.multiple_of(base, ROW_ALIGN)
        pcol = jnp.sum(jnp.where(elane, post_ref[tok, :].astype(F32), 0.0), axis=1, keepdims=True)
        gcol = jnp.sum(jnp.where(elane, gt_ref[tok, :], 0.0), axis=1, keepdims=True)
        rel = pcol.astype(I32) - base
        onehot = jnp.where(lanes == rel, 1.0, 0.0).astype(BF16)
        strip = y_ref[0, pl.ds(base, WIN), :]
        o_ref[tok, :] = o_ref[tok, :] + gcol * _dot(onehot, strip)

    plan = [_block_windows(s0_ref, c * sub + i, e_idx) for i in range(sub)]
    single = functools.reduce(jnp.logical_and, [nwin <= 1 for _, nwin in plan])

    @pl.when(single)
    def _():
        for i, (start, _) in enumerate(plan):
            window(i, start)

    @pl.when(jnp.logical_not(single))
    def _():
        def body(i, carry):
            start, nwin = _block_windows(s0_ref, c * sub + i, e_idx)
            return lax.fori_loop(0, nwin, lambda w, cc: (window(i, start + w * WIN), cc)[1], carry)

        lax.fori_loop(0, sub, body, 0)

    @pl.when(e_idx == N_EXPERTS - 1)
    def _():
        o_ref[...] = x_ref[...] + gate_ref[...] * o_ref[...]


def _combine(s0_flat, pos_t, g_t, y, xres, gate_row):
    n, d = xres.shape
    e, cap_pad, _ = y.shape
    chunk = min(2048, n)
    sub = chunk // TOK_BLK
    return pl.pallas_call(
        functools.partial(_combine_kernel, sub=sub),
        grid_spec=pltpu.PrefetchScalarGridSpec(
            num_scalar_prefetch=1,
            grid=(n // chunk, e),
            in_specs=[pl.BlockSpec((chunk, e), lambda c, ei, s0: (c, 0)),
                      pl.BlockSpec((chunk, e), lambda c, ei, s0: (c, 0)),
                      pl.BlockSpec((1, cap_pad, d), lambda c, ei, s0: (ei, 0, 0)),
                      pl.BlockSpec((chunk, d), lambda c, ei, s0: (c, 0)),
                      pl.BlockSpec((1, d), lambda c, ei, s0: (0, 0))],
            out_specs=pl.BlockSpec((chunk, d), lambda c, ei, s0: (c, 0)),
        ),
        out_shape=jax.ShapeDtypeStruct((n, d), F32),
        compiler_params=_cp("parallel", "arbitrary"),
        name="moe_combine",
    )(s0_flat, pos_t, g_t, y, xres, gate_row)


def _moe_block(x, mod, gate_row, w_router, wg, wu, wd, layer):
    n = x.shape[0]
    cap = EC_CAPACITY * n // N_EXPERTS
    cap_pad = cap + TOK_BLK
    h, aff_t = _ffn_in(x, mod, w_router)
    pos, s0 = _select(aff_t, cap)
    s0_flat = s0[:, :, 0].reshape(-1)
    xe = _gather(s0_flat, pos, h, cap_pad)
    y = _experts(xe, wg, wu, wd, layer, cap)
    return _combine(s0_flat, pos.T, aff_t.T, y, x, gate_row)


QK_SCALE = (DA_HEAD_DIM ** -0.5) * math.log2(math.e)


def _split_bf16(x):
    hi = x.astype(BF16)
    return hi, (x - hi.astype(F32)).astype(BF16)


def _group_rms(u, gsel_ref, gain, eps):
    g = gsel_ref[...]
    hi, lo = _split_bf16(u * u)
    r = lax.rsqrt((_dot(hi, g) + _dot(lo, g)) * (1.0 / DA_HEAD_DIM) + eps)
    rhi, rlo = _split_bf16(r)
    return u * (_dot_nt(rhi, g) + _dot_nt(rlo, g)) * gain


def _rope(u, cos, sin_signed):
    d = u.shape[1]
    half = ROPE_AXIS_DIM // 2
    lane = lax.broadcasted_iota(I32, u.shape, 1)
    first = (lane & half) == 0
    swapped = jnp.where(first, pltpu.roll(u, d - half, 1), pltpu.roll(u, half, 1))
    return u * cos + swapped * sin_signed


def _qkv_kernel(*refs, rope):
    x_ref, mod_ref, w_ref, gsum_ref, qn_ref, kn_ref = refs[:6]
    q_ref, k_ref, v_ref = refs[-3:]
    if rope:
        cos_ref, sin_ref = refs[6:8]
    d = D_MODEL
    h = _norm_mod(x_ref[...], mod_ref[...]).astype(BF16)
    if rope:
        reps = d // cos_ref.shape[1]
        cos = _lane_tile(cos_ref[...], reps)
        sin = _lane_tile(sin_ref[...], reps)
    for part, (o_ref, gain_ref) in enumerate(((q_ref, qn_ref), (k_ref, kn_ref))):
        u = _dot(h, w_ref[:, part * d:(part + 1) * d])
        u = _group_rms(u, gsum_ref, gain_ref[...], NORM_EPS)
        if rope:
            u = _rope(u, cos, sin)
        if part == 0:
            u = u * QK_SCALE
        o_ref[...] = u.astype(BF16)
    v_ref[...] = _dot(h, w_ref[:, 2 * d:]).astype(BF16)


def _qkv(x, mod, w_bf, gsum, qn_row, kn_row, cos=None, sin=None, kv_rows=None, kv_offset=0, kv_into=None):
    n, d = x.shape
    tm = min(256, n)
    kv_rows = n if kv_rows is None else kv_rows
    off = kv_offset // tm
    assert kv_offset % tm == 0
    rope = cos is not None
    row = pl.BlockSpec((tm, d), lambda i: (i, 0))
    kv_row = pl.BlockSpec((tm, d), lambda i: (i + off, 0))
    in_specs = [row, _const_spec((8, d)), _const_spec((d, 3 * d)), _const_spec(gsum.shape),
                _const_spec((1, d)), _const_spec((1, d))]
    args = [x, mod, w_bf, gsum, qn_row, kn_row]
    if rope:
        tw = cos.shape[1]
        in_specs += [pl.BlockSpec((tm, tw), lambda i: (i, 0)), pl.BlockSpec((tm, tw), lambda i: (i, 0))]
        args += [cos, sin]
    aliases = {}
    if kv_into is not None:
        aliases = {len(args): 1, len(args) + 1: 2}
        in_specs += [pl.BlockSpec(memory_space=pl.ANY), pl.BlockSpec(memory_space=pl.ANY)]
        args += list(kv_into)
    kv = jax.ShapeDtypeStruct((kv_rows, d), BF16)
    return pl.pallas_call(
        functools.partial(_qkv_kernel, rope=rope),
        grid=(n // tm,),
        in_specs=in_specs,
        out_specs=[row, kv_row, kv_row],
        out_shape=[jax.ShapeDtypeStruct((n, d), BF16), kv, kv],
        input_output_aliases=aliases,
        compiler_params=_cp("parallel"),
        name="attn_qkv",
    )(*args)


NEG_BIG = -1e30
HEAD_UNROLL = 8


def _attn_kernel(q_ref, k_ref, v_ref, lam_ref, sub_ref, o_ref, m_scr, acc_scr, sa_scr, sb_scr, ma_scr, mb_scr,
                 pa_scr, pb_scr, vprev_scr, *,
                 lam_init):
    j = pl.program_id(1)
    nj = pl.num_programs(1)
    hd, vd = DA_HEAD_DIM, DA_V_DIM
    tk = k_ref.shape[0]

    last = DA_HEADS - 1

    @pl.when(j == 0)
    def _():
        m_scr[...] = jnp.full_like(m_scr, NEG_BIG)
        acc_scr[...] = jnp.zeros_like(acc_scr)
        pa_scr[...] = jnp.zeros_like(pa_scr)
        sb_scr[...] = jnp.full_like(sb_scr, 2.0 * NEG_BIG)
        mb_scr[...] = jnp.full_like(mb_scr, 2.0 * NEG_BIG)
        vprev_scr[...] = jnp.zeros_like(vprev_scr)

    lane = lax.broadcasted_iota(I32, (q_ref.shape[0], vd), 1)
    ones_col = jnp.where(lax.broadcasted_iota(I32, (tk, vd), 1) == 0, 1.0, 0.0).astype(BF16)

    def head_cols(h):
        return pl.ds(pl.multiple_of(h * vd, vd), vd)

    def scores(h, comp, s_ref, mx_ref):
        qb = q_ref[:, head_cols(h)]
        in_comp = (lane >= comp * hd) & (lane < (comp + 1) * hd)
        qm = jnp.where(in_comp, qb, jnp.zeros_like(qb))
        s = _dot_nt(qm, k_ref[:, head_cols(h)])
        s_ref[...] = s
        mx_ref[...] = jnp.broadcast_to(jnp.max(s, axis=1, keepdims=True), mx_ref.shape)

    def softmax_step(h, comp, s_ref, mx_ref, p_ref):
        idx = 2 * h + comp
        m_prev = m_scr[idx]
        m_new = jnp.maximum(m_prev, mx_ref[...])
        alpha = _lane_tile(jnp.exp2(m_prev - m_new), 2)
        p_ref[...] = jnp.exp2(s_ref[...] - m_new[:, 0:1]).astype(BF16)
        m_scr[idx] = m_new
        acc_scr[idx] = alpha * acc_scr[idx]

    def pv_step(h, comp, p_ref):
        idx = 2 * h + comp
        acc_scr[idx] = acc_scr[idx] + _dot(p_ref[...], jnp.concatenate([vprev_scr[...], ones_col], axis=1))

    def head(h, carry):
        hp = (h + last) & last
        scores(h, 0, sa_scr, ma_scr)
        pv_step(hp, 0, pa_scr)
        softmax_step(hp, 1, sb_scr, mb_scr, pb_scr)
        scores(h, 1, sb_scr, mb_scr)
        pv_step(hp, 1, pb_scr)
        softmax_step(h, 0, sa_scr, ma_scr, pa_scr)
        vprev_scr[...] = v_ref[:, head_cols(h)]
        return carry

    def head_group(g, carry):
        for u in range(HEAD_UNROLL):
            carry = head(HEAD_UNROLL * g + u, carry)
        return carry

    lax.fori_loop(0, DA_HEADS // HEAD_UNROLL, head_group, 0)

    @pl.when(j == nj - 1)
    def _():
        pv_step(last, 0, pa_scr)
        softmax_step(last, 1, sb_scr, mb_scr, pb_scr)
        pv_step(last, 1, pb_scr)
        lp = lam_ref[...]
        lam = (jnp.exp(jnp.sum(lp[0:1] * lp[1:2], axis=1, keepdims=True))
               - jnp.exp(jnp.sum(lp[2:3] * lp[3:4], axis=1, keepdims=True)) + lam_init)
        for h in range(DA_HEADS):
            a0 = acc_scr[2 * h]
            a1 = acc_scr[2 * h + 1]
            o = a0[:, :vd] / a0[:, vd:vd + 1] - lam * (a1[:, :vd] / a1[:, vd:vd + 1])
            ms = jnp.mean(o * o, axis=1, keepdims=True)
            o = o * lax.rsqrt(ms + SUBLN_EPS) * (sub_ref[...] * (1.0 - lam_init))
            o_ref[:, h * vd:(h + 1) * vd] = o.astype(BF16)


def _attention(q, k_all, v_all, lam_rows, subln_row, lam_init):
    n, d = q.shape
    nk = k_all.shape[0]
    tq = min(512, n)
    tk = 1280 if nk % 1280 == 0 else 256
    assert nk % tk == 0
    nc = 2 * DA_HEADS
    return pl.pallas_call(
        functools.partial(_attn_kernel, lam_init=lam_init),
        grid=(n // tq, nk // tk),
        in_specs=[pl.BlockSpec((tq, d), lambda i, j: (i, 0)),
                  pl.BlockSpec((tk, d), lambda i, j: (j, 0)),
                  pl.BlockSpec((tk, d), lambda i, j: (j, 0)),
                  _const_spec((8, DA_HEAD_DIM)), _const_spec((1, DA_V_DIM))],
        out_specs=pl.BlockSpec((tq, d), lambda i, j: (i, 0)),
        out_shape=jax.ShapeDtypeStruct((n, d), BF16),
        scratch_shapes=[pltpu.VMEM((nc, tq, DA_V_DIM), F32), pltpu.VMEM((nc, tq, 2 * DA_V_DIM), F32),
                        pltpu.VMEM((tq, tk), F32), pltpu.VMEM((tq, tk), F32),
                        pltpu.VMEM((tq, DA_V_DIM), F32), pltpu.VMEM((tq, DA_V_DIM), F32),
                        pltpu.VMEM((tq, tk), BF16), pltpu.VMEM((tq, tk), BF16),
                        pltpu.VMEM((tk, DA_V_DIM), BF16)],
        compiler_params=_cp("parallel", "arbitrary"),
        name="diff_attention",
    )(q, k_all, v_all, lam_rows, subln_row)


def _rope_tables(n):
    lane = np.arange(2 * DA_HEAD_DIM) % DA_HEAD_DIM
    nf = ROPE_AXIS_DIM // 2
    inv = ROPE_THETA ** (-np.arange(0, ROPE_AXIS_DIM, 2, dtype=np.float32) / ROPE_AXIS_DIM)
    by_row = jnp.asarray(np.where(lane < ROPE_AXIS_DIM, inv[lane % nf], 0.0).astype(np.float32))
    by_col = jnp.asarray(np.where(lane >= ROPE_AXIS_DIM, inv[lane % nf], 0.0).astype(np.float32))
    sign = jnp.asarray(np.where(lane % ROPE_AXIS_DIM < nf, -1.0, 1.0).astype(np.float32))
    t = jnp.arange(n, dtype=I32)
    row = (t // GRID_W).astype(F32)[:, None]
    col = (t % GRID_W).astype(F32)[:, None]
    ang = row * by_row[None] + col * by_col[None]
    return jnp.cos(ang), jnp.sin(ang) * sign[None]


def _hyena_layer(x, mod, gate_row, kspec, inv_norm, w_in_bf, b_in, conv_w, conv_b, skip, w_out_bf, b_out):
    seq = x.shape[0]
    v, x1, x2 = _hy_in(x, mod, w_in_bf, b_in, conv_w, conv_b)
    z = _long_conv_gate(v, x1, skip[0:1], kspec, inv_norm, 0, seq)
    z = _long_conv_gate(z, x2, skip[1:2], kspec, inv_norm, 1, seq)
    return _proj_res(z, w_out_bf, b_out.reshape(1, -1), gate_row, x)


def kernel(x, c, ctx, c_ctx, ada_w, ada_b, norm_mix, norm_ffn, hy_w_in, hy_b_in, hy_conv_w, hy_conv_b, hy_f_w1, hy_f_b1, hy_f_w2, hy_f_b2, hy_f_w3, hy_f_b3, hy_f_w4, hy_f_freq, hy_skip, hy_w_out, hy_b_out, da_w_qkv, da_q_norm, da_k_norm, da_lam_q1, da_lam_k1, da_lam_q2, da_lam_k2, da_subln, da_w_out, moe_router, moe_w_gate, moe_w_up, moe_w_down):
    d = D_MODEL
    depth = ada_w.shape[0]
    assert x.shape[0] == 1 and x.shape[2] == d
    xs = x[0]
    cs = ctx[0]
    cond8 = jnp.concatenate([c[0:1], c_ctx[None], jnp.zeros((6, d), F32)], axis=0)
    mods = _adaln(cond8, ada_w, ada_b)

    def mod_slice(i, row, k):
        return mods[i, row, k * d:(k + 1) * d][None]

    for i in range(depth):
        last = i == depth - 1
        j = i // 2
        mix_x = _mod_rows(norm_mix[i], mods[i], 0, 0)
        mix_c = _mod_rows(norm_mix[i], mods[i], 1, 0)
        if i % 2 == 0:
            fparams = (hy_f_w1[j], hy_f_b1[j], hy_f_w2[j], hy_f_b2[j], hy_f_w3[j], hy_f_b3[j], hy_f_w4[j],
                       hy_f_freq[j])
            shared = (hy_w_in[j].astype(BF16), hy_b_in[j], hy_conv_w[j], hy_conv_b[j], hy_skip[j],
                      hy_w_out[j].astype(BF16), hy_b_out[j])
            kspec, asum = _filter_spectrum(xs.shape[0], fparams)
            inv_norm = 1.0 / (asum[0:1] + HY_FILTER_EPS)
            new_x = _hyena_layer(xs, mix_x, mod_slice(i, 0, 2), kspec, inv_norm, *shared)
            if not last:
                kspec_c, asum_c = _filter_spectrum(cs.shape[0], fparams)
                inv_c = 1.0 / (asum_c[0:1] + HY_FILTER_EPS)
                cs = _hyena_layer(cs, mix_c, mod_slice(i, 1, 2), kspec_c, inv_c, *shared)
            xs = new_x
        else:
            lam_init = 0.8 - 0.6 * math.exp(-0.3 * i)
            w_qkv = da_w_qkv[j].astype(BF16)
            gidx = np.arange(d) // DA_HEAD_DIM
            gsum = jnp.asarray((gidx[:, None] == np.arange(128)[None]).astype(np.float32)).astype(BF16)
            qn = jnp.tile(da_q_norm[j], 2 * DA_HEADS)[None]
            kn = jnp.tile(da_k_norm[j], 2 * DA_HEADS)[None]
            cos, sin = _rope_tables(xs.shape[0])
            n_ctx = cs.shape[0]
            nk = n_ctx + xs.shape[0]
            qx, k_all, v_all = _qkv(xs, mix_x, w_qkv, gsum, qn, kn, cos, sin, kv_rows=nk, kv_offset=n_ctx)
            qc, k_all, v_all = _qkv(cs, mix_c, w_qkv, gsum, qn, kn, kv_rows=nk, kv_into=(k_all, v_all))
            lam_rows = jnp.concatenate([da_lam_q1[j][None], da_lam_k1[j][None], da_lam_q2[j][None],
                                        da_lam_k2[j][None], jnp.zeros((4, DA_HEAD_DIM), F32)], axis=0)
            w_out = da_w_out[j].astype(BF16)
            zero_b = jnp.zeros((1, d), F32)
            ox = _attention(qx, k_all, v_all, lam_rows, da_subln[j][None], lam_init)
            new_x = _proj_res(ox, w_out, zero_b, mod_slice(i, 0, 2), xs)
            if not last:
                oc = _attention(qc, k_all[:n_ctx], v_all[:n_ctx], lam_rows, da_subln[j][None], lam_init)
                cs = _proj_res(oc, w_out, zero_b, mod_slice(i, 1, 2), cs)
            xs = new_x
        experts = (moe_w_gate, moe_w_up, moe_w_down, i)
        if not last:
            cs = _moe_block(cs, _mod_rows(norm_ffn[i], mods[i], 1, 3), mod_slice(i, 1, 5), moe_router[i], *experts)
        xs = _moe_block(xs, _mod_rows(norm_ffn[i], mods[i], 0, 3), mod_slice(i, 0, 5), moe_router[i], *experts)
    return xs[None]
```

```python
import functools
import math

import jax
import jax.numpy as jnp
import numpy as np
from jax import lax
from jax.experimental import pallas as pl
from jax.experimental.pallas import tpu as pltpu

F32 = jnp.float32
BF16 = jnp.bfloat16
I32 = jnp.int32
HIGHEST = lax.Precision.HIGHEST

D_MODEL = 1024
N_MOD = 6
NORM_EPS = 1e-6
GRID_W = 64
HY_ORDER = 2
HY_SHORT = 3
HY_EMB_BANDS = 16
HY_EMB_DIM = 1 + 2 * HY_EMB_BANDS
HY_FILTER_HIDDEN = 64
HY_DECAY_FAST = 0.3
HY_DECAY_SLOW = 1.5
HY_DECAY_TARGET = 1e-2
HY_FILTER_EPS = 1e-6
Z_SIGN_COL = 33
DA_HEADS = 8
DA_HEAD_DIM = 64
DA_V_DIM = 128
ROPE_AXIS_DIM = 32
ROPE_THETA = 10000.0
SUBLN_EPS = 1e-5
N_EXPERTS = 16
EC_CAPACITY = 2
D_EXPERT = 1024
TOK_BLK = 256
ROW_ALIGN = 16
WIN_SHIFT = 7
WIN = 1 << WIN_SHIFT

VMEM_LIMIT = 56 * 1024 * 1024


def _cp(*sem):
    return pltpu.CompilerParams(dimension_semantics=sem, vmem_limit_bytes=VMEM_LIMIT)


def _const_spec(shape):
    nd = len(shape)
    return pl.BlockSpec(shape, lambda *_: (0,) * nd)


def _dot(a, b):
    return jnp.dot(a, b, preferred_element_type=F32)


def _dot_nt(a, b):
    return lax.dot_general(a, b, (((1,), (1,)), ((), ())), preferred_element_type=F32)


def _norm_mod(x, mod, eps=NORM_EPS):
    ms = jnp.mean(x * x, axis=-1, keepdims=True)
    return x * lax.rsqrt(ms + eps) * (mod[0:1] * (1.0 + mod[1:2])) + mod[2:3]


def _lane_tile(x, reps):
    return jnp.concatenate([x] * reps, axis=1) if reps > 1 else x


def _adaln_kernel(c_ref, w_ref, b_ref, o_ref):
    c = c_ref[...]
    s = c / (1.0 + jnp.exp(-c))
    o_ref[0] = jnp.dot(s, w_ref[0], precision=HIGHEST, preferred_element_type=F32) + b_ref[0]


def _adaln(cond8, ada_w, ada_b):
    depth, d, nout = ada_w.shape
    tn = 1536
    return pl.pallas_call(
        _adaln_kernel,
        grid=(depth, nout // tn),
        in_specs=[_const_spec((8, d)),
                  pl.BlockSpec((1, d, tn), lambda l, j: (l, 0, j)),
                  pl.BlockSpec((1, 1, tn), lambda l, j: (l, 0, j))],
        out_specs=pl.BlockSpec((1, 8, tn), lambda l, j: (l, 0, j)),
        out_shape=jax.ShapeDtypeStruct((depth, 8, nout), F32),
        compiler_params=_cp("parallel", "parallel"),
        name="adaln",
    )(cond8, ada_w, ada_b.reshape(depth, 1, nout))


def _mod_rows(norm_g, mods, row, k0):
    d = D_MODEL
    shift = mods[row, k0 * d:(k0 + 1) * d]
    scale = mods[row, (k0 + 1) * d:(k0 + 2) * d]
    z = jnp.zeros((5, d), F32)
    return jnp.concatenate([norm_g[None], scale[None], shift[None], z], axis=0)


HALO = 16


def _hy_in_kernel(x_ref, xp_ref, xn_ref, mod_ref, w_ref, b_ref, cw_ref, v_ref, x1_ref, x2_ref, *, tm, n_rows):
    i = pl.program_id(0)
    mod = mod_ref[...]
    hm = _norm_mod(x_ref[...], mod).astype(BF16)
    hp = _norm_mod(xp_ref[...], mod).astype(BF16)
    hn = _norm_mod(xn_ref[...], mod).astype(BF16)
    hcat = jnp.concatenate([hp, hm, hn], axis=0)
    row = lax.broadcasted_iota(I32, (tm + 2 * HALO, 1), 0) + (i * tm - HALO)
    valid = jnp.logical_and(row >= 0, row < n_rows)
    d = D_MODEL
    for c, o_ref in enumerate((v_ref, x1_ref, x2_ref)):
        u = _dot(hcat, w_ref[:, c * d:(c + 1) * d]) + b_ref[:, c * d:(c + 1) * d]
        u = jnp.where(valid, u, 0.0)
        cw = cw_ref[:, c * d:(c + 1) * d]
        y = (cw[3:4] + cw[0:1] * u[HALO - 1:HALO - 1 + tm] + cw[1:2] * u[HALO:HALO + tm]
             + cw[2:3] * u[HALO + 1:HALO + 1 + tm])
        o_ref[...] = y.astype(BF16)


def _hy_in(x, mod, w_bf, b_in, conv_w, conv_b):
    n, d = x.shape
    tm = min(512, n)
    nh = n // HALO
    cw = jnp.concatenate([conv_w, conv_b[None], jnp.zeros((4, 3 * d), F32)], axis=0)
    out = jax.ShapeDtypeStruct((n, d), BF16)
    row_spec = pl.BlockSpec((tm, d), lambda i: (i, 0))
    return pl.pallas_call(
        functools.partial(_hy_in_kernel, tm=tm, n_rows=n),
        grid=(n // tm,),
        in_specs=[row_spec,
                  pl.BlockSpec((HALO, d), lambda i: (jnp.maximum(i * (tm // HALO) - 1, 0), 0)),
                  pl.BlockSpec((HALO, d), lambda i: (jnp.minimum((i + 1) * (tm // HALO), nh - 1), 0)),
                  _const_spec((8, d)), _const_spec((d, 3 * d)), _const_spec((1, 3 * d)),
                  _const_spec((8, 3 * d))],
        out_specs=[row_spec, row_spec, row_spec],
        out_shape=[out, out, out],
        compiler_params=_cp("parallel"),
        name="hyena_in",
    )(x, x, x, mod, w_bf, b_in.reshape(1, 3 * d), cw)


Z_HALF = 64


def _filter_s1_kernel(z_ref, w1_ref, b1_ref, w2_ref, b2_ref, w3_ref, b3_ref, fr_ref, w4f_ref, w4b_ref, dl_ref,
                      f1_ref, tre_ref, tim_ref, o_ref, asum_ref, *, n1):
    j0 = pl.program_id(0) * N2_GRP
    fr = fr_ref[...]
    dl = dl_ref[...]
    r = n1 // 2

    def lin(a, w_ref, b_ref):
        return jnp.dot(a, w_ref[...], precision=HIGHEST, preferred_element_type=F32) + b_ref[...]

    @pl.when(j0 == 0)
    def _():
        asum_ref[...] = jnp.zeros_like(asum_ref)

    for slot in range(N2_GRP):
        z = z_ref[slot * r:(slot + 1) * r, :]
        hid = jnp.sin(fr * lin(z, w1_ref, b1_ref))
        hid = jnp.sin(fr * lin(hid, w2_ref, b2_ref))
        hid = jnp.sin(fr * lin(hid, w3_ref, b3_ref)).astype(BF16)

        def taps(w4_ref, col):
            t = z[:, col:col + 1]
            sgn = z[:, col + Z_SIGN_COL:col + Z_SIGN_COL + 1]
            return _dot(hid, w4_ref[...]) * jnp.exp(-t * dl) * sgn

        k = jnp.concatenate([taps(w4f_ref, 0), taps(w4b_ref, Z_HALF)], axis=0)
        asum_ref[0:1, :] += jnp.sum(jnp.abs(k), axis=0, keepdims=True)
        _s1_store(_dot(f1_ref[...], k.astype(BF16)), tre_ref, tim_ref, j0 + slot, o_ref, n1, slot)


def _filter_positions(seq, n1, n2):
    i = np.arange(n1 // 2)[None, :]
    j = np.arange(n2)[:, None]
    bands = np.linspace(1e-4, HY_EMB_BANDS - 1, HY_EMB_BANDS)
    z = np.zeros((n2, n1 // 2, 2 * Z_HALF), np.float64)
    for col, r in ((0, i * n2 + j), (Z_HALF, i * n2 + j + seq)):
        pos = np.minimum(np.where(r < seq, r, 2 * seq - r), seq - 1).astype(np.float64)
        w = 2.0 * np.pi * pos / seq
        z[:, :, col] = pos / (seq - 1)
        z[:, :, col + 1:col + 1 + HY_EMB_BANDS] = np.cos(w[..., None] * bands)
        z[:, :, col + 1 + HY_EMB_BANDS:col + HY_EMB_DIM] = -np.sin(w[..., None] * bands)
        z[:, :, col + Z_SIGN_COL] = np.where(r < seq, 1.0, np.where(r == seq, 0.0, -1.0))
    return jnp.asarray(z.reshape(n2 * (n1 // 2), 2 * Z_HALF).astype(np.float32))


def _filter_stage1(seq, f_w1, f_b1, f_w2, f_b2, f_w3, f_b3, f_w4, f_freq):
    d, hid = D_MODEL, HY_FILTER_HIDDEN
    od = HY_ORDER * d
    n1, n2 = _fft_factors(seq)
    cst = _fft_consts(seq)
    z = _filter_positions(seq, n1, n2)
    zw = 2 * Z_HALF
    zero = jnp.zeros((hid, hid), F32)
    pad = jnp.zeros((Z_HALF - HY_EMB_DIM, hid), F32)
    w1 = jnp.concatenate([jnp.concatenate([f_w1, pad], axis=0), jnp.zeros((Z_HALF, hid), F32)], axis=0)
    w1 = jnp.concatenate([w1, jnp.roll(w1, Z_HALF, axis=0)], axis=1)
    blockdiag = lambda w: jnp.concatenate([jnp.concatenate([w, zero], axis=1),
                                           jnp.concatenate([zero, w], axis=1)], axis=0)
    twice = lambda v: jnp.tile(v, 2).reshape(1, 2 * hid)
    w4d = f_w4.reshape(hid, HY_ORDER, 2, d).transpose(2, 0, 1, 3).reshape(2, hid, od)
    zrows = jnp.zeros((hid, od), F32)
    w4f = jnp.concatenate([w4d[0], zrows], axis=0).astype(BF16)
    w4b = jnp.concatenate([zrows, w4d[1]], axis=0).astype(BF16)
    max_decay = math.log(HY_DECAY_TARGET) / HY_DECAY_FAST
    min_decay = math.log(HY_DECAY_TARGET) / HY_DECAY_SLOW
    deltas = np.abs(np.linspace(min_decay, max_decay, d, dtype=np.float32))
    dl = jnp.asarray(np.tile(deltas, HY_ORDER)[None])
    r = n1 // 2
    return pl.pallas_call(
        functools.partial(_filter_s1_kernel, n1=n1),
        grid=(n2 // N2_GRP,),
        in_specs=[pl.BlockSpec((N2_GRP * r, zw), lambda j: (j, 0)),
                  _const_spec((zw, 2 * hid)), _const_spec((1, 2 * hid)),
                  _const_spec((2 * hid, 2 * hid)), _const_spec((1, 2 * hid)),
                  _const_spec((2 * hid, 2 * hid)), _const_spec((1, 2 * hid)),
                  _const_spec((1, 2 * hid)),
                  _const_spec((2 * hid, od)), _const_spec((2 * hid, od)), _const_spec((1, od)),
                  _const_spec((2 * n1, n1)), _const_spec((n1, n2)), _const_spec((n1, n2))],
        out_specs=[pl.BlockSpec((2, n1 // K1_GRP, od // LANES, N2_GRP * PAIRS, LANES), lambda j: (0, 0, 0, j, 0)),
                   _const_spec((8, od))],
        out_shape=[jax.ShapeDtypeStruct((2, n1 // K1_GRP, od // LANES, n2 * PAIRS, LANES), jnp.uint32),
                   jax.ShapeDtypeStruct((8, od), F32)],
        compiler_params=_cp("arbitrary"),
        name="hyena_filter",
    )(z, w1, twice(f_b1), blockdiag(f_w2), twice(f_b2), blockdiag(f_w3), twice(f_b3), twice(f_freq),
      w4f, w4b, dl, cst["f1_full"], cst["tre"], cst["tim"])


def _fft_factors(seq):
    n = 2 * seq
    n1 = 256 if n >= 32768 else 32
    n2 = n // n1
    assert n1 * n2 == n and n2 % 16 == 0 and n1 % 32 == 0
    return n1, n2


@functools.lru_cache(maxsize=None)
def _fft_consts(seq):
    n = 2 * seq
    n1, n2 = _fft_factors(seq)
    k1 = np.arange(n1, dtype=np.float64)[:, None] + 0.5
    th1 = 2.0 * np.pi * k1 * np.arange(n1, dtype=np.float64)[None] / n1
    perm = np.concatenate([np.arange(0, n1, 2), np.arange(1, n1, 2)])
    f1_full = np.concatenate([np.cos(th1)[perm], -np.sin(th1)[perm]], axis=0)
    f1_half = f1_full[:, :n1 // 2]
    tw = 2.0 * np.pi * k1 * np.arange(n2, dtype=np.float64)[None] / n
    tre, tim = np.cos(tw), -np.sin(tw)
    h2 = n2 // 2
    th2 = 2.0 * np.pi * np.arange(h2, dtype=np.float64)[:, None] * np.arange(n2, dtype=np.float64)[None] / n2
    c2, s2 = np.cos(th2), np.sin(th2)
    m2f = np.block([[c2, s2], [-s2, c2]])
    m2i = np.block([[c2.T, -s2.T], [s2.T, c2.T]])
    thb = th1[:, :n1 // 2].T
    gi = (2.0 / n) * np.concatenate([np.cos(thb)[:, perm], -np.sin(thb)[:, perm]], axis=1)
    bf = lambda a: jnp.asarray(a.astype(np.float32)).astype(BF16)
    f32 = lambda a: jnp.asarray(a.astype(np.float32))
    grp = lambda a: a.T.reshape(n2, n1 // K1_GRP, K1_GRP).transpose(1, 0, 2)
    return dict(f1_full=bf(f1_full), f1_half=bf(f1_half), tre=f32(tre[perm]), tim=f32(tim[perm]),
                tre_grp=f32(grp(tre)), tim_grp=f32(grp(tim)), m2f=bf(m2f), m2i=bf(m2i), gi=bf(gi))


def _pick_col(tbl, idx):
    lane = lax.broadcasted_iota(I32, tbl.shape, 1)
    return jnp.sum(jnp.where(lane == idx, tbl, 0.0), axis=1, keepdims=True)


K1_GRP = 16
PAIRS = K1_GRP // 2
LANES = 128
N2_GRP = 4
MID_TC = 512


def _unpack_pair(w, half):
    bits = lax.shift_left(w, jnp.uint32(16)) if half == 0 else (w & jnp.uint32(0xFFFF0000))
    return pltpu.bitcast(bits, F32).astype(BF16)


def _pack_pair(even, odd):
    ue = pltpu.bitcast(even.astype(BF16).astype(F32), jnp.uint32)
    uo = pltpu.bitcast(odd.astype(BF16).astype(F32), jnp.uint32)
    return lax.shift_right_logical(ue, jnp.uint32(16)) | uo


def _s1_store(a, tre_ref, tim_ref, j, o_ref, n1, slot):
    are, aim = a[:n1], a[n1:]
    tre = _pick_col(tre_ref[...], j)
    tim = _pick_col(tim_ref[...], j)
    re = are * tre - aim * tim
    im = are * tim + aim * tre
    h = n1 // 2
    shape = (n1 // K1_GRP, PAIRS, a.shape[1])
    for part, val in enumerate((re, im)):
        words = _pack_pair(val[:h], val[h:]).reshape(shape)
        for ct in range(a.shape[1] // LANES):
            o_ref[part, :, ct, slot * PAIRS:(slot + 1) * PAIRS, :] = words[:, :, ct * LANES:(ct + 1) * LANES]


def _fft_s1_kernel(f1_ref, x_ref, tre_ref, tim_ref, o_ref, *, n1, tc):
    j0 = pl.program_id(0) * N2_GRP
    for slot in range(N2_GRP):
        a = _dot(f1_ref[...], x_ref[:, slot * tc:(slot + 1) * tc])
        _s1_store(a, tre_ref, tim_ref, j0 + slot, o_ref, n1, slot)


def _fft_stage1(x2d, f1, tre, tim, n1, n2, chans):
    r = x2d.shape[0]
    return pl.pallas_call(
        functools.partial(_fft_s1_kernel, n1=n1, tc=chans),
        grid=(n2 // N2_GRP,),
        in_specs=[_const_spec((2 * n1, r)),
                  pl.BlockSpec((r, N2_GRP * chans), lambda j: (0, j)),
                  _const_spec((n1, n2)), _const_spec((n1, n2))],
        out_specs=pl.BlockSpec((2, n1 // K1_GRP, chans // LANES, N2_GRP * PAIRS, LANES),
                               lambda j: (0, 0, 0, j, 0)),
        out_shape=jax.ShapeDtypeStruct((2, n1 // K1_GRP, chans // LANES, n2 * PAIRS, LANES), jnp.uint32),
        compiler_params=_cp("parallel"),
        name="fft_stage1",
    )(f1, x2d, tre, tim)


def _load_pair(b_ref, part, s, n2):
    return jnp.concatenate([b_ref[part, 0, ct, pl.ds(s, n2, stride=PAIRS), :] for ct in range(b_ref.shape[2])],
                           axis=1)


def _store_pair(o_ref, part, s, n2, words):
    for ct in range(o_ref.shape[2]):
        o_ref[part, 0, ct, pl.ds(s, n2, stride=PAIRS), :] = words[:, ct * LANES:(ct + 1) * LANES]


def _fft_s2_kernel(m2f_ref, b_ref, o_ref):
    n2 = m2f_ref.shape[0]
    for s in range(PAIRS):
        wre = _load_pair(b_ref, 0, s, n2)
        wim = _load_pair(b_ref, 1, s, n2)
        for half in range(2):
            b = jnp.concatenate([_unpack_pair(wre, half), _unpack_pair(wim, half)], axis=0)
            o_ref[2 * s + half] = _dot(m2f_ref[...], b).astype(BF16)


def _fft_stage2(b5, m2f, n1, n2, chans):
    return pl.pallas_call(
        _fft_s2_kernel,
        grid=(n1 // K1_GRP, chans // MID_TC),
        in_specs=[_const_spec((n2, 2 * n2)),
                  pl.BlockSpec((2, 1, MID_TC // LANES, n2 * PAIRS, LANES), lambda g, c: (0, g, c, 0, 0))],
        out_specs=pl.BlockSpec((K1_GRP, n2, MID_TC), lambda g, c: (g, 0, c)),
        out_shape=jax.ShapeDtypeStruct((n1, n2, chans), BF16),
        compiler_params=_cp("parallel", "parallel"),
        name="fft_stage2",
    )(m2f, b5)


def _fft_mid_kernel(m2f_ref, m2i_ref, b_ref, k_ref, inv_ref, tre_ref, tim_ref, o_ref, *, n2):
    h2 = n2 // 2
    inv = inv_ref[...]
    tre_g = tre_ref[0]
    tim_g = tim_ref[0]
    for s in range(PAIRS):
        wre = _load_pair(b_ref, 0, s, n2)
        wim = _load_pair(b_ref, 1, s, n2)
        res = []
        for half in range(2):
            kl = 2 * s + half
            b = jnp.concatenate([_unpack_pair(wre, half), _unpack_pair(wim, half)], axis=0)
            x = _dot(m2f_ref[...], b)
            kk = k_ref[kl].astype(F32) * inv
            xre, xim = x[:h2], x[h2:]
            kre, kim = kk[:h2], kk[h2:]
            y = jnp.concatenate([xre * kre - xim * kim, xre * kim + xim * kre], axis=0).astype(BF16)
            c = _dot(m2i_ref[...], y)
            cre, cim = c[:n2], c[n2:]
            tre = tre_g[:, kl:kl + 1]
            tim = tim_g[:, kl:kl + 1]
            res.append((cre * tre + cim * tim, cim * tre - cre * tim))
        _store_pair(o_ref, 0, s, n2, _pack_pair(res[0][0], res[1][0]))
        _store_pair(o_ref, 1, s, n2, _pack_pair(res[0][1], res[1][1]))


def _fft_mid(b5, kspec, inv_norm, order, cst, n1, n2):
    d = D_MODEL
    nc = d // MID_TC
    blk = pl.BlockSpec((2, 1, MID_TC // LANES, n2 * PAIRS, LANES), lambda g, c: (0, g, c, 0, 0))
    tw = pl.BlockSpec((1, n2, K1_GRP), lambda g, c: (g, 0, 0))
    return pl.pallas_call(
        functools.partial(_fft_mid_kernel, n2=n2),
        grid=(n1 // K1_GRP, nc),
        in_specs=[_const_spec((n2, 2 * n2)), _const_spec((2 * n2, n2)), blk,
                  pl.BlockSpec((K1_GRP, n2, MID_TC), lambda g, c: (g, 0, order * nc + c)),
                  pl.BlockSpec((1, MID_TC), lambda g, c: (0, order * nc + c)),
                  tw, tw],
        out_specs=blk,
        out_shape=jax.ShapeDtypeStruct((2, n1 // K1_GRP, d // LANES, n2 * PAIRS, LANES), jnp.uint32),
        compiler_params=_cp("parallel", "parallel"),
        name="fft_mid",
    )(cst["m2f"], cst["m2i"], b5, kspec, inv_norm, cst["tre_grp"], cst["tim_grp"])


def _fft_last_kernel(gi_ref, c_ref, gate_ref, z_ref, skip_ref, o_ref):
    h = c_ref.shape[1] * PAIRS
    d = c_ref.shape[2] * LANES
    for slot in range(N2_GRP):
        rows = slice(slot * PAIRS, (slot + 1) * PAIRS)
        cols = slice(slot * d, (slot + 1) * d)

        def words(part):
            return jnp.concatenate([c_ref[part, :, ct, rows, :].reshape(h, LANES)
                                    for ct in range(c_ref.shape[2])], axis=1)

        wre = words(0)
        wim = words(1)
        c = jnp.concatenate([_unpack_pair(wre, 0), _unpack_pair(wre, 1), _unpack_pair(wim, 0),
                             _unpack_pair(wim, 1)], axis=0)
        y = _dot(gi_ref[...], c)
        z = z_ref[:, cols].astype(F32)
        o_ref[:, cols] = (gate_ref[:, cols].astype(F32) * (y + skip_ref[...] * z)).astype(BF16)


def _fft_last(c5, gate2d, z2d, skip_row, gi, n1, n2):
    d = D_MODEL
    r = n1 // 2
    col = pl.BlockSpec((r, N2_GRP * d), lambda j: (0, j))
    return pl.pallas_call(
        _fft_last_kernel,
        grid=(n2 // N2_GRP,),
        in_specs=[_const_spec((r, 2 * n1)),
                  pl.BlockSpec((2, n1 // K1_GRP, d // LANES, N2_GRP * PAIRS, LANES), lambda j: (0, 0, 0, j, 0)),
                  col, col, _const_spec((1, d))],
        out_specs=col,
        out_shape=jax.ShapeDtypeStruct((r, n2 * d), BF16),
        compiler_params=_cp("parallel"),
        name="fft_last",
    )(gi, c5, gate2d, z2d, skip_row)


def _long_conv_gate(z_in, gate, skip_row, kspec, inv_norm, order, seq):
    d = D_MODEL
    n1, n2 = _fft_factors(seq)
    cst = _fft_consts(seq)
    z2d = z_in.reshape(n1 // 2, n2 * d)
    b = _fft_stage1(z2d, cst["f1_half"], cst["tre"], cst["tim"], n1, n2, d)
    c = _fft_mid(b, kspec, inv_norm, order, cst, n1, n2)
    out = _fft_last(c, gate.reshape(n1 // 2, n2 * d), z2d, skip_row, cst["gi"], n1, n2)
    return out.reshape(seq, d)


def _filter_spectrum(seq, fparams):
    d = D_MODEL
    od = HY_ORDER * d
    n1, n2 = _fft_factors(seq)
    cst = _fft_consts(seq)
    b, asum = _filter_stage1(seq, *fparams)
    kspec = _fft_stage2(b, cst["m2f"], n1, n2, od)
    return kspec, asum


def _proj_res_kernel(a_ref, w_ref, b_ref, g_ref, x_ref, o_ref):
    y = _dot(a_ref[...], w_ref[...]) + b_ref[...]
    o_ref[...] = x_ref[...] + g_ref[...] * y


def _proj_res(a, w_bf, b_row, gate_row, xres):
    n, d = xres.shape
    tm = min(512, n)
    row = pl.BlockSpec((tm, d), lambda i: (i, 0))
    return pl.pallas_call(
        _proj_res_kernel,
        grid=(n // tm,),
        in_specs=[row, _const_spec((d, d)), _const_spec((1, d)), _const_spec((1, d)), row],
        out_specs=row,
        out_shape=jax.ShapeDtypeStruct((n, d), F32),
        compiler_params=_cp("parallel"),
        name="proj_residual",
    )(a, w_bf, b_row, gate_row, xres)


def _ffn_in_kernel(x_ref, mod_ref, wt_ref, h_ref, aff_ref):
    h = _norm_mod(x_ref[...], mod_ref[...])
    hi = h.astype(BF16)
    lo = (h - hi.astype(F32)).astype(BF16)
    wt = wt_ref[...]
    whi = wt.astype(BF16)
    wlo = (wt - whi.astype(F32)).astype(BF16)
    logits = _dot_nt(whi, hi) + (_dot_nt(whi, lo) + _dot_nt(wlo, hi))
    m = jnp.max(logits, axis=0, keepdims=True)
    p = jnp.exp(logits - m)
    aff_ref[...] = p / jnp.sum(p, axis=0, keepdims=True)
    h_ref[...] = hi


def _ffn_in(x, mod, w_router):
    n, d = x.shape
    e = N_EXPERTS
    tm = min(512, n)
    return pl.pallas_call(
        _ffn_in_kernel,
        grid=(n // tm,),
        in_specs=[pl.BlockSpec((tm, d), lambda i: (i, 0)), _const_spec((8, d)), _const_spec((e, d))],
        out_specs=[pl.BlockSpec((tm, d), lambda i: (i, 0)), pl.BlockSpec((e, tm), lambda i: (0, i))],
        out_shape=[jax.ShapeDtypeStruct((n, d), BF16), jax.ShapeDtypeStruct((e, n), F32)],
        compiler_params=_cp("parallel"),
        name="moe_router",
    )(x, mod, w_router.T)


def _select_kernel(a_ref, pos_ref, s0_ref, sel_ref, *, cap, nblk):
    e = N_EXPERTS
    bits = pltpu.bitcast(a_ref[...], I32)

    def bisect(i, thr):
        cand = thr | jnp.left_shift(jnp.int32(1), 30 - i)
        cnt = jnp.sum(jnp.where(bits >= cand, 1.0, 0.0), axis=1, keepdims=True)
        return jnp.where(cnt >= cap, cand, thr)

    thr = lax.fori_loop(0, 31, bisect, jnp.zeros((e, 1), I32))
    n_gt = jnp.sum(jnp.where(bits > thr, 1.0, 0.0), axis=1, keepdims=True)
    need = cap - n_gt
    r = lax.broadcasted_iota(I32, (TOK_BLK, TOK_BLK), 0)
    c = lax.broadcasted_iota(I32, (TOK_BLK, TOK_BLK), 1)
    upper = jnp.where(r < c, 1.0, 0.0).astype(BF16)

    def pass1(j, carry):
        sl = pl.ds(pl.multiple_of(j * TOK_BLK, TOK_BLK), TOK_BLK)
        bj = pltpu.bitcast(a_ref[:, sl], I32)
        eq = jnp.where(bj == thr, 1.0, 0.0)
        rank = _dot(eq.astype(BF16), upper) + carry
        keep = jnp.logical_or(bj > thr, jnp.logical_and(bj == thr, rank < need))
        sel_ref[:, sl] = jnp.where(keep, 1.0, 0.0)
        return carry + jnp.sum(eq, axis=1, keepdims=True)

    lax.fori_loop(0, nblk, pass1, jnp.zeros((e, 1), F32))

    def pass2(j, carry):
        sl = pl.ds(pl.multiple_of(j * TOK_BLK, TOK_BLK), TOK_BLK)
        s = sel_ref[:, sl]
        slot = _dot(s.astype(BF16), upper) + carry
        pos_ref[:, sl] = jnp.where(s > 0.5, slot, -1.0).astype(I32)
        s0_ref[j] = jnp.broadcast_to(carry, (e, 128)).astype(I32)
        return carry + jnp.sum(s, axis=1, keepdims=True)

    total = lax.fori_loop(0, nblk, pass2, jnp.zeros((e, 1), F32))
    s0_ref[nblk] = jnp.broadcast_to(total, (e, 128)).astype(I32)


def _select(aff_t, cap):
    e, n = aff_t.shape
    nblk = n // TOK_BLK
    return pl.pallas_call(
        functools.partial(_select_kernel, cap=cap, nblk=nblk),
        out_shape=[jax.ShapeDtypeStruct((e, n), I32), jax.ShapeDtypeStruct((nblk + 1, e, 128), I32)],
        scratch_shapes=[pltpu.VMEM((e, n), F32)],
        compiler_params=pltpu.CompilerParams(vmem_limit_bytes=VMEM_LIMIT),
        name="moe_select",
    )(aff_t)


def _block_windows(s0_ref, blk, e_idx):
    s0 = s0_ref[blk * N_EXPERTS + e_idx]
    s1 = s0_ref[(blk + 1) * N_EXPERTS + e_idx]
    start = lax.shift_left(lax.shift_right_logical(s0, 4), 4)
    nwin = jnp.where(s1 > s0, lax.shift_right_logical(s1 - start + (WIN - 1), WIN_SHIFT), 0)
    return start, nwin


def _token_block(i):
    start = i * TOK_BLK
    return pl.ds(start if isinstance(i, int) else pl.multiple_of(start, TOK_BLK), TOK_BLK)


def _gather_kernel(s0_ref, pos_ref, h_ref, xe_ref, *, sub, nchunk):
    e_idx = pl.program_id(0)
    xe_ref[...] = jnp.zeros_like(xe_ref)
    rows = lax.broadcasted_iota(I32, (WIN, TOK_BLK), 0)

    def window(blk, base):
        tok = _token_block(blk)
        base = pl.multiple_of(base, ROW_ALIGN)
        onehot = jnp.where(rows == pos_ref[0, :, tok] - base, 1.0, 0.0).astype(BF16)
        got = _dot(onehot, h_ref[tok, :]).astype(BF16)
        win = pl.ds(base, WIN)
        xe_ref[0, win, :] = xe_ref[0, win, :] + got

    def chunk(c, carry):
        plan = [_block_windows(s0_ref, c * sub + i, e_idx) for i in range(sub)]
        single = functools.reduce(jnp.logical_and, [nwin <= 1 for _, nwin in plan])

        @pl.when(single)
        def _():
            for i, (start, _) in enumerate(plan):
                window(c * sub + i, start)

        @pl.when(jnp.logical_not(single))
        def _():
            def body(i, carry2):
                start, nwin = _block_windows(s0_ref, c * sub + i, e_idx)
                return lax.fori_loop(0, nwin, lambda w, cc: (window(c * sub + i, start + w * WIN), cc)[1], carry2)

            lax.fori_loop(0, sub, body, 0)

        return carry

    lax.fori_loop(0, nchunk, chunk, 0)


def _gather(s0_flat, pos, h, cap_pad):
    e, n = pos.shape
    d = h.shape[1]
    sub = min(8, n // TOK_BLK)
    return pl.pallas_call(
        functools.partial(_gather_kernel, sub=sub, nchunk=n // (sub * TOK_BLK)),
        grid_spec=pltpu.PrefetchScalarGridSpec(
            num_scalar_prefetch=1,
            grid=(e,),
            in_specs=[pl.BlockSpec((1, 1, n), lambda ei, s0: (ei, 0, 0)),
                      pl.BlockSpec((n, d), lambda ei, s0: (0, 0), pipeline_mode=pl.Buffered(1))],
            out_specs=pl.BlockSpec((1, cap_pad, d), lambda ei, s0: (ei, 0, 0)),
        ),
        out_shape=jax.ShapeDtypeStruct((e, cap_pad, d), BF16),
        compiler_params=_cp("parallel"),
        name="moe_gather",
    )(s0_flat, pos.reshape(e, 1, n), h)


def _expert_kernel(x_ref, wg_ref, wu_ref, wd_ref, y_ref, wg_scr, wu_scr, wd_scr, *, n_real):
    j = pl.program_id(1)

    @pl.when(j == 0)
    def _():
        wg_scr[...] = wg_ref[0].astype(BF16)
        wu_scr[...] = wu_ref[0].astype(BF16)
        wd_scr[...] = wd_ref[0].astype(BF16)

    @pl.when(j < n_real)
    def _():
        x = x_ref[0]
        g = _dot(x, wg_scr[...])
        u = _dot(x, wu_scr[...])
        a = (g / (1.0 + jnp.exp(-g))) * u
        y_ref[0] = _dot(a.astype(BF16), wd_scr[...]).astype(BF16)

    @pl.when(j >= n_real)
    def _():
        y_ref[0] = jnp.zeros_like(y_ref[0])


def _experts(xe, wg, wu, wd, layer, cap):
    e, cap_pad, d = xe.shape
    f = wg.shape[3]
    tm = 256 if cap % 256 == 0 else cap_pad
    tile = pl.BlockSpec((1, tm, d), lambda ei, j: (ei, j, 0))
    return pl.pallas_call(
        functools.partial(_expert_kernel, n_real=pl.cdiv(cap, tm)),
        grid=(e, cap_pad // tm),
        in_specs=[tile,
                  pl.BlockSpec((None, 1, d, f), lambda ei, j: (layer, ei, 0, 0)),
                  pl.BlockSpec((None, 1, d, f), lambda ei, j: (layer, ei, 0, 0)),
                  pl.BlockSpec((None, 1, f, d), lambda ei, j: (layer, ei, 0, 0))],
        out_specs=tile,
        out_shape=jax.ShapeDtypeStruct((e, cap_pad, d), BF16),
        scratch_shapes=[pltpu.VMEM((d, f), BF16), pltpu.VMEM((d, f), BF16), pltpu.VMEM((f, d), BF16)],
        compiler_params=_cp("parallel", "arbitrary"),
        name="moe_experts",
    )(xe, wg, wu, wd)


COMBINE_COLS = 512
COMBINE_BLKS = 2


def _combine_kernel(s0_ref, pos_ref, aff_ref, y_ref, x_ref, gate_ref, o_ref, acc_ref, *, nsub):
    t = pl.program_id(1)
    rows = lax.broadcasted_iota(I32, (WIN, TOK_BLK), 0)

    def selector(i, e, base):
        tok = _token_block(i)
        rel = pos_ref[e:e + 1, tok] - base
        return jnp.where(rows == rel, aff_ref[e:e + 1, tok], 0.0).astype(BF16)

    def contract(sel, ywin):
        return lax.dot_general(sel, ywin, (((0,), (0,)), ((), ())), preferred_element_type=F32)

    for i in range(nsub):
        plan = [_block_windows(s0_ref, t * nsub + i, e) for e in range(N_EXPERTS)]
        single = functools.reduce(jnp.logical_and, [nwin <= 1 for _, nwin in plan])
        tok = _token_block(i)

        @pl.when(single)
        def _():
            bases = [pl.multiple_of(start, ROW_ALIGN) for start, _ in plan]
            sel = jnp.concatenate([selector(i, e, b) for e, b in enumerate(bases)], axis=0)
            ywin = jnp.concatenate([y_ref[e, pl.ds(b, WIN), :] for e, b in enumerate(bases)], axis=0)
            o_ref[tok, :] = x_ref[tok, :] + gate_ref[...] * contract(sel, ywin)

        @pl.when(jnp.logical_not(single))
        def _():
            acc_ref[...] = jnp.zeros_like(acc_ref)
            for e, (start, nwin) in enumerate(plan):
                def window(w, carry):
                    base = pl.multiple_of(start + w * WIN, ROW_ALIGN)
                    acc_ref[...] += contract(selector(i, e, base), y_ref[e, pl.ds(base, WIN), :])
                    return carry

                lax.fori_loop(0, nwin, window, 0)
            o_ref[tok, :] = x_ref[tok, :] + gate_ref[...] * acc_ref[...]


def _combine(s0_flat, pos, aff_t, y, xres, gate_row):
    n, d = xres.shape
    e, cap_pad, _ = y.shape
    nsub = min(COMBINE_BLKS, n // TOK_BLK)
    rows = nsub * TOK_BLK
    cq = COMBINE_COLS
    tile = pl.BlockSpec((rows, cq), lambda q, t, s0: (t, q))
    return pl.pallas_call(
        functools.partial(_combine_kernel, nsub=nsub),
        grid_spec=pltpu.PrefetchScalarGridSpec(
            num_scalar_prefetch=1,
            grid=(d // cq, n // rows),
            in_specs=[pl.BlockSpec((e, rows), lambda q, t, s0: (0, t)),
                      pl.BlockSpec((e, rows), lambda q, t, s0: (0, t)),
                      pl.BlockSpec((e, cap_pad, cq), lambda q, t, s0: (0, 0, q), pipeline_mode=pl.Buffered(1)),
                      tile,
                      pl.BlockSpec((1, cq), lambda q, t, s0: (0, q))],
            out_specs=tile,
            scratch_shapes=[pltpu.VMEM((TOK_BLK, cq), F32)],
        ),
        out_shape=jax.ShapeDtypeStruct((n, d), F32),
        compiler_params=_cp("parallel", "arbitrary"),
        name="moe_combine",
    )(s0_flat, pos, aff_t, y, xres, gate_row)


def _moe_block(x, mod, gate_row, w_router, wg, wu, wd, layer):
    n = x.shape[0]
    cap = EC_CAPACITY * n // N_EXPERTS
    cap_pad = cap + TOK_BLK
    h, aff_t = _ffn_in(x, mod, w_router)
    pos, s0 = _select(aff_t, cap)
    s0_flat = s0[:, :, 0].reshape(-1)
    xe = _gather(s0_flat, pos, h, cap_pad)
    y = _experts(xe, wg, wu, wd, layer, cap)
    return _combine(s0_flat, pos, aff_t, y, x, gate_row)


QK_SCALE = (DA_HEAD_DIM ** -0.5) * math.log2(math.e)


def _split_bf16(x):
    hi = x.astype(BF16)
    return hi, (x - hi.astype(F32)).astype(BF16)


def _group_rms(u, gsel_ref, gain, eps):
    g = gsel_ref[...]
    hi, lo = _split_bf16(u * u)
    r = lax.rsqrt((_dot(hi, g) + _dot(lo, g)) * (1.0 / DA_HEAD_DIM) + eps)
    rhi, rlo = _split_bf16(r)
    return u * (_dot_nt(rhi, g) + _dot_nt(rlo, g)) * gain


def _rope(u, cos, sin_signed):
    d = u.shape[1]
    half = ROPE_AXIS_DIM // 2
    lane = lax.broadcasted_iota(I32, u.shape, 1)
    first = (lane & half) == 0
    swapped = jnp.where(first, pltpu.roll(u, d - half, 1), pltpu.roll(u, half, 1))
    return u * cos + swapped * sin_signed


def _qkv_kernel(*refs, rope):
    x_ref, mod_ref, w_ref, gsum_ref, qn_ref, kn_ref = refs[:6]
    q_ref, k_ref, v_ref = refs[-3:]
    if rope:
        cos_ref, sin_ref = refs[6:8]
    d = D_MODEL
    h = _norm_mod(x_ref[...], mod_ref[...]).astype(BF16)
    if rope:
        reps = d // cos_ref.shape[1]
        cos = _lane_tile(cos_ref[...], reps)
        sin = _lane_tile(sin_ref[...], reps)
    for part, (o_ref, gain_ref) in enumerate(((q_ref, qn_ref), (k_ref, kn_ref))):
        u = _dot(h, w_ref[:, part * d:(part + 1) * d])
        u = _group_rms(u, gsum_ref, gain_ref[...], NORM_EPS)
        if rope:
            u = _rope(u, cos, sin)
        if part == 0:
            u = u * QK_SCALE
        o_ref[...] = u.astype(BF16)
    v_ref[...] = _dot(h, w_ref[:, 2 * d:]).astype(BF16)


def _qkv(x, mod, w_bf, gsum, qn_row, kn_row, cos=None, sin=None, kv_rows=None, kv_offset=0, kv_into=None):
    n, d = x.shape
    tm = min(256, n)
    kv_rows = n if kv_rows is None else kv_rows
    off = kv_offset // tm
    assert kv_offset % tm == 0
    rope = cos is not None
    row = pl.BlockSpec((tm, d), lambda i: (i, 0))
    kv_row = pl.BlockSpec((tm, d), lambda i: (i + off, 0))
    in_specs = [row, _const_spec((8, d)), _const_spec((d, 3 * d)), _const_spec(gsum.shape),
                _const_spec((1, d)), _const_spec((1, d))]
    args = [x, mod, w_bf, gsum, qn_row, kn_row]
    if rope:
        tw = cos.shape[1]
        in_specs += [pl.BlockSpec((tm, tw), lambda i: (i, 0)), pl.BlockSpec((tm, tw), lambda i: (i, 0))]
        args += [cos, sin]
    aliases = {}
    if kv_into is not None:
        aliases = {len(args): 1, len(args) + 1: 2}
        in_specs += [pl.BlockSpec(memory_space=pl.ANY), pl.BlockSpec(memory_space=pl.ANY)]
        args += list(kv_into)
    kv = jax.ShapeDtypeStruct((kv_rows, d), BF16)
    return pl.pallas_call(
        functools.partial(_qkv_kernel, rope=rope),
        grid=(n // tm,),
        in_specs=in_specs,
        out_specs=[row, kv_row, kv_row],
        out_shape=[jax.ShapeDtypeStruct((n, d), BF16), kv, kv],
        input_output_aliases=aliases,
        compiler_params=_cp("parallel"),
        name="attn_qkv",
    )(*args)


NEG_BIG = -1e30
HEAD_UNROLL = 8


def _attn_kernel(q_ref, k_ref, v_ref, lam_ref, sub_ref, o_ref, m_scr, acc_scr, sa_scr, sb_scr, ma_scr, mb_scr,
                 pa_scr, pb_scr, vprev_scr, aa_scr, ab_scr, *,
                 lam_init):
    j = pl.program_id(1)
    nj = pl.num_programs(1)
    hd, vd = DA_HEAD_DIM, DA_V_DIM
    tk = k_ref.shape[0]

    last = DA_HEADS - 1

    @pl.when(j == 0)
    def _():
        m_scr[...] = jnp.full_like(m_scr, NEG_BIG)
        acc_scr[...] = jnp.zeros_like(acc_scr)
        pa_scr[...] = jnp.zeros_like(pa_scr)
        aa_scr[...] = jnp.ones_like(aa_scr)
        sb_scr[...] = jnp.full_like(sb_scr, 2.0 * NEG_BIG)
        mb_scr[...] = jnp.full_like(mb_scr, 2.0 * NEG_BIG)
        vprev_scr[...] = jnp.zeros_like(vprev_scr)

    lane = lax.broadcasted_iota(I32, (q_ref.shape[0], vd), 1)
    ones_col = jnp.where(lax.broadcasted_iota(I32, (tk, vd), 1) == 0, 1.0, 0.0).astype(BF16)

    def head_cols(h):
        return pl.ds(pl.multiple_of(h * vd, vd), vd)

    def scores(h, comp, s_ref, mx_ref):
        qb = q_ref[:, head_cols(h)]
        in_comp = (lane >= comp * hd) & (lane < (comp + 1) * hd)
        qm = jnp.where(in_comp, qb, jnp.zeros_like(qb))
        s = _dot_nt(qm, k_ref[:, head_cols(h)])
        s_ref[...] = s
        mx_ref[...] = jnp.broadcast_to(jnp.max(s, axis=1, keepdims=True), mx_ref.shape)

    def softmax_step(h, comp, s_ref, mx_ref, p_ref, alpha_ref):
        idx = 2 * h + comp
        m_prev = m_scr[idx]
        m_new = jnp.maximum(m_prev, mx_ref[...])
        alpha_ref[...] = jnp.exp2(m_prev - m_new)
        p_ref[...] = jnp.exp2(s_ref[...] - m_new[:, 0:1]).astype(BF16)
        m_scr[idx] = m_new

    def pv_step(h, comp, p_ref, alpha_ref):
        idx = 2 * h + comp
        pv = _dot(p_ref[...], jnp.concatenate([vprev_scr[...], ones_col], axis=1))
        acc_scr[idx] = _lane_tile(alpha_ref[...], 2) * acc_scr[idx] + pv

    def head(h, carry):
        hp = (h + last) & last
        scores(h, 0, sa_scr, ma_scr)
        pv_step(hp, 0, pa_scr, aa_scr)
        softmax_step(hp, 1, sb_scr, mb_scr, pb_scr, ab_scr)
        scores(h, 1, sb_scr, mb_scr)
        pv_step(hp, 1, pb_scr, ab_scr)
        softmax_step(h, 0, sa_scr, ma_scr, pa_scr, aa_scr)
        vprev_scr[...] = v_ref[:, head_cols(h)]
        return carry

    def head_group(g, carry):
        for u in range(HEAD_UNROLL):
            carry = head(HEAD_UNROLL * g + u, carry)
        return carry

    lax.fori_loop(0, DA_HEADS // HEAD_UNROLL, head_group, 0)

    @pl.when(j == nj - 1)
    def _():
        pv_step(last, 0, pa_scr, aa_scr)
        softmax_step(last, 1, sb_scr, mb_scr, pb_scr, ab_scr)
        pv_step(last, 1, pb_scr, ab_scr)
        lp = lam_ref[...]
        lam = (jnp.exp(jnp.sum(lp[0:1] * lp[1:2], axis=1, keepdims=True))
               - jnp.exp(jnp.sum(lp[2:3] * lp[3:4], axis=1, keepdims=True)) + lam_init)
        for h in range(DA_HEADS):
            a0 = acc_scr[2 * h]
            a1 = acc_scr[2 * h + 1]
            o = a0[:, :vd] / a0[:, vd:vd + 1] - lam * (a1[:, :vd] / a1[:, vd:vd + 1])
            ms = jnp.mean(o * o, axis=1, keepdims=True)
            o = o * lax.rsqrt(ms + SUBLN_EPS) * (sub_ref[...] * (1.0 - lam_init))
            o_ref[:, h * vd:(h + 1) * vd] = o.astype(BF16)


def _attention(q, k_all, v_all, lam_rows, subln_row, lam_init):
    n, d = q.shape
    nk = k_all.shape[0]
    tq = min(512, n)
    tk = 1280 if nk % 1280 == 0 else 256
    assert nk % tk == 0
    nc = 2 * DA_HEADS
    return pl.pallas_call(
        functools.partial(_attn_kernel, lam_init=lam_init),
        grid=(n // tq, nk // tk),
        in_specs=[pl.BlockSpec((tq, d), lambda i, j: (i, 0)),
                  pl.BlockSpec((tk, d), lambda i, j: (j, 0)),
                  pl.BlockSpec((tk, d), lambda i, j: (j, 0)),
                  _const_spec((8, DA_HEAD_DIM)), _const_spec((1, DA_V_DIM))],
        out_specs=pl.BlockSpec((tq, d), lambda i, j: (i, 0)),
        out_shape=jax.ShapeDtypeStruct((n, d), BF16),
        scratch_shapes=[pltpu.VMEM((nc, tq, DA_V_DIM), F32), pltpu.VMEM((nc, tq, 2 * DA_V_DIM), F32),
                        pltpu.VMEM((tq, tk), F32), pltpu.VMEM((tq, tk), F32),
                        pltpu.VMEM((tq, DA_V_DIM), F32), pltpu.VMEM((tq, DA_V_DIM), F32),
                        pltpu.VMEM((tq, tk), BF16), pltpu.VMEM((tq, tk), BF16),
                        pltpu.VMEM((tk, DA_V_DIM), BF16),
                        pltpu.VMEM((tq, DA_V_DIM), F32), pltpu.VMEM((tq, DA_V_DIM), F32)],
        compiler_params=_cp("parallel", "arbitrary"),
        name="diff_attention",
    )(q, k_all, v_all, lam_rows, subln_row)


def _rope_tables(n):
    lane = np.arange(2 * DA_HEAD_DIM) % DA_HEAD_DIM
    nf = ROPE_AXIS_DIM // 2
    inv = ROPE_THETA ** (-np.arange(0, ROPE_AXIS_DIM, 2, dtype=np.float32) / ROPE_AXIS_DIM)
    by_row = jnp.asarray(np.where(lane < ROPE_AXIS_DIM, inv[lane % nf], 0.0).astype(np.float32))
    by_col = jnp.asarray(np.where(lane >= ROPE_AXIS_DIM, inv[lane % nf], 0.0).astype(np.float32))
    sign = jnp.asarray(np.where(lane % ROPE_AXIS_DIM < nf, -1.0, 1.0).astype(np.float32))
    t = jnp.arange(n, dtype=I32)
    row = (t // GRID_W).astype(F32)[:, None]
    col = (t % GRID_W).astype(F32)[:, None]
    ang = row * by_row[None] + col * by_col[None]
    return jnp.cos(ang), jnp.sin(ang) * sign[None]


def _hyena_layer(x, mod, gate_row, kspec, inv_norm, w_in_bf, b_in, conv_w, conv_b, skip, w_out_bf, b_out):
    seq = x.shape[0]
    v, x1, x2 = _hy_in(x, mod, w_in_bf, b_in, conv_w, conv_b)
    z = _long_conv_gate(v, x1, skip[0:1], kspec, inv_norm, 0, seq)
    z = _long_conv_gate(z, x2, skip[1:2], kspec, inv_norm, 1, seq)
    return _proj_res(z, w_out_bf, b_out.reshape(1, -1), gate_row, x)


def kernel(x, c, ctx, c_ctx, ada_w, ada_b, norm_mix, norm_ffn, hy_w_in, hy_b_in, hy_conv_w, hy_conv_b, hy_f_w1, hy_f_b1, hy_f_w2, hy_f_b2, hy_f_w3, hy_f_b3, hy_f_w4, hy_f_freq, hy_skip, hy_w_out, hy_b_out, da_w_qkv, da_q_norm, da_k_norm, da_lam_q1, da_lam_k1, da_lam_q2, da_lam_k2, da_subln, da_w_out, moe_router, moe_w_gate, moe_w_up, moe_w_down):
    d = D_MODEL
    depth = ada_w.shape[0]
    assert x.shape[0] == 1 and x.shape[2] == d
    xs = x[0]
    cs = ctx[0]
    cond8 = jnp.concatenate([c[0:1], c_ctx[None], jnp.zeros((6, d), F32)], axis=0)
    mods = _adaln(cond8, ada_w, ada_b)

    def mod_slice(i, row, k):
        return mods[i, row, k * d:(k + 1) * d][None]

    for i in range(depth):
        last = i == depth - 1
        j = i // 2
        mix_x = _mod_rows(norm_mix[i], mods[i], 0, 0)
        mix_c = _mod_rows(norm_mix[i], mods[i], 1, 0)
        if i % 2 == 0:
            fparams = (hy_f_w1[j], hy_f_b1[j], hy_f_w2[j], hy_f_b2[j], hy_f_w3[j], hy_f_b3[j], hy_f_w4[j],
                       hy_f_freq[j])
            shared = (hy_w_in[j].astype(BF16), hy_b_in[j], hy_conv_w[j], hy_conv_b[j], hy_skip[j],
                      hy_w_out[j].astype(BF16), hy_b_out[j])
            kspec, asum = _filter_spectrum(xs.shape[0], fparams)
            inv_norm = 1.0 / (asum[0:1] + HY_FILTER_EPS)
            new_x = _hyena_layer(xs, mix_x, mod_slice(i, 0, 2), kspec, inv_norm, *shared)
            if not last:
                kspec_c, asum_c = _filter_spectrum(cs.shape[0], fparams)
                inv_c = 1.0 / (asum_c[0:1] + HY_FILTER_EPS)
                cs = _hyena_layer(cs, mix_c, mod_slice(i, 1, 2), kspec_c, inv_c, *shared)
            xs = new_x
        else:
            lam_init = 0.8 - 0.6 * math.exp(-0.3 * i)
            w_qkv = da_w_qkv[j].astype(BF16)
            gidx = np.arange(d) // DA_HEAD_DIM
            gsum = jnp.asarray((gidx[:, None] == np.arange(128)[None]).astype(np.float32)).astype(BF16)
            qn = jnp.tile(da_q_norm[j], 2 * DA_HEADS)[None]
            kn = jnp.tile(da_k_norm[j], 2 * DA_HEADS)[None]
            cos, sin = _rope_tables(xs.shape[0])
            n_ctx = cs.shape[0]
            nk = n_ctx + xs.shape[0]
            qx, k_all, v_all = _qkv(xs, mix_x, w_qkv, gsum, qn, kn, cos, sin, kv_rows=nk, kv_offset=n_ctx)
            qc, k_all, v_all = _qkv(cs, mix_c, w_qkv, gsum, qn, kn, kv_rows=nk, kv_into=(k_all, v_all))
            lam_rows = jnp.concatenate([da_lam_q1[j][None], da_lam_k1[j][None], da_lam_q2[j][None],
                                        da_lam_k2[j][None], jnp.zeros((4, DA_HEAD_DIM), F32)], axis=0)
            w_out = da_w_out[j].astype(BF16)
            zero_b = jnp.zeros((1, d), F32)
            ox = _attention(qx, k_all, v_all, lam_rows, da_subln[j][None], lam_init)
            new_x = _proj_res(ox, w_out, zero_b, mod_slice(i, 0, 2), xs)
            if not last:
                oc = _attention(qc, k_all[:n_ctx], v_all[:n_ctx], lam_rows, da_subln[j][None], lam_init)
                cs = _proj_res(oc, w_out, zero_b, mod_slice(i, 1, 2), cs)
            xs = new_x
        experts = (moe_w_gate, moe_w_up, moe_w_down, i)
        if not last:
            cs = _moe_block(cs, _mod_rows(norm_ffn[i], mods[i], 1, 3), mod_slice(i, 1, 5), moe_router[i], *experts)
        xs = _moe_block(xs, _mod_rows(norm_ffn[i], mods[i], 0, 3), mod_slice(i, 0, 5), moe_router[i], *experts)
    return xs[None]
```

```python
import functools
import math

import jax
import jax.numpy as jnp
import numpy as np
from jax import lax
from jax.experimental import pallas as pl
from jax.experimental.pallas import tpu as pltpu

F32 = jnp.float32
BF16 = jnp.bfloat16
I32 = jnp.int32
HIGHEST = lax.Precision.HIGHEST

D_MODEL = 1024
N_MOD = 6
NORM_EPS = 1e-6
GRID_W = 64
HY_ORDER = 2
HY_SHORT = 3
HY_EMB_BANDS = 16
HY_EMB_DIM = 1 + 2 * HY_EMB_BANDS
HY_FILTER_HIDDEN = 64
HY_DECAY_FAST = 0.3
HY_DECAY_SLOW = 1.5
HY_DECAY_TARGET = 1e-2
HY_FILTER_EPS = 1e-6
Z_SIGN_COL = 33
DA_HEADS = 8
DA_HEAD_DIM = 64
DA_V_DIM = 128
ROPE_AXIS_DIM = 32
ROPE_THETA = 10000.0
SUBLN_EPS = 1e-5
N_EXPERTS = 16
EC_CAPACITY = 2
D_EXPERT = 1024
TOK_BLK = 256
ROW_ALIGN = 16
WIN_SHIFT = 7
WIN = 1 << WIN_SHIFT

VMEM_LIMIT = 56 * 1024 * 1024


def _cp(*sem):
    return pltpu.CompilerParams(dimension_semantics=sem, vmem_limit_bytes=VMEM_LIMIT)


def _const_spec(shape):
    nd = len(shape)
    return pl.BlockSpec(shape, lambda *_: (0,) * nd)


def _dot(a, b):
    return jnp.dot(a, b, preferred_element_type=F32)


def _dot_nt(a, b):
    return lax.dot_general(a, b, (((1,), (1,)), ((), ())), preferred_element_type=F32)


def _norm_mod(x, mod, eps=NORM_EPS):
    ms = jnp.mean(x * x, axis=-1, keepdims=True)
    return x * lax.rsqrt(ms + eps) * (mod[0:1] * (1.0 + mod[1:2])) + mod[2:3]


def _lane_tile(x, reps):
    return jnp.concatenate([x] * reps, axis=1) if reps > 1 else x


def _adaln_kernel(c_ref, w_ref, b_ref, o_ref):
    c = c_ref[...]
    s = c / (1.0 + jnp.exp(-c))
    o_ref[0] = jnp.dot(s, w_ref[0], precision=HIGHEST, preferred_element_type=F32) + b_ref[0]


def _adaln(cond8, ada_w, ada_b):
    depth, d, nout = ada_w.shape
    tn = 1536
    return pl.pallas_call(
        _adaln_kernel,
        grid=(depth, nout // tn),
        in_specs=[_const_spec((8, d)),
                  pl.BlockSpec((1, d, tn), lambda l, j: (l, 0, j)),
                  pl.BlockSpec((1, 1, tn), lambda l, j: (l, 0, j))],
        out_specs=pl.BlockSpec((1, 8, tn), lambda l, j: (l, 0, j)),
        out_shape=jax.ShapeDtypeStruct((depth, 8, nout), F32),
        compiler_params=_cp("parallel", "parallel"),
        name="adaln",
    )(cond8, ada_w, ada_b.reshape(depth, 1, nout))


def _mod_rows(norm_g, mods, row, k0):
    d = D_MODEL
    shift = mods[row, k0 * d:(k0 + 1) * d]
    scale = mods[row, (k0 + 1) * d:(k0 + 2) * d]
    z = jnp.zeros((5, d), F32)
    return jnp.concatenate([norm_g[None], scale[None], shift[None], z], axis=0)


HALO = 16


def _hy_in_kernel(x_ref, xp_ref, xn_ref, mod_ref, w_ref, b_ref, cw_ref, v_ref, x1_ref, x2_ref, *, tm, n_rows):
    i = pl.program_id(0)
    mod = mod_ref[...]
    hm = _norm_mod(x_ref[...], mod).astype(BF16)
    hp = _norm_mod(xp_ref[...], mod).astype(BF16)
    hn = _norm_mod(xn_ref[...], mod).astype(BF16)
    hcat = jnp.concatenate([hp, hm, hn], axis=0)
    row = lax.broadcasted_iota(I32, (tm + 2 * HALO, 1), 0) + (i * tm - HALO)
    valid = jnp.logical_and(row >= 0, row < n_rows)
    d = D_MODEL
    for c, o_ref in enumerate((v_ref, x1_ref, x2_ref)):
        u = _dot(hcat, w_ref[:, c * d:(c + 1) * d]) + b_ref[:, c * d:(c + 1) * d]
        u = jnp.where(valid, u, 0.0)
        cw = cw_ref[:, c * d:(c + 1) * d]
        y = (cw[3:4] + cw[0:1] * u[HALO - 1:HALO - 1 + tm] + cw[1:2] * u[HALO:HALO + tm]
             + cw[2:3] * u[HALO + 1:HALO + 1 + tm])
        o_ref[...] = y.astype(BF16)


def _hy_in(x, mod, w_bf, b_in, conv_w, conv_b):
    n, d = x.shape
    tm = min(512, n)
    nh = n // HALO
    cw = jnp.concatenate([conv_w, conv_b[None], jnp.zeros((4, 3 * d), F32)], axis=0)
    out = jax.ShapeDtypeStruct((n, d), BF16)
    row_spec = pl.BlockSpec((tm, d), lambda i: (i, 0))
    return pl.pallas_call(
        functools.partial(_hy_in_kernel, tm=tm, n_rows=n),
        grid=(n // tm,),
        in_specs=[row_spec,
                  pl.BlockSpec((HALO, d), lambda i: (jnp.maximum(i * (tm // HALO) - 1, 0), 0)),
                  pl.BlockSpec((HALO, d), lambda i: (jnp.minimum((i + 1) * (tm // HALO), nh - 1), 0)),
                  _const_spec((8, d)), _const_spec((d, 3 * d)), _const_spec((1, 3 * d)),
                  _const_spec((8, 3 * d))],
        out_specs=[row_spec, row_spec, row_spec],
        out_shape=[out, out, out],
        compiler_params=_cp("parallel"),
        name="hyena_in",
    )(x, x, x, mod, w_bf, b_in.reshape(1, 3 * d), cw)


Z_HALF = 64


def _filter_s1_kernel(z_ref, w1_ref, b1_ref, w2_ref, b2_ref, w3_ref, b3_ref, fr_ref, w4f_ref, w4b_ref, dl_ref,
                      f1_ref, tre_ref, tim_ref, o_ref, asum_ref, *, n1):
    j0 = pl.program_id(0) * N2_GRP
    fr = fr_ref[...]
    dl = dl_ref[...]
    r = n1 // 2

    def lin(a, w_ref, b_ref):
        return jnp.dot(a, w_ref[...], precision=HIGHEST, preferred_element_type=F32) + b_ref[...]

    @pl.when(j0 == 0)
    def _():
        asum_ref[...] = jnp.zeros_like(asum_ref)

    for slot in range(N2_GRP):
        z = z_ref[slot * r:(slot + 1) * r, :]
        hid = jnp.sin(fr * lin(z, w1_ref, b1_ref))
        hid = jnp.sin(fr * lin(hid, w2_ref, b2_ref))
        hid = jnp.sin(fr * lin(hid, w3_ref, b3_ref)).astype(BF16)

        def taps(w4_ref, col):
            t = z[:, col:col + 1]
            sgn = z[:, col + Z_SIGN_COL:col + Z_SIGN_COL + 1]
            return _dot(hid, w4_ref[...]) * jnp.exp(-t * dl) * sgn

        k = jnp.concatenate([taps(w4f_ref, 0), taps(w4b_ref, Z_HALF)], axis=0)
        asum_ref[0:1, :] += jnp.sum(jnp.abs(k), axis=0, keepdims=True)
        _s1_store(_dot(f1_ref[...], k.astype(BF16)), tre_ref, tim_ref, j0 + slot, o_ref, n1, slot)


def _filter_positions(seq, n1, n2):
    i = np.arange(n1 // 2)[None, :]
    j = np.arange(n2)[:, None]
    bands = np.linspace(1e-4, HY_EMB_BANDS - 1, HY_EMB_BANDS)
    z = np.zeros((n2, n1 // 2, 2 * Z_HALF), np.float64)
    for col, r in ((0, i * n2 + j), (Z_HALF, i * n2 + j + seq)):
        pos = np.minimum(np.where(r < seq, r, 2 * seq - r), seq - 1).astype(np.float64)
        w = 2.0 * np.pi * pos / seq
        z[:, :, col] = pos / (seq - 1)
        z[:, :, col + 1:col + 1 + HY_EMB_BANDS] = np.cos(w[..., None] * bands)
        z[:, :, col + 1 + HY_EMB_BANDS:col + HY_EMB_DIM] = -np.sin(w[..., None] * bands)
        z[:, :, col + Z_SIGN_COL] = np.where(r < seq, 1.0, np.where(r == seq, 0.0, -1.0))
    return jnp.asarray(z.reshape(n2 * (n1 // 2), 2 * Z_HALF).astype(np.float32))


def _filter_stage1(seq, f_w1, f_b1, f_w2, f_b2, f_w3, f_b3, f_w4, f_freq):
    d, hid = D_MODEL, HY_FILTER_HIDDEN
    od = HY_ORDER * d
    n1, n2 = _fft_factors(seq)
    cst = _fft_consts(seq)
    z = _filter_positions(seq, n1, n2)
    zw = 2 * Z_HALF
    zero = jnp.zeros((hid, hid), F32)
    pad = jnp.zeros((Z_HALF - HY_EMB_DIM, hid), F32)
    w1 = jnp.concatenate([jnp.concatenate([f_w1, pad], axis=0), jnp.zeros((Z_HALF, hid), F32)], axis=0)
    w1 = jnp.concatenate([w1, jnp.roll(w1, Z_HALF, axis=0)], axis=1)
    blockdiag = lambda w: jnp.concatenate([jnp.concatenate([w, zero], axis=1),
                                           jnp.concatenate([zero, w], axis=1)], axis=0)
    twice = lambda v: jnp.tile(v, 2).reshape(1, 2 * hid)
    w4d = f_w4.reshape(hid, HY_ORDER, 2, d).transpose(2, 0, 1, 3).reshape(2, hid, od)
    zrows = jnp.zeros((hid, od), F32)
    w4f = jnp.concatenate([w4d[0], zrows], axis=0).astype(BF16)
    w4b = jnp.concatenate([zrows, w4d[1]], axis=0).astype(BF16)
    max_decay = math.log(HY_DECAY_TARGET) / HY_DECAY_FAST
    min_decay = math.log(HY_DECAY_TARGET) / HY_DECAY_SLOW
    deltas = np.abs(np.linspace(min_decay, max_decay, d, dtype=np.float32))
    dl = jnp.asarray(np.tile(deltas, HY_ORDER)[None])
    r = n1 // 2
    return pl.pallas_call(
        functools.partial(_filter_s1_kernel, n1=n1),
        grid=(n2 // N2_GRP,),
        in_specs=[pl.BlockSpec((N2_GRP * r, zw), lambda j: (j, 0)),
                  _const_spec((zw, 2 * hid)), _const_spec((1, 2 * hid)),
                  _const_spec((2 * hid, 2 * hid)), _const_spec((1, 2 * hid)),
                  _const_spec((2 * hid, 2 * hid)), _const_spec((1, 2 * hid)),
                  _const_spec((1, 2 * hid)),
                  _const_spec((2 * hid, od)), _const_spec((2 * hid, od)), _const_spec((1, od)),
                  _const_spec((2 * n1, n1)), _const_spec((n1, n2)), _const_spec((n1, n2))],
        out_specs=[pl.BlockSpec((2, n1 // K1_GRP, od // LANES, N2_GRP * PAIRS, LANES), lambda j: (0, 0, 0, j, 0)),
                   _const_spec((8, od))],
        out_shape=[jax.ShapeDtypeStruct((2, n1 // K1_GRP, od // LANES, n2 * PAIRS, LANES), jnp.uint32),
                   jax.ShapeDtypeStruct((8, od), F32)],
        compiler_params=_cp("arbitrary"),
        name="hyena_filter",
    )(z, w1, twice(f_b1), blockdiag(f_w2), twice(f_b2), blockdiag(f_w3), twice(f_b3), twice(f_freq),
      w4f, w4b, dl, cst["f1_full"], cst["tre"], cst["tim"])


def _fft_factors(seq):
    n = 2 * seq
    n1 = 256 if n >= 32768 else 32
    n2 = n // n1
    assert n1 * n2 == n and n2 % 16 == 0 and n1 % 32 == 0
    return n1, n2


@functools.lru_cache(maxsize=None)
def _fft_consts(seq):
    n = 2 * seq
    n1, n2 = _fft_factors(seq)
    k1 = np.arange(n1, dtype=np.float64)[:, None] + 0.5
    th1 = 2.0 * np.pi * k1 * np.arange(n1, dtype=np.float64)[None] / n1
    perm = np.concatenate([np.arange(0, n1, 2), np.arange(1, n1, 2)])
    f1_full = np.concatenate([np.cos(th1)[perm], -np.sin(th1)[perm]], axis=0)
    f1_half = f1_full[:, :n1 // 2]
    tw = 2.0 * np.pi * k1 * np.arange(n2, dtype=np.float64)[None] / n
    tre, tim = np.cos(tw), -np.sin(tw)
    h2 = n2 // 2
    th2 = 2.0 * np.pi * np.arange(h2, dtype=np.float64)[:, None] * np.arange(n2, dtype=np.float64)[None] / n2
    c2, s2 = np.cos(th2), np.sin(th2)
    m2f = np.block([[c2, s2], [-s2, c2]])
    m2i = np.block([[c2.T, -s2.T], [s2.T, c2.T]])
    thb = th1[:, :n1 // 2].T
    gi = (2.0 / n) * np.concatenate([np.cos(thb)[:, perm], -np.sin(thb)[:, perm]], axis=1)
    bf = lambda a: jnp.asarray(a.astype(np.float32)).astype(BF16)
    f32 = lambda a: jnp.asarray(a.astype(np.float32))
    grp = lambda a: a.T.reshape(n2, n1 // K1_GRP, K1_GRP).transpose(1, 0, 2)
    return dict(f1_full=bf(f1_full), f1_half=bf(f1_half), tre=f32(tre[perm]), tim=f32(tim[perm]),
                tre_grp=f32(grp(tre)), tim_grp=f32(grp(tim)), m2f=bf(m2f), m2i=bf(m2i), gi=bf(gi))


def _pick_col(tbl, idx):
    lane = lax.broadcasted_iota(I32, tbl.shape, 1)
    return jnp.sum(jnp.where(lane == idx, tbl, 0.0), axis=1, keepdims=True)


K1_GRP = 16
PAIRS = K1_GRP // 2
LANES = 128
N2_GRP = 4
MID_TC = 512


def _unpack_pair(w, half):
    bits = lax.shift_left(w, jnp.uint32(16)) if half == 0 else (w & jnp.uint32(0xFFFF0000))
    return pltpu.bitcast(bits, F32).astype(BF16)


def _pack_pair(even, odd):
    ue = pltpu.bitcast(even.astype(BF16).astype(F32), jnp.uint32)
    uo = pltpu.bitcast(odd.astype(BF16).astype(F32), jnp.uint32)
    return lax.shift_right_logical(ue, jnp.uint32(16)) | uo


def _s1_store(a, tre_ref, tim_ref, j, o_ref, n1, slot):
    are, aim = a[:n1], a[n1:]
    tre = _pick_col(tre_ref[...], j)
    tim = _pick_col(tim_ref[...], j)
    re = are * tre - aim * tim
    im = are * tim + aim * tre
    h = n1 // 2
    shape = (n1 // K1_GRP, PAIRS, a.shape[1])
    for part, val in enumerate((re, im)):
        words = _pack_pair(val[:h], val[h:]).reshape(shape)
        for ct in range(a.shape[1] // LANES):
            o_ref[part, :, ct, slot * PAIRS:(slot + 1) * PAIRS, :] = words[:, :, ct * LANES:(ct + 1) * LANES]


def _fft_s1_kernel(f1_ref, x_ref, tre_ref, tim_ref, o_ref, *, n1, tc):
    j0 = pl.program_id(0) * N2_GRP
    for slot in range(N2_GRP):
        a = _dot(f1_ref[...], x_ref[:, slot * tc:(slot + 1) * tc])
        _s1_store(a, tre_ref, tim_ref, j0 + slot, o_ref, n1, slot)


def _fft_stage1(x2d, f1, tre, tim, n1, n2, chans):
    r = x2d.shape[0]
    return pl.pallas_call(
        functools.partial(_fft_s1_kernel, n1=n1, tc=chans),
        grid=(n2 // N2_GRP,),
        in_specs=[_const_spec((2 * n1, r)),
                  pl.BlockSpec((r, N2_GRP * chans), lambda j: (0, j)),
                  _const_spec((n1, n2)), _const_spec((n1, n2))],
        out_specs=pl.BlockSpec((2, n1 // K1_GRP, chans // LANES, N2_GRP * PAIRS, LANES),
                               lambda j: (0, 0, 0, j, 0)),
        out_shape=jax.ShapeDtypeStruct((2, n1 // K1_GRP, chans // LANES, n2 * PAIRS, LANES), jnp.uint32),
        compiler_params=_cp("parallel"),
        name="fft_stage1",
    )(f1, x2d, tre, tim)


def _load_pair(b_ref, part, s, n2):
    return jnp.concatenate([b_ref[part, 0, ct, pl.ds(s, n2, stride=PAIRS), :] for ct in range(b_ref.shape[2])],
                           axis=1)


def _store_pair(o_ref, part, s, n2, words):
    for ct in range(o_ref.shape[2]):
        o_ref[part, 0, ct, pl.ds(s, n2, stride=PAIRS), :] = words[:, ct * LANES:(ct + 1) * LANES]


def _fft_s2_kernel(m2f_ref, b_ref, o_ref):
    n2 = m2f_ref.shape[0]
    for s in range(PAIRS):
        wre = _load_pair(b_ref, 0, s, n2)
        wim = _load_pair(b_ref, 1, s, n2)
        for half in range(2):
            b = jnp.concatenate([_unpack_pair(wre, half), _unpack_pair(wim, half)], axis=0)
            o_ref[2 * s + half] = _dot(m2f_ref[...], b).astype(BF16)


def _fft_stage2(b5, m2f, n1, n2, chans):
    return pl.pallas_call(
        _fft_s2_kernel,
        grid=(n1 // K1_GRP, chans // MID_TC),
        in_specs=[_const_spec((n2, 2 * n2)),
                  pl.BlockSpec((2, 1, MID_TC // LANES, n2 * PAIRS, LANES), lambda g, c: (0, g, c, 0, 0))],
        out_specs=pl.BlockSpec((K1_GRP, n2, MID_TC), lambda g, c: (g, 0, c)),
        out_shape=jax.ShapeDtypeStruct((n1, n2, chans), BF16),
        compiler_params=_cp("parallel", "parallel"),
        name="fft_stage2",
    )(m2f, b5)


def _fft_mid_kernel(m2f_ref, m2i_ref, b_ref, k_ref, inv_ref, tre_ref, tim_ref, o_ref, *, n2):
    h2 = n2 // 2
    inv = inv_ref[...]
    tre_g = tre_ref[0]
    tim_g = tim_ref[0]
    for s in range(PAIRS):
        wre = _load_pair(b_ref, 0, s, n2)
        wim = _load_pair(b_ref, 1, s, n2)
        res = []
        for half in range(2):
            kl = 2 * s + half
            b = jnp.concatenate([_unpack_pair(wre, half), _unpack_pair(wim, half)], axis=0)
            x = _dot(m2f_ref[...], b)
            kk = k_ref[kl].astype(F32) * inv
            xre, xim = x[:h2], x[h2:]
            kre, kim = kk[:h2], kk[h2:]
            y = jnp.concatenate([xre * kre - xim * kim, xre * kim + xim * kre], axis=0).astype(BF16)
            c = _dot(m2i_ref[...], y)
            cre, cim = c[:n2], c[n2:]
            tre = tre_g[:, kl:kl + 1]
            tim = tim_g[:, kl:kl + 1]
            res.append((cre * tre + cim * tim, cim * tre - cre * tim))
        _store_pair(o_ref, 0, s, n2, _pack_pair(res[0][0], res[1][0]))
        _store_pair(o_ref, 1, s, n2, _pack_pair(res[0][1], res[1][1]))


def _fft_mid(b5, kspec, inv_norm, order, cst, n1, n2):
    d = D_MODEL
    nc = d // MID_TC
    blk = pl.BlockSpec((2, 1, MID_TC // LANES, n2 * PAIRS, LANES), lambda g, c: (0, g, c, 0, 0))
    tw = pl.BlockSpec((1, n2, K1_GRP), lambda g, c: (g, 0, 0))
    return pl.pallas_call(
        functools.partial(_fft_mid_kernel, n2=n2),
        grid=(n1 // K1_GRP, nc),
        in_specs=[_const_spec((n2, 2 * n2)), _const_spec((2 * n2, n2)), blk,
                  pl.BlockSpec((K1_GRP, n2, MID_TC), lambda g, c: (g, 0, order * nc + c)),
                  pl.BlockSpec((1, MID_TC), lambda g, c: (0, order * nc + c)),
                  tw, tw],
        out_specs=blk,
        out_shape=jax.ShapeDtypeStruct((2, n1 // K1_GRP, d // LANES, n2 * PAIRS, LANES), jnp.uint32),
        compiler_params=_cp("parallel", "parallel"),
        name="fft_mid",
    )(cst["m2f"], cst["m2i"], b5, kspec, inv_norm, cst["tre_grp"], cst["tim_grp"])


def _fft_last_kernel(gi_ref, c_ref, gate_ref, z_ref, skip_ref, o_ref):
    h = c_ref.shape[1] * PAIRS
    d = c_ref.shape[2] * LANES
    for slot in range(N2_GRP):
        rows = slice(slot * PAIRS, (slot + 1) * PAIRS)
        cols = slice(slot * d, (slot + 1) * d)

        def words(part):
            return jnp.concatenate([c_ref[part, :, ct, rows, :].reshape(h, LANES)
                                    for ct in range(c_ref.shape[2])], axis=1)

        wre = words(0)
        wim = words(1)
        c = jnp.concatenate([_unpack_pair(wre, 0), _unpack_pair(wre, 1), _unpack_pair(wim, 0),
                             _unpack_pair(wim, 1)], axis=0)
        y = _dot(gi_ref[...], c)
        z = z_ref[:, cols].astype(F32)
        o_ref[:, cols] = (gate_ref[:, cols].astype(F32) * (y + skip_ref[...] * z)).astype(BF16)


def _fft_last(c5, gate2d, z2d, skip_row, gi, n1, n2):
    d = D_MODEL
    r = n1 // 2
    col = pl.BlockSpec((r, N2_GRP * d), lambda j: (0, j))
    return pl.pallas_call(
        _fft_last_kernel,
        grid=(n2 // N2_GRP,),
        in_specs=[_const_spec((r, 2 * n1)),
                  pl.BlockSpec((2, n1 // K1_GRP, d // LANES, N2_GRP * PAIRS, LANES), lambda j: (0, 0, 0, j, 0)),
                  col, col, _const_spec((1, d))],
        out_specs=col,
        out_shape=jax.ShapeDtypeStruct((r, n2 * d), BF16),
        compiler_params=_cp("parallel"),
        name="fft_last",
    )(gi, c5, gate2d, z2d, skip_row)


def _long_conv_gate(z_in, gate, skip_row, kspec, inv_norm, order, seq):
    d = D_MODEL
    n1, n2 = _fft_factors(seq)
    cst = _fft_consts(seq)
    z2d = z_in.reshape(n1 // 2, n2 * d)
    b = _fft_stage1(z2d, cst["f1_half"], cst["tre"], cst["tim"], n1, n2, d)
    c = _fft_mid(b, kspec, inv_norm, order, cst, n1, n2)
    out = _fft_last(c, gate.reshape(n1 // 2, n2 * d), z2d, skip_row, cst["gi"], n1, n2)
    return out.reshape(seq, d)


def _filter_spectrum(seq, fparams):
    d = D_MODEL
    od = HY_ORDER * d
    n1, n2 = _fft_factors(seq)
    cst = _fft_consts(seq)
    b, asum = _filter_stage1(seq, *fparams)
    kspec = _fft_stage2(b, cst["m2f"], n1, n2, od)
    return kspec, asum


def _proj_res_kernel(a_ref, w_ref, b_ref, g_ref, x_ref, o_ref):
    y = _dot(a_ref[...], w_ref[...]) + b_ref[...]
    o_ref[...] = x_ref[...] + g_ref[...] * y


def _proj_res(a, w_bf, b_row, gate_row, xres):
    n, d = xres.shape
    tm = min(512, n)
    row = pl.BlockSpec((tm, d), lambda i: (i, 0))
    return pl.pallas_call(
        _proj_res_kernel,
        grid=(n // tm,),
        in_specs=[row, _const_spec((d, d)), _const_spec((1, d)), _const_spec((1, d)), row],
        out_specs=row,
        out_shape=jax.ShapeDtypeStruct((n, d), F32),
        compiler_params=_cp("parallel"),
        name="proj_residual",
    )(a, w_bf, b_row, gate_row, xres)


def _ffn_in_kernel(x_ref, mod_ref, wt_ref, h_ref, aff_ref):
    h = _norm_mod(x_ref[...], mod_ref[...])
    hi = h.astype(BF16)
    lo = (h - hi.astype(F32)).astype(BF16)
    wt = wt_ref[...]
    whi = wt.astype(BF16)
    wlo = (wt - whi.astype(F32)).astype(BF16)
    logits = _dot_nt(whi, hi) + (_dot_nt(whi, lo) + _dot_nt(wlo, hi))
    m = jnp.max(logits, axis=0, keepdims=True)
    p = jnp.exp(logits - m)
    aff_ref[...] = p / jnp.sum(p, axis=0, keepdims=True)
    h_ref[...] = hi


def _ffn_in(x, mod, w_router):
    n, d = x.shape
    e = N_EXPERTS
    tm = min(512, n)
    return pl.pallas_call(
        _ffn_in_kernel,
        grid=(n // tm,),
        in_specs=[pl.BlockSpec((tm, d), lambda i: (i, 0)), _const_spec((8, d)), _const_spec((e, d))],
        out_specs=[pl.BlockSpec((tm, d), lambda i: (i, 0)), pl.BlockSpec((e, tm), lambda i: (0, i))],
        out_shape=[jax.ShapeDtypeStruct((n, d), BF16), jax.ShapeDtypeStruct((e, n), F32)],
        compiler_params=_cp("parallel"),
        name="moe_router",
    )(x, mod, w_router.T)


def _select_kernel(a_ref, pos_ref, s0_ref, sel_ref, *, cap, nblk):
    e = N_EXPERTS
    bits = pltpu.bitcast(a_ref[...], I32)

    def bisect(i, thr):
        cand = thr | jnp.left_shift(jnp.int32(1), 30 - i)
        cnt = jnp.sum(jnp.where(bits >= cand, 1.0, 0.0), axis=1, keepdims=True)
        return jnp.where(cnt >= cap, cand, thr)

    thr = lax.fori_loop(0, 31, bisect, jnp.zeros((e, 1), I32))
    n_gt = jnp.sum(jnp.where(bits > thr, 1.0, 0.0), axis=1, keepdims=True)
    need = cap - n_gt
    r = lax.broadcasted_iota(I32, (TOK_BLK, TOK_BLK), 0)
    c = lax.broadcasted_iota(I32, (TOK_BLK, TOK_BLK), 1)
    upper = jnp.where(r < c, 1.0, 0.0).astype(BF16)

    def pass1(j, carry):
        sl = pl.ds(pl.multiple_of(j * TOK_BLK, TOK_BLK), TOK_BLK)
        bj = pltpu.bitcast(a_ref[:, sl], I32)
        eq = jnp.where(bj == thr, 1.0, 0.0)
        rank = _dot(eq.astype(BF16), upper) + carry
        keep = jnp.logical_or(bj > thr, jnp.logical_and(bj == thr, rank < need))
        sel_ref[:, sl] = jnp.where(keep, 1.0, 0.0)
        return carry + jnp.sum(eq, axis=1, keepdims=True)

    lax.fori_loop(0, nblk, pass1, jnp.zeros((e, 1), F32))

    def pass2(j, carry):
        sl = pl.ds(pl.multiple_of(j * TOK_BLK, TOK_BLK), TOK_BLK)
        s = sel_ref[:, sl]
        slot = _dot(s.astype(BF16), upper) + carry
        pos_ref[:, sl] = jnp.where(s > 0.5, slot, -1.0).astype(I32)
        s0_ref[j] = jnp.broadcast_to(carry, (e, 128)).astype(I32)
        return carry + jnp.sum(s, axis=1, keepdims=True)

    total = lax.fori_loop(0, nblk, pass2, jnp.zeros((e, 1), F32))
    s0_ref[nblk] = jnp.broadcast_to(total, (e, 128)).astype(I32)


def _select(aff_t, cap):
    e, n = aff_t.shape
    nblk = n // TOK_BLK
    return pl.pallas_call(
        functools.partial(_select_kernel, cap=cap, nblk=nblk),
        out_shape=[jax.ShapeDtypeStruct((e, n), I32), jax.ShapeDtypeStruct((nblk + 1, e, 128), I32)],
        scratch_shapes=[pltpu.VMEM((e, n), F32)],
        compiler_params=pltpu.CompilerParams(vmem_limit_bytes=VMEM_LIMIT),
        name="moe_select",
    )(aff_t)


def _block_windows(s0_ref, blk, e_idx):
    s0 = s0_ref[blk * N_EXPERTS + e_idx]
    s1 = s0_ref[(blk + 1) * N_EXPERTS + e_idx]
    start = lax.shift_left(lax.shift_right_logical(s0, 4), 4)
    nwin = jnp.where(s1 > s0, lax.shift_right_logical(s1 - start + (WIN - 1), WIN_SHIFT), 0)
    return start, nwin


def _token_block(i):
    start = i * TOK_BLK
    return pl.ds(start if isinstance(i, int) else pl.multiple_of(start, TOK_BLK), TOK_BLK)


def _gather_kernel(s0_ref, pos_ref, h_ref, xe_ref, *, sub, nchunk):
    e_idx = pl.program_id(0)
    xe_ref[...] = jnp.zeros_like(xe_ref)
    rows = lax.broadcasted_iota(I32, (WIN, TOK_BLK), 0)

    def window(blk, base):
        tok = _token_block(blk)
        base = pl.multiple_of(base, ROW_ALIGN)
        onehot = jnp.where(rows == pos_ref[0, :, tok] - base, 1.0, 0.0).astype(BF16)
        got = _dot(onehot, h_ref[tok, :]).astype(BF16)
        win = pl.ds(base, WIN)
        xe_ref[0, win, :] = xe_ref[0, win, :] + got

    def chunk(c, carry):
        plan = [_block_windows(s0_ref, c * sub + i, e_idx) for i in range(sub)]
        single = functools.reduce(jnp.logical_and, [nwin <= 1 for _, nwin in plan])

        @pl.when(single)
        def _():
            for i, (start, _) in enumerate(plan):
                window(c * sub + i, start)

        @pl.when(jnp.logical_not(single))
        def _():
            def body(i, carry2):
                start, nwin = _block_windows(s0_ref, c * sub + i, e_idx)
                return lax.fori_loop(0, nwin, lambda w, cc: (window(c * sub + i, start + w * WIN), cc)[1], carry2)

            lax.fori_loop(0, sub, body, 0)

        return carry

    lax.fori_loop(0, nchunk, chunk, 0)


def _gather(s0_flat, pos, h, cap_pad):
    e, n = pos.shape
    d = h.shape[1]
    sub = min(8, n // TOK_BLK)
    return pl.pallas_call(
        functools.partial(_gather_kernel, sub=sub, nchunk=n // (sub * TOK_BLK)),
        grid_spec=pltpu.PrefetchScalarGridSpec(
            num_scalar_prefetch=1,
            grid=(e,),
            in_specs=[pl.BlockSpec((1, 1, n), lambda ei, s0: (ei, 0, 0)),
                      pl.BlockSpec((n, d), lambda ei, s0: (0, 0), pipeline_mode=pl.Buffered(1))],
            out_specs=pl.BlockSpec((1, cap_pad, d), lambda ei, s0: (ei, 0, 0)),
        ),
        out_shape=jax.ShapeDtypeStruct((e, cap_pad, d), BF16),
        compiler_params=_cp("parallel"),
        name="moe_gather",
    )(s0_flat, pos.reshape(e, 1, n), h)


def _expert_kernel(x_ref, wg_ref, wu_ref, wd_ref, y_ref, wg_scr, wu_scr, wd_scr, *, n_real):
    j = pl.program_id(1)

    @pl.when(j == 0)
    def _():
        wg_scr[...] = wg_ref[0].astype(BF16)
        wu_scr[...] = wu_ref[0].astype(BF16)
        wd_scr[...] = wd_ref[0].astype(BF16)

    @pl.when(j < n_real)
    def _():
        x = x_ref[0]
        g = _dot(x, wg_scr[...])
        u = _dot(x, wu_scr[...])
        a = (g / (1.0 + jnp.exp(-g))) * u
        y_ref[0] = _dot(a.astype(BF16), wd_scr[...]).astype(BF16)

    @pl.when(j >= n_real)
    def _():
        y_ref[0] = jnp.zeros_like(y_ref[0])


def _experts(xe, wg, wu, wd, layer, cap):
    e, cap_pad, d = xe.shape
    f = wg.shape[3]
    tm = cap_pad // 3 if cap_pad % (3 * ROW_ALIGN) == 0 and cap >= 3 * TOK_BLK else cap_pad
    tile = pl.BlockSpec((1, tm, d), lambda ei, j: (ei, j, 0))
    return pl.pallas_call(
        functools.partial(_expert_kernel, n_real=pl.cdiv(cap, tm)),
        grid=(e, cap_pad // tm),
        in_specs=[tile,
                  pl.BlockSpec((None, 1, d, f), lambda ei, j: (layer, ei, 0, 0)),
                  pl.BlockSpec((None, 1, d, f), lambda ei, j: (layer, ei, 0, 0)),
                  pl.BlockSpec((None, 1, f, d), lambda ei, j: (layer, ei, 0, 0))],
        out_specs=tile,
        out_shape=jax.ShapeDtypeStruct((e, cap_pad, d), BF16),
        scratch_shapes=[pltpu.VMEM((d, f), BF16), pltpu.VMEM((d, f), BF16), pltpu.VMEM((f, d), BF16)],
        compiler_params=_cp("parallel", "arbitrary"),
        name="moe_experts",
    )(xe, wg, wu, wd)


COMBINE_COLS = 512
COMBINE_BLKS = 2


def _combine_kernel(s0_ref, pos_ref, aff_ref, y_ref, x_ref, gate_ref, o_ref, acc_ref, *, nsub):
    t = pl.program_id(1)
    rows = lax.broadcasted_iota(I32, (WIN, TOK_BLK), 0)

    def selector(i, e, base):
        tok = _token_block(i)
        rel = pos_ref[e:e + 1, tok] - base
        return jnp.where(rows == rel, aff_ref[e:e + 1, tok], 0.0).astype(BF16)

    def contract(sel, ywin):
        return lax.dot_general(sel, ywin, (((0,), (0,)), ((), ())), preferred_element_type=F32)

    for i in range(nsub):
        plan = [_block_windows(s0_ref, t * nsub + i, e) for e in range(N_EXPERTS)]
        single = functools.reduce(jnp.logical_and, [nwin <= 1 for _, nwin in plan])
        tok = _token_block(i)

        @pl.when(single)
        def _():
            bases = [pl.multiple_of(start, ROW_ALIGN) for start, _ in plan]
            sel = jnp.concatenate([selector(i, e, b) for e, b in enumerate(bases)], axis=0)
            ywin = jnp.concatenate([y_ref[e, pl.ds(b, WIN), :] for e, b in enumerate(bases)], axis=0)
            o_ref[tok, :] = x_ref[tok, :] + gate_ref[...] * contract(sel, ywin)

        @pl.when(jnp.logical_not(single))
        def _():
            acc_ref[...] = jnp.zeros_like(acc_ref)
            for e, (start, nwin) in enumerate(plan):
                def window(w, carry):
                    base = pl.multiple_of(start + w * WIN, ROW_ALIGN)
                    acc_ref[...] += contract(selector(i, e, base), y_ref[e, pl.ds(base, WIN), :])
                    return carry

                lax.fori_loop(0, nwin, window, 0)
            o_ref[tok, :] = x_ref[tok, :] + gate_ref[...] * acc_ref[...]


def _combine(s0_flat, pos, aff_t, y, xres, gate_row):
    n, d = xres.shape
    e, cap_pad, _ = y.shape
    nsub = min(COMBINE_BLKS, n // TOK_BLK)
    rows = nsub * TOK_BLK
    cq = COMBINE_COLS
    tile = pl.BlockSpec((rows, cq), lambda q, t, s0: (t, q))
    return pl.pallas_call(
        functools.partial(_combine_kernel, nsub=nsub),
        grid_spec=pltpu.PrefetchScalarGridSpec(
            num_scalar_prefetch=1,
            grid=(d // cq, n // rows),
            in_specs=[pl.BlockSpec((e, rows), lambda q, t, s0: (0, t)),
                      pl.BlockSpec((e, rows), lambda q, t, s0: (0, t)),
                      pl.BlockSpec((e, cap_pad, cq), lambda q, t, s0: (0, 0, q), pipeline_mode=pl.Buffered(1)),
                      tile,
                      pl.BlockSpec((1, cq), lambda q, t, s0: (0, q))],
            out_specs=tile,
            scratch_shapes=[pltpu.VMEM((TOK_BLK, cq), F32)],
        ),
        out_shape=jax.ShapeDtypeStruct((n, d), F32),
        compiler_params=_cp("parallel", "arbitrary"),
        name="moe_combine",
    )(s0_flat, pos, aff_t, y, xres, gate_row)


def _moe_block(x, mod, gate_row, w_router, wg, wu, wd, layer):
    n = x.shape[0]
    cap = EC_CAPACITY * n // N_EXPERTS
    cap_pad = cap + TOK_BLK
    h, aff_t = _ffn_in(x, mod, w_router)
    pos, s0 = _select(aff_t, cap)
    s0_flat = s0[:, :, 0].reshape(-1)
    xe = _gather(s0_flat, pos, h, cap_pad)
    y = _experts(xe, wg, wu, wd, layer, cap)
    return _combine(s0_flat, pos, aff_t, y, x, gate_row)


QK_SCALE = (DA_HEAD_DIM ** -0.5) * math.log2(math.e)


def _split_bf16(x):
    hi = x.astype(BF16)
    return hi, (x - hi.astype(F32)).astype(BF16)


def _group_rms(u, gsel_ref, gain, eps):
    g = gsel_ref[...]
    hi, lo = _split_bf16(u * u)
    r = lax.rsqrt((_dot(hi, g) + _dot(lo, g)) * (1.0 / DA_HEAD_DIM) + eps)
    rhi, rlo = _split_bf16(r)
    return u * (_dot_nt(rhi, g) + _dot_nt(rlo, g)) * gain


def _rope(u, cos, sin_signed):
    d = u.shape[1]
    half = ROPE_AXIS_DIM // 2
    lane = lax.broadcasted_iota(I32, u.shape, 1)
    first = (lane & half) == 0
    swapped = jnp.where(first, pltpu.roll(u, d - half, 1), pltpu.roll(u, half, 1))
    return u * cos + swapped * sin_signed


def _qkv_kernel(*refs, rope):
    x_ref, mod_ref, w_ref, gsum_ref, qn_ref, kn_ref = refs[:6]
    q_ref, k_ref, v_ref = refs[-3:]
    if rope:
        cos_ref, sin_ref = refs[6:8]
    d = D_MODEL
    h = _norm_mod(x_ref[...], mod_ref[...]).astype(BF16)
    if rope:
        reps = d // cos_ref.shape[1]
        cos = _lane_tile(cos_ref[...], reps)
        sin = _lane_tile(sin_ref[...], reps)
    for part, (o_ref, gain_ref) in enumerate(((q_ref, qn_ref), (k_ref, kn_ref))):
        u = _dot(h, w_ref[:, part * d:(part + 1) * d])
        u = _group_rms(u, gsum_ref, gain_ref[...], NORM_EPS)
        if rope:
            u = _rope(u, cos, sin)
        if part == 0:
            u = u * QK_SCALE
        o_ref[...] = u.astype(BF16)
    v_ref[...] = _dot(h, w_ref[:, 2 * d:]).astype(BF16)


def _qkv(x, mod, w_bf, gsum, qn_row, kn_row, cos=None, sin=None, kv_rows=None, kv_offset=0, kv_into=None):
    n, d = x.shape
    tm = min(256, n)
    kv_rows = n if kv_rows is None else kv_rows
    off = kv_offset // tm
    assert kv_offset % tm == 0
    rope = cos is not None
    row = pl.BlockSpec((tm, d), lambda i: (i, 0))
    kv_row = pl.BlockSpec((tm, d), lambda i: (i + off, 0))
    in_specs = [row, _const_spec((8, d)), _const_spec((d, 3 * d)), _const_spec(gsum.shape),
                _const_spec((1, d)), _const_spec((1, d))]
    args = [x, mod, w_bf, gsum, qn_row, kn_row]
    if rope:
        tw = cos.shape[1]
        in_specs += [pl.BlockSpec((tm, tw), lambda i: (i, 0)), pl.BlockSpec((tm, tw), lambda i: (i, 0))]
        args += [cos, sin]
    aliases = {}
    if kv_into is not None:
        aliases = {len(args): 1, len(args) + 1: 2}
        in_specs += [pl.BlockSpec(memory_space=pl.ANY), pl.BlockSpec(memory_space=pl.ANY)]
        args += list(kv_into)
    kv = jax.ShapeDtypeStruct((kv_rows, d), BF16)
    return pl.pallas_call(
        functools.partial(_qkv_kernel, rope=rope),
        grid=(n // tm,),
        in_specs=in_specs,
        out_specs=[row, kv_row, kv_row],
        out_shape=[jax.ShapeDtypeStruct((n, d), BF16), kv, kv],
        input_output_aliases=aliases,
        compiler_params=_cp("parallel"),
        name="attn_qkv",
    )(*args)


NEG_BIG = -1e30
HEAD_UNROLL = 8


def _attn_kernel(q_ref, k_ref, v_ref, lam_ref, sub_ref, o_ref, m_scr, acc_scr, sa_scr, sb_scr, ma_scr, mb_scr,
                 pa_scr, pb_scr, vprev_scr, aa_scr, ab_scr, *,
                 lam_init):
    j = pl.program_id(1)
    nj = pl.num_programs(1)
    hd, vd = DA_HEAD_DIM, DA_V_DIM
    tk = k_ref.shape[0]

    last = DA_HEADS - 1

    @pl.when(j == 0)
    def _():
        m_scr[...] = jnp.full_like(m_scr, NEG_BIG)
        acc_scr[...] = jnp.zeros_like(acc_scr)
        pa_scr[...] = jnp.zeros_like(pa_scr)
        aa_scr[...] = jnp.ones_like(aa_scr)
        sb_scr[...] = jnp.full_like(sb_scr, 2.0 * NEG_BIG)
        mb_scr[...] = jnp.full_like(mb_scr, 2.0 * NEG_BIG)
        vprev_scr[...] = jnp.zeros_like(vprev_scr)

    lane = lax.broadcasted_iota(I32, (q_ref.shape[0], vd), 1)
    ones_col = jnp.where(lax.broadcasted_iota(I32, (tk, vd), 1) == 0, 1.0, 0.0).astype(BF16)

    def head_cols(h):
        return pl.ds(pl.multiple_of(h * vd, vd), vd)

    def scores(h, comp, s_ref, mx_ref):
        qb = q_ref[:, head_cols(h)]
        in_comp = (lane >= comp * hd) & (lane < (comp + 1) * hd)
        qm = jnp.where(in_comp, qb, jnp.zeros_like(qb))
        s = _dot_nt(qm, k_ref[:, head_cols(h)])
        s_ref[...] = s
        mx_ref[...] = jnp.broadcast_to(jnp.max(s, axis=1, keepdims=True), mx_ref.shape)

    def softmax_step(h, comp, s_ref, mx_ref, p_ref, alpha_ref):
        idx = 2 * h + comp
        m_prev = m_scr[idx]
        m_new = jnp.maximum(m_prev, mx_ref[...])
        alpha_ref[...] = jnp.exp2(m_prev - m_new)
        p_ref[...] = jnp.exp2(s_ref[...] - m_new[:, 0:1]).astype(BF16)
        m_scr[idx] = m_new

    def pv_step(h, comp, p_ref, alpha_ref):
        idx = 2 * h + comp
        pv = _dot(p_ref[...], jnp.concatenate([vprev_scr[...], ones_col], axis=1))
        acc_scr[idx] = _lane_tile(alpha_ref[...], 2) * acc_scr[idx] + pv

    def head(h, carry):
        hp = (h + last) & last
        scores(h, 0, sa_scr, ma_scr)
        pv_step(hp, 0, pa_scr, aa_scr)
        softmax_step(hp, 1, sb_scr, mb_scr, pb_scr, ab_scr)
        scores(h, 1, sb_scr, mb_scr)
        pv_step(hp, 1, pb_scr, ab_scr)
        softmax_step(h, 0, sa_scr, ma_scr, pa_scr, aa_scr)
        vprev_scr[...] = v_ref[:, head_cols(h)]
        return carry

    def head_group(g, carry):
        for u in range(HEAD_UNROLL):
            carry = head(HEAD_UNROLL * g + u, carry)
        return carry

    lax.fori_loop(0, DA_HEADS // HEAD_UNROLL, head_group, 0)

    @pl.when(j == nj - 1)
    def _():
        pv_step(last, 0, pa_scr, aa_scr)
        softmax_step(last, 1, sb_scr, mb_scr, pb_scr, ab_scr)
        pv_step(last, 1, pb_scr, ab_scr)
        lp = lam_ref[...]
        lam = (jnp.exp(jnp.sum(lp[0:1] * lp[1:2], axis=1, keepdims=True))
               - jnp.exp(jnp.sum(lp[2:3] * lp[3:4], axis=1, keepdims=True)) + lam_init)
        for h in range(DA_HEADS):
            a0 = acc_scr[2 * h]
            a1 = acc_scr[2 * h + 1]
            o = a0[:, :vd] / a0[:, vd:vd + 1] - lam * (a1[:, :vd] / a1[:, vd:vd + 1])
            ms = jnp.mean(o * o, axis=1, keepdims=True)
            o = o * lax.rsqrt(ms + SUBLN_EPS) * (sub_ref[...] * (1.0 - lam_init))
            o_ref[:, h * vd:(h + 1) * vd] = o.astype(BF16)


def _attention(q, k_all, v_all, lam_rows, subln_row, lam_init):
    n, d = q.shape
    nk = k_all.shape[0]
    tq = min(512, n)
    tk = 1280 if nk % 1280 == 0 else 256
    assert nk % tk == 0
    nc = 2 * DA_HEADS
    return pl.pallas_call(
        functools.partial(_attn_kernel, lam_init=lam_init),
        grid=(n // tq, nk // tk),
        in_specs=[pl.BlockSpec((tq, d), lambda i, j: (i, 0)),
                  pl.BlockSpec((tk, d), lambda i, j: (j, 0)),
                  pl.BlockSpec((tk, d), lambda i, j: (j, 0)),
                  _const_spec((8, DA_HEAD_DIM)), _const_spec((1, DA_V_DIM))],
        out_specs=pl.BlockSpec((tq, d), lambda i, j: (i, 0)),
        out_shape=jax.ShapeDtypeStruct((n, d), BF16),
        scratch_shapes=[pltpu.VMEM((nc, tq, DA_V_DIM), F32), pltpu.VMEM((nc, tq, 2 * DA_V_DIM), F32),
                        pltpu.VMEM((tq, tk), F32), pltpu.VMEM((tq, tk), F32),
                        pltpu.VMEM((tq, DA_V_DIM), F32), pltpu.VMEM((tq, DA_V_DIM), F32),
                        pltpu.VMEM((tq, tk), BF16), pltpu.VMEM((tq, tk), BF16),
                        pltpu.VMEM((tk, DA_V_DIM), BF16),
                        pltpu.VMEM((tq, DA_V_DIM), F32), pltpu.VMEM((tq, DA_V_DIM), F32)],
        compiler_params=_cp("parallel", "arbitrary"),
        name="diff_attention",
    )(q, k_all, v_all, lam_rows, subln_row)


def _rope_tables(n):
    lane = np.arange(2 * DA_HEAD_DIM) % DA_HEAD_DIM
    nf = ROPE_AXIS_DIM // 2
    inv = ROPE_THETA ** (-np.arange(0, ROPE_AXIS_DIM, 2, dtype=np.float32) / ROPE_AXIS_DIM)
    by_row = jnp.asarray(np.where(lane < ROPE_AXIS_DIM, inv[lane % nf], 0.0).astype(np.float32))
    by_col = jnp.asarray(np.where(lane >= ROPE_AXIS_DIM, inv[lane % nf], 0.0).astype(np.float32))
    sign = jnp.asarray(np.where(lane % ROPE_AXIS_DIM < nf, -1.0, 1.0).astype(np.float32))
    t = jnp.arange(n, dtype=I32)
    row = (t // GRID_W).astype(F32)[:, None]
    col = (t % GRID_W).astype(F32)[:, None]
    ang = row * by_row[None] + col * by_col[None]
    return jnp.cos(ang), jnp.sin(ang) * sign[None]


def _hyena_layer(x, mod, gate_row, kspec, inv_norm, w_in_bf, b_in, conv_w, conv_b, skip, w_out_bf, b_out):
    seq = x.shape[0]
    v, x1, x2 = _hy_in(x, mod, w_in_bf, b_in, conv_w, conv_b)
    z = _long_conv_gate(v, x1, skip[0:1], kspec, inv_norm, 0, seq)
    z = _long_conv_gate(z, x2, skip[1:2], kspec, inv_norm, 1, seq)
    return _proj_res(z, w_out_bf, b_out.reshape(1, -1), gate_row, x)


def kernel(x, c, ctx, c_ctx, ada_w, ada_b, norm_mix, norm_ffn, hy_w_in, hy_b_in, hy_conv_w, hy_conv_b, hy_f_w1, hy_f_b1, hy_f_w2, hy_f_b2, hy_f_w3, hy_f_b3, hy_f_w4, hy_f_freq, hy_skip, hy_w_out, hy_b_out, da_w_qkv, da_q_norm, da_k_norm, da_lam_q1, da_lam_k1, da_lam_q2, da_lam_k2, da_subln, da_w_out, moe_router, moe_w_gate, moe_w_up, moe_w_down):
    d = D_MODEL
    depth = ada_w.shape[0]
    assert x.shape[0] == 1 and x.shape[2] == d
    xs = x[0]
    cs = ctx[0]
    cond8 = jnp.concatenate([c[0:1], c_ctx[None], jnp.zeros((6, d), F32)], axis=0)
    mods = _adaln(cond8, ada_w, ada_b)

    def mod_slice(i, row, k):
        return mods[i, row, k * d:(k + 1) * d][None]

    for i in range(depth):
        last = i == depth - 1
        j = i // 2
        mix_x = _mod_rows(norm_mix[i], mods[i], 0, 0)
        mix_c = _mod_rows(norm_mix[i], mods[i], 1, 0)
        if i % 2 == 0:
            fparams = (hy_f_w1[j], hy_f_b1[j], hy_f_w2[j], hy_f_b2[j], hy_f_w3[j], hy_f_b3[j], hy_f_w4[j],
                       hy_f_freq[j])
            shared = (hy_w_in[j].astype(BF16), hy_b_in[j], hy_conv_w[j], hy_conv_b[j], hy_skip[j],
                      hy_w_out[j].astype(BF16), hy_b_out[j])
            kspec, asum = _filter_spectrum(xs.shape[0], fparams)
            inv_norm = 1.0 / (asum[0:1] + HY_FILTER_EPS)
            new_x = _hyena_layer(xs, mix_x, mod_slice(i, 0, 2), kspec, inv_norm, *shared)
            if not last:
                kspec_c, asum_c = _filter_spectrum(cs.shape[0], fparams)
                inv_c = 1.0 / (asum_c[0:1] + HY_FILTER_EPS)
                cs = _hyena_layer(cs, mix_c, mod_slice(i, 1, 2), kspec_c, inv_c, *shared)
            xs = new_x
        else:
            lam_init = 0.8 - 0.6 * math.exp(-0.3 * i)
            w_qkv = da_w_qkv[j].astype(BF16)
            gidx = np.arange(d) // DA_HEAD_DIM
            gsum = jnp.asarray((gidx[:, None] == np.arange(128)[None]).astype(np.float32)).astype(BF16)
            qn = jnp.tile(da_q_norm[j], 2 * DA_HEADS)[None]
            kn = jnp.tile(da_k_norm[j], 2 * DA_HEADS)[None]
            cos, sin = _rope_tables(xs.shape[0])
            n_ctx = cs.shape[0]
            nk = n_ctx + xs.shape[0]
            qx, k_all, v_all = _qkv(xs, mix_x, w_qkv, gsum, qn, kn, cos, sin, kv_rows=nk, kv_offset=n_ctx)
            qc, k_all, v_all = _qkv(cs, mix_c, w_qkv, gsum, qn, kn, kv_rows=nk, kv_into=(k_all, v_all))
            lam_rows = jnp.concatenate([da_lam_q1[j][None], da_lam_k1[j][None], da_lam_q2[j][None],
                                        da_lam_k2[j][None], jnp.zeros((4, DA_HEAD_DIM), F32)], axis=0)
            w_out = da_w_out[j].astype(BF16)
            zero_b = jnp.zeros((1, d), F32)
            ox = _attention(qx, k_all, v_all, lam_rows, da_subln[j][None], lam_init)
            new_x = _proj_res(ox, w_out, zero_b, mod_slice(i, 0, 2), xs)
            if not last:
                oc = _attention(qc, k_all[:n_ctx], v_all[:n_ctx], lam_rows, da_subln[j][None], lam_init)
                cs = _proj_res(oc, w_out, zero_b, mod_slice(i, 1, 2), cs)
            xs = new_x
        experts = (moe_w_gate, moe_w_up, moe_w_down, i)
        if not last:
            cs = _moe_block(cs, _mod_rows(norm_ffn[i], mods[i], 1, 3), mod_slice(i, 1, 5), moe_router[i], *experts)
        xs = _moe_block(xs, _mod_rows(norm_ffn[i], mods[i], 0, 3), mod_slice(i, 0, 5), moe_router[i], *experts)
    return xs[None]
```

```python
import functools
import math

import jax
import jax.numpy as jnp
import numpy as np
from jax import lax
from jax.experimental import pallas as pl
from jax.experimental.pallas import tpu as pltpu

F32 = jnp.float32
BF16 = jnp.bfloat16
I32 = jnp.int32
HIGHEST = lax.Precision.HIGHEST

D_MODEL = 1024
N_MOD = 6
NORM_EPS = 1e-6
GRID_W = 64
HY_ORDER = 2
HY_SHORT = 3
HY_EMB_BANDS = 16
HY_EMB_DIM = 1 + 2 * HY_EMB_BANDS
HY_FILTER_HIDDEN = 64
HY_DECAY_FAST = 0.3
HY_DECAY_SLOW = 1.5
HY_DECAY_TARGET = 1e-2
HY_FILTER_EPS = 1e-6
Z_SIGN_COL = 33
DA_HEADS = 8
DA_HEAD_DIM = 64
DA_V_DIM = 128
ROPE_AXIS_DIM = 32
ROPE_THETA = 10000.0
SUBLN_EPS = 1e-5
N_EXPERTS = 16
EC_CAPACITY = 2
D_EXPERT = 1024
TOK_BLK = 256
ROW_ALIGN = 16
WIN_SHIFT = 7
WIN = 1 << WIN_SHIFT

ROW_TILE = 1024
VMEM_LIMIT = 56 * 1024 * 1024


def _cp(*sem):
    return pltpu.CompilerParams(dimension_semantics=sem, vmem_limit_bytes=VMEM_LIMIT)


def _const_spec(shape):
    nd = len(shape)
    return pl.BlockSpec(shape, lambda *_: (0,) * nd)


def _dot(a, b):
    return jnp.dot(a, b, preferred_element_type=F32)


def _dot_nt(a, b):
    return lax.dot_general(a, b, (((1,), (1,)), ((), ())), preferred_element_type=F32)


def _norm_mod(x, mod, eps=NORM_EPS):
    ms = jnp.mean(x * x, axis=-1, keepdims=True)
    return x * lax.rsqrt(ms + eps) * (mod[0:1] * (1.0 + mod[1:2])) + mod[2:3]


def _lane_tile(x, reps):
    return jnp.concatenate([x] * reps, axis=1) if reps > 1 else x


def _adaln_kernel(c_ref, w_ref, b_ref, o_ref):
    c = c_ref[...]
    s = c / (1.0 + jnp.exp(-c))
    o_ref[0] = jnp.dot(s, w_ref[0], precision=HIGHEST, preferred_element_type=F32) + b_ref[0]


def _adaln(cond8, ada_w, ada_b):
    depth, d, nout = ada_w.shape
    tn = 1536
    return pl.pallas_call(
        _adaln_kernel,
        grid=(depth, nout // tn),
        in_specs=[_const_spec((8, d)),
                  pl.BlockSpec((1, d, tn), lambda l, j: (l, 0, j)),
                  pl.BlockSpec((1, 1, tn), lambda l, j: (l, 0, j))],
        out_specs=pl.BlockSpec((1, 8, tn), lambda l, j: (l, 0, j)),
        out_shape=jax.ShapeDtypeStruct((depth, 8, nout), F32),
        compiler_params=_cp("parallel", "parallel"),
        name="adaln",
    )(cond8, ada_w, ada_b.reshape(depth, 1, nout))


def _mod_rows(norm_g, mods, row, k0):
    d = D_MODEL
    shift = mods[row, k0 * d:(k0 + 1) * d]
    scale = mods[row, (k0 + 1) * d:(k0 + 2) * d]
    z = jnp.zeros((5, d), F32)
    return jnp.concatenate([norm_g[None], scale[None], shift[None], z], axis=0)


HALO = 16


def _hy_in_kernel(x_ref, xp_ref, xn_ref, mod_ref, w_ref, b_ref, cw_ref, v_ref, x1_ref, x2_ref, *, tm, n_rows):
    i = pl.program_id(0)
    mod = mod_ref[...]
    hm = _norm_mod(x_ref[...], mod).astype(BF16)
    hp = _norm_mod(xp_ref[...], mod).astype(BF16)
    hn = _norm_mod(xn_ref[...], mod).astype(BF16)
    hcat = jnp.concatenate([hp, hm, hn], axis=0)
    row = lax.broadcasted_iota(I32, (tm + 2 * HALO, 1), 0) + (i * tm - HALO)
    valid = jnp.logical_and(row >= 0, row < n_rows)
    d = D_MODEL
    for c, o_ref in enumerate((v_ref, x1_ref, x2_ref)):
        u = _dot(hcat, w_ref[:, c * d:(c + 1) * d]) + b_ref[:, c * d:(c + 1) * d]
        u = jnp.where(valid, u, 0.0)
        cw = cw_ref[:, c * d:(c + 1) * d]
        y = (cw[3:4] + cw[0:1] * u[HALO - 1:HALO - 1 + tm] + cw[1:2] * u[HALO:HALO + tm]
             + cw[2:3] * u[HALO + 1:HALO + 1 + tm])
        o_ref[...] = y.astype(BF16)


def _hy_in(x, mod, w_bf, b_in, conv_w, conv_b):
    n, d = x.shape
    tm = min(ROW_TILE, n)
    nh = n // HALO
    cw = jnp.concatenate([conv_w, conv_b[None], jnp.zeros((4, 3 * d), F32)], axis=0)
    out = jax.ShapeDtypeStruct((n, d), BF16)
    row_spec = pl.BlockSpec((tm, d), lambda i: (i, 0))
    return pl.pallas_call(
        functools.partial(_hy_in_kernel, tm=tm, n_rows=n),
        grid=(n // tm,),
        in_specs=[row_spec,
                  pl.BlockSpec((HALO, d), lambda i: (jnp.maximum(i * (tm // HALO) - 1, 0), 0)),
                  pl.BlockSpec((HALO, d), lambda i: (jnp.minimum((i + 1) * (tm // HALO), nh - 1), 0)),
                  _const_spec((8, d)), _const_spec((d, 3 * d)), _const_spec((1, 3 * d)),
                  _const_spec((8, 3 * d))],
        out_specs=[row_spec, row_spec, row_spec],
        out_shape=[out, out, out],
        compiler_params=_cp("parallel"),
        name="hyena_in",
    )(x, x, x, mod, w_bf, b_in.reshape(1, 3 * d), cw)


Z_HALF = 64


def _filter_s1_kernel(z_ref, w1_ref, b1_ref, w2_ref, b2_ref, w3_ref, b3_ref, fr_ref, w4f_ref, w4b_ref, dl_ref,
                      f1_ref, tre_ref, tim_ref, o_ref, asum_ref, *, n1):
    j0 = pl.program_id(0) * N2_GRP
    fr = fr_ref[...]
    dl = dl_ref[...]
    r = n1 // 2

    def lin(a, w_ref, b_ref):
        return jnp.dot(a, w_ref[...], precision=HIGHEST, preferred_element_type=F32) + b_ref[...]

    @pl.when(j0 == 0)
    def _():
        asum_ref[...] = jnp.zeros_like(asum_ref)

    for slot in range(N2_GRP):
        z = z_ref[slot * r:(slot + 1) * r, :]
        hid = jnp.sin(fr * lin(z, w1_ref, b1_ref))
        hid = jnp.sin(fr * lin(hid, w2_ref, b2_ref))
        hid = jnp.sin(fr * lin(hid, w3_ref, b3_ref)).astype(BF16)

        def taps(w4_ref, col):
            t = z[:, col:col + 1]
            sgn = z[:, col + Z_SIGN_COL:col + Z_SIGN_COL + 1]
            return _dot(hid, w4_ref[...]) * jnp.exp(-t * dl) * sgn

        k = jnp.concatenate([taps(w4f_ref, 0), taps(w4b_ref, Z_HALF)], axis=0)
        asum_ref[0:1, :] += jnp.sum(jnp.abs(k), axis=0, keepdims=True)
        _s1_store(_dot(f1_ref[...], k.astype(BF16)), tre_ref, tim_ref, j0 + slot, o_ref, n1, slot)


def _filter_positions(seq, n1, n2):
    i = np.arange(n1 // 2)[None, :]
    j = np.arange(n2)[:, None]
    bands = np.linspace(1e-4, HY_EMB_BANDS - 1, HY_EMB_BANDS)
    z = np.zeros((n2, n1 // 2, 2 * Z_HALF), np.float64)
    for col, r in ((0, i * n2 + j), (Z_HALF, i * n2 + j + seq)):
        pos = np.minimum(np.where(r < seq, r, 2 * seq - r), seq - 1).astype(np.float64)
        w = 2.0 * np.pi * pos / seq
        z[:, :, col] = pos / (seq - 1)
        z[:, :, col + 1:col + 1 + HY_EMB_BANDS] = np.cos(w[..., None] * bands)
        z[:, :, col + 1 + HY_EMB_BANDS:col + HY_EMB_DIM] = -np.sin(w[..., None] * bands)
        z[:, :, col + Z_SIGN_COL] = np.where(r < seq, 1.0, np.where(r == seq, 0.0, -1.0))
    return jnp.asarray(z.reshape(n2 * (n1 // 2), 2 * Z_HALF).astype(np.float32))


def _filter_stage1(seq, f_w1, f_b1, f_w2, f_b2, f_w3, f_b3, f_w4, f_freq):
    d, hid = D_MODEL, HY_FILTER_HIDDEN
    od = HY_ORDER * d
    n1, n2 = _fft_factors(seq)
    cst = _fft_consts(seq)
    z = _filter_positions(seq, n1, n2)
    zw = 2 * Z_HALF
    zero = jnp.zeros((hid, hid), F32)
    pad = jnp.zeros((Z_HALF - HY_EMB_DIM, hid), F32)
    w1 = jnp.concatenate([jnp.concatenate([f_w1, pad], axis=0), jnp.zeros((Z_HALF, hid), F32)], axis=0)
    w1 = jnp.concatenate([w1, jnp.roll(w1, Z_HALF, axis=0)], axis=1)
    blockdiag = lambda w: jnp.concatenate([jnp.concatenate([w, zero], axis=1),
                                           jnp.concatenate([zero, w], axis=1)], axis=0)
    twice = lambda v: jnp.tile(v, 2).reshape(1, 2 * hid)
    w4d = f_w4.reshape(hid, HY_ORDER, 2, d).transpose(2, 0, 1, 3).reshape(2, hid, od)
    zrows = jnp.zeros((hid, od), F32)
    w4f = jnp.concatenate([w4d[0], zrows], axis=0).astype(BF16)
    w4b = jnp.concatenate([zrows, w4d[1]], axis=0).astype(BF16)
    max_decay = math.log(HY_DECAY_TARGET) / HY_DECAY_FAST
    min_decay = math.log(HY_DECAY_TARGET) / HY_DECAY_SLOW
    deltas = np.abs(np.linspace(min_decay, max_decay, d, dtype=np.float32))
    dl = jnp.asarray(np.tile(deltas, HY_ORDER)[None])
    r = n1 // 2
    return pl.pallas_call(
        functools.partial(_filter_s1_kernel, n1=n1),
        grid=(n2 // N2_GRP,),
        in_specs=[pl.BlockSpec((N2_GRP * r, zw), lambda j: (j, 0)),
                  _const_spec((zw, 2 * hid)), _const_spec((1, 2 * hid)),
                  _const_spec((2 * hid, 2 * hid)), _const_spec((1, 2 * hid)),
                  _const_spec((2 * hid, 2 * hid)), _const_spec((1, 2 * hid)),
                  _const_spec((1, 2 * hid)),
                  _const_spec((2 * hid, od)), _const_spec((2 * hid, od)), _const_spec((1, od)),
                  _const_spec((2 * n1, n1)), _const_spec((n1, n2)), _const_spec((n1, n2))],
        out_specs=[pl.BlockSpec((2, n1 // K1_GRP, od // LANES, N2_GRP * PAIRS, LANES), lambda j: (0, 0, 0, j, 0)),
                   _const_spec((8, od))],
        out_shape=[jax.ShapeDtypeStruct((2, n1 // K1_GRP, od // LANES, n2 * PAIRS, LANES), jnp.uint32),
                   jax.ShapeDtypeStruct((8, od), F32)],
        compiler_params=_cp("arbitrary"),
        name="hyena_filter",
    )(z, w1, twice(f_b1), blockdiag(f_w2), twice(f_b2), blockdiag(f_w3), twice(f_b3), twice(f_freq),
      w4f, w4b, dl, cst["f1_full"], cst["tre"], cst["tim"])


def _fft_factors(seq):
    n = 2 * seq
    n1 = 256 if n >= 32768 else 32
    n2 = n // n1
    assert n1 * n2 == n and n2 % 16 == 0 and n1 % 32 == 0
    return n1, n2


@functools.lru_cache(maxsize=None)
def _fft_consts(seq):
    n = 2 * seq
    n1, n2 = _fft_factors(seq)
    k1 = np.arange(n1, dtype=np.float64)[:, None] + 0.5
    th1 = 2.0 * np.pi * k1 * np.arange(n1, dtype=np.float64)[None] / n1
    perm = np.concatenate([np.arange(0, n1, 2), np.arange(1, n1, 2)])
    f1_full = np.concatenate([np.cos(th1)[perm], -np.sin(th1)[perm]], axis=0)
    f1_half = f1_full[:, :n1 // 2]
    tw = 2.0 * np.pi * k1 * np.arange(n2, dtype=np.float64)[None] / n
    tre, tim = np.cos(tw), -np.sin(tw)
    h2 = n2 // 2
    th2 = 2.0 * np.pi * np.arange(h2, dtype=np.float64)[:, None] * np.arange(n2, dtype=np.float64)[None] / n2
    c2, s2 = np.cos(th2), np.sin(th2)
    m2f = np.block([[c2, s2], [-s2, c2]])
    m2i = np.block([[c2.T, -s2.T], [s2.T, c2.T]])
    thb = th1[:, :n1 // 2].T
    gi = (2.0 / n) * np.concatenate([np.cos(thb)[:, perm], -np.sin(thb)[:, perm]], axis=1)
    bf = lambda a: jnp.asarray(a.astype(np.float32)).astype(BF16)
    f32 = lambda a: jnp.asarray(a.astype(np.float32))
    grp = lambda a: a.T.reshape(n2, n1 // K1_GRP, K1_GRP).transpose(1, 0, 2)
    return dict(f1_full=bf(f1_full), f1_half=bf(f1_half), tre=f32(tre[perm]), tim=f32(tim[perm]),
                tre_grp=f32(grp(tre)), tim_grp=f32(grp(tim)), m2f=bf(m2f), m2i=bf(m2i), gi=bf(gi))


def _pick_col(tbl, idx):
    lane = lax.broadcasted_iota(I32, tbl.shape, 1)
    return jnp.sum(jnp.where(lane == idx, tbl, 0.0), axis=1, keepdims=True)


K1_GRP = 16
PAIRS = K1_GRP // 2
LANES = 128
N2_GRP = 4
MID_TC = 512


def _unpack_pair(w, half):
    bits = lax.shift_left(w, jnp.uint32(16)) if half == 0 else (w & jnp.uint32(0xFFFF0000))
    return pltpu.bitcast(bits, F32).astype(BF16)


def _pack_pair(even, odd):
    ue = pltpu.bitcast(even.astype(BF16).astype(F32), jnp.uint32)
    uo = pltpu.bitcast(odd.astype(BF16).astype(F32), jnp.uint32)
    return lax.shift_right_logical(ue, jnp.uint32(16)) | uo


def _s1_store(a, tre_ref, tim_ref, j, o_ref, n1, slot):
    are, aim = a[:n1], a[n1:]
    tre = _pick_col(tre_ref[...], j)
    tim = _pick_col(tim_ref[...], j)
    re = are * tre - aim * tim
    im = are * tim + aim * tre
    h = n1 // 2
    shape = (n1 // K1_GRP, PAIRS, a.shape[1])
    for part, val in enumerate((re, im)):
        words = _pack_pair(val[:h], val[h:]).reshape(shape)
        for ct in range(a.shape[1] // LANES):
            o_ref[part, :, ct, slot * PAIRS:(slot + 1) * PAIRS, :] = words[:, :, ct * LANES:(ct + 1) * LANES]


def _fft_s1_kernel(f1_ref, x_ref, tre_ref, tim_ref, o_ref, *, n1, tc):
    j0 = pl.program_id(0) * N2_GRP
    for slot in range(N2_GRP):
        a = _dot(f1_ref[...], x_ref[:, slot * tc:(slot + 1) * tc])
        _s1_store(a, tre_ref, tim_ref, j0 + slot, o_ref, n1, slot)


def _fft_stage1(x2d, f1, tre, tim, n1, n2, chans):
    r = x2d.shape[0]
    return pl.pallas_call(
        functools.partial(_fft_s1_kernel, n1=n1, tc=chans),
        grid=(n2 // N2_GRP,),
        in_specs=[_const_spec((2 * n1, r)),
                  pl.BlockSpec((r, N2_GRP * chans), lambda j: (0, j)),
                  _const_spec((n1, n2)), _const_spec((n1, n2))],
        out_specs=pl.BlockSpec((2, n1 // K1_GRP, chans // LANES, N2_GRP * PAIRS, LANES),
                               lambda j: (0, 0, 0, j, 0)),
        out_shape=jax.ShapeDtypeStruct((2, n1 // K1_GRP, chans // LANES, n2 * PAIRS, LANES), jnp.uint32),
        compiler_params=_cp("parallel"),
        name="fft_stage1",
    )(f1, x2d, tre, tim)


def _load_pair(b_ref, part, s, n2):
    return jnp.concatenate([b_ref[part, 0, ct, pl.ds(s, n2, stride=PAIRS), :] for ct in range(b_ref.shape[2])],
                           axis=1)


def _store_pair(o_ref, part, s, n2, words):
    for ct in range(o_ref.shape[2]):
        o_ref[part, 0, ct, pl.ds(s, n2, stride=PAIRS), :] = words[:, ct * LANES:(ct + 1) * LANES]


def _fft_s2_kernel(m2f_ref, b_ref, o_ref):
    n2 = m2f_ref.shape[0]
    for s in range(PAIRS):
        wre = _load_pair(b_ref, 0, s, n2)
        wim = _load_pair(b_ref, 1, s, n2)
        for half in range(2):
            b = jnp.concatenate([_unpack_pair(wre, half), _unpack_pair(wim, half)], axis=0)
            o_ref[2 * s + half] = _dot(m2f_ref[...], b).astype(BF16)


def _fft_stage2(b5, m2f, n1, n2, chans):
    return pl.pallas_call(
        _fft_s2_kernel,
        grid=(n1 // K1_GRP, chans // MID_TC),
        in_specs=[_const_spec((n2, 2 * n2)),
                  pl.BlockSpec((2, 1, MID_TC // LANES, n2 * PAIRS, LANES), lambda g, c: (0, g, c, 0, 0))],
        out_specs=pl.BlockSpec((K1_GRP, n2, MID_TC), lambda g, c: (g, 0, c)),
        out_shape=jax.ShapeDtypeStruct((n1, n2, chans), BF16),
        compiler_params=_cp("parallel", "parallel"),
        name="fft_stage2",
    )(m2f, b5)


def _fft_mid_kernel(m2f_ref, m2i_ref, b_ref, k_ref, inv_ref, tre_ref, tim_ref, o_ref, *, n2):
    h2 = n2 // 2
    inv = inv_ref[...]
    tre_g = tre_ref[0]
    tim_g = tim_ref[0]
    for s in range(PAIRS):
        wre = _load_pair(b_ref, 0, s, n2)
        wim = _load_pair(b_ref, 1, s, n2)
        res = []
        for half in range(2):
            kl = 2 * s + half
            b = jnp.concatenate([_unpack_pair(wre, half), _unpack_pair(wim, half)], axis=0)
            x = _dot(m2f_ref[...], b)
            kk = k_ref[kl].astype(F32) * inv
            xre, xim = x[:h2], x[h2:]
            kre, kim = kk[:h2], kk[h2:]
            y = jnp.concatenate([xre * kre - xim * kim, xre * kim + xim * kre], axis=0).astype(BF16)
            c = _dot(m2i_ref[...], y)
            cre, cim = c[:n2], c[n2:]
            tre = tre_g[:, kl:kl + 1]
            tim = tim_g[:, kl:kl + 1]
            res.append((cre * tre + cim * tim, cim * tre - cre * tim))
        _store_pair(o_ref, 0, s, n2, _pack_pair(res[0][0], res[1][0]))
        _store_pair(o_ref, 1, s, n2, _pack_pair(res[0][1], res[1][1]))


def _fft_mid(b5, kspec, inv_norm, order, cst, n1, n2):
    d = D_MODEL
    nc = d // MID_TC
    blk = pl.BlockSpec((2, 1, MID_TC // LANES, n2 * PAIRS, LANES), lambda g, c: (0, g, c, 0, 0))
    tw = pl.BlockSpec((1, n2, K1_GRP), lambda g, c: (g, 0, 0))
    return pl.pallas_call(
        functools.partial(_fft_mid_kernel, n2=n2),
        grid=(n1 // K1_GRP, nc),
        in_specs=[_const_spec((n2, 2 * n2)), _const_spec((2 * n2, n2)), blk,
                  pl.BlockSpec((K1_GRP, n2, MID_TC), lambda g, c: (g, 0, order * nc + c)),
                  pl.BlockSpec((1, MID_TC), lambda g, c: (0, order * nc + c)),
                  tw, tw],
        out_specs=blk,
        out_shape=jax.ShapeDtypeStruct((2, n1 // K1_GRP, d // LANES, n2 * PAIRS, LANES), jnp.uint32),
        compiler_params=_cp("parallel", "parallel"),
        name="fft_mid",
    )(cst["m2f"], cst["m2i"], b5, kspec, inv_norm, cst["tre_grp"], cst["tim_grp"])


def _fft_last_kernel(gi_ref, c_ref, gate_ref, z_ref, skip_ref, o_ref):
    h = c_ref.shape[1] * PAIRS
    d = c_ref.shape[2] * LANES
    for slot in range(N2_GRP):
        rows = slice(slot * PAIRS, (slot + 1) * PAIRS)
        cols = slice(slot * d, (slot + 1) * d)

        def words(part):
            return jnp.concatenate([c_ref[part, :, ct, rows, :].reshape(h, LANES)
                                    for ct in range(c_ref.shape[2])], axis=1)

        wre = words(0)
        wim = words(1)
        c = jnp.concatenate([_unpack_pair(wre, 0), _unpack_pair(wre, 1), _unpack_pair(wim, 0),
                             _unpack_pair(wim, 1)], axis=0)
        y = _dot(gi_ref[...], c)
        z = z_ref[:, cols].astype(F32)
        o_ref[:, cols] = (gate_ref[:, cols].astype(F32) * (y + skip_ref[...] * z)).astype(BF16)


def _fft_last(c5, gate2d, z2d, skip_row, gi, n1, n2):
    d = D_MODEL
    r = n1 // 2
    col = pl.BlockSpec((r, N2_GRP * d), lambda j: (0, j))
    return pl.pallas_call(
        _fft_last_kernel,
        grid=(n2 // N2_GRP,),
        in_specs=[_const_spec((r, 2 * n1)),
                  pl.BlockSpec((2, n1 // K1_GRP, d // LANES, N2_GRP * PAIRS, LANES), lambda j: (0, 0, 0, j, 0)),
                  col, col, _const_spec((1, d))],
        out_specs=col,
        out_shape=jax.ShapeDtypeStruct((r, n2 * d), BF16),
        compiler_params=_cp("parallel"),
        name="fft_last",
    )(gi, c5, gate2d, z2d, skip_row)


def _long_conv_gate(z_in, gate, skip_row, kspec, inv_norm, order, seq):
    d = D_MODEL
    n1, n2 = _fft_factors(seq)
    cst = _fft_consts(seq)
    z2d = z_in.reshape(n1 // 2, n2 * d)
    b = _fft_stage1(z2d, cst["f1_half"], cst["tre"], cst["tim"], n1, n2, d)
    c = _fft_mid(b, kspec, inv_norm, order, cst, n1, n2)
    out = _fft_last(c, gate.reshape(n1 // 2, n2 * d), z2d, skip_row, cst["gi"], n1, n2)
    return out.reshape(seq, d)


def _filter_spectrum(seq, fparams):
    d = D_MODEL
    od = HY_ORDER * d
    n1, n2 = _fft_factors(seq)
    cst = _fft_consts(seq)
    b, asum = _filter_stage1(seq, *fparams)
    kspec = _fft_stage2(b, cst["m2f"], n1, n2, od)
    return kspec, asum


def _proj_res_kernel(a_ref, w_ref, b_ref, g_ref, x_ref, o_ref):
    y = _dot(a_ref[...], w_ref[...]) + b_ref[...]
    o_ref[...] = x_ref[...] + g_ref[...] * y


def _proj_res(a, w_bf, b_row, gate_row, xres):
    n, d = xres.shape
    tm = min(ROW_TILE, n)
    row = pl.BlockSpec((tm, d), lambda i: (i, 0))
    return pl.pallas_call(
        _proj_res_kernel,
        grid=(n // tm,),
        in_specs=[row, _const_spec((d, d)), _const_spec((1, d)), _const_spec((1, d)), row],
        out_specs=row,
        out_shape=jax.ShapeDtypeStruct((n, d), F32),
        compiler_params=_cp("parallel"),
        name="proj_residual",
    )(a, w_bf, b_row, gate_row, xres)


def _ffn_in_kernel(x_ref, mod_ref, wt_ref, h_ref, aff_ref):
    h = _norm_mod(x_ref[...], mod_ref[...])
    hi = h.astype(BF16)
    lo = (h - hi.astype(F32)).astype(BF16)
    wt = wt_ref[...]
    whi = wt.astype(BF16)
    wlo = (wt - whi.astype(F32)).astype(BF16)
    logits = _dot_nt(whi, hi) + (_dot_nt(whi, lo) + _dot_nt(wlo, hi))
    m = jnp.max(logits, axis=0, keepdims=True)
    p = jnp.exp(logits - m)
    aff_ref[...] = p / jnp.sum(p, axis=0, keepdims=True)
    h_ref[...] = hi


def _ffn_in(x, mod, w_router):
    n, d = x.shape
    e = N_EXPERTS
    tm = min(ROW_TILE, n)
    return pl.pallas_call(
        _ffn_in_kernel,
        grid=(n // tm,),
        in_specs=[pl.BlockSpec((tm, d), lambda i: (i, 0)), _const_spec((8, d)), _const_spec((e, d))],
        out_specs=[pl.BlockSpec((tm, d), lambda i: (i, 0)), pl.BlockSpec((e, tm), lambda i: (0, i))],
        out_shape=[jax.ShapeDtypeStruct((n, d), BF16), jax.ShapeDtypeStruct((e, n), F32)],
        compiler_params=_cp("parallel"),
        name="moe_router",
    )(x, mod, w_router.T)


def _select_kernel(a_ref, pos_ref, s0_ref, sel_ref, *, cap, nblk):
    e = N_EXPERTS
    bits = pltpu.bitcast(a_ref[...], I32)

    def bisect(i, thr):
        cand = thr | jnp.left_shift(jnp.int32(1), 30 - i)
        cnt = jnp.sum(jnp.where(bits >= cand, 1.0, 0.0), axis=1, keepdims=True)
        return jnp.where(cnt >= cap, cand, thr)

    thr = lax.fori_loop(0, 31, bisect, jnp.zeros((e, 1), I32))
    n_gt = jnp.sum(jnp.where(bits > thr, 1.0, 0.0), axis=1, keepdims=True)
    need = cap - n_gt
    r = lax.broadcasted_iota(I32, (TOK_BLK, TOK_BLK), 0)
    c = lax.broadcasted_iota(I32, (TOK_BLK, TOK_BLK), 1)
    upper = jnp.where(r < c, 1.0, 0.0).astype(BF16)

    def pass1(j, carry):
        sl = pl.ds(pl.multiple_of(j * TOK_BLK, TOK_BLK), TOK_BLK)
        bj = pltpu.bitcast(a_ref[:, sl], I32)
        eq = jnp.where(bj == thr, 1.0, 0.0)
        rank = _dot(eq.astype(BF16), upper) + carry
        keep = jnp.logical_or(bj > thr, jnp.logical_and(bj == thr, rank < need))
        sel_ref[:, sl] = jnp.where(keep, 1.0, 0.0)
        return carry + jnp.sum(eq, axis=1, keepdims=True)

    lax.fori_loop(0, nblk, pass1, jnp.zeros((e, 1), F32))

    def pass2(j, carry):
        sl = pl.ds(pl.multiple_of(j * TOK_BLK, TOK_BLK), TOK_BLK)
        s = sel_ref[:, sl]
        slot = _dot(s.astype(BF16), upper) + carry
        pos_ref[:, sl] = jnp.where(s > 0.5, slot, -1.0).astype(I32)
        s0_ref[j] = jnp.broadcast_to(carry, (e, 128)).astype(I32)
        return carry + jnp.sum(s, axis=1, keepdims=True)

    total = lax.fori_loop(0, nblk, pass2, jnp.zeros((e, 1), F32))
    s0_ref[nblk] = jnp.broadcast_to(total, (e, 128)).astype(I32)


def _select(aff_t, cap):
    e, n = aff_t.shape
    nblk = n // TOK_BLK
    return pl.pallas_call(
        functools.partial(_select_kernel, cap=cap, nblk=nblk),
        out_shape=[jax.ShapeDtypeStruct((e, n), I32), jax.ShapeDtypeStruct((nblk + 1, e, 128), I32)],
        scratch_shapes=[pltpu.VMEM((e, n), F32)],
        compiler_params=pltpu.CompilerParams(vmem_limit_bytes=VMEM_LIMIT),
        name="moe_select",
    )(aff_t)


def _block_windows(s0_ref, blk, e_idx):
    s0 = s0_ref[blk * N_EXPERTS + e_idx]
    s1 = s0_ref[(blk + 1) * N_EXPERTS + e_idx]
    start = lax.shift_left(lax.shift_right_logical(s0, 4), 4)
    nwin = jnp.where(s1 > s0, lax.shift_right_logical(s1 - start + (WIN - 1), WIN_SHIFT), 0)
    return start, nwin


def _token_block(i):
    start = i * TOK_BLK
    return pl.ds(start if isinstance(i, int) else pl.multiple_of(start, TOK_BLK), TOK_BLK)


def _gather_kernel(s0_ref, pos_ref, h_ref, xe_ref, *, sub, nchunk):
    e_idx = pl.program_id(0)
    xe_ref[...] = jnp.zeros_like(xe_ref)
    rows = lax.broadcasted_iota(I32, (WIN, TOK_BLK), 0)

    def window(blk, base):
        tok = _token_block(blk)
        base = pl.multiple_of(base, ROW_ALIGN)
        onehot = jnp.where(rows == pos_ref[0, :, tok] - base, 1.0, 0.0).astype(BF16)
        got = _dot(onehot, h_ref[tok, :]).astype(BF16)
        win = pl.ds(base, WIN)
        xe_ref[0, win, :] = xe_ref[0, win, :] + got

    def chunk(c, carry):
        plan = [_block_windows(s0_ref, c * sub + i, e_idx) for i in range(sub)]
        single = functools.reduce(jnp.logical_and, [nwin <= 1 for _, nwin in plan])

        @pl.when(single)
        def _():
            for i, (start, _) in enumerate(plan):
                window(c * sub + i, start)

        @pl.when(jnp.logical_not(single))
        def _():
            def body(i, carry2):
                start, nwin = _block_windows(s0_ref, c * sub + i, e_idx)
                return lax.fori_loop(0, nwin, lambda w, cc: (window(c * sub + i, start + w * WIN), cc)[1], carry2)

            lax.fori_loop(0, sub, body, 0)

        return carry

    lax.fori_loop(0, nchunk, chunk, 0)


def _gather(s0_flat, pos, h, cap_pad):
    e, n = pos.shape
    d = h.shape[1]
    sub = min(8, n // TOK_BLK)
    return pl.pallas_call(
        functools.partial(_gather_kernel, sub=sub, nchunk=n // (sub * TOK_BLK)),
        grid_spec=pltpu.PrefetchScalarGridSpec(
            num_scalar_prefetch=1,
            grid=(e,),
            in_specs=[pl.BlockSpec((1, 1, n), lambda ei, s0: (ei, 0, 0)),
                      pl.BlockSpec((n, d), lambda ei, s0: (0, 0), pipeline_mode=pl.Buffered(1))],
            out_specs=pl.BlockSpec((1, cap_pad, d), lambda ei, s0: (ei, 0, 0)),
        ),
        out_shape=jax.ShapeDtypeStruct((e, cap_pad, d), BF16),
        compiler_params=_cp("parallel"),
        name="moe_gather",
    )(s0_flat, pos.reshape(e, 1, n), h)


def _expert_kernel(x_ref, wg_ref, wu_ref, wd_ref, y_ref, wg_scr, wu_scr, wd_scr, *, n_real):
    j = pl.program_id(1)

    @pl.when(j == 0)
    def _():
        wg_scr[...] = wg_ref[0].astype(BF16)
        wu_scr[...] = wu_ref[0].astype(BF16)
        wd_scr[...] = wd_ref[0].astype(BF16)

    @pl.when(j < n_real)
    def _():
        x = x_ref[0]
        g = _dot(x, wg_scr[...])
        u = _dot(x, wu_scr[...])
        a = (g / (1.0 + jnp.exp(-g))) * u
        y_ref[0] = _dot(a.astype(BF16), wd_scr[...]).astype(BF16)

    @pl.when(j >= n_real)
    def _():
        y_ref[0] = jnp.zeros_like(y_ref[0])


def _experts(xe, wg, wu, wd, layer, cap):
    e, cap_pad, d = xe.shape
    f = wg.shape[3]
    tm = cap_pad // 3 if cap_pad % (3 * ROW_ALIGN) == 0 and cap >= 3 * TOK_BLK else cap_pad
    tile = pl.BlockSpec((1, tm, d), lambda ei, j: (ei, j, 0))
    return pl.pallas_call(
        functools.partial(_expert_kernel, n_real=pl.cdiv(cap, tm)),
        grid=(e, cap_pad // tm),
        in_specs=[tile,
                  pl.BlockSpec((None, 1, d, f), lambda ei, j: (layer, ei, 0, 0)),
                  pl.BlockSpec((None, 1, d, f), lambda ei, j: (layer, ei, 0, 0)),
                  pl.BlockSpec((None, 1, f, d), lambda ei, j: (layer, ei, 0, 0))],
        out_specs=tile,
        out_shape=jax.ShapeDtypeStruct((e, cap_pad, d), BF16),
        scratch_shapes=[pltpu.VMEM((d, f), BF16), pltpu.VMEM((d, f), BF16), pltpu.VMEM((f, d), BF16)],
        compiler_params=_cp("parallel", "arbitrary"),
        name="moe_experts",
    )(xe, wg, wu, wd)


COMBINE_COLS = 512
COMBINE_BLKS = 2


def _combine_kernel(s0_ref, pos_ref, aff_ref, y_ref, x_ref, gate_ref, o_ref, acc_ref, *, nsub):
    t = pl.program_id(1)
    rows = lax.broadcasted_iota(I32, (WIN, TOK_BLK), 0)

    def selector(i, e, base):
        tok = _token_block(i)
        rel = pos_ref[e:e + 1, tok] - base
        return jnp.where(rows == rel, aff_ref[e:e + 1, tok], 0.0).astype(BF16)

    def contract(sel, ywin):
        return lax.dot_general(sel, ywin, (((0,), (0,)), ((), ())), preferred_element_type=F32)

    for i in range(nsub):
        plan = [_block_windows(s0_ref, t * nsub + i, e) for e in range(N_EXPERTS)]
        single = functools.reduce(jnp.logical_and, [nwin <= 1 for _, nwin in plan])
        tok = _token_block(i)

        @pl.when(single)
        def _():
            bases = [pl.multiple_of(start, ROW_ALIGN) for start, _ in plan]
            sel = jnp.concatenate([selector(i, e, b) for e, b in enumerate(bases)], axis=0)
            ywin = jnp.concatenate([y_ref[e, pl.ds(b, WIN), :] for e, b in enumerate(bases)], axis=0)
            o_ref[tok, :] = x_ref[tok, :] + gate_ref[...] * contract(sel, ywin)

        @pl.when(jnp.logical_not(single))
        def _():
            acc_ref[...] = jnp.zeros_like(acc_ref)
            for e, (start, nwin) in enumerate(plan):
                def window(w, carry):
                    base = pl.multiple_of(start + w * WIN, ROW_ALIGN)
                    acc_ref[...] += contract(selector(i, e, base), y_ref[e, pl.ds(base, WIN), :])
                    return carry

                lax.fori_loop(0, nwin, window, 0)
            o_ref[tok, :] = x_ref[tok, :] + gate_ref[...] * acc_ref[...]


def _combine(s0_flat, pos, aff_t, y, xres, gate_row):
    n, d = xres.shape
    e, cap_pad, _ = y.shape
    nsub = min(COMBINE_BLKS, n // TOK_BLK)
    rows = nsub * TOK_BLK
    cq = COMBINE_COLS
    tile = pl.BlockSpec((rows, cq), lambda q, t, s0: (t, q))
    return pl.pallas_call(
        functools.partial(_combine_kernel, nsub=nsub),
        grid_spec=pltpu.PrefetchScalarGridSpec(
            num_scalar_prefetch=1,
            grid=(d // cq, n // rows),
            in_specs=[pl.BlockSpec((e, rows), lambda q, t, s0: (0, t)),
                      pl.BlockSpec((e, rows), lambda q, t, s0: (0, t)),
                      pl.BlockSpec((e, cap_pad, cq), lambda q, t, s0: (0, 0, q), pipeline_mode=pl.Buffered(1)),
                      tile,
                      pl.BlockSpec((1, cq), lambda q, t, s0: (0, q))],
            out_specs=tile,
            scratch_shapes=[pltpu.VMEM((TOK_BLK, cq), F32)],
        ),
        out_shape=jax.ShapeDtypeStruct((n, d), F32),
        compiler_params=_cp("parallel", "arbitrary"),
        name="moe_combine",
    )(s0_flat, pos, aff_t, y, xres, gate_row)


def _moe_block(x, mod, gate_row, w_router, wg, wu, wd, layer):
    n = x.shape[0]
    cap = EC_CAPACITY * n // N_EXPERTS
    cap_pad = cap + TOK_BLK
    h, aff_t = _ffn_in(x, mod, w_router)
    pos, s0 = _select(aff_t, cap)
    s0_flat = s0[:, :, 0].reshape(-1)
    xe = _gather(s0_flat, pos, h, cap_pad)
    y = _experts(xe, wg, wu, wd, layer, cap)
    return _combine(s0_flat, pos, aff_t, y, x, gate_row)


QK_SCALE = (DA_HEAD_DIM ** -0.5) * math.log2(math.e)


def _split_bf16(x):
    hi = x.astype(BF16)
    return hi, (x - hi.astype(F32)).astype(BF16)


def _group_rms(u, gsel_ref, gain, eps):
    g = gsel_ref[...]
    hi, lo = _split_bf16(u * u)
    r = lax.rsqrt((_dot(hi, g) + _dot(lo, g)) * (1.0 / DA_HEAD_DIM) + eps)
    rhi, rlo = _split_bf16(r)
    return u * (_dot_nt(rhi, g) + _dot_nt(rlo, g)) * gain


def _rope(u, cos, sin_signed):
    d = u.shape[1]
    half = ROPE_AXIS_DIM // 2
    lane = lax.broadcasted_iota(I32, u.shape, 1)
    first = (lane & half) == 0
    swapped = jnp.where(first, pltpu.roll(u, d - half, 1), pltpu.roll(u, half, 1))
    return u * cos + swapped * sin_signed


def _qkv_kernel(*refs, rope):
    x_ref, mod_ref, w_ref, gsum_ref, qn_ref, kn_ref = refs[:6]
    q_ref, k_ref, v_ref = refs[-3:]
    if rope:
        cos_ref, sin_ref = refs[6:8]
    d = D_MODEL
    h = _norm_mod(x_ref[...], mod_ref[...]).astype(BF16)
    if rope:
        reps = d // cos_ref.shape[1]
        cos = _lane_tile(cos_ref[...], reps)
        sin = _lane_tile(sin_ref[...], reps)
    for part, (o_ref, gain_ref) in enumerate(((q_ref, qn_ref), (k_ref, kn_ref))):
        u = _dot(h, w_ref[:, part * d:(part + 1) * d])
        u = _group_rms(u, gsum_ref, gain_ref[...], NORM_EPS)
        if rope:
            u = _rope(u, cos, sin)
        if part == 0:
            u = u * QK_SCALE
        o_ref[...] = u.astype(BF16)
    v_ref[...] = _dot(h, w_ref[:, 2 * d:]).astype(BF16)


def _qkv(x, mod, w_bf, gsum, qn_row, kn_row, cos=None, sin=None, kv_rows=None, kv_offset=0, kv_into=None):
    n, d = x.shape
    tm = min(ROW_TILE, n)
    kv_rows = n if kv_rows is None else kv_rows
    off = kv_offset // tm
    assert kv_offset % tm == 0
    rope = cos is not None
    row = pl.BlockSpec((tm, d), lambda i: (i, 0))
    kv_row = pl.BlockSpec((tm, d), lambda i: (i + off, 0))
    in_specs = [row, _const_spec((8, d)), _const_spec((d, 3 * d)), _const_spec(gsum.shape),
                _const_spec((1, d)), _const_spec((1, d))]
    args = [x, mod, w_bf, gsum, qn_row, kn_row]
    if rope:
        tw = cos.shape[1]
        in_specs += [pl.BlockSpec((tm, tw), lambda i: (i, 0)), pl.BlockSpec((tm, tw), lambda i: (i, 0))]
        args += [cos, sin]
    aliases = {}
    if kv_into is not None:
        aliases = {len(args): 1, len(args) + 1: 2}
        in_specs += [pl.BlockSpec(memory_space=pl.ANY), pl.BlockSpec(memory_space=pl.ANY)]
        args += list(kv_into)
    kv = jax.ShapeDtypeStruct((kv_rows, d), BF16)
    return pl.pallas_call(
        functools.partial(_qkv_kernel, rope=rope),
        grid=(n // tm,),
        in_specs=in_specs,
        out_specs=[row, kv_row, kv_row],
        out_shape=[jax.ShapeDtypeStruct((n, d), BF16), kv, kv],
        input_output_aliases=aliases,
        compiler_params=_cp("parallel"),
        name="attn_qkv",
    )(*args)


NEG_BIG = -1e30
HEAD_UNROLL = 8


def _attn_kernel(q_ref, k_ref, v_ref, lam_ref, sub_ref, o_ref, m_scr, acc_scr, sa_scr, sb_scr, ma_scr, mb_scr,
                 pa_scr, pb_scr, vprev_scr, aa_scr, ab_scr, *,
                 lam_init):
    j = pl.program_id(1)
    nj = pl.num_programs(1)
    hd, vd = DA_HEAD_DIM, DA_V_DIM
    tk = k_ref.shape[0]

    last = DA_HEADS - 1

    @pl.when(j == 0)
    def _():
        m_scr[...] = jnp.full_like(m_scr, NEG_BIG)
        acc_scr[...] = jnp.zeros_like(acc_scr)
        pa_scr[...] = jnp.zeros_like(pa_scr)
        aa_scr[...] = jnp.ones_like(aa_scr)
        sb_scr[...] = jnp.full_like(sb_scr, 2.0 * NEG_BIG)
        mb_scr[...] = jnp.full_like(mb_scr, 2.0 * NEG_BIG)
        vprev_scr[...] = jnp.zeros_like(vprev_scr)

    lane = lax.broadcasted_iota(I32, (q_ref.shape[0], vd), 1)
    ones_col = jnp.where(lax.broadcasted_iota(I32, (tk, vd), 1) == 0, 1.0, 0.0).astype(BF16)

    def head_cols(h):
        return pl.ds(pl.multiple_of(h * vd, vd), vd)

    def scores(h, comp, s_ref, mx_ref):
        qb = q_ref[:, head_cols(h)]
        in_comp = (lane >= comp * hd) & (lane < (comp + 1) * hd)
        qm = jnp.where(in_comp, qb, jnp.zeros_like(qb))
        s = _dot_nt(qm, k_ref[:, head_cols(h)])
        s_ref[...] = s
        mx_ref[...] = jnp.broadcast_to(jnp.max(s, axis=1, keepdims=True), mx_ref.shape)

    def softmax_step(h, comp, s_ref, mx_ref, p_ref, alpha_ref):
        idx = 2 * h + comp
        m_prev = m_scr[idx]
        m_new = jnp.maximum(m_prev, mx_ref[...])
        alpha_ref[...] = jnp.exp2(m_prev - m_new)
        p_ref[...] = jnp.exp2(s_ref[...] - m_new[:, 0:1]).astype(BF16)
        m_scr[idx] = m_new

    def pv_step(h, comp, p_ref, alpha_ref):
        idx = 2 * h + comp
        pv = _dot(p_ref[...], jnp.concatenate([vprev_scr[...], ones_col], axis=1))
        acc_scr[idx] = _lane_tile(alpha_ref[...], 2) * acc_scr[idx] + pv

    def head(h, carry):
        hp = (h + last) & last
        scores(h, 0, sa_scr, ma_scr)
        pv_step(hp, 0, pa_scr, aa_scr)
        softmax_step(hp, 1, sb_scr, mb_scr, pb_scr, ab_scr)
        scores(h, 1, sb_scr, mb_scr)
        pv_step(hp, 1, pb_scr, ab_scr)
        softmax_step(h, 0, sa_scr, ma_scr, pa_scr, aa_scr)
        vprev_scr[...] = v_ref[:, head_cols(h)]
        return carry

    def head_group(g, carry):
        for u in range(HEAD_UNROLL):
            carry = head(HEAD_UNROLL * g + u, carry)
        return carry

    lax.fori_loop(0, DA_HEADS // HEAD_UNROLL, head_group, 0)

    @pl.when(j == nj - 1)
    def _():
        pv_step(last, 0, pa_scr, aa_scr)
        softmax_step(last, 1, sb_scr, mb_scr, pb_scr, ab_scr)
        pv_step(last, 1, pb_scr, ab_scr)
        lp = lam_ref[...]
        lam = (jnp.exp(jnp.sum(lp[0:1] * lp[1:2], axis=1, keepdims=True))
               - jnp.exp(jnp.sum(lp[2:3] * lp[3:4], axis=1, keepdims=True)) + lam_init)
        for h in range(DA_HEADS):
            a0 = acc_scr[2 * h]
            a1 = acc_scr[2 * h + 1]
            o = a0[:, :vd] / a0[:, vd:vd + 1] - lam * (a1[:, :vd] / a1[:, vd:vd + 1])
            ms = jnp.mean(o * o, axis=1, keepdims=True)
            o = o * lax.rsqrt(ms + SUBLN_EPS) * (sub_ref[...] * (1.0 - lam_init))
            o_ref[:, h * vd:(h + 1) * vd] = o.astype(BF16)


def _attention(q, k_all, v_all, lam_rows, subln_row, lam_init):
    n, d = q.shape
    nk = k_all.shape[0]
    tq = min(512, n)
    tk = 1280 if nk % 1280 == 0 else 256
    assert nk % tk == 0
    nc = 2 * DA_HEADS
    return pl.pallas_call(
        functools.partial(_attn_kernel, lam_init=lam_init),
        grid=(n // tq, nk // tk),
        in_specs=[pl.BlockSpec((tq, d), lambda i, j: (i, 0)),
                  pl.BlockSpec((tk, d), lambda i, j: (j, 0)),
                  pl.BlockSpec((tk, d), lambda i, j: (j, 0)),
                  _const_spec((8, DA_HEAD_DIM)), _const_spec((1, DA_V_DIM))],
        out_specs=pl.BlockSpec((tq, d), lambda i, j: (i, 0)),
        out_shape=jax.ShapeDtypeStruct((n, d), BF16),
        scratch_shapes=[pltpu.VMEM((nc, tq, DA_V_DIM), F32), pltpu.VMEM((nc, tq, 2 * DA_V_DIM), F32),
                        pltpu.VMEM((tq, tk), F32), pltpu.VMEM((tq, tk), F32),
                        pltpu.VMEM((tq, DA_V_DIM), F32), pltpu.VMEM((tq, DA_V_DIM), F32),
                        pltpu.VMEM((tq, tk), BF16), pltpu.VMEM((tq, tk), BF16),
                        pltpu.VMEM((tk, DA_V_DIM), BF16),
                        pltpu.VMEM((tq, DA_V_DIM), F32), pltpu.VMEM((tq, DA_V_DIM), F32)],
        compiler_params=_cp("parallel", "arbitrary"),
        name="diff_attention",
    )(q, k_all, v_all, lam_rows, subln_row)


def _rope_tables(n):
    lane = np.arange(2 * DA_HEAD_DIM) % DA_HEAD_DIM
    nf = ROPE_AXIS_DIM // 2
    inv = ROPE_THETA ** (-np.arange(0, ROPE_AXIS_DIM, 2, dtype=np.float32) / ROPE_AXIS_DIM)
    by_row = jnp.asarray(np.where(lane < ROPE_AXIS_DIM, inv[lane % nf], 0.0).astype(np.float32))
    by_col = jnp.asarray(np.where(lane >= ROPE_AXIS_DIM, inv[lane % nf], 0.0).astype(np.float32))
    sign = jnp.asarray(np.where(lane % ROPE_AXIS_DIM < nf, -1.0, 1.0).astype(np.float32))
    t = jnp.arange(n, dtype=I32)
    row = (t // GRID_W).astype(F32)[:, None]
    col = (t % GRID_W).astype(F32)[:, None]
    ang = row * by_row[None] + col * by_col[None]
    return jnp.cos(ang), jnp.sin(ang) * sign[None]


def _hyena_layer(x, mod, gate_row, kspec, inv_norm, w_in_bf, b_in, conv_w, conv_b, skip, w_out_bf, b_out):
    seq = x.shape[0]
    v, x1, x2 = _hy_in(x, mod, w_in_bf, b_in, conv_w, conv_b)
    z = _long_conv_gate(v, x1, skip[0:1], kspec, inv_norm, 0, seq)
    z = _long_conv_gate(z, x2, skip[1:2], kspec, inv_norm, 1, seq)
    return _proj_res(z, w_out_bf, b_out.reshape(1, -1), gate_row, x)


def kernel(x, c, ctx, c_ctx, ada_w, ada_b, norm_mix, norm_ffn, hy_w_in, hy_b_in, hy_conv_w, hy_conv_b, hy_f_w1, hy_f_b1, hy_f_w2, hy_f_b2, hy_f_w3, hy_f_b3, hy_f_w4, hy_f_freq, hy_skip, hy_w_out, hy_b_out, da_w_qkv, da_q_norm, da_k_norm, da_lam_q1, da_lam_k1, da_lam_q2, da_lam_k2, da_subln, da_w_out, moe_router, moe_w_gate, moe_w_up, moe_w_down):
    d = D_MODEL
    depth = ada_w.shape[0]
    assert x.shape[0] == 1 and x.shape[2] == d
    xs = x[0]
    cs = ctx[0]
    cond8 = jnp.concatenate([c[0:1], c_ctx[None], jnp.zeros((6, d), F32)], axis=0)
    mods = _adaln(cond8, ada_w, ada_b)

    def mod_slice(i, row, k):
        return mods[i, row, k * d:(k + 1) * d][None]

    for i in range(depth):
        last = i == depth - 1
        j = i // 2
        mix_x = _mod_rows(norm_mix[i], mods[i], 0, 0)
        mix_c = _mod_rows(norm_mix[i], mods[i], 1, 0)
        if i % 2 == 0:
            fparams = (hy_f_w1[j], hy_f_b1[j], hy_f_w2[j], hy_f_b2[j], hy_f_w3[j], hy_f_b3[j], hy_f_w4[j],
                       hy_f_freq[j])
            shared = (hy_w_in[j].astype(BF16), hy_b_in[j], hy_conv_w[j], hy_conv_b[j], hy_skip[j],
                      hy_w_out[j].astype(BF16), hy_b_out[j])
            kspec, asum = _filter_spectrum(xs.shape[0], fparams)
            inv_norm = 1.0 / (asum[0:1] + HY_FILTER_EPS)
            new_x = _hyena_layer(xs, mix_x, mod_slice(i, 0, 2), kspec, inv_norm, *shared)
            if not last:
                kspec_c, asum_c = _filter_spectrum(cs.shape[0], fparams)
                inv_c = 1.0 / (asum_c[0:1] + HY_FILTER_EPS)
                cs = _hyena_layer(cs, mix_c, mod_slice(i, 1, 2), kspec_c, inv_c, *shared)
            xs = new_x
        else:
            lam_init = 0.8 - 0.6 * math.exp(-0.3 * i)
            w_qkv = da_w_qkv[j].astype(BF16)
            gidx = np.arange(d) // DA_HEAD_DIM
            gsum = jnp.asarray((gidx[:, None] == np.arange(128)[None]).astype(np.float32)).astype(BF16)
            qn = jnp.tile(da_q_norm[j], 2 * DA_HEADS)[None]
            kn = jnp.tile(da_k_norm[j], 2 * DA_HEADS)[None]
            cos, sin = _rope_tables(xs.shape[0])
            n_lat = xs.shape[0]
            nk = n_lat + cs.shape[0]
            qx, k_all, v_all = _qkv(xs, mix_x, w_qkv, gsum, qn, kn, cos, sin, kv_rows=nk)
            qc, k_all, v_all = _qkv(cs, mix_c, w_qkv, gsum, qn, kn, kv_rows=nk, kv_offset=n_lat,
                                    kv_into=(k_all, v_all))
            lam_rows = jnp.concatenate([da_lam_q1[j][None], da_lam_k1[j][None], da_lam_q2[j][None],
                                        da_lam_k2[j][None], jnp.zeros((4, DA_HEAD_DIM), F32)], axis=0)
            w_out = da_w_out[j].astype(BF16)
            zero_b = jnp.zeros((1, d), F32)
            ox = _attention(qx, k_all, v_all, lam_rows, da_subln[j][None], lam_init)
            new_x = _proj_res(ox, w_out, zero_b, mod_slice(i, 0, 2), xs)
            if not last:
                oc = _attention(qc, k_all[n_lat:], v_all[n_lat:], lam_rows, da_subln[j][None], lam_init)
                cs = _proj_res(oc, w_out, zero_b, mod_slice(i, 1, 2), cs)
            xs = new_x
        experts = (moe_w_gate, moe_w_up, moe_w_down, i)
        if not last:
            cs = _moe_block(cs, _mod_rows(norm_ffn[i], mods[i], 1, 3), mod_slice(i, 1, 5), moe_router[i], *experts)
        xs = _moe_block(xs, _mod_rows(norm_ffn[i], mods[i], 0, 3), mod_slice(i, 0, 5), moe_router[i], *experts)
    return xs[None]
```

```python
import functools
import math

import jax
import jax.numpy as jnp
import numpy as np
from jax import lax
from jax.experimental import pallas as pl
from jax.experimental.pallas import tpu as pltpu

F32 = jnp.float32
BF16 = jnp.bfloat16
I32 = jnp.int32
HIGHEST = lax.Precision.HIGHEST

D_MODEL = 1024
N_MOD = 6
NORM_EPS = 1e-6
GRID_W = 64
HY_ORDER = 2
HY_SHORT = 3
HY_EMB_BANDS = 16
HY_EMB_DIM = 1 + 2 * HY_EMB_BANDS
HY_FILTER_HIDDEN = 64
HY_DECAY_FAST = 0.3
HY_DECAY_SLOW = 1.5
HY_DECAY_TARGET = 1e-2
HY_FILTER_EPS = 1e-6
Z_SIGN_COL = 33
DA_HEADS = 8
DA_HEAD_DIM = 64
DA_V_DIM = 128
ROPE_AXIS_DIM = 32
ROPE_THETA = 10000.0
SUBLN_EPS = 1e-5
N_EXPERTS = 16
EC_CAPACITY = 2
D_EXPERT = 1024
TOK_BLK = 256
ROW_ALIGN = 16
WIN_SHIFT = 7
WIN = 1 << WIN_SHIFT

ROW_TILE = 1024
QKV_TILE = 512
VMEM_LIMIT = 56 * 1024 * 1024


def _cp(*sem):
    return pltpu.CompilerParams(dimension_semantics=sem, vmem_limit_bytes=VMEM_LIMIT)


def _const_spec(shape):
    nd = len(shape)
    return pl.BlockSpec(shape, lambda *_: (0,) * nd)


def _dot(a, b):
    return jnp.dot(a, b, preferred_element_type=F32)


def _dot_nt(a, b):
    return lax.dot_general(a, b, (((1,), (1,)), ((), ())), preferred_element_type=F32)


def _norm_mod(x, mod, eps=NORM_EPS):
    ms = jnp.mean(x * x, axis=-1, keepdims=True)
    return x * lax.rsqrt(ms + eps) * (mod[0:1] * (1.0 + mod[1:2])) + mod[2:3]


def _lane_tile(x, reps):
    return jnp.concatenate([x] * reps, axis=1) if reps > 1 else x


def _adaln_kernel(c_ref, w_ref, b_ref, o_ref):
    c = c_ref[...]
    s = c / (1.0 + jnp.exp(-c))
    o_ref[0] = jnp.dot(s, w_ref[0], precision=HIGHEST, preferred_element_type=F32) + b_ref[0]


def _adaln(cond8, ada_w, ada_b):
    depth, d, nout = ada_w.shape
    tn = 1536
    return pl.pallas_call(
        _adaln_kernel,
        grid=(depth, nout // tn),
        in_specs=[_const_spec((8, d)),
                  pl.BlockSpec((1, d, tn), lambda l, j: (l, 0, j)),
                  pl.BlockSpec((1, 1, tn), lambda l, j: (l, 0, j))],
        out_specs=pl.BlockSpec((1, 8, tn), lambda l, j: (l, 0, j)),
        out_shape=jax.ShapeDtypeStruct((depth, 8, nout), F32),
        compiler_params=_cp("parallel", "parallel"),
        name="adaln",
    )(cond8, ada_w, ada_b.reshape(depth, 1, nout))


def _mod_rows(norm_g, mods, row, k0):
    d = D_MODEL
    shift = mods[row, k0 * d:(k0 + 1) * d]
    scale = mods[row, (k0 + 1) * d:(k0 + 2) * d]
    z = jnp.zeros((5, d), F32)
    return jnp.concatenate([norm_g[None], scale[None], shift[None], z], axis=0)


HALO = 16


def _hy_in_kernel(x_ref, xp_ref, xn_ref, mod_ref, w_ref, b_ref, cw_ref, v_ref, x1_ref, x2_ref, *, tm, n_rows):
    i = pl.program_id(0)
    mod = mod_ref[...]
    hm = _norm_mod(x_ref[...], mod).astype(BF16)
    hp = _norm_mod(xp_ref[...], mod).astype(BF16)
    hn = _norm_mod(xn_ref[...], mod).astype(BF16)
    hcat = jnp.concatenate([hp, hm, hn], axis=0)
    row = lax.broadcasted_iota(I32, (tm + 2 * HALO, 1), 0) + (i * tm - HALO)
    valid = jnp.logical_and(row >= 0, row < n_rows)
    d = D_MODEL
    for c, o_ref in enumerate((v_ref, x1_ref, x2_ref)):
        u = _dot(hcat, w_ref[:, c * d:(c + 1) * d]) + b_ref[:, c * d:(c + 1) * d]
        u = jnp.where(valid, u, 0.0)
        cw = cw_ref[:, c * d:(c + 1) * d]
        y = (cw[3:4] + cw[0:1] * u[HALO - 1:HALO - 1 + tm] + cw[1:2] * u[HALO:HALO + tm]
             + cw[2:3] * u[HALO + 1:HALO + 1 + tm])
        o_ref[...] = y.astype(BF16)


def _hy_in(x, mod, w_bf, b_in, conv_w, conv_b):
    n, d = x.shape
    tm = min(ROW_TILE, n)
    nh = n // HALO
    cw = jnp.concatenate([conv_w, conv_b[None], jnp.zeros((4, 3 * d), F32)], axis=0)
    out = jax.ShapeDtypeStruct((n, d), BF16)
    row_spec = pl.BlockSpec((tm, d), lambda i: (i, 0))
    return pl.pallas_call(
        functools.partial(_hy_in_kernel, tm=tm, n_rows=n),
        grid=(n // tm,),
        in_specs=[row_spec,
                  pl.BlockSpec((HALO, d), lambda i: (jnp.maximum(i * (tm // HALO) - 1, 0), 0)),
                  pl.BlockSpec((HALO, d), lambda i: (jnp.minimum((i + 1) * (tm // HALO), nh - 1), 0)),
                  _const_spec((8, d)), _const_spec((d, 3 * d)), _const_spec((1, 3 * d)),
                  _const_spec((8, 3 * d))],
        out_specs=[row_spec, row_spec, row_spec],
        out_shape=[out, out, out],
        compiler_params=_cp("parallel"),
        name="hyena_in",
    )(x, x, x, mod, w_bf, b_in.reshape(1, 3 * d), cw)


Z_HALF = 64


def _filter_s1_kernel(z_ref, w1_ref, b1_ref, w2_ref, b2_ref, w3_ref, b3_ref, fr_ref, w4f_ref, w4b_ref, dl_ref,
                      f1_ref, tre_ref, tim_ref, o_ref, asum_ref, *, n1):
    j0 = pl.program_id(0) * N2_GRP
    fr = fr_ref[...]
    dl = dl_ref[...]
    r = n1 // 2

    def lin(a, w_ref, b_ref):
        return jnp.dot(a, w_ref[...], precision=HIGHEST, preferred_element_type=F32) + b_ref[...]

    @pl.when(j0 == 0)
    def _():
        asum_ref[...] = jnp.zeros_like(asum_ref)

    for slot in range(N2_GRP):
        z = z_ref[slot * r:(slot + 1) * r, :]
        hid = jnp.sin(fr * lin(z, w1_ref, b1_ref))
        hid = jnp.sin(fr * lin(hid, w2_ref, b2_ref))
        hid = jnp.sin(fr * lin(hid, w3_ref, b3_ref)).astype(BF16)

        def taps(w4_ref, col):
            t = z[:, col:col + 1]
            sgn = z[:, col + Z_SIGN_COL:col + Z_SIGN_COL + 1]
            return _dot(hid, w4_ref[...]) * jnp.exp(-t * dl) * sgn

        k = jnp.concatenate([taps(w4f_ref, 0), taps(w4b_ref, Z_HALF)], axis=0)
        asum_ref[0:1, :] += jnp.sum(jnp.abs(k), axis=0, keepdims=True)
        _s1_store(_dot(f1_ref[...], k.astype(BF16)), tre_ref, tim_ref, j0 + slot, o_ref, n1, slot)


def _filter_positions(seq, n1, n2):
    i = np.arange(n1 // 2)[None, :]
    j = np.arange(n2)[:, None]
    bands = np.linspace(1e-4, HY_EMB_BANDS - 1, HY_EMB_BANDS)
    z = np.zeros((n2, n1 // 2, 2 * Z_HALF), np.float64)
    for col, r in ((0, i * n2 + j), (Z_HALF, i * n2 + j + seq)):
        pos = np.minimum(np.where(r < seq, r, 2 * seq - r), seq - 1).astype(np.float64)
        w = 2.0 * np.pi * pos / seq
        z[:, :, col] = pos / (seq - 1)
        z[:, :, col + 1:col + 1 + HY_EMB_BANDS] = np.cos(w[..., None] * bands)
        z[:, :, col + 1 + HY_EMB_BANDS:col + HY_EMB_DIM] = -np.sin(w[..., None] * bands)
        z[:, :, col + Z_SIGN_COL] = np.where(r < seq, 1.0, np.where(r == seq, 0.0, -1.0))
    return jnp.asarray(z.reshape(n2 * (n1 // 2), 2 * Z_HALF).astype(np.float32))


def _filter_stage1(seq, f_w1, f_b1, f_w2, f_b2, f_w3, f_b3, f_w4, f_freq):
    d, hid = D_MODEL, HY_FILTER_HIDDEN
    od = HY_ORDER * d
    n1, n2 = _fft_factors(seq)
    cst = _fft_consts(seq)
    z = _filter_positions(seq, n1, n2)
    zw = 2 * Z_HALF
    zero = jnp.zeros((hid, hid), F32)
    pad = jnp.zeros((Z_HALF - HY_EMB_DIM, hid), F32)
    w1 = jnp.concatenate([jnp.concatenate([f_w1, pad], axis=0), jnp.zeros((Z_HALF, hid), F32)], axis=0)
    w1 = jnp.concatenate([w1, jnp.roll(w1, Z_HALF, axis=0)], axis=1)
    blockdiag = lambda w: jnp.concatenate([jnp.concatenate([w, zero], axis=1),
                                           jnp.concatenate([zero, w], axis=1)], axis=0)
    twice = lambda v: jnp.tile(v, 2).reshape(1, 2 * hid)
    w4d = f_w4.reshape(hid, HY_ORDER, 2, d).transpose(2, 0, 1, 3).reshape(2, hid, od)
    zrows = jnp.zeros((hid, od), F32)
    w4f = jnp.concatenate([w4d[0], zrows], axis=0).astype(BF16)
    w4b = jnp.concatenate([zrows, w4d[1]], axis=0).astype(BF16)
    max_decay = math.log(HY_DECAY_TARGET) / HY_DECAY_FAST
    min_decay = math.log(HY_DECAY_TARGET) / HY_DECAY_SLOW
    deltas = np.abs(np.linspace(min_decay, max_decay, d, dtype=np.float32))
    dl = jnp.asarray(np.tile(deltas, HY_ORDER)[None])
    r = n1 // 2
    return pl.pallas_call(
        functools.partial(_filter_s1_kernel, n1=n1),
        grid=(n2 // N2_GRP,),
        in_specs=[pl.BlockSpec((N2_GRP * r, zw), lambda j: (j, 0)),
                  _const_spec((zw, 2 * hid)), _const_spec((1, 2 * hid)),
                  _const_spec((2 * hid, 2 * hid)), _const_spec((1, 2 * hid)),
                  _const_spec((2 * hid, 2 * hid)), _const_spec((1, 2 * hid)),
                  _const_spec((1, 2 * hid)),
                  _const_spec((2 * hid, od)), _const_spec((2 * hid, od)), _const_spec((1, od)),
                  _const_spec((2 * n1, n1)), _const_spec((n1, n2)), _const_spec((n1, n2))],
        out_specs=[pl.BlockSpec((2, n1 // K1_GRP, od // LANES, N2_GRP * PAIRS, LANES), lambda j: (0, 0, 0, j, 0)),
                   _const_spec((8, od))],
        out_shape=[jax.ShapeDtypeStruct((2, n1 // K1_GRP, od // LANES, n2 * PAIRS, LANES), jnp.uint32),
                   jax.ShapeDtypeStruct((8, od), F32)],
        compiler_params=_cp("arbitrary"),
        name="hyena_filter",
    )(z, w1, twice(f_b1), blockdiag(f_w2), twice(f_b2), blockdiag(f_w3), twice(f_b3), twice(f_freq),
      w4f, w4b, dl, cst["f1_full"], cst["tre"], cst["tim"])


def _fft_factors(seq):
    n = 2 * seq
    n1 = 256 if n >= 32768 else 32
    n2 = n // n1
    assert n1 * n2 == n and n2 % 16 == 0 and n1 % 32 == 0
    return n1, n2


@functools.lru_cache(maxsize=None)
def _fft_consts(seq):
    n = 2 * seq
    n1, n2 = _fft_factors(seq)
    k1 = np.arange(n1, dtype=np.float64)[:, None] + 0.5
    th1 = 2.0 * np.pi * k1 * np.arange(n1, dtype=np.float64)[None] / n1
    perm = np.concatenate([np.arange(0, n1, 2), np.arange(1, n1, 2)])
    f1_full = np.concatenate([np.cos(th1)[perm], -np.sin(th1)[perm]], axis=0)
    f1_half = f1_full[:, :n1 // 2]
    tw = 2.0 * np.pi * k1 * np.arange(n2, dtype=np.float64)[None] / n
    tre, tim = np.cos(tw), -np.sin(tw)
    h2 = n2 // 2
    th2 = 2.0 * np.pi * np.arange(h2, dtype=np.float64)[:, None] * np.arange(n2, dtype=np.float64)[None] / n2
    c2, s2 = np.cos(th2), np.sin(th2)
    m2f = np.block([[c2, s2], [-s2, c2]])
    m2i = np.block([[c2.T, -s2.T], [s2.T, c2.T]])
    thb = th1[:, :n1 // 2].T
    gi = (2.0 / n) * np.concatenate([np.cos(thb)[:, perm], -np.sin(thb)[:, perm]], axis=1)
    bf = lambda a: jnp.asarray(a.astype(np.float32)).astype(BF16)
    f32 = lambda a: jnp.asarray(a.astype(np.float32))
    grp = lambda a: a.T.reshape(n2, n1 // K1_GRP, K1_GRP).transpose(1, 0, 2)
    return dict(f1_full=bf(f1_full), f1_half=bf(f1_half), tre=f32(tre[perm]), tim=f32(tim[perm]),
                tre_grp=f32(grp(tre)), tim_grp=f32(grp(tim)), m2f=bf(m2f), m2i=bf(m2i), gi=bf(gi))


def _pick_col(tbl, idx):
    lane = lax.broadcasted_iota(I32, tbl.shape, 1)
    return jnp.sum(jnp.where(lane == idx, tbl, 0.0), axis=1, keepdims=True)


K1_GRP = 16
PAIRS = K1_GRP // 2
LANES = 128
N2_GRP = 4
MID_TC = 512


def _unpack_pair(w, half):
    bits = lax.shift_left(w, jnp.uint32(16)) if half == 0 else (w & jnp.uint32(0xFFFF0000))
    return pltpu.bitcast(bits, F32).astype(BF16)


def _pack_pair(even, odd):
    ue = pltpu.bitcast(even.astype(BF16).astype(F32), jnp.uint32)
    uo = pltpu.bitcast(odd.astype(BF16).astype(F32), jnp.uint32)
    return lax.shift_right_logical(ue, jnp.uint32(16)) | uo


def _s1_store(a, tre_ref, tim_ref, j, o_ref, n1, slot):
    are, aim = a[:n1], a[n1:]
    tre = _pick_col(tre_ref[...], j)
    tim = _pick_col(tim_ref[...], j)
    re = are * tre - aim * tim
    im = are * tim + aim * tre
    h = n1 // 2
    shape = (n1 // K1_GRP, PAIRS, a.shape[1])
    for part, val in enumerate((re, im)):
        words = _pack_pair(val[:h], val[h:]).reshape(shape)
        for ct in range(a.shape[1] // LANES):
            o_ref[part, :, ct, slot * PAIRS:(slot + 1) * PAIRS, :] = words[:, :, ct * LANES:(ct + 1) * LANES]


def _fft_s1_kernel(f1_ref, x_ref, tre_ref, tim_ref, o_ref, *, n1, tc):
    j0 = pl.program_id(0) * N2_GRP
    for slot in range(N2_GRP):
        a = _dot(f1_ref[...], x_ref[:, slot * tc:(slot + 1) * tc])
        _s1_store(a, tre_ref, tim_ref, j0 + slot, o_ref, n1, slot)


def _fft_stage1(x2d, f1, tre, tim, n1, n2, chans):
    r = x2d.shape[0]
    return pl.pallas_call(
        functools.partial(_fft_s1_kernel, n1=n1, tc=chans),
        grid=(n2 // N2_GRP,),
        in_specs=[_const_spec((2 * n1, r)),
                  pl.BlockSpec((r, N2_GRP * chans), lambda j: (0, j)),
                  _const_spec((n1, n2)), _const_spec((n1, n2))],
        out_specs=pl.BlockSpec((2, n1 // K1_GRP, chans // LANES, N2_GRP * PAIRS, LANES),
                               lambda j: (0, 0, 0, j, 0)),
        out_shape=jax.ShapeDtypeStruct((2, n1 // K1_GRP, chans // LANES, n2 * PAIRS, LANES), jnp.uint32),
        compiler_params=_cp("parallel"),
        name="fft_stage1",
    )(f1, x2d, tre, tim)


def _load_pair(b_ref, part, s, n2):
    return jnp.concatenate([b_ref[part, 0, ct, pl.ds(s, n2, stride=PAIRS), :] for ct in range(b_ref.shape[2])],
                           axis=1)


def _store_pair(o_ref, part, s, n2, words):
    for ct in range(o_ref.shape[2]):
        o_ref[part, 0, ct, pl.ds(s, n2, stride=PAIRS), :] = words[:, ct * LANES:(ct + 1) * LANES]


def _fft_s2_kernel(m2f_ref, b_ref, o_ref):
    n2 = m2f_ref.shape[0]
    for s in range(PAIRS):
        wre = _load_pair(b_ref, 0, s, n2)
        wim = _load_pair(b_ref, 1, s, n2)
        for half in range(2):
            b = jnp.concatenate([_unpack_pair(wre, half), _unpack_pair(wim, half)], axis=0)
            o_ref[2 * s + half] = _dot(m2f_ref[...], b).astype(BF16)


def _fft_stage2(b5, m2f, n1, n2, chans):
    return pl.pallas_call(
        _fft_s2_kernel,
        grid=(n1 // K1_GRP, chans // MID_TC),
        in_specs=[_const_spec((n2, 2 * n2)),
                  pl.BlockSpec((2, 1, MID_TC // LANES, n2 * PAIRS, LANES), lambda g, c: (0, g, c, 0, 0))],
        out_specs=pl.BlockSpec((K1_GRP, n2, MID_TC), lambda g, c: (g, 0, c)),
        out_shape=jax.ShapeDtypeStruct((n1, n2, chans), BF16),
        compiler_params=_cp("parallel", "parallel"),
        name="fft_stage2",
    )(m2f, b5)


def _fft_mid_kernel(m2f_ref, m2i_ref, b_ref, k_ref, inv_ref, tre_ref, tim_ref, o_ref, *, n2):
    h2 = n2 // 2
    inv = inv_ref[...]
    tre_g = tre_ref[0]
    tim_g = tim_ref[0]
    for s in range(PAIRS):
        wre = _load_pair(b_ref, 0, s, n2)
        wim = _load_pair(b_ref, 1, s, n2)
        res = []
        for half in range(2):
            kl = 2 * s + half
            b = jnp.concatenate([_unpack_pair(wre, half), _unpack_pair(wim, half)], axis=0)
            x = _dot(m2f_ref[...], b)
            kk = k_ref[kl].astype(F32) * inv
            xre, xim = x[:h2], x[h2:]
            kre, kim = kk[:h2], kk[h2:]
            y = jnp.concatenate([xre * kre - xim * kim, xre * kim + xim * kre], axis=0).astype(BF16)
            c = _dot(m2i_ref[...], y)
            cre, cim = c[:n2], c[n2:]
            tre = tre_g[:, kl:kl + 1]
            tim = tim_g[:, kl:kl + 1]
            res.append((cre * tre + cim * tim, cim * tre - cre * tim))
        _store_pair(o_ref, 0, s, n2, _pack_pair(res[0][0], res[1][0]))
        _store_pair(o_ref, 1, s, n2, _pack_pair(res[0][1], res[1][1]))


def _fft_mid(b5, kspec, inv_norm, order, cst, n1, n2):
    d = D_MODEL
    nc = d // MID_TC
    blk = pl.BlockSpec((2, 1, MID_TC // LANES, n2 * PAIRS, LANES), lambda g, c: (0, g, c, 0, 0))
    tw = pl.BlockSpec((1, n2, K1_GRP), lambda g, c: (g, 0, 0))
    return pl.pallas_call(
        functools.partial(_fft_mid_kernel, n2=n2),
        grid=(n1 // K1_GRP, nc),
        in_specs=[_const_spec((n2, 2 * n2)), _const_spec((2 * n2, n2)), blk,
                  pl.BlockSpec((K1_GRP, n2, MID_TC), lambda g, c: (g, 0, order * nc + c)),
                  pl.BlockSpec((1, MID_TC), lambda g, c: (0, order * nc + c)),
                  tw, tw],
        out_specs=blk,
        out_shape=jax.ShapeDtypeStruct((2, n1 // K1_GRP, d // LANES, n2 * PAIRS, LANES), jnp.uint32),
        compiler_params=_cp("parallel", "parallel"),
        name="fft_mid",
    )(cst["m2f"], cst["m2i"], b5, kspec, inv_norm, cst["tre_grp"], cst["tim_grp"])


def _fft_last_kernel(gi_ref, c_ref, gate_ref, z_ref, skip_ref, o_ref):
    h = c_ref.shape[1] * PAIRS
    d = c_ref.shape[2] * LANES
    for slot in range(N2_GRP):
        rows = slice(slot * PAIRS, (slot + 1) * PAIRS)
        cols = slice(slot * d, (slot + 1) * d)

        def words(part):
            return jnp.concatenate([c_ref[part, :, ct, rows, :].reshape(h, LANES)
                                    for ct in range(c_ref.shape[2])], axis=1)

        wre = words(0)
        wim = words(1)
        c = jnp.concatenate([_unpack_pair(wre, 0), _unpack_pair(wre, 1), _unpack_pair(wim, 0),
                             _unpack_pair(wim, 1)], axis=0)
        y = _dot(gi_ref[...], c)
        z = z_ref[:, cols].astype(F32)
        o_ref[:, cols] = (gate_ref[:, cols].astype(F32) * (y + skip_ref[...] * z)).astype(BF16)


def _fft_last(c5, gate2d, z2d, skip_row, gi, n1, n2):
    d = D_MODEL
    r = n1 // 2
    col = pl.BlockSpec((r, N2_GRP * d), lambda j: (0, j))
    return pl.pallas_call(
        _fft_last_kernel,
        grid=(n2 // N2_GRP,),
        in_specs=[_const_spec((r, 2 * n1)),
                  pl.BlockSpec((2, n1 // K1_GRP, d // LANES, N2_GRP * PAIRS, LANES), lambda j: (0, 0, 0, j, 0)),
                  col, col, _const_spec((1, d))],
        out_specs=col,
        out_shape=jax.ShapeDtypeStruct((r, n2 * d), BF16),
        compiler_params=_cp("parallel"),
        name="fft_last",
    )(gi, c5, gate2d, z2d, skip_row)


def _long_conv_gate(z_in, gate, skip_row, kspec, inv_norm, order, seq):
    d = D_MODEL
    n1, n2 = _fft_factors(seq)
    cst = _fft_consts(seq)
    z2d = z_in.reshape(n1 // 2, n2 * d)
    b = _fft_stage1(z2d, cst["f1_half"], cst["tre"], cst["tim"], n1, n2, d)
    c = _fft_mid(b, kspec, inv_norm, order, cst, n1, n2)
    out = _fft_last(c, gate.reshape(n1 // 2, n2 * d), z2d, skip_row, cst["gi"], n1, n2)
    return out.reshape(seq, d)


def _filter_spectrum(seq, fparams):
    d = D_MODEL
    od = HY_ORDER * d
    n1, n2 = _fft_factors(seq)
    cst = _fft_consts(seq)
    b, asum = _filter_stage1(seq, *fparams)
    kspec = _fft_stage2(b, cst["m2f"], n1, n2, od)
    return kspec, asum


def _proj_res_kernel(a_ref, w_ref, b_ref, g_ref, x_ref, o_ref):
    y = _dot(a_ref[...], w_ref[...]) + b_ref[...]
    o_ref[...] = x_ref[...] + g_ref[...] * y


def _proj_res(a, w_bf, b_row, gate_row, xres):
    n, d = xres.shape
    tm = min(ROW_TILE, n)
    row = pl.BlockSpec((tm, d), lambda i: (i, 0))
    return pl.pallas_call(
        _proj_res_kernel,
        grid=(n // tm,),
        in_specs=[row, _const_spec((d, d)), _const_spec((1, d)), _const_spec((1, d)), row],
        out_specs=row,
        out_shape=jax.ShapeDtypeStruct((n, d), F32),
        compiler_params=_cp("parallel"),
        name="proj_residual",
    )(a, w_bf, b_row, gate_row, xres)


def _ffn_in_kernel(x_ref, mod_ref, wt_ref, h_ref, aff_ref):
    h = _norm_mod(x_ref[...], mod_ref[...])
    hi = h.astype(BF16)
    lo = (h - hi.astype(F32)).astype(BF16)
    wt = wt_ref[...]
    whi = wt.astype(BF16)
    wlo = (wt - whi.astype(F32)).astype(BF16)
    logits = _dot_nt(whi, hi) + (_dot_nt(whi, lo) + _dot_nt(wlo, hi))
    m = jnp.max(logits, axis=0, keepdims=True)
    p = jnp.exp(logits - m)
    aff_ref[...] = p / jnp.sum(p, axis=0, keepdims=True)
    h_ref[...] = hi


def _ffn_in(x, mod, w_router):
    n, d = x.shape
    e = N_EXPERTS
    tm = min(ROW_TILE, n)
    return pl.pallas_call(
        _ffn_in_kernel,
        grid=(n // tm,),
        in_specs=[pl.BlockSpec((tm, d), lambda i: (i, 0)), _const_spec((8, d)), _const_spec((e, d))],
        out_specs=[pl.BlockSpec((tm, d), lambda i: (i, 0)), pl.BlockSpec((e, tm), lambda i: (0, i))],
        out_shape=[jax.ShapeDtypeStruct((n, d), BF16), jax.ShapeDtypeStruct((e, n), F32)],
        compiler_params=_cp("parallel"),
        name="moe_router",
    )(x, mod, w_router.T)


def _select_kernel(a_ref, pos_ref, s0_ref, sel_ref, *, cap, nblk):
    e = N_EXPERTS
    bits = pltpu.bitcast(a_ref[...], I32)

    def bisect(i, thr):
        cand = thr | jnp.left_shift(jnp.int32(1), 30 - i)
        cnt = jnp.sum(jnp.where(bits >= cand, 1.0, 0.0), axis=1, keepdims=True)
        return jnp.where(cnt >= cap, cand, thr)

    thr = lax.fori_loop(0, 31, bisect, jnp.zeros((e, 1), I32))
    n_gt = jnp.sum(jnp.where(bits > thr, 1.0, 0.0), axis=1, keepdims=True)
    need = cap - n_gt
    r = lax.broadcasted_iota(I32, (TOK_BLK, TOK_BLK), 0)
    c = lax.broadcasted_iota(I32, (TOK_BLK, TOK_BLK), 1)
    upper = jnp.where(r < c, 1.0, 0.0).astype(BF16)

    def pass1(j, carry):
        sl = pl.ds(pl.multiple_of(j * TOK_BLK, TOK_BLK), TOK_BLK)
        bj = pltpu.bitcast(a_ref[:, sl], I32)
        eq = jnp.where(bj == thr, 1.0, 0.0)
        rank = _dot(eq.astype(BF16), upper) + carry
        keep = jnp.logical_or(bj > thr, jnp.logical_and(bj == thr, rank < need))
        sel_ref[:, sl] = jnp.where(keep, 1.0, 0.0)
        return carry + jnp.sum(eq, axis=1, keepdims=True)

    lax.fori_loop(0, nblk, pass1, jnp.zeros((e, 1), F32))

    def pass2(j, carry):
        sl = pl.ds(pl.multiple_of(j * TOK_BLK, TOK_BLK), TOK_BLK)
        s = sel_ref[:, sl]
        slot = _dot(s.astype(BF16), upper) + carry
        pos_ref[:, sl] = jnp.where(s > 0.5, slot, -1.0).astype(I32)
        s0_ref[j] = jnp.broadcast_to(carry, (e, 128)).astype(I32)
        return carry + jnp.sum(s, axis=1, keepdims=True)

    total = lax.fori_loop(0, nblk, pass2, jnp.zeros((e, 1), F32))
    s0_ref[nblk] = jnp.broadcast_to(total, (e, 128)).astype(I32)


def _select(aff_t, cap):
    e, n = aff_t.shape
    nblk = n // TOK_BLK
    return pl.pallas_call(
        functools.partial(_select_kernel, cap=cap, nblk=nblk),
        out_shape=[jax.ShapeDtypeStruct((e, n), I32), jax.ShapeDtypeStruct((nblk + 1, e, 128), I32)],
        scratch_shapes=[pltpu.VMEM((e, n), F32)],
        compiler_params=pltpu.CompilerParams(vmem_limit_bytes=VMEM_LIMIT),
        name="moe_select",
    )(aff_t)


def _block_windows(s0_ref, blk, e_idx):
    s0 = s0_ref[blk * N_EXPERTS + e_idx]
    s1 = s0_ref[(blk + 1) * N_EXPERTS + e_idx]
    start = lax.shift_left(lax.shift_right_logical(s0, 4), 4)
    nwin = jnp.where(s1 > s0, lax.shift_right_logical(s1 - start + (WIN - 1), WIN_SHIFT), 0)
    return start, nwin


def _token_block(i):
    start = i * TOK_BLK
    return pl.ds(start if isinstance(i, int) else pl.multiple_of(start, TOK_BLK), TOK_BLK)


def _gather_kernel(s0_ref, pos_ref, h_ref, xe_ref, *, sub, nchunk):
    e_idx = pl.program_id(0)
    xe_ref[...] = jnp.zeros_like(xe_ref)
    rows = lax.broadcasted_iota(I32, (WIN, TOK_BLK), 0)

    def window(blk, base):
        tok = _token_block(blk)
        base = pl.multiple_of(base, ROW_ALIGN)
        onehot = jnp.where(rows == pos_ref[0, :, tok] - base, 1.0, 0.0).astype(BF16)
        got = _dot(onehot, h_ref[tok, :]).astype(BF16)
        win = pl.ds(base, WIN)
        xe_ref[0, win, :] = xe_ref[0, win, :] + got

    def chunk(c, carry):
        plan = [_block_windows(s0_ref, c * sub + i, e_idx) for i in range(sub)]
        single = functools.reduce(jnp.logical_and, [nwin <= 1 for _, nwin in plan])

        @pl.when(single)
        def _():
            for i, (start, _) in enumerate(plan):
                window(c * sub + i, start)

        @pl.when(jnp.logical_not(single))
        def _():
            def body(i, carry2):
                start, nwin = _block_windows(s0_ref, c * sub + i, e_idx)
                return lax.fori_loop(0, nwin, lambda w, cc: (window(c * sub + i, start + w * WIN), cc)[1], carry2)

            lax.fori_loop(0, sub, body, 0)

        return carry

    lax.fori_loop(0, nchunk, chunk, 0)


def _gather(s0_flat, pos, h, cap_pad):
    e, n = pos.shape
    d = h.shape[1]
    sub = min(8, n // TOK_BLK)
    return pl.pallas_call(
        functools.partial(_gather_kernel, sub=sub, nchunk=n // (sub * TOK_BLK)),
        grid_spec=pltpu.PrefetchScalarGridSpec(
            num_scalar_prefetch=1,
            grid=(e,),
            in_specs=[pl.BlockSpec((1, 1, n), lambda ei, s0: (ei, 0, 0)),
                      pl.BlockSpec((n, d), lambda ei, s0: (0, 0), pipeline_mode=pl.Buffered(1))],
            out_specs=pl.BlockSpec((1, cap_pad, d), lambda ei, s0: (ei, 0, 0)),
        ),
        out_shape=jax.ShapeDtypeStruct((e, cap_pad, d), BF16),
        compiler_params=_cp("parallel"),
        name="moe_gather",
    )(s0_flat, pos.reshape(e, 1, n), h)


def _expert_kernel(x_ref, wg_ref, wu_ref, wd_ref, y_ref, wg_scr, wu_scr, wd_scr, *, n_real):
    j = pl.program_id(1)

    @pl.when(j == 0)
    def _():
        wg_scr[...] = wg_ref[0].astype(BF16)
        wu_scr[...] = wu_ref[0].astype(BF16)
        wd_scr[...] = wd_ref[0].astype(BF16)

    @pl.when(j < n_real)
    def _():
        x = x_ref[0]
        g = _dot(x, wg_scr[...])
        u = _dot(x, wu_scr[...])
        a = (g / (1.0 + jnp.exp(-g))) * u
        y_ref[0] = _dot(a.astype(BF16), wd_scr[...]).astype(BF16)

    @pl.when(j >= n_real)
    def _():
        y_ref[0] = jnp.zeros_like(y_ref[0])


def _experts(xe, wg, wu, wd, layer, cap):
    e, cap_pad, d = xe.shape
    f = wg.shape[3]
    tm = cap_pad // 3 if cap_pad % (3 * ROW_ALIGN) == 0 and cap >= 3 * TOK_BLK else cap_pad
    tile = pl.BlockSpec((1, tm, d), lambda ei, j: (ei, j, 0))
    return pl.pallas_call(
        functools.partial(_expert_kernel, n_real=pl.cdiv(cap, tm)),
        grid=(e, cap_pad // tm),
        in_specs=[tile,
                  pl.BlockSpec((None, 1, d, f), lambda ei, j: (layer, ei, 0, 0)),
                  pl.BlockSpec((None, 1, d, f), lambda ei, j: (layer, ei, 0, 0)),
                  pl.BlockSpec((None, 1, f, d), lambda ei, j: (layer, ei, 0, 0))],
        out_specs=tile,
        out_shape=jax.ShapeDtypeStruct((e, cap_pad, d), BF16),
        scratch_shapes=[pltpu.VMEM((d, f), BF16), pltpu.VMEM((d, f), BF16), pltpu.VMEM((f, d), BF16)],
        compiler_params=_cp("parallel", "arbitrary"),
        name="moe_experts",
    )(xe, wg, wu, wd)


COMBINE_COLS = 512
COMBINE_BLKS = 2


def _combine_kernel(s0_ref, pos_ref, aff_ref, y_ref, x_ref, gate_ref, o_ref, acc_ref, *, nsub):
    t = pl.program_id(1)
    rows = lax.broadcasted_iota(I32, (WIN, TOK_BLK), 0)

    def selector(i, e, base):
        tok = _token_block(i)
        rel = pos_ref[e:e + 1, tok] - base
        return jnp.where(rows == rel, aff_ref[e:e + 1, tok], 0.0).astype(BF16)

    def contract(sel, ywin):
        return lax.dot_general(sel, ywin, (((0,), (0,)), ((), ())), preferred_element_type=F32)

    for i in range(nsub):
        plan = [_block_windows(s0_ref, t * nsub + i, e) for e in range(N_EXPERTS)]
        single = functools.reduce(jnp.logical_and, [nwin <= 1 for _, nwin in plan])
        tok = _token_block(i)

        @pl.when(single)
        def _():
            bases = [pl.multiple_of(start, ROW_ALIGN) for start, _ in plan]
            sel = jnp.concatenate([selector(i, e, b) for e, b in enumerate(bases)], axis=0)
            ywin = jnp.concatenate([y_ref[e, pl.ds(b, WIN), :] for e, b in enumerate(bases)], axis=0)
            o_ref[tok, :] = x_ref[tok, :] + gate_ref[...] * contract(sel, ywin)

        @pl.when(jnp.logical_not(single))
        def _():
            acc_ref[...] = jnp.zeros_like(acc_ref)
            for e, (start, nwin) in enumerate(plan):
                def window(w, carry):
                    base = pl.multiple_of(start + w * WIN, ROW_ALIGN)
                    acc_ref[...] += contract(selector(i, e, base), y_ref[e, pl.ds(base, WIN), :])
                    return carry

                lax.fori_loop(0, nwin, window, 0)
            o_ref[tok, :] = x_ref[tok, :] + gate_ref[...] * acc_ref[...]


def _combine(s0_flat, pos, aff_t, y, xres, gate_row):
    n, d = xres.shape
    e, cap_pad, _ = y.shape
    nsub = min(COMBINE_BLKS, n // TOK_BLK)
    rows = nsub * TOK_BLK
    cq = COMBINE_COLS
    tile = pl.BlockSpec((rows, cq), lambda q, t, s0: (t, q))
    return pl.pallas_call(
        functools.partial(_combine_kernel, nsub=nsub),
        grid_spec=pltpu.PrefetchScalarGridSpec(
            num_scalar_prefetch=1,
            grid=(d // cq, n // rows),
            in_specs=[pl.BlockSpec((e, rows), lambda q, t, s0: (0, t)),
                      pl.BlockSpec((e, rows), lambda q, t, s0: (0, t)),
                      pl.BlockSpec((e, cap_pad, cq), lambda q, t, s0: (0, 0, q), pipeline_mode=pl.Buffered(1)),
                      tile,
                      pl.BlockSpec((1, cq), lambda q, t, s0: (0, q))],
            out_specs=tile,
            scratch_shapes=[pltpu.VMEM((TOK_BLK, cq), F32)],
        ),
        out_shape=jax.ShapeDtypeStruct((n, d), F32),
        compiler_params=_cp("parallel", "arbitrary"),
        name="moe_combine",
    )(s0_flat, pos, aff_t, y, xres, gate_row)


def _moe_block(x, mod, gate_row, w_router, wg, wu, wd, layer):
    n = x.shape[0]
    cap = EC_CAPACITY * n // N_EXPERTS
    cap_pad = cap + TOK_BLK
    h, aff_t = _ffn_in(x, mod, w_router)
    pos, s0 = _select(aff_t, cap)
    s0_flat = s0[:, :, 0].reshape(-1)
    xe = _gather(s0_flat, pos, h, cap_pad)
    y = _experts(xe, wg, wu, wd, layer, cap)
    return _combine(s0_flat, pos, aff_t, y, x, gate_row)


QK_SCALE = (DA_HEAD_DIM ** -0.5) * math.log2(math.e)


def _split_bf16(x):
    hi = x.astype(BF16)
    return hi, (x - hi.astype(F32)).astype(BF16)


def _group_rms(u, gsel_ref, gain, eps):
    g = gsel_ref[...]
    hi, lo = _split_bf16(u * u)
    r = lax.rsqrt((_dot(hi, g) + _dot(lo, g)) * (1.0 / DA_HEAD_DIM) + eps)
    rhi, rlo = _split_bf16(r)
    return u * (_dot_nt(rhi, g) + _dot_nt(rlo, g)) * gain


def _rope(u, cos, sin_signed):
    d = u.shape[1]
    half = ROPE_AXIS_DIM // 2
    lane = lax.broadcasted_iota(I32, u.shape, 1)
    first = (lane & half) == 0
    swapped = jnp.where(first, pltpu.roll(u, d - half, 1), pltpu.roll(u, half, 1))
    return u * cos + swapped * sin_signed


def _qkv_kernel(*refs, rope):
    x_ref, mod_ref, w_ref, gsum_ref, qn_ref, kn_ref = refs[:6]
    q_ref, k_ref, v_ref = refs[-3:]
    if rope:
        cos_ref, sin_ref = refs[6:8]
    d = D_MODEL
    h = _norm_mod(x_ref[...], mod_ref[...]).astype(BF16)
    if rope:
        reps = d // cos_ref.shape[1]
        cos = _lane_tile(cos_ref[...], reps)
        sin = _lane_tile(sin_ref[...], reps)
    for part, (o_ref, gain_ref) in enumerate(((q_ref, qn_ref), (k_ref, kn_ref))):
        u = _dot(h, w_ref[:, part * d:(part + 1) * d])
        u = _group_rms(u, gsum_ref, gain_ref[...], NORM_EPS)
        if rope:
            u = _rope(u, cos, sin)
        if part == 0:
            u = u * QK_SCALE
        o_ref[...] = u.astype(BF16)
    v_ref[...] = _dot(h, w_ref[:, 2 * d:]).astype(BF16)


def _qkv(x, mod, w_bf, gsum, qn_row, kn_row, cos=None, sin=None, kv_rows=None, kv_offset=0, kv_into=None):
    n, d = x.shape
    tm = min(QKV_TILE, n)
    kv_rows = n if kv_rows is None else kv_rows
    off = kv_offset // tm
    assert kv_offset % tm == 0
    rope = cos is not None
    row = pl.BlockSpec((tm, d), lambda i: (i, 0))
    kv_row = pl.BlockSpec((tm, d), lambda i: (i + off, 0))
    in_specs = [row, _const_spec((8, d)), _const_spec((d, 3 * d)), _const_spec(gsum.shape),
                _const_spec((1, d)), _const_spec((1, d))]
    args = [x, mod, w_bf, gsum, qn_row, kn_row]
    if rope:
        tw = cos.shape[1]
        in_specs += [pl.BlockSpec((tm, tw), lambda i: (i, 0)), pl.BlockSpec((tm, tw), lambda i: (i, 0))]
        args += [cos, sin]
    aliases = {}
    if kv_into is not None:
        aliases = {len(args): 1, len(args) + 1: 2}
        in_specs += [pl.BlockSpec(memory_space=pl.ANY), pl.BlockSpec(memory_space=pl.ANY)]
        args += list(kv_into)
    kv = jax.ShapeDtypeStruct((kv_rows, d), BF16)
    return pl.pallas_call(
        functools.partial(_qkv_kernel, rope=rope),
        grid=(n // tm,),
        in_specs=in_specs,
        out_specs=[row, kv_row, kv_row],
        out_shape=[jax.ShapeDtypeStruct((n, d), BF16), kv, kv],
        input_output_aliases=aliases,
        compiler_params=_cp("parallel"),
        name="attn_qkv",
    )(*args)


NEG_BIG = -1e30
HEAD_UNROLL = 8


def _attn_kernel(q_ref, k_ref, v_ref, lam_ref, sub_ref, o_ref, m_scr, acc_scr, sa_scr, sb_scr, ma_scr, mb_scr,
                 pa_scr, pb_scr, vprev_scr, aa_scr, ab_scr, *,
                 lam_init):
    j = pl.program_id(1)
    nj = pl.num_programs(1)
    hd, vd = DA_HEAD_DIM, DA_V_DIM
    tk = k_ref.shape[0]

    last = DA_HEADS - 1

    @pl.when(j == 0)
    def _():
        m_scr[...] = jnp.full_like(m_scr, NEG_BIG)
        acc_scr[...] = jnp.zeros_like(acc_scr)
        pa_scr[...] = jnp.zeros_like(pa_scr)
        aa_scr[...] = jnp.ones_like(aa_scr)
        sb_scr[...] = jnp.full_like(sb_scr, 2.0 * NEG_BIG)
        mb_scr[...] = jnp.full_like(mb_scr, 2.0 * NEG_BIG)
        vprev_scr[...] = jnp.zeros_like(vprev_scr)

    lane = lax.broadcasted_iota(I32, (q_ref.shape[0], vd), 1)
    ones_col = jnp.where(lax.broadcasted_iota(I32, (tk, vd), 1) == 0, 1.0, 0.0).astype(BF16)

    def head_cols(h):
        return pl.ds(pl.multiple_of(h * vd, vd), vd)

    def scores(h, comp, s_ref, mx_ref):
        qb = q_ref[:, head_cols(h)]
        in_comp = (lane >= comp * hd) & (lane < (comp + 1) * hd)
        qm = jnp.where(in_comp, qb, jnp.zeros_like(qb))
        s = _dot_nt(qm, k_ref[:, head_cols(h)])
        s_ref[...] = s
        mx_ref[...] = jnp.broadcast_to(jnp.max(s, axis=1, keepdims=True), mx_ref.shape)

    def softmax_step(h, comp, s_ref, mx_ref, p_ref, alpha_ref):
        idx = 2 * h + comp
        m_prev = m_scr[idx]
        m_new = jnp.maximum(m_prev, mx_ref[...])
        alpha_ref[...] = jnp.exp2(m_prev - m_new)
        p_ref[...] = jnp.exp2(s_ref[...] - m_new[:, 0:1]).astype(BF16)
        m_scr[idx] = m_new

    def pv_step(h, comp, p_ref, alpha_ref):
        idx = 2 * h + comp
        pv = _dot(p_ref[...], jnp.concatenate([vprev_scr[...], ones_col], axis=1))
        acc_scr[idx] = _lane_tile(alpha_ref[...], 2) * acc_scr[idx] + pv

    def head(h, carry):
        hp = (h + last) & last
        scores(h, 0, sa_scr, ma_scr)
        pv_step(hp, 0, pa_scr, aa_scr)
        softmax_step(hp, 1, sb_scr, mb_scr, pb_scr, ab_scr)
        scores(h, 1, sb_scr, mb_scr)
        pv_step(hp, 1, pb_scr, ab_scr)
        softmax_step(h, 0, sa_scr, ma_scr, pa_scr, aa_scr)
        vprev_scr[...] = v_ref[:, head_cols(h)]
        return carry

    def head_group(g, carry):
        for u in range(HEAD_UNROLL):
            carry = head(HEAD_UNROLL * g + u, carry)
        return carry

    lax.fori_loop(0, DA_HEADS // HEAD_UNROLL, head_group, 0)

    @pl.when(j == nj - 1)
    def _():
        pv_step(last, 0, pa_scr, aa_scr)
        softmax_step(last, 1, sb_scr, mb_scr, pb_scr, ab_scr)
        pv_step(last, 1, pb_scr, ab_scr)
        lp = lam_ref[...]
        lam = (jnp.exp(jnp.sum(lp[0:1] * lp[1:2], axis=1, keepdims=True))
               - jnp.exp(jnp.sum(lp[2:3] * lp[3:4], axis=1, keepdims=True)) + lam_init)
        for h in range(DA_HEADS):
            a0 = acc_scr[2 * h]
            a1 = acc_scr[2 * h + 1]
            o = a0[:, :vd] / a0[:, vd:vd + 1] - lam * (a1[:, :vd] / a1[:, vd:vd + 1])
            ms = jnp.mean(o * o, axis=1, keepdims=True)
            o = o * lax.rsqrt(ms + SUBLN_EPS) * (sub_ref[...] * (1.0 - lam_init))
            o_ref[:, h * vd:(h + 1) * vd] = o.astype(BF16)


def _attention(q, k_all, v_all, lam_rows, subln_row, lam_init):
    n, d = q.shape
    nk = k_all.shape[0]
    tq = min(512, n)
    tk = 1280 if nk % 1280 == 0 else 256
    assert nk % tk == 0
    nc = 2 * DA_HEADS
    return pl.pallas_call(
        functools.partial(_attn_kernel, lam_init=lam_init),
        grid=(n // tq, nk // tk),
        in_specs=[pl.BlockSpec((tq, d), lambda i, j: (i, 0)),
                  pl.BlockSpec((tk, d), lambda i, j: (j, 0)),
                  pl.BlockSpec((tk, d), lambda i, j: (j, 0)),
                  _const_spec((8, DA_HEAD_DIM)), _const_spec((1, DA_V_DIM))],
        out_specs=pl.BlockSpec((tq, d), lambda i, j: (i, 0)),
        out_shape=jax.ShapeDtypeStruct((n, d), BF16),
        scratch_shapes=[pltpu.VMEM((nc, tq, DA_V_DIM), F32), pltpu.VMEM((nc, tq, 2 * DA_V_DIM), F32),
                        pltpu.VMEM((tq, tk), F32), pltpu.VMEM((tq, tk), F32),
                        pltpu.VMEM((tq, DA_V_DIM), F32), pltpu.VMEM((tq, DA_V_DIM), F32),
                        pltpu.VMEM((tq, tk), BF16), pltpu.VMEM((tq, tk), BF16),
                        pltpu.VMEM((tk, DA_V_DIM), BF16),
                        pltpu.VMEM((tq, DA_V_DIM), F32), pltpu.VMEM((tq, DA_V_DIM), F32)],
        compiler_params=_cp("parallel", "arbitrary"),
        name="diff_attention",
    )(q, k_all, v_all, lam_rows, subln_row)


def _rope_tables(n):
    lane = np.arange(2 * DA_HEAD_DIM) % DA_HEAD_DIM
    nf = ROPE_AXIS_DIM // 2
    inv = ROPE_THETA ** (-np.arange(0, ROPE_AXIS_DIM, 2, dtype=np.float32) / ROPE_AXIS_DIM)
    by_row = jnp.asarray(np.where(lane < ROPE_AXIS_DIM, inv[lane % nf], 0.0).astype(np.float32))
    by_col = jnp.asarray(np.where(lane >= ROPE_AXIS_DIM, inv[lane % nf], 0.0).astype(np.float32))
    sign = jnp.asarray(np.where(lane % ROPE_AXIS_DIM < nf, -1.0, 1.0).astype(np.float32))
    t = jnp.arange(n, dtype=I32)
    row = (t // GRID_W).astype(F32)[:, None]
    col = (t % GRID_W).astype(F32)[:, None]
    ang = row * by_row[None] + col * by_col[None]
    return jnp.cos(ang), jnp.sin(ang) * sign[None]


def _hyena_layer(x, mod, gate_row, kspec, inv_norm, w_in_bf, b_in, conv_w, conv_b, skip, w_out_bf, b_out):
    seq = x.shape[0]
    v, x1, x2 = _hy_in(x, mod, w_in_bf, b_in, conv_w, conv_b)
    z = _long_conv_gate(v, x1, skip[0:1], kspec, inv_norm, 0, seq)
    z = _long_conv_gate(z, x2, skip[1:2], kspec, inv_norm, 1, seq)
    return _proj_res(z, w_out_bf, b_out.reshape(1, -1), gate_row, x)


def kernel(x, c, ctx, c_ctx, ada_w, ada_b, norm_mix, norm_ffn, hy_w_in, hy_b_in, hy_conv_w, hy_conv_b, hy_f_w1, hy_f_b1, hy_f_w2, hy_f_b2, hy_f_w3, hy_f_b3, hy_f_w4, hy_f_freq, hy_skip, hy_w_out, hy_b_out, da_w_qkv, da_q_norm, da_k_norm, da_lam_q1, da_lam_k1, da_lam_q2, da_lam_k2, da_subln, da_w_out, moe_router, moe_w_gate, moe_w_up, moe_w_down):
    d = D_MODEL
    depth = ada_w.shape[0]
    assert x.shape[0] == 1 and x.shape[2] == d
    xs = x[0]
    cs = ctx[0]
    cond8 = jnp.concatenate([c[0:1], c_ctx[None], jnp.zeros((6, d), F32)], axis=0)
    mods = _adaln(cond8, ada_w, ada_b)

    def mod_slice(i, row, k):
        return mods[i, row, k * d:(k + 1) * d][None]

    for i in range(depth):
        last = i == depth - 1
        j = i // 2
        mix_x = _mod_rows(norm_mix[i], mods[i], 0, 0)
        mix_c = _mod_rows(norm_mix[i], mods[i], 1, 0)
        if i % 2 == 0:
            fparams = (hy_f_w1[j], hy_f_b1[j], hy_f_w2[j], hy_f_b2[j], hy_f_w3[j], hy_f_b3[j], hy_f_w4[j],
                       hy_f_freq[j])
            shared = (hy_w_in[j].astype(BF16), hy_b_in[j], hy_conv_w[j], hy_conv_b[j], hy_skip[j],
                      hy_w_out[j].astype(BF16), hy_b_out[j])
            kspec, asum = _filter_spectrum(xs.shape[0], fparams)
            inv_norm = 1.0 / (asum[0:1] + HY_FILTER_EPS)
            new_x = _hyena_layer(xs, mix_x, mod_slice(i, 0, 2), kspec, inv_norm, *shared)
            if not last:
                kspec_c, asum_c = _filter_spectrum(cs.shape[0], fparams)
                inv_c = 1.0 / (asum_c[0:1] + HY_FILTER_EPS)
                cs = _hyena_layer(cs, mix_c, mod_slice(i, 1, 2), kspec_c, inv_c, *shared)
            xs = new_x
        else:
            lam_init = 0.8 - 0.6 * math.exp(-0.3 * i)
            w_qkv = da_w_qkv[j].astype(BF16)
            gidx = np.arange(d) // DA_HEAD_DIM
            gsum = jnp.asarray((gidx[:, None] == np.arange(128)[None]).astype(np.float32)).astype(BF16)
            qn = jnp.tile(da_q_norm[j], 2 * DA_HEADS)[None]
            kn = jnp.tile(da_k_norm[j], 2 * DA_HEADS)[None]
            cos, sin = _rope_tables(xs.shape[0])
            n_lat = xs.shape[0]
            nk = n_lat + cs.shape[0]
            qx, k_all, v_all = _qkv(xs, mix_x, w_qkv, gsum, qn, kn, cos, sin, kv_rows=nk)
            qc, k_all, v_all = _qkv(cs, mix_c, w_qkv, gsum, qn, kn, kv_rows=nk, kv_offset=n_lat,
                                    kv_into=(k_all, v_all))
            lam_rows = jnp.concatenate([da_lam_q1[j][None], da_lam_k1[j][None], da_lam_q2[j][None],
                                        da_lam_k2[j][None], jnp.zeros((4, DA_HEAD_DIM), F32)], axis=0)
            w_out = da_w_out[j].astype(BF16)
            zero_b = jnp.zeros((1, d), F32)
            ox = _attention(qx, k_all, v_all, lam_rows, da_subln[j][None], lam_init)
            new_x = _proj_res(ox, w_out, zero_b, mod_slice(i, 0, 2), xs)
            if not last:
                oc = _attention(qc, k_all[n_lat:], v_all[n_lat:], lam_rows, da_subln[j][None], lam_init)
                cs = _proj_res(oc, w_out, zero_b, mod_slice(i, 1, 2), cs)
            xs = new_x
        experts = (moe_w_gate, moe_w_up, moe_w_down, i)
        if not last:
            cs = _moe_block(cs, _mod_rows(norm_ffn[i], mods[i], 1, 3), mod_slice(i, 1, 5), moe_router[i], *experts)
        xs = _moe_block(xs, _mod_rows(norm_ffn[i], mods[i], 0, 3), mod_slice(i, 0, 5), moe_router[i], *experts)
    return xs[None]
```

```python
import functools
import math

import jax
import jax.numpy as jnp
import numpy as np
from jax import lax
from jax.experimental import pallas as pl
from jax.experimental.pallas import tpu as pltpu

F32 = jnp.float32
BF16 = jnp.bfloat16
I32 = jnp.int32
HIGHEST = lax.Precision.HIGHEST

D_MODEL = 1024
N_MOD = 6
NORM_EPS = 1e-6
GRID_W = 64
HY_ORDER = 2
HY_SHORT = 3
HY_EMB_BANDS = 16
HY_EMB_DIM = 1 + 2 * HY_EMB_BANDS
HY_FILTER_HIDDEN = 64
HY_DECAY_FAST = 0.3
HY_DECAY_SLOW = 1.5
HY_DECAY_TARGET = 1e-2
HY_FILTER_EPS = 1e-6
Z_SIGN_COL = 33
DA_HEADS = 8
DA_HEAD_DIM = 64
DA_V_DIM = 128
ROPE_AXIS_DIM = 32
ROPE_THETA = 10000.0
SUBLN_EPS = 1e-5
N_EXPERTS = 16
EC_CAPACITY = 2
D_EXPERT = 1024
TOK_BLK = 256
ROW_ALIGN = 16
WIN_SHIFT = 7
WIN = 1 << WIN_SHIFT

ROW_TILE = 1024
QKV_TILE = 256
VMEM_LIMIT = 56 * 1024 * 1024


def _cp(*sem):
    return pltpu.CompilerParams(dimension_semantics=sem, vmem_limit_bytes=VMEM_LIMIT)


def _const_spec(shape):
    nd = len(shape)
    return pl.BlockSpec(shape, lambda *_: (0,) * nd)


def _dot(a, b):
    return jnp.dot(a, b, preferred_element_type=F32)


def _dot_nt(a, b):
    return lax.dot_general(a, b, (((1,), (1,)), ((), ())), preferred_element_type=F32)


def _norm_mod(x, mod, eps=NORM_EPS):
    ms = jnp.mean(x * x, axis=-1, keepdims=True)
    return x * lax.rsqrt(ms + eps) * (mod[0:1] * (1.0 + mod[1:2])) + mod[2:3]


def _lane_tile(x, reps):
    return jnp.concatenate([x] * reps, axis=1) if reps > 1 else x


def _adaln_kernel(c_ref, w_ref, b_ref, o_ref):
    c = c_ref[...]
    s = c / (1.0 + jnp.exp(-c))
    o_ref[0] = jnp.dot(s, w_ref[0], precision=HIGHEST, preferred_element_type=F32) + b_ref[0]


def _adaln(cond8, ada_w, ada_b):
    depth, d, nout = ada_w.shape
    tn = 1536
    return pl.pallas_call(
        _adaln_kernel,
        grid=(depth, nout // tn),
        in_specs=[_const_spec((8, d)),
                  pl.BlockSpec((1, d, tn), lambda l, j: (l, 0, j)),
                  pl.BlockSpec((1, 1, tn), lambda l, j: (l, 0, j))],
        out_specs=pl.BlockSpec((1, 8, tn), lambda l, j: (l, 0, j)),
        out_shape=jax.ShapeDtypeStruct((depth, 8, nout), F32),
        compiler_params=_cp("parallel", "parallel"),
        name="adaln",
    )(cond8, ada_w, ada_b.reshape(depth, 1, nout))


def _mod_rows(norm_g, mods, row, k0):
    d = D_MODEL
    shift = mods[row, k0 * d:(k0 + 1) * d]
    scale = mods[row, (k0 + 1) * d:(k0 + 2) * d]
    z = jnp.zeros((5, d), F32)
    return jnp.concatenate([norm_g[None], scale[None], shift[None], z], axis=0)


HALO = 16


def _hy_in_kernel(x_ref, xp_ref, xn_ref, mod_ref, w_ref, b_ref, cw_ref, v_ref, x1_ref, x2_ref, *, tm, n_rows):
    i = pl.program_id(0)
    mod = mod_ref[...]
    hm = _norm_mod(x_ref[...], mod).astype(BF16)
    hp = _norm_mod(xp_ref[...], mod).astype(BF16)
    hn = _norm_mod(xn_ref[...], mod).astype(BF16)
    hcat = jnp.concatenate([hp, hm, hn], axis=0)
    row = lax.broadcasted_iota(I32, (tm + 2 * HALO, 1), 0) + (i * tm - HALO)
    valid = jnp.logical_and(row >= 0, row < n_rows)
    d = D_MODEL
    for c, o_ref in enumerate((v_ref, x1_ref, x2_ref)):
        u = _dot(hcat, w_ref[:, c * d:(c + 1) * d]) + b_ref[:, c * d:(c + 1) * d]
        u = jnp.where(valid, u, 0.0)
        cw = cw_ref[:, c * d:(c + 1) * d]
        y = (cw[3:4] + cw[0:1] * u[HALO - 1:HALO - 1 + tm] + cw[1:2] * u[HALO:HALO + tm]
             + cw[2:3] * u[HALO + 1:HALO + 1 + tm])
        o_ref[...] = y.astype(BF16)


def _hy_in(x, mod, w_bf, b_in, conv_w, conv_b):
    n, d = x.shape
    tm = min(ROW_TILE, n)
    nh = n // HALO
    cw = jnp.concatenate([conv_w, conv_b[None], jnp.zeros((4, 3 * d), F32)], axis=0)
    out = jax.ShapeDtypeStruct((n, d), BF16)
    row_spec = pl.BlockSpec((tm, d), lambda i: (i, 0))
    return pl.pallas_call(
        functools.partial(_hy_in_kernel, tm=tm, n_rows=n),
        grid=(n // tm,),
        in_specs=[row_spec,
                  pl.BlockSpec((HALO, d), lambda i: (jnp.maximum(i * (tm // HALO) - 1, 0), 0)),
                  pl.BlockSpec((HALO, d), lambda i: (jnp.minimum((i + 1) * (tm // HALO), nh - 1), 0)),
                  _const_spec((8, d)), _const_spec((d, 3 * d)), _const_spec((1, 3 * d)),
                  _const_spec((8, 3 * d))],
        out_specs=[row_spec, row_spec, row_spec],
        out_shape=[out, out, out],
        compiler_params=_cp("parallel"),
        name="hyena_in",
    )(x, x, x, mod, w_bf, b_in.reshape(1, 3 * d), cw)


Z_HALF = 64


def _filter_s1_kernel(z_ref, w1_ref, b1_ref, w2_ref, b2_ref, w3_ref, b3_ref, fr_ref, w4f_ref, w4b_ref, dl_ref,
                      f1_ref, tre_ref, tim_ref, o_ref, asum_ref, *, n1):
    j0 = pl.program_id(0) * N2_GRP
    fr = fr_ref[...]
    dl = dl_ref[...]
    r = n1 // 2

    def lin(a, w_ref, b_ref):
        return jnp.dot(a, w_ref[...], precision=HIGHEST, preferred_element_type=F32) + b_ref[...]

    @pl.when(j0 == 0)
    def _():
        asum_ref[...] = jnp.zeros_like(asum_ref)

    for slot in range(N2_GRP):
        z = z_ref[slot * r:(slot + 1) * r, :]
        hid = jnp.sin(fr * lin(z, w1_ref, b1_ref))
        hid = jnp.sin(fr * lin(hid, w2_ref, b2_ref))
        hid = jnp.sin(fr * lin(hid, w3_ref, b3_ref)).astype(BF16)

        def taps(w4_ref, col):
            t = z[:, col:col + 1]
            sgn = z[:, col + Z_SIGN_COL:col + Z_SIGN_COL + 1]
            return _dot(hid, w4_ref[...]) * jnp.exp(-t * dl) * sgn

        k = jnp.concatenate([taps(w4f_ref, 0), taps(w4b_ref, Z_HALF)], axis=0)
        asum_ref[0:1, :] += jnp.sum(jnp.abs(k), axis=0, keepdims=True)
        _s1_store(_dot(f1_ref[...], k.astype(BF16)), tre_ref, tim_ref, j0 + slot, o_ref, n1, slot)


def _filter_positions(seq, n1, n2):
    i = np.arange(n1 // 2)[None, :]
    j = np.arange(n2)[:, None]
    bands = np.linspace(1e-4, HY_EMB_BANDS - 1, HY_EMB_BANDS)
    z = np.zeros((n2, n1 // 2, 2 * Z_HALF), np.float64)
    for col, r in ((0, i * n2 + j), (Z_HALF, i * n2 + j + seq)):
        pos = np.minimum(np.where(r < seq, r, 2 * seq - r), seq - 1).astype(np.float64)
        w = 2.0 * np.pi * pos / seq
        z[:, :, col] = pos / (seq - 1)
        z[:, :, col + 1:col + 1 + HY_EMB_BANDS] = np.cos(w[..., None] * bands)
        z[:, :, col + 1 + HY_EMB_BANDS:col + HY_EMB_DIM] = -np.sin(w[..., None] * bands)
        z[:, :, col + Z_SIGN_COL] = np.where(r < seq, 1.0, np.where(r == seq, 0.0, -1.0))
    return jnp.asarray(z.reshape(n2 * (n1 // 2), 2 * Z_HALF).astype(np.float32))


def _filter_stage1(seq, f_w1, f_b1, f_w2, f_b2, f_w3, f_b3, f_w4, f_freq):
    d, hid = D_MODEL, HY_FILTER_HIDDEN
    od = HY_ORDER * d
    n1, n2 = _fft_factors(seq)
    cst = _fft_consts(seq)
    z = _filter_positions(seq, n1, n2)
    zw = 2 * Z_HALF
    zero = jnp.zeros((hid, hid), F32)
    pad = jnp.zeros((Z_HALF - HY_EMB_DIM, hid), F32)
    w1 = jnp.concatenate([jnp.concatenate([f_w1, pad], axis=0), jnp.zeros((Z_HALF, hid), F32)], axis=0)
    w1 = jnp.concatenate([w1, jnp.roll(w1, Z_HALF, axis=0)], axis=1)
    blockdiag = lambda w: jnp.concatenate([jnp.concatenate([w, zero], axis=1),
                                           jnp.concatenate([zero, w], axis=1)], axis=0)
    twice = lambda v: jnp.tile(v, 2).reshape(1, 2 * hid)
    w4d = f_w4.reshape(hid, HY_ORDER, 2, d).transpose(2, 0, 1, 3).reshape(2, hid, od)
    zrows = jnp.zeros((hid, od), F32)
    w4f = jnp.concatenate([w4d[0], zrows], axis=0).astype(BF16)
    w4b = jnp.concatenate([zrows, w4d[1]], axis=0).astype(BF16)
    max_decay = math.log(HY_DECAY_TARGET) / HY_DECAY_FAST
    min_decay = math.log(HY_DECAY_TARGET) / HY_DECAY_SLOW
    deltas = np.abs(np.linspace(min_decay, max_decay, d, dtype=np.float32))
    dl = jnp.asarray(np.tile(deltas, HY_ORDER)[None])
    r = n1 // 2
    return pl.pallas_call(
        functools.partial(_filter_s1_kernel, n1=n1),
        grid=(n2 // N2_GRP,),
        in_specs=[pl.BlockSpec((N2_GRP * r, zw), lambda j: (j, 0)),
                  _const_spec((zw, 2 * hid)), _const_spec((1, 2 * hid)),
                  _const_spec((2 * hid, 2 * hid)), _const_spec((1, 2 * hid)),
                  _const_spec((2 * hid, 2 * hid)), _const_spec((1, 2 * hid)),
                  _const_spec((1, 2 * hid)),
                  _const_spec((2 * hid, od)), _const_spec((2 * hid, od)), _const_spec((1, od)),
                  _const_spec((2 * n1, n1)), _const_spec((n1, n2)), _const_spec((n1, n2))],
        out_specs=[pl.BlockSpec((2, n1 // K1_GRP, od // LANES, N2_GRP * PAIRS, LANES), lambda j: (0, 0, 0, j, 0)),
                   _const_spec((8, od))],
        out_shape=[jax.ShapeDtypeStruct((2, n1 // K1_GRP, od // LANES, n2 * PAIRS, LANES), jnp.uint32),
                   jax.ShapeDtypeStruct((8, od), F32)],
        compiler_params=_cp("arbitrary"),
        name="hyena_filter",
    )(z, w1, twice(f_b1), blockdiag(f_w2), twice(f_b2), blockdiag(f_w3), twice(f_b3), twice(f_freq),
      w4f, w4b, dl, cst["f1_full"], cst["tre"], cst["tim"])


def _fft_factors(seq):
    n = 2 * seq
    n1 = 256 if n >= 32768 else 32
    n2 = n // n1
    assert n1 * n2 == n and n2 % 16 == 0 and n1 % 32 == 0
    return n1, n2


@functools.lru_cache(maxsize=None)
def _fft_consts(seq):
    n = 2 * seq
    n1, n2 = _fft_factors(seq)
    k1 = np.arange(n1, dtype=np.float64)[:, None] + 0.5
    th1 = 2.0 * np.pi * k1 * np.arange(n1, dtype=np.float64)[None] / n1
    perm = np.concatenate([np.arange(0, n1, 2), np.arange(1, n1, 2)])
    f1_full = np.concatenate([np.cos(th1)[perm], -np.sin(th1)[perm]], axis=0)
    f1_half = f1_full[:, :n1 // 2]
    tw = 2.0 * np.pi * k1 * np.arange(n2, dtype=np.float64)[None] / n
    tre, tim = np.cos(tw), -np.sin(tw)
    h2 = n2 // 2
    th2 = 2.0 * np.pi * np.arange(h2, dtype=np.float64)[:, None] * np.arange(n2, dtype=np.float64)[None] / n2
    c2, s2 = np.cos(th2), np.sin(th2)
    m2f = np.block([[c2, s2], [-s2, c2]])
    m2i = np.block([[c2.T, -s2.T], [s2.T, c2.T]])
    thb = th1[:, :n1 // 2].T
    gi = (2.0 / n) * np.concatenate([np.cos(thb)[:, perm], -np.sin(thb)[:, perm]], axis=1)
    bf = lambda a: jnp.asarray(a.astype(np.float32)).astype(BF16)
    f32 = lambda a: jnp.asarray(a.astype(np.float32))
    grp = lambda a: a.T.reshape(n2, n1 // K1_GRP, K1_GRP).transpose(1, 0, 2)
    return dict(f1_full=bf(f1_full), f1_half=bf(f1_half), tre=f32(tre[perm]), tim=f32(tim[perm]),
                tre_grp=f32(grp(tre)), tim_grp=f32(grp(tim)), m2f=bf(m2f), m2i=bf(m2i), gi=bf(gi))


def _pick_col(tbl, idx):
    lane = lax.broadcasted_iota(I32, tbl.shape, 1)
    return jnp.sum(jnp.where(lane == idx, tbl, 0.0), axis=1, keepdims=True)


K1_GRP = 16
PAIRS = K1_GRP // 2
LANES = 128
N2_GRP = 4
MID_TC = 512


def _unpack_pair(w, half):
    bits = lax.shift_left(w, jnp.uint32(16)) if half == 0 else (w & jnp.uint32(0xFFFF0000))
    return pltpu.bitcast(bits, F32).astype(BF16)


def _pack_pair(even, odd):
    ue = pltpu.bitcast(even.astype(BF16).astype(F32), jnp.uint32)
    uo = pltpu.bitcast(odd.astype(BF16).astype(F32), jnp.uint32)
    return lax.shift_right_logical(ue, jnp.uint32(16)) | uo


def _s1_store(a, tre_ref, tim_ref, j, o_ref, n1, slot):
    are, aim = a[:n1], a[n1:]
    tre = _pick_col(tre_ref[...], j)
    tim = _pick_col(tim_ref[...], j)
    re = are * tre - aim * tim
    im = are * tim + aim * tre
    h = n1 // 2
    shape = (n1 // K1_GRP, PAIRS, a.shape[1])
    for part, val in enumerate((re, im)):
        words = _pack_pair(val[:h], val[h:]).reshape(shape)
        for ct in range(a.shape[1] // LANES):
            o_ref[part, :, ct, slot * PAIRS:(slot + 1) * PAIRS, :] = words[:, :, ct * LANES:(ct + 1) * LANES]


def _fft_s1_kernel(f1_ref, x_ref, tre_ref, tim_ref, o_ref, *, n1, tc):
    j0 = pl.program_id(0) * N2_GRP
    for slot in range(N2_GRP):
        a = _dot(f1_ref[...], x_ref[:, slot * tc:(slot + 1) * tc])
        _s1_store(a, tre_ref, tim_ref, j0 + slot, o_ref, n1, slot)


def _fft_stage1(x2d, f1, tre, tim, n1, n2, chans):
    r = x2d.shape[0]
    return pl.pallas_call(
        functools.partial(_fft_s1_kernel, n1=n1, tc=chans),
        grid=(n2 // N2_GRP,),
        in_specs=[_const_spec((2 * n1, r)),
                  pl.BlockSpec((r, N2_GRP * chans), lambda j: (0, j)),
                  _const_spec((n1, n2)), _const_spec((n1, n2))],
        out_specs=pl.BlockSpec((2, n1 // K1_GRP, chans // LANES, N2_GRP * PAIRS, LANES),
                               lambda j: (0, 0, 0, j, 0)),
        out_shape=jax.ShapeDtypeStruct((2, n1 // K1_GRP, chans // LANES, n2 * PAIRS, LANES), jnp.uint32),
        compiler_params=_cp("parallel"),
        name="fft_stage1",
    )(f1, x2d, tre, tim)


def _load_pair(b_ref, part, s, n2):
    return jnp.concatenate([b_ref[part, 0, ct, pl.ds(s, n2, stride=PAIRS), :] for ct in range(b_ref.shape[2])],
                           axis=1)


def _store_pair(o_ref, part, s, n2, words):
    for ct in range(o_ref.shape[2]):
        o_ref[part, 0, ct, pl.ds(s, n2, stride=PAIRS), :] = words[:, ct * LANES:(ct + 1) * LANES]


def _fft_s2_kernel(m2f_ref, b_ref, o_ref):
    n2 = m2f_ref.shape[0]
    for s in range(PAIRS):
        wre = _load_pair(b_ref, 0, s, n2)
        wim = _load_pair(b_ref, 1, s, n2)
        for half in range(2):
            b = jnp.concatenate([_unpack_pair(wre, half), _unpack_pair(wim, half)], axis=0)
            o_ref[2 * s + half] = _dot(m2f_ref[...], b).astype(BF16)


def _fft_stage2(b5, m2f, n1, n2, chans):
    return pl.pallas_call(
        _fft_s2_kernel,
        grid=(n1 // K1_GRP, chans // MID_TC),
        in_specs=[_const_spec((n2, 2 * n2)),
                  pl.BlockSpec((2, 1, MID_TC // LANES, n2 * PAIRS, LANES), lambda g, c: (0, g, c, 0, 0))],
        out_specs=pl.BlockSpec((K1_GRP, n2, MID_TC), lambda g, c: (g, 0, c)),
        out_shape=jax.ShapeDtypeStruct((n1, n2, chans), BF16),
        compiler_params=_cp("parallel", "parallel"),
        name="fft_stage2",
    )(m2f, b5)


def _fft_mid_kernel(m2f_ref, m2i_ref, b_ref, k_ref, inv_ref, tre_ref, tim_ref, o_ref, *, n2):
    h2 = n2 // 2
    inv = inv_ref[...]
    tre_g = tre_ref[0]
    tim_g = tim_ref[0]
    for s in range(PAIRS):
        wre = _load_pair(b_ref, 0, s, n2)
        wim = _load_pair(b_ref, 1, s, n2)
        res = []
        for half in range(2):
            kl = 2 * s + half
            b = jnp.concatenate([_unpack_pair(wre, half), _unpack_pair(wim, half)], axis=0)
            x = _dot(m2f_ref[...], b)
            kk = k_ref[kl].astype(F32) * inv
            xre, xim = x[:h2], x[h2:]
            kre, kim = kk[:h2], kk[h2:]
            y = jnp.concatenate([xre * kre - xim * kim, xre * kim + xim * kre], axis=0).astype(BF16)
            c = _dot(m2i_ref[...], y)
            cre, cim = c[:n2], c[n2:]
            tre = tre_g[:, kl:kl + 1]
            tim = tim_g[:, kl:kl + 1]
            res.append((cre * tre + cim * tim, cim * tre - cre * tim))
        _store_pair(o_ref, 0, s, n2, _pack_pair(res[0][0], res[1][0]))
        _store_pair(o_ref, 1, s, n2, _pack_pair(res[0][1], res[1][1]))


def _fft_mid(b5, kspec, inv_norm, order, cst, n1, n2):
    d = D_MODEL
    nc = d // MID_TC
    blk = pl.BlockSpec((2, 1, MID_TC // LANES, n2 * PAIRS, LANES), lambda g, c: (0, g, c, 0, 0))
    tw = pl.BlockSpec((1, n2, K1_GRP), lambda g, c: (g, 0, 0))
    return pl.pallas_call(
        functools.partial(_fft_mid_kernel, n2=n2),
        grid=(n1 // K1_GRP, nc),
        in_specs=[_const_spec((n2, 2 * n2)), _const_spec((2 * n2, n2)), blk,
                  pl.BlockSpec((K1_GRP, n2, MID_TC), lambda g, c: (g, 0, order * nc + c)),
                  pl.BlockSpec((1, MID_TC), lambda g, c: (0, order * nc + c)),
                  tw, tw],
        out_specs=blk,
        out_shape=jax.ShapeDtypeStruct((2, n1 // K1_GRP, d // LANES, n2 * PAIRS, LANES), jnp.uint32),
        compiler_params=_cp("parallel", "parallel"),
        name="fft_mid",
    )(cst["m2f"], cst["m2i"], b5, kspec, inv_norm, cst["tre_grp"], cst["tim_grp"])


def _fft_last_kernel(gi_ref, c_ref, gate_ref, z_ref, skip_ref, o_ref):
    h = c_ref.shape[1] * PAIRS
    d = c_ref.shape[2] * LANES
    for slot in range(N2_GRP):
        rows = slice(slot * PAIRS, (slot + 1) * PAIRS)
        cols = slice(slot * d, (slot + 1) * d)

        def words(part):
            return jnp.concatenate([c_ref[part, :, ct, rows, :].reshape(h, LANES)
                                    for ct in range(c_ref.shape[2])], axis=1)

        wre = words(0)
        wim = words(1)
        c = jnp.concatenate([_unpack_pair(wre, 0), _unpack_pair(wre, 1), _unpack_pair(wim, 0),
                             _unpack_pair(wim, 1)], axis=0)
        y = _dot(gi_ref[...], c)
        z = z_ref[:, cols].astype(F32)
        o_ref[:, cols] = (gate_ref[:, cols].astype(F32) * (y + skip_ref[...] * z)).astype(BF16)


def _fft_last(c5, gate2d, z2d, skip_row, gi, n1, n2):
    d = D_MODEL
    r = n1 // 2
    col = pl.BlockSpec((r, N2_GRP * d), lambda j: (0, j))
    return pl.pallas_call(
        _fft_last_kernel,
        grid=(n2 // N2_GRP,),
        in_specs=[_const_spec((r, 2 * n1)),
                  pl.BlockSpec((2, n1 // K1_GRP, d // LANES, N2_GRP * PAIRS, LANES), lambda j: (0, 0, 0, j, 0)),
                  col, col, _const_spec((1, d))],
        out_specs=col,
        out_shape=jax.ShapeDtypeStruct((r, n2 * d), BF16),
        compiler_params=_cp("parallel"),
        name="fft_last",
    )(gi, c5, gate2d, z2d, skip_row)


def _long_conv_gate(z_in, gate, skip_row, kspec, inv_norm, order, seq):
    d = D_MODEL
    n1, n2 = _fft_factors(seq)
    cst = _fft_consts(seq)
    z2d = z_in.reshape(n1 // 2, n2 * d)
    b = _fft_stage1(z2d, cst["f1_half"], cst["tre"], cst["tim"], n1, n2, d)
    c = _fft_mid(b, kspec, inv_norm, order, cst, n1, n2)
    out = _fft_last(c, gate.reshape(n1 // 2, n2 * d), z2d, skip_row, cst["gi"], n1, n2)
    return out.reshape(seq, d)


def _filter_spectrum(seq, fparams):
    d = D_MODEL
    od = HY_ORDER * d
    n1, n2 = _fft_factors(seq)
    cst = _fft_consts(seq)
    b, asum = _filter_stage1(seq, *fparams)
    kspec = _fft_stage2(b, cst["m2f"], n1, n2, od)
    return kspec, asum


def _proj_res_kernel(a_ref, w_ref, b_ref, g_ref, x_ref, o_ref):
    y = _dot(a_ref[...], w_ref[...]) + b_ref[...]
    o_ref[...] = x_ref[...] + g_ref[...] * y


def _proj_res(a, w_bf, b_row, gate_row, xres):
    n, d = xres.shape
    tm = min(ROW_TILE, n)
    row = pl.BlockSpec((tm, d), lambda i: (i, 0))
    return pl.pallas_call(
        _proj_res_kernel,
        grid=(n // tm,),
        in_specs=[row, _const_spec((d, d)), _const_spec((1, d)), _const_spec((1, d)), row],
        out_specs=row,
        out_shape=jax.ShapeDtypeStruct((n, d), F32),
        compiler_params=_cp("parallel"),
        name="proj_residual",
    )(a, w_bf, b_row, gate_row, xres)


def _ffn_in_kernel(x_ref, mod_ref, wt_ref, h_ref, aff_ref):
    h = _norm_mod(x_ref[...], mod_ref[...])
    hi = h.astype(BF16)
    lo = (h - hi.astype(F32)).astype(BF16)
    wt = wt_ref[...]
    whi = wt.astype(BF16)
    wlo = (wt - whi.astype(F32)).astype(BF16)
    logits = _dot_nt(whi, hi) + (_dot_nt(whi, lo) + _dot_nt(wlo, hi))
    m = jnp.max(logits, axis=0, keepdims=True)
    p = jnp.exp(logits - m)
    aff_ref[...] = p / jnp.sum(p, axis=0, keepdims=True)
    h_ref[...] = hi


def _ffn_in(x, mod, w_router):
    n, d = x.shape
    e = N_EXPERTS
    tm = min(ROW_TILE, n)
    return pl.pallas_call(
        _ffn_in_kernel,
        grid=(n // tm,),
        in_specs=[pl.BlockSpec((tm, d), lambda i: (i, 0)), _const_spec((8, d)), _const_spec((e, d))],
        out_specs=[pl.BlockSpec((tm, d), lambda i: (i, 0)), pl.BlockSpec((e, tm), lambda i: (0, i))],
        out_shape=[jax.ShapeDtypeStruct((n, d), BF16), jax.ShapeDtypeStruct((e, n), F32)],
        compiler_params=_cp("parallel"),
        name="moe_router",
    )(x, mod, w_router.T)


def _select_kernel(a_ref, pos_ref, s0_ref, sel_ref, *, cap, nblk):
    e = N_EXPERTS
    bits = pltpu.bitcast(a_ref[...], I32)

    def bisect(i, thr):
        cand = thr | jnp.left_shift(jnp.int32(1), 30 - i)
        cnt = jnp.sum(jnp.where(bits >= cand, 1.0, 0.0), axis=1, keepdims=True)
        return jnp.where(cnt >= cap, cand, thr)

    thr = lax.fori_loop(0, 31, bisect, jnp.zeros((e, 1), I32))
    n_gt = jnp.sum(jnp.where(bits > thr, 1.0, 0.0), axis=1, keepdims=True)
    need = cap - n_gt
    r = lax.broadcasted_iota(I32, (TOK_BLK, TOK_BLK), 0)
    c = lax.broadcasted_iota(I32, (TOK_BLK, TOK_BLK), 1)
    upper = jnp.where(r < c, 1.0, 0.0).astype(BF16)

    def pass1(j, carry):
        sl = pl.ds(pl.multiple_of(j * TOK_BLK, TOK_BLK), TOK_BLK)
        bj = pltpu.bitcast(a_ref[:, sl], I32)
        eq = jnp.where(bj == thr, 1.0, 0.0)
        rank = _dot(eq.astype(BF16), upper) + carry
        keep = jnp.logical_or(bj > thr, jnp.logical_and(bj == thr, rank < need))
        sel_ref[:, sl] = jnp.where(keep, 1.0, 0.0)
        return carry + jnp.sum(eq, axis=1, keepdims=True)

    lax.fori_loop(0, nblk, pass1, jnp.zeros((e, 1), F32))

    def pass2(j, carry):
        sl = pl.ds(pl.multiple_of(j * TOK_BLK, TOK_BLK), TOK_BLK)
        s = sel_ref[:, sl]
        slot = _dot(s.astype(BF16), upper) + carry
        pos_ref[:, sl] = jnp.where(s > 0.5, slot, -1.0).astype(I32)
        s0_ref[j] = jnp.broadcast_to(carry, (e, 128)).astype(I32)
        return carry + jnp.sum(s, axis=1, keepdims=True)

    total = lax.fori_loop(0, nblk, pass2, jnp.zeros((e, 1), F32))
    s0_ref[nblk] = jnp.broadcast_to(total, (e, 128)).astype(I32)


def _select(aff_t, cap):
    e, n = aff_t.shape
    nblk = n // TOK_BLK
    return pl.pallas_call(
        functools.partial(_select_kernel, cap=cap, nblk=nblk),
        out_shape=[jax.ShapeDtypeStruct((e, n), I32), jax.ShapeDtypeStruct((nblk + 1, e, 128), I32)],
        scratch_shapes=[pltpu.VMEM((e, n), F32)],
        compiler_params=pltpu.CompilerParams(vmem_limit_bytes=VMEM_LIMIT),
        name="moe_select",
    )(aff_t)


def _block_windows(s0_ref, blk, e_idx):
    s0 = s0_ref[blk * N_EXPERTS + e_idx]
    s1 = s0_ref[(blk + 1) * N_EXPERTS + e_idx]
    start = lax.shift_left(lax.shift_right_logical(s0, 4), 4)
    nwin = jnp.where(s1 > s0, lax.shift_right_logical(s1 - start + (WIN - 1), WIN_SHIFT), 0)
    return start, nwin


def _token_block(i):
    start = i * TOK_BLK
    return pl.ds(start if isinstance(i, int) else pl.multiple_of(start, TOK_BLK), TOK_BLK)


def _gather_kernel(s0_ref, pos_ref, h_ref, xe_ref, *, sub, nchunk):
    e_idx = pl.program_id(0)
    xe_ref[...] = jnp.zeros_like(xe_ref)
    rows = lax.broadcasted_iota(I32, (WIN, TOK_BLK), 0)

    def window(blk, base):
        tok = _token_block(blk)
        base = pl.multiple_of(base, ROW_ALIGN)
        onehot = jnp.where(rows == pos_ref[0, :, tok] - base, 1.0, 0.0).astype(BF16)
        got = _dot(onehot, h_ref[tok, :]).astype(BF16)
        win = pl.ds(base, WIN)
        xe_ref[0, win, :] = xe_ref[0, win, :] + got

    def chunk(c, carry):
        plan = [_block_windows(s0_ref, c * sub + i, e_idx) for i in range(sub)]
        single = functools.reduce(jnp.logical_and, [nwin <= 1 for _, nwin in plan])

        @pl.when(single)
        def _():
            for i, (start, _) in enumerate(plan):
                window(c * sub + i, start)

        @pl.when(jnp.logical_not(single))
        def _():
            def body(i, carry2):
                start, nwin = _block_windows(s0_ref, c * sub + i, e_idx)
                return lax.fori_loop(0, nwin, lambda w, cc: (window(c * sub + i, start + w * WIN), cc)[1], carry2)

            lax.fori_loop(0, sub, body, 0)

        return carry

    lax.fori_loop(0, nchunk, chunk, 0)


def _gather(s0_flat, pos, h, cap_pad):
    e, n = pos.shape
    d = h.shape[1]
    sub = min(8, n // TOK_BLK)
    return pl.pallas_call(
        functools.partial(_gather_kernel, sub=sub, nchunk=n // (sub * TOK_BLK)),
        grid_spec=pltpu.PrefetchScalarGridSpec(
            num_scalar_prefetch=1,
            grid=(e,),
            in_specs=[pl.BlockSpec((1, 1, n), lambda ei, s0: (ei, 0, 0)),
                      pl.BlockSpec((n, d), lambda ei, s0: (0, 0), pipeline_mode=pl.Buffered(1))],
            out_specs=pl.BlockSpec((1, cap_pad, d), lambda ei, s0: (ei, 0, 0)),
        ),
        out_shape=jax.ShapeDtypeStruct((e, cap_pad, d), BF16),
        compiler_params=_cp("parallel"),
        name="moe_gather",
    )(s0_flat, pos.reshape(e, 1, n), h)


def _expert_kernel(x_ref, wg_ref, wu_ref, wd_ref, y_ref, wg_scr, wu_scr, wd_scr, *, n_real):
    j = pl.program_id(1)

    @pl.when(j == 0)
    def _():
        wg_scr[...] = wg_ref[0].astype(BF16)
        wu_scr[...] = wu_ref[0].astype(BF16)
        wd_scr[...] = wd_ref[0].astype(BF16)

    @pl.when(j < n_real)
    def _():
        x = x_ref[0]
        g = _dot(x, wg_scr[...])
        u = _dot(x, wu_scr[...])
        a = (g / (1.0 + jnp.exp(-g))) * u
        y_ref[0] = _dot(a.astype(BF16), wd_scr[...]).astype(BF16)

    @pl.when(j >= n_real)
    def _():
        y_ref[0] = jnp.zeros_like(y_ref[0])


def _experts(xe, wg, wu, wd, layer, cap):
    e, cap_pad, d = xe.shape
    f = wg.shape[3]
    tm = cap_pad // 3 if cap_pad % (3 * ROW_ALIGN) == 0 and cap >= 3 * TOK_BLK else cap_pad
    tile = pl.BlockSpec((1, tm, d), lambda ei, j: (ei, j, 0))
    return pl.pallas_call(
        functools.partial(_expert_kernel, n_real=pl.cdiv(cap, tm)),
        grid=(e, cap_pad // tm),
        in_specs=[tile,
                  pl.BlockSpec((None, 1, d, f), lambda ei, j: (layer, ei, 0, 0)),
                  pl.BlockSpec((None, 1, d, f), lambda ei, j: (layer, ei, 0, 0)),
                  pl.BlockSpec((None, 1, f, d), lambda ei, j: (layer, ei, 0, 0))],
        out_specs=tile,
        out_shape=jax.ShapeDtypeStruct((e, cap_pad, d), BF16),
        scratch_shapes=[pltpu.VMEM((d, f), BF16), pltpu.VMEM((d, f), BF16), pltpu.VMEM((f, d), BF16)],
        compiler_params=_cp("parallel", "arbitrary"),
        name="moe_experts",
    )(xe, wg, wu, wd)


COMBINE_COLS = 512
COMBINE_BLKS = 2


def _combine_kernel(s0_ref, pos_ref, aff_ref, y_ref, x_ref, gate_ref, o_ref, acc_ref, *, nsub):
    t = pl.program_id(1)
    rows = lax.broadcasted_iota(I32, (WIN, TOK_BLK), 0)

    def selector(i, e, base):
        tok = _token_block(i)
        rel = pos_ref[e:e + 1, tok] - base
        return jnp.where(rows == rel, aff_ref[e:e + 1, tok], 0.0).astype(BF16)

    def contract(sel, ywin):
        return lax.dot_general(sel, ywin, (((0,), (0,)), ((), ())), preferred_element_type=F32)

    for i in range(nsub):
        plan = [_block_windows(s0_ref, t * nsub + i, e) for e in range(N_EXPERTS)]
        single = functools.reduce(jnp.logical_and, [nwin <= 1 for _, nwin in plan])
        tok = _token_block(i)

        @pl.when(single)
        def _():
            bases = [pl.multiple_of(start, ROW_ALIGN) for start, _ in plan]
            sel = jnp.concatenate([selector(i, e, b) for e, b in enumerate(bases)], axis=0)
            ywin = jnp.concatenate([y_ref[e, pl.ds(b, WIN), :] for e, b in enumerate(bases)], axis=0)
            o_ref[tok, :] = x_ref[tok, :] + gate_ref[...] * contract(sel, ywin)

        @pl.when(jnp.logical_not(single))
        def _():
            acc_ref[...] = jnp.zeros_like(acc_ref)
            for e, (start, nwin) in enumerate(plan):
                def window(w, carry):
                    base = pl.multiple_of(start + w * WIN, ROW_ALIGN)
                    acc_ref[...] += contract(selector(i, e, base), y_ref[e, pl.ds(base, WIN), :])
                    return carry

                lax.fori_loop(0, nwin, window, 0)
            o_ref[tok, :] = x_ref[tok, :] + gate_ref[...] * acc_ref[...]


def _combine(s0_flat, pos, aff_t, y, xres, gate_row):
    n, d = xres.shape
    e, cap_pad, _ = y.shape
    nsub = min(COMBINE_BLKS, n // TOK_BLK)
    rows = nsub * TOK_BLK
    cq = COMBINE_COLS
    tile = pl.BlockSpec((rows, cq), lambda q, t, s0: (t, q))
    return pl.pallas_call(
        functools.partial(_combine_kernel, nsub=nsub),
        grid_spec=pltpu.PrefetchScalarGridSpec(
            num_scalar_prefetch=1,
            grid=(d // cq, n // rows),
            in_specs=[pl.BlockSpec((e, rows), lambda q, t, s0: (0, t)),
                      pl.BlockSpec((e, rows), lambda q, t, s0: (0, t)),
                      pl.BlockSpec((e, cap_pad, cq), lambda q, t, s0: (0, 0, q), pipeline_mode=pl.Buffered(1)),
                      tile,
                      pl.BlockSpec((1, cq), lambda q, t, s0: (0, q))],
            out_specs=tile,
            scratch_shapes=[pltpu.VMEM((TOK_BLK, cq), F32)],
        ),
        out_shape=jax.ShapeDtypeStruct((n, d), F32),
        compiler_params=_cp("parallel", "arbitrary"),
        name="moe_combine",
    )(s0_flat, pos, aff_t, y, xres, gate_row)


def _moe_block(x, mod, gate_row, w_router, wg, wu, wd, layer):
    n = x.shape[0]
    cap = EC_CAPACITY * n // N_EXPERTS
    cap_pad = cap + TOK_BLK
    h, aff_t = _ffn_in(x, mod, w_router)
    pos, s0 = _select(aff_t, cap)
    s0_flat = s0[:, :, 0].reshape(-1)
    xe = _gather(s0_flat, pos, h, cap_pad)
    y = _experts(xe, wg, wu, wd, layer, cap)
    return _combine(s0_flat, pos, aff_t, y, x, gate_row)


QK_SCALE = (DA_HEAD_DIM ** -0.5) * math.log2(math.e)


def _split_bf16(x):
    hi = x.astype(BF16)
    return hi, (x - hi.astype(F32)).astype(BF16)


def _group_rms(u, gsel_ref, gain, eps):
    g = gsel_ref[...]
    hi, lo = _split_bf16(u * u)
    r = lax.rsqrt((_dot(hi, g) + _dot(lo, g)) * (1.0 / DA_HEAD_DIM) + eps)
    rhi, rlo = _split_bf16(r)
    return u * (_dot_nt(rhi, g) + _dot_nt(rlo, g)) * gain


def _rope(u, cos, sin_signed):
    d = u.shape[1]
    half = ROPE_AXIS_DIM // 2
    lane = lax.broadcasted_iota(I32, u.shape, 1)
    first = (lane & half) == 0
    swapped = jnp.where(first, pltpu.roll(u, d - half, 1), pltpu.roll(u, half, 1))
    return u * cos + swapped * sin_signed


def _qkv_kernel(*refs, rope):
    x_ref, mod_ref, w_ref, gsum_ref, qn_ref, kn_ref = refs[:6]
    q_ref, k_ref, v_ref = refs[-3:]
    if rope:
        cos_ref, sin_ref = refs[6:8]
    d = D_MODEL
    h = _norm_mod(x_ref[...], mod_ref[...]).astype(BF16)
    if rope:
        reps = d // cos_ref.shape[1]
        cos = _lane_tile(cos_ref[...], reps)
        sin = _lane_tile(sin_ref[...], reps)
    for part, (o_ref, gain_ref) in enumerate(((q_ref, qn_ref), (k_ref, kn_ref))):
        u = _dot(h, w_ref[:, part * d:(part + 1) * d])
        u = _group_rms(u, gsum_ref, gain_ref[...], NORM_EPS)
        if rope:
            u = _rope(u, cos, sin)
        if part == 0:
            u = u * QK_SCALE
        o_ref[...] = u.astype(BF16)
    v_ref[...] = _dot(h, w_ref[:, 2 * d:]).astype(BF16)


def _qkv(x, mod, w_bf, gsum, qn_row, kn_row, cos=None, sin=None, kv_rows=None, kv_offset=0, kv_into=None):
    n, d = x.shape
    tm = min(QKV_TILE, n)
    kv_rows = n if kv_rows is None else kv_rows
    off = kv_offset // tm
    assert kv_offset % tm == 0
    rope = cos is not None
    row = pl.BlockSpec((tm, d), lambda i: (i, 0))
    kv_row = pl.BlockSpec((tm, d), lambda i: (i + off, 0))
    in_specs = [row, _const_spec((8, d)), _const_spec((d, 3 * d)), _const_spec(gsum.shape),
                _const_spec((1, d)), _const_spec((1, d))]
    args = [x, mod, w_bf, gsum, qn_row, kn_row]
    if rope:
        tw = cos.shape[1]
        in_specs += [pl.BlockSpec((tm, tw), lambda i: (i, 0)), pl.BlockSpec((tm, tw), lambda i: (i, 0))]
        args += [cos, sin]
    aliases = {}
    if kv_into is not None:
        aliases = {len(args): 1, len(args) + 1: 2}
        in_specs += [pl.BlockSpec(memory_space=pl.ANY), pl.BlockSpec(memory_space=pl.ANY)]
        args += list(kv_into)
    kv = jax.ShapeDtypeStruct((kv_rows, d), BF16)
    return pl.pallas_call(
        functools.partial(_qkv_kernel, rope=rope),
        grid=(n // tm,),
        in_specs=in_specs,
        out_specs=[row, kv_row, kv_row],
        out_shape=[jax.ShapeDtypeStruct((n, d), BF16), kv, kv],
        input_output_aliases=aliases,
        compiler_params=_cp("parallel"),
        name="attn_qkv",
    )(*args)


NEG_BIG = -1e30
HEAD_UNROLL = 8


def _attn_kernel(q_ref, k_ref, v_ref, lam_ref, sub_ref, o_ref, m_scr, acc_scr, sa_scr, sb_scr, ma_scr, mb_scr,
                 pa_scr, pb_scr, vprev_scr, aa_scr, ab_scr, *,
                 lam_init):
    j = pl.program_id(1)
    nj = pl.num_programs(1)
    hd, vd = DA_HEAD_DIM, DA_V_DIM
    tk = k_ref.shape[0]

    last = DA_HEADS - 1

    @pl.when(j == 0)
    def _():
        m_scr[...] = jnp.full_like(m_scr, NEG_BIG)
        acc_scr[...] = jnp.zeros_like(acc_scr)
        pa_scr[...] = jnp.zeros_like(pa_scr)
        aa_scr[...] = jnp.ones_like(aa_scr)
        sb_scr[...] = jnp.full_like(sb_scr, 2.0 * NEG_BIG)
        mb_scr[...] = jnp.full_like(mb_scr, 2.0 * NEG_BIG)
        vprev_scr[...] = jnp.zeros_like(vprev_scr)

    lane = lax.broadcasted_iota(I32, (q_ref.shape[0], vd), 1)
    ones_col = jnp.where(lax.broadcasted_iota(I32, (tk, vd), 1) == 0, 1.0, 0.0).astype(BF16)

    def head_cols(h):
        return pl.ds(pl.multiple_of(h * vd, vd), vd)

    def scores(h, comp, s_ref, mx_ref):
        qb = q_ref[:, head_cols(h)]
        in_comp = (lane >= comp * hd) & (lane < (comp + 1) * hd)
        qm = jnp.where(in_comp, qb, jnp.zeros_like(qb))
        s = _dot_nt(qm, k_ref[:, head_cols(h)])
        s_ref[...] = s
        mx_ref[...] = jnp.broadcast_to(jnp.max(s, axis=1, keepdims=True), mx_ref.shape)

    def softmax_step(h, comp, s_ref, mx_ref, p_ref, alpha_ref):
        idx = 2 * h + comp
        m_prev = m_scr[idx]
        m_new = jnp.maximum(m_prev, mx_ref[...])
        alpha_ref[...] = jnp.exp2(m_prev - m_new)
        p_ref[...] = jnp.exp2(s_ref[...] - m_new[:, 0:1]).astype(BF16)
        m_scr[idx] = m_new

    def pv_step(h, comp, p_ref, alpha_ref):
        idx = 2 * h + comp
        pv = _dot(p_ref[...], jnp.concatenate([vprev_scr[...], ones_col], axis=1))
        acc_scr[idx] = _lane_tile(alpha_ref[...], 2) * acc_scr[idx] + pv

    def head(h, carry):
        hp = (h + last) & last
        scores(h, 0, sa_scr, ma_scr)
        pv_step(hp, 0, pa_scr, aa_scr)
        softmax_step(hp, 1, sb_scr, mb_scr, pb_scr, ab_scr)
        scores(h, 1, sb_scr, mb_scr)
        pv_step(hp, 1, pb_scr, ab_scr)
        softmax_step(h, 0, sa_scr, ma_scr, pa_scr, aa_scr)
        vprev_scr[...] = v_ref[:, head_cols(h)]
        return carry

    def head_group(g, carry):
        for u in range(HEAD_UNROLL):
            carry = head(HEAD_UNROLL * g + u, carry)
        return carry

    lax.fori_loop(0, DA_HEADS // HEAD_UNROLL, head_group, 0)

    @pl.when(j == nj - 1)
    def _():
        pv_step(last, 0, pa_scr, aa_scr)
        softmax_step(last, 1, sb_scr, mb_scr, pb_scr, ab_scr)
        pv_step(last, 1, pb_scr, ab_scr)
        lp = lam_ref[...]
        lam = (jnp.exp(jnp.sum(lp[0:1] * lp[1:2], axis=1, keepdims=True))
               - jnp.exp(jnp.sum(lp[2:3] * lp[3:4], axis=1, keepdims=True)) + lam_init)
        for h in range(DA_HEADS):
            a0 = acc_scr[2 * h]
            a1 = acc_scr[2 * h + 1]
            o = a0[:, :vd] / a0[:, vd:vd + 1] - lam * (a1[:, :vd] / a1[:, vd:vd + 1])
            ms = jnp.mean(o * o, axis=1, keepdims=True)
            o = o * lax.rsqrt(ms + SUBLN_EPS) * (sub_ref[...] * (1.0 - lam_init))
            o_ref[:, h * vd:(h + 1) * vd] = o.astype(BF16)


def _attention(q, k_all, v_all, lam_rows, subln_row, lam_init):
    n, d = q.shape
    nk = k_all.shape[0]
    tq = min(512, n)
    tk = 1280 if nk % 1280 == 0 else 256
    assert nk % tk == 0
    nc = 2 * DA_HEADS
    return pl.pallas_call(
        functools.partial(_attn_kernel, lam_init=lam_init),
        grid=(n // tq, nk // tk),
        in_specs=[pl.BlockSpec((tq, d), lambda i, j: (i, 0)),
                  pl.BlockSpec((tk, d), lambda i, j: (j, 0)),
                  pl.BlockSpec((tk, d), lambda i, j: (j, 0)),
                  _const_spec((8, DA_HEAD_DIM)), _const_spec((1, DA_V_DIM))],
        out_specs=pl.BlockSpec((tq, d), lambda i, j: (i, 0)),
        out_shape=jax.ShapeDtypeStruct((n, d), BF16),
        scratch_shapes=[pltpu.VMEM((nc, tq, DA_V_DIM), F32), pltpu.VMEM((nc, tq, 2 * DA_V_DIM), F32),
                        pltpu.VMEM((tq, tk), F32), pltpu.VMEM((tq, tk), F32),
                        pltpu.VMEM((tq, DA_V_DIM), F32), pltpu.VMEM((tq, DA_V_DIM), F32),
                        pltpu.VMEM((tq, tk), BF16), pltpu.VMEM((tq, tk), BF16),
                        pltpu.VMEM((tk, DA_V_DIM), BF16),
                        pltpu.VMEM((tq, DA_V_DIM), F32), pltpu.VMEM((tq, DA_V_DIM), F32)],
        compiler_params=_cp("parallel", "arbitrary"),
        name="diff_attention",
    )(q, k_all, v_all, lam_rows, subln_row)


def _rope_tables(n):
    lane = np.arange(2 * DA_HEAD_DIM) % DA_HEAD_DIM
    nf = ROPE_AXIS_DIM // 2
    inv = ROPE_THETA ** (-np.arange(0, ROPE_AXIS_DIM, 2, dtype=np.float32) / ROPE_AXIS_DIM)
    by_row = jnp.asarray(np.where(lane < ROPE_AXIS_DIM, inv[lane % nf], 0.0).astype(np.float32))
    by_col = jnp.asarray(np.where(lane >= ROPE_AXIS_DIM, inv[lane % nf], 0.0).astype(np.float32))
    sign = jnp.asarray(np.where(lane % ROPE_AXIS_DIM < nf, -1.0, 1.0).astype(np.float32))
    t = jnp.arange(n, dtype=I32)
    row = (t // GRID_W).astype(F32)[:, None]
    col = (t % GRID_W).astype(F32)[:, None]
    ang = row * by_row[None] + col * by_col[None]
    return jnp.cos(ang), jnp.sin(ang) * sign[None]


def _hyena_layer(x, mod, gate_row, kspec, inv_norm, w_in_bf, b_in, conv_w, conv_b, skip, w_out_bf, b_out):
    seq = x.shape[0]
    v, x1, x2 = _hy_in(x, mod, w_in_bf, b_in, conv_w, conv_b)
    z = _long_conv_gate(v, x1, skip[0:1], kspec, inv_norm, 0, seq)
    z = _long_conv_gate(z, x2, skip[1:2], kspec, inv_norm, 1, seq)
    return _proj_res(z, w_out_bf, b_out.reshape(1, -1), gate_row, x)


def kernel(x, c, ctx, c_ctx, ada_w, ada_b, norm_mix, norm_ffn, hy_w_in, hy_b_in, hy_conv_w, hy_conv_b, hy_f_w1, hy_f_b1, hy_f_w2, hy_f_b2, hy_f_w3, hy_f_b3, hy_f_w4, hy_f_freq, hy_skip, hy_w_out, hy_b_out, da_w_qkv, da_q_norm, da_k_norm, da_lam_q1, da_lam_k1, da_lam_q2, da_lam_k2, da_subln, da_w_out, moe_router, moe_w_gate, moe_w_up, moe_w_down):
    d = D_MODEL
    depth = ada_w.shape[0]
    assert x.shape[0] == 1 and x.shape[2] == d
    xs = x[0]
    cs = ctx[0]
    cond8 = jnp.concatenate([c[0:1], c_ctx[None], jnp.zeros((6, d), F32)], axis=0)
    mods = _adaln(cond8, ada_w, ada_b)

    def mod_slice(i, row, k):
        return mods[i, row, k * d:(k + 1) * d][None]

    for i in range(depth):
        last = i == depth - 1
        j = i // 2
        mix_x = _mod_rows(norm_mix[i], mods[i], 0, 0)
        mix_c = _mod_rows(norm_mix[i], mods[i], 1, 0)
        if i % 2 == 0:
            fparams = (hy_f_w1[j], hy_f_b1[j], hy_f_w2[j], hy_f_b2[j], hy_f_w3[j], hy_f_b3[j], hy_f_w4[j],
                       hy_f_freq[j])
            shared = (hy_w_in[j].astype(BF16), hy_b_in[j], hy_conv_w[j], hy_conv_b[j], hy_skip[j],
                      hy_w_out[j].astype(BF16), hy_b_out[j])
            kspec, asum = _filter_spectrum(xs.shape[0], fparams)
            inv_norm = 1.0 / (asum[0:1] + HY_FILTER_EPS)
            new_x = _hyena_layer(xs, mix_x, mod_slice(i, 0, 2), kspec, inv_norm, *shared)
            if not last:
                kspec_c, asum_c = _filter_spectrum(cs.shape[0], fparams)
                inv_c = 1.0 / (asum_c[0:1] + HY_FILTER_EPS)
                cs = _hyena_layer(cs, mix_c, mod_slice(i, 1, 2), kspec_c, inv_c, *shared)
            xs = new_x
        else:
            lam_init = 0.8 - 0.6 * math.exp(-0.3 * i)
            w_qkv = da_w_qkv[j].astype(BF16)
            gidx = np.arange(d) // DA_HEAD_DIM
            gsum = jnp.asarray((gidx[:, None] == np.arange(128)[None]).astype(np.float32)).astype(BF16)
            qn = jnp.tile(da_q_norm[j], 2 * DA_HEADS)[None]
            kn = jnp.tile(da_k_norm[j], 2 * DA_HEADS)[None]
            cos, sin = _rope_tables(xs.shape[0])
            n_lat = xs.shape[0]
            nk = n_lat + cs.shape[0]
            qx, k_all, v_all = _qkv(xs, mix_x, w_qkv, gsum, qn, kn, cos, sin, kv_rows=nk)
            qc, k_all, v_all = _qkv(cs, mix_c, w_qkv, gsum, qn, kn, kv_rows=nk, kv_offset=n_lat,
                                    kv_into=(k_all, v_all))
            lam_rows = jnp.concatenate([da_lam_q1[j][None], da_lam_k1[j][None], da_lam_q2[j][None],
                                        da_lam_k2[j][None], jnp.zeros((4, DA_HEAD_DIM), F32)], axis=0)
            w_out = da_w_out[j].astype(BF16)
            zero_b = jnp.zeros((1, d), F32)
            ox = _attention(qx, k_all, v_all, lam_rows, da_subln[j][None], lam_init)
            new_x = _proj_res(ox, w_out, zero_b, mod_slice(i, 0, 2), xs)
            if not last:
                oc = _attention(qc, k_all[n_lat:], v_all[n_lat:], lam_rows, da_subln[j][None], lam_init)
                cs = _proj_res(oc, w_out, zero_b, mod_slice(i, 1, 2), cs)
            xs = new_x
        experts = (moe_w_gate, moe_w_up, moe_w_down, i)
        if not last:
            cs = _moe_block(cs, _mod_rows(norm_ffn[i], mods[i], 1, 3), mod_slice(i, 1, 5), moe_router[i], *experts)
        xs = _moe_block(xs, _mod_rows(norm_ffn[i], mods[i], 0, 3), mod_slice(i, 0, 5), moe_router[i], *experts)
    return xs[None]
```

```python
import functools
import math

import jax
import jax.numpy as jnp
import numpy as np
from jax import lax
from jax.experimental import pallas as pl
from jax.experimental.pallas import tpu as pltpu

F32 = jnp.float32
BF16 = jnp.bfloat16
I32 = jnp.int32
HIGHEST = lax.Precision.HIGHEST

D_MODEL = 1024
N_MOD = 6
NORM_EPS = 1e-6
GRID_W = 64
HY_ORDER = 2
HY_SHORT = 3
HY_EMB_BANDS = 16
HY_EMB_DIM = 1 + 2 * HY_EMB_BANDS
HY_FILTER_HIDDEN = 64
HY_DECAY_FAST = 0.3
HY_DECAY_SLOW = 1.5
HY_DECAY_TARGET = 1e-2
HY_FILTER_EPS = 1e-6
Z_SIGN_COL = 33
DA_HEADS = 8
DA_HEAD_DIM = 64
DA_V_DIM = 128
ROPE_AXIS_DIM = 32
ROPE_THETA = 10000.0
SUBLN_EPS = 1e-5
N_EXPERTS = 16
EC_CAPACITY = 2
D_EXPERT = 1024
TOK_BLK = 256
ROW_ALIGN = 16
WIN_SHIFT = 7
WIN = 1 << WIN_SHIFT

ROW_TILE = 1024
QKV_TILE = 256
VMEM_LIMIT = 56 * 1024 * 1024


def _cp(*sem):
    return pltpu.CompilerParams(dimension_semantics=sem, vmem_limit_bytes=VMEM_LIMIT)


def _const_spec(shape):
    nd = len(shape)
    return pl.BlockSpec(shape, lambda *_: (0,) * nd)


def _dot(a, b):
    return jnp.dot(a, b, preferred_element_type=F32)


def _dot_nt(a, b):
    return lax.dot_general(a, b, (((1,), (1,)), ((), ())), preferred_element_type=F32)


def _norm_mod(x, mod, eps=NORM_EPS):
    ms = jnp.mean(x * x, axis=-1, keepdims=True)
    return x * lax.rsqrt(ms + eps) * (mod[0:1] * (1.0 + mod[1:2])) + mod[2:3]


def _lane_tile(x, reps):
    return jnp.concatenate([x] * reps, axis=1) if reps > 1 else x


def _adaln_kernel(c_ref, w_ref, b_ref, o_ref):
    c = c_ref[...]
    s = c / (1.0 + jnp.exp(-c))
    o_ref[0] = jnp.dot(s, w_ref[0], precision=HIGHEST, preferred_element_type=F32) + b_ref[0]


def _adaln(cond8, ada_w, ada_b):
    depth, d, nout = ada_w.shape
    tn = 1536
    return pl.pallas_call(
        _adaln_kernel,
        grid=(depth, nout // tn),
        in_specs=[_const_spec((8, d)),
                  pl.BlockSpec((1, d, tn), lambda l, j: (l, 0, j)),
                  pl.BlockSpec((1, 1, tn), lambda l, j: (l, 0, j))],
        out_specs=pl.BlockSpec((1, 8, tn), lambda l, j: (l, 0, j)),
        out_shape=jax.ShapeDtypeStruct((depth, 8, nout), F32),
        compiler_params=_cp("parallel", "parallel"),
        name="adaln",
    )(cond8, ada_w, ada_b.reshape(depth, 1, nout))


def _mod_rows(norm_g, mods, row, k0):
    d = D_MODEL
    shift = mods[row, k0 * d:(k0 + 1) * d]
    scale = mods[row, (k0 + 1) * d:(k0 + 2) * d]
    z = jnp.zeros((5, d), F32)
    return jnp.concatenate([norm_g[None], scale[None], shift[None], z], axis=0)


HALO = 16


def _hy_in_kernel(x_ref, xp_ref, xn_ref, mod_ref, w_ref, b_ref, cw_ref, v_ref, x1_ref, x2_ref, *, tm, n_rows):
    i = pl.program_id(0)
    mod = mod_ref[...]
    hm = _norm_mod(x_ref[...], mod).astype(BF16)
    hp = _norm_mod(xp_ref[...], mod).astype(BF16)
    hn = _norm_mod(xn_ref[...], mod).astype(BF16)
    hcat = jnp.concatenate([hp, hm, hn], axis=0)
    row = lax.broadcasted_iota(I32, (tm + 2 * HALO, 1), 0) + (i * tm - HALO)
    valid = jnp.logical_and(row >= 0, row < n_rows)
    d = D_MODEL
    for c, o_ref in enumerate((v_ref, x1_ref, x2_ref)):
        u = _dot(hcat, w_ref[:, c * d:(c + 1) * d]) + b_ref[:, c * d:(c + 1) * d]
        u = jnp.where(valid, u, 0.0)
        cw = cw_ref[:, c * d:(c + 1) * d]
        y = (cw[3:4] + cw[0:1] * u[HALO - 1:HALO - 1 + tm] + cw[1:2] * u[HALO:HALO + tm]
             + cw[2:3] * u[HALO + 1:HALO + 1 + tm])
        o_ref[...] = y.astype(BF16)


def _hy_in(x, mod, w_bf, b_in, conv_w, conv_b):
    n, d = x.shape
    tm = min(ROW_TILE, n)
    nh = n // HALO
    cw = jnp.concatenate([conv_w, conv_b[None], jnp.zeros((4, 3 * d), F32)], axis=0)
    out = jax.ShapeDtypeStruct((n, d), BF16)
    row_spec = pl.BlockSpec((tm, d), lambda i: (i, 0))
    return pl.pallas_call(
        functools.partial(_hy_in_kernel, tm=tm, n_rows=n),
        grid=(n // tm,),
        in_specs=[row_spec,
                  pl.BlockSpec((HALO, d), lambda i: (jnp.maximum(i * (tm // HALO) - 1, 0), 0)),
                  pl.BlockSpec((HALO, d), lambda i: (jnp.minimum((i + 1) * (tm // HALO), nh - 1), 0)),
                  _const_spec((8, d)), _const_spec((d, 3 * d)), _const_spec((1, 3 * d)),
                  _const_spec((8, 3 * d))],
        out_specs=[row_spec, row_spec, row_spec],
        out_shape=[out, out, out],
        compiler_params=_cp("parallel"),
        name="hyena_in",
    )(x, x, x, mod, w_bf, b_in.reshape(1, 3 * d), cw)


Z_HALF = 64


def _filter_s1_kernel(z_ref, w1_ref, b1_ref, w2_ref, b2_ref, w3_ref, b3_ref, fr_ref, w4f_ref, w4b_ref, dl_ref,
                      f1_ref, tre_ref, tim_ref, o_ref, asum_ref, *, n1):
    j0 = pl.program_id(0) * N2_GRP
    fr = fr_ref[...]
    dl = dl_ref[...]
    r = n1 // 2

    def lin(a, w_ref, b_ref):
        return jnp.dot(a, w_ref[...], precision=HIGHEST, preferred_element_type=F32) + b_ref[...]

    @pl.when(j0 == 0)
    def _():
        asum_ref[...] = jnp.zeros_like(asum_ref)

    for slot in range(N2_GRP):
        z = z_ref[slot * r:(slot + 1) * r, :]
        hid = jnp.sin(fr * lin(z, w1_ref, b1_ref))
        hid = jnp.sin(fr * lin(hid, w2_ref, b2_ref))
        hid = jnp.sin(fr * lin(hid, w3_ref, b3_ref)).astype(BF16)

        def taps(w4_ref, col):
            t = z[:, col:col + 1]
            sgn = z[:, col + Z_SIGN_COL:col + Z_SIGN_COL + 1]
            return _dot(hid, w4_ref[...]) * jnp.exp(-t * dl) * sgn

        k = jnp.concatenate([taps(w4f_ref, 0), taps(w4b_ref, Z_HALF)], axis=0)
        asum_ref[0:1, :] += jnp.sum(jnp.abs(k), axis=0, keepdims=True)
        _s1_store(_dot(f1_ref[...], k.astype(BF16)), tre_ref, tim_ref, j0 + slot, o_ref, n1, slot)


def _filter_positions(seq, n1, n2):
    i = np.arange(n1 // 2)[None, :]
    j = np.arange(n2)[:, None]
    bands = np.linspace(1e-4, HY_EMB_BANDS - 1, HY_EMB_BANDS)
    z = np.zeros((n2, n1 // 2, 2 * Z_HALF), np.float64)
    for col, r in ((0, i * n2 + j), (Z_HALF, i * n2 + j + seq)):
        pos = np.minimum(np.where(r < seq, r, 2 * seq - r), seq - 1).astype(np.float64)
        w = 2.0 * np.pi * pos / seq
        z[:, :, col] = pos / (seq - 1)
        z[:, :, col + 1:col + 1 + HY_EMB_BANDS] = np.cos(w[..., None] * bands)
        z[:, :, col + 1 + HY_EMB_BANDS:col + HY_EMB_DIM] = -np.sin(w[..., None] * bands)
        z[:, :, col + Z_SIGN_COL] = np.where(r < seq, 1.0, np.where(r == seq, 0.0, -1.0))
    return jnp.asarray(z.reshape(n2 * (n1 // 2), 2 * Z_HALF).astype(np.float32))


def _filter_stage1(seq, f_w1, f_b1, f_w2, f_b2, f_w3, f_b3, f_w4, f_freq):
    d, hid = D_MODEL, HY_FILTER_HIDDEN
    od = HY_ORDER * d
    n1, n2 = _fft_factors(seq)
    cst = _fft_consts(seq)
    z = _filter_positions(seq, n1, n2)
    zw = 2 * Z_HALF
    zero = jnp.zeros((hid, hid), F32)
    pad = jnp.zeros((Z_HALF - HY_EMB_DIM, hid), F32)
    w1 = jnp.concatenate([jnp.concatenate([f_w1, pad], axis=0), jnp.zeros((Z_HALF, hid), F32)], axis=0)
    w1 = jnp.concatenate([w1, jnp.roll(w1, Z_HALF, axis=0)], axis=1)
    blockdiag = lambda w: jnp.concatenate([jnp.concatenate([w, zero], axis=1),
                                           jnp.concatenate([zero, w], axis=1)], axis=0)
    twice = lambda v: jnp.tile(v, 2).reshape(1, 2 * hid)
    w4d = f_w4.reshape(hid, HY_ORDER, 2, d).transpose(2, 0, 1, 3).reshape(2, hid, od)
    zrows = jnp.zeros((hid, od), F32)
    w4f = jnp.concatenate([w4d[0], zrows], axis=0).astype(BF16)
    w4b = jnp.concatenate([zrows, w4d[1]], axis=0).astype(BF16)
    max_decay = math.log(HY_DECAY_TARGET) / HY_DECAY_FAST
    min_decay = math.log(HY_DECAY_TARGET) / HY_DECAY_SLOW
    deltas = np.abs(np.linspace(min_decay, max_decay, d, dtype=np.float32))
    dl = jnp.asarray(np.tile(deltas, HY_ORDER)[None])
    r = n1 // 2
    return pl.pallas_call(
        functools.partial(_filter_s1_kernel, n1=n1),
        grid=(n2 // N2_GRP,),
        in_specs=[pl.BlockSpec((N2_GRP * r, zw), lambda j: (j, 0)),
                  _const_spec((zw, 2 * hid)), _const_spec((1, 2 * hid)),
                  _const_spec((2 * hid, 2 * hid)), _const_spec((1, 2 * hid)),
                  _const_spec((2 * hid, 2 * hid)), _const_spec((1, 2 * hid)),
                  _const_spec((1, 2 * hid)),
                  _const_spec((2 * hid, od)), _const_spec((2 * hid, od)), _const_spec((1, od)),
                  _const_spec((2 * n1, n1)), _const_spec((n1, n2)), _const_spec((n1, n2))],
        out_specs=[pl.BlockSpec((2, n1 // K1_GRP, od // LANES, N2_GRP * PAIRS, LANES), lambda j: (0, 0, 0, j, 0)),
                   _const_spec((8, od))],
        out_shape=[jax.ShapeDtypeStruct((2, n1 // K1_GRP, od // LANES, n2 * PAIRS, LANES), jnp.uint32),
                   jax.ShapeDtypeStruct((8, od), F32)],
        compiler_params=_cp("arbitrary"),
        name="hyena_filter",
    )(z, w1, twice(f_b1), blockdiag(f_w2), twice(f_b2), blockdiag(f_w3), twice(f_b3), twice(f_freq),
      w4f, w4b, dl, cst["f1_full"], cst["tre"], cst["tim"])


def _fft_factors(seq):
    n = 2 * seq
    n1 = 256 if n >= 32768 else 32
    n2 = n // n1
    assert n1 * n2 == n and n2 % 16 == 0 and n1 % 32 == 0
    return n1, n2


@functools.lru_cache(maxsize=None)
def _fft_consts(seq):
    n = 2 * seq
    n1, n2 = _fft_factors(seq)
    k1 = np.arange(n1, dtype=np.float64)[:, None] + 0.5
    th1 = 2.0 * np.pi * k1 * np.arange(n1, dtype=np.float64)[None] / n1
    perm = np.concatenate([np.arange(0, n1, 2), np.arange(1, n1, 2)])
    f1_full = np.concatenate([np.cos(th1)[perm], -np.sin(th1)[perm]], axis=0)
    f1_half = f1_full[:, :n1 // 2]
    tw = 2.0 * np.pi * k1 * np.arange(n2, dtype=np.float64)[None] / n
    tre, tim = np.cos(tw), -np.sin(tw)
    h2 = n2 // 2
    th2 = 2.0 * np.pi * np.arange(h2, dtype=np.float64)[:, None] * np.arange(n2, dtype=np.float64)[None] / n2
    c2, s2 = np.cos(th2), np.sin(th2)
    m2f = np.block([[c2, s2], [-s2, c2]])
    m2i = np.block([[c2.T, -s2.T], [s2.T, c2.T]])
    thb = th1[:, :n1 // 2].T
    gi = (2.0 / n) * np.concatenate([np.cos(thb)[:, perm], -np.sin(thb)[:, perm]], axis=1)
    bf = lambda a: jnp.asarray(a.astype(np.float32)).astype(BF16)
    f32 = lambda a: jnp.asarray(a.astype(np.float32))
    grp = lambda a: a.T.reshape(n2, n1 // K1_GRP, K1_GRP).transpose(1, 0, 2)
    return dict(f1_full=bf(f1_full), f1_half=bf(f1_half), tre=f32(tre[perm]), tim=f32(tim[perm]),
                tre_grp=f32(grp(tre)), tim_grp=f32(grp(tim)), m2f=bf(m2f), m2i=bf(m2i), gi=bf(gi))


def _pick_col(tbl, idx):
    lane = lax.broadcasted_iota(I32, tbl.shape, 1)
    return jnp.sum(jnp.where(lane == idx, tbl, 0.0), axis=1, keepdims=True)


K1_GRP = 16
PAIRS = K1_GRP // 2
LANES = 128
N2_GRP = 8
MID_TC = 512


def _unpack_pair(w, half):
    bits = lax.shift_left(w, jnp.uint32(16)) if half == 0 else (w & jnp.uint32(0xFFFF0000))
    return pltpu.bitcast(bits, F32).astype(BF16)


def _pack_pair(even, odd):
    ue = pltpu.bitcast(even.astype(BF16).astype(F32), jnp.uint32)
    uo = pltpu.bitcast(odd.astype(BF16).astype(F32), jnp.uint32)
    return lax.shift_right_logical(ue, jnp.uint32(16)) | uo


def _s1_store(a, tre_ref, tim_ref, j, o_ref, n1, slot):
    are, aim = a[:n1], a[n1:]
    tre = _pick_col(tre_ref[...], j)
    tim = _pick_col(tim_ref[...], j)
    re = are * tre - aim * tim
    im = are * tim + aim * tre
    h = n1 // 2
    shape = (n1 // K1_GRP, PAIRS, a.shape[1])
    for part, val in enumerate((re, im)):
        words = _pack_pair(val[:h], val[h:]).reshape(shape)
        for ct in range(a.shape[1] // LANES):
            o_ref[part, :, ct, slot * PAIRS:(slot + 1) * PAIRS, :] = words[:, :, ct * LANES:(ct + 1) * LANES]


def _fft_s1_kernel(f1_ref, x_ref, tre_ref, tim_ref, o_ref, *, n1, tc):
    j0 = pl.program_id(0) * N2_GRP
    for slot in range(N2_GRP):
        a = _dot(f1_ref[...], x_ref[:, slot * tc:(slot + 1) * tc])
        _s1_store(a, tre_ref, tim_ref, j0 + slot, o_ref, n1, slot)


def _fft_stage1(x2d, f1, tre, tim, n1, n2, chans):
    r = x2d.shape[0]
    return pl.pallas_call(
        functools.partial(_fft_s1_kernel, n1=n1, tc=chans),
        grid=(n2 // N2_GRP,),
        in_specs=[_const_spec((2 * n1, r)),
                  pl.BlockSpec((r, N2_GRP * chans), lambda j: (0, j)),
                  _const_spec((n1, n2)), _const_spec((n1, n2))],
        out_specs=pl.BlockSpec((2, n1 // K1_GRP, chans // LANES, N2_GRP * PAIRS, LANES),
                               lambda j: (0, 0, 0, j, 0)),
        out_shape=jax.ShapeDtypeStruct((2, n1 // K1_GRP, chans // LANES, n2 * PAIRS, LANES), jnp.uint32),
        compiler_params=_cp("parallel"),
        name="fft_stage1",
    )(f1, x2d, tre, tim)


def _load_pair(b_ref, part, s, n2):
    return jnp.concatenate([b_ref[part, 0, ct, pl.ds(s, n2, stride=PAIRS), :] for ct in range(b_ref.shape[2])],
                           axis=1)


def _store_pair(o_ref, part, s, n2, words):
    for ct in range(o_ref.shape[2]):
        o_ref[part, 0, ct, pl.ds(s, n2, stride=PAIRS), :] = words[:, ct * LANES:(ct + 1) * LANES]


def _fft_s2_kernel(m2f_ref, b_ref, o_ref):
    n2 = m2f_ref.shape[0]
    for s in range(PAIRS):
        wre = _load_pair(b_ref, 0, s, n2)
        wim = _load_pair(b_ref, 1, s, n2)
        for half in range(2):
            b = jnp.concatenate([_unpack_pair(wre, half), _unpack_pair(wim, half)], axis=0)
            o_ref[2 * s + half] = _dot(m2f_ref[...], b).astype(BF16)


def _fft_stage2(b5, m2f, n1, n2, chans):
    return pl.pallas_call(
        _fft_s2_kernel,
        grid=(n1 // K1_GRP, chans // MID_TC),
        in_specs=[_const_spec((n2, 2 * n2)),
                  pl.BlockSpec((2, 1, MID_TC // LANES, n2 * PAIRS, LANES), lambda g, c: (0, g, c, 0, 0))],
        out_specs=pl.BlockSpec((K1_GRP, n2, MID_TC), lambda g, c: (g, 0, c)),
        out_shape=jax.ShapeDtypeStruct((n1, n2, chans), BF16),
        compiler_params=_cp("parallel", "parallel"),
        name="fft_stage2",
    )(m2f, b5)


def _fft_mid_kernel(m2f_ref, m2i_ref, b_ref, k_ref, inv_ref, tre_ref, tim_ref, o_ref, *, n2):
    h2 = n2 // 2
    inv = inv_ref[...]
    tre_g = tre_ref[0]
    tim_g = tim_ref[0]
    for s in range(PAIRS):
        wre = _load_pair(b_ref, 0, s, n2)
        wim = _load_pair(b_ref, 1, s, n2)
        res = []
        for half in range(2):
            kl = 2 * s + half
            b = jnp.concatenate([_unpack_pair(wre, half), _unpack_pair(wim, half)], axis=0)
            x = _dot(m2f_ref[...], b)
            kk = k_ref[kl].astype(F32) * inv
            xre, xim = x[:h2], x[h2:]
            kre, kim = kk[:h2], kk[h2:]
            y = jnp.concatenate([xre * kre - xim * kim, xre * kim + xim * kre], axis=0).astype(BF16)
            c = _dot(m2i_ref[...], y)
            cre, cim = c[:n2], c[n2:]
            tre = tre_g[:, kl:kl + 1]
            tim = tim_g[:, kl:kl + 1]
            res.append((cre * tre + cim * tim, cim * tre - cre * tim))
        _store_pair(o_ref, 0, s, n2, _pack_pair(res[0][0], res[1][0]))
        _store_pair(o_ref, 1, s, n2, _pack_pair(res[0][1], res[1][1]))


def _fft_mid(b5, kspec, inv_norm, order, cst, n1, n2):
    d = D_MODEL
    nc = d // MID_TC
    blk = pl.BlockSpec((2, 1, MID_TC // LANES, n2 * PAIRS, LANES), lambda g, c: (0, g, c, 0, 0))
    tw = pl.BlockSpec((1, n2, K1_GRP), lambda g, c: (g, 0, 0))
    return pl.pallas_call(
        functools.partial(_fft_mid_kernel, n2=n2),
        grid=(n1 // K1_GRP, nc),
        in_specs=[_const_spec((n2, 2 * n2)), _const_spec((2 * n2, n2)), blk,
                  pl.BlockSpec((K1_GRP, n2, MID_TC), lambda g, c: (g, 0, order * nc + c)),
                  pl.BlockSpec((1, MID_TC), lambda g, c: (0, order * nc + c)),
                  tw, tw],
        out_specs=blk,
        out_shape=jax.ShapeDtypeStruct((2, n1 // K1_GRP, d // LANES, n2 * PAIRS, LANES), jnp.uint32),
        compiler_params=_cp("parallel", "parallel"),
        name="fft_mid",
    )(cst["m2f"], cst["m2i"], b5, kspec, inv_norm, cst["tre_grp"], cst["tim_grp"])


def _fft_last_kernel(gi_ref, c_ref, gate_ref, z_ref, skip_ref, o_ref):
    h = c_ref.shape[1] * PAIRS
    d = c_ref.shape[2] * LANES
    for slot in range(N2_GRP):
        rows = slice(slot * PAIRS, (slot + 1) * PAIRS)
        cols = slice(slot * d, (slot + 1) * d)

        def words(part):
            return jnp.concatenate([c_ref[part, :, ct, rows, :].reshape(h, LANES)
                                    for ct in range(c_ref.shape[2])], axis=1)

        wre = words(0)
        wim = words(1)
        c = jnp.concatenate([_unpack_pair(wre, 0), _unpack_pair(wre, 1), _unpack_pair(wim, 0),
                             _unpack_pair(wim, 1)], axis=0)
        y = _dot(gi_ref[...], c)
        z = z_ref[:, cols].astype(F32)
        o_ref[:, cols] = (gate_ref[:, cols].astype(F32) * (y + skip_ref[...] * z)).astype(BF16)


def _fft_last(c5, gate2d, z2d, skip_row, gi, n1, n2):
    d = D_MODEL
    r = n1 // 2
    col = pl.BlockSpec((r, N2_GRP * d), lambda j: (0, j))
    return pl.pallas_call(
        _fft_last_kernel,
        grid=(n2 // N2_GRP,),
        in_specs=[_const_spec((r, 2 * n1)),
                  pl.BlockSpec((2, n1 // K1_GRP, d // LANES, N2_GRP * PAIRS, LANES), lambda j: (0, 0, 0, j, 0)),
                  col, col, _const_spec((1, d))],
        out_specs=col,
        out_shape=jax.ShapeDtypeStruct((r, n2 * d), BF16),
        compiler_params=_cp("parallel"),
        name="fft_last",
    )(gi, c5, gate2d, z2d, skip_row)


def _long_conv_gate(z_in, gate, skip_row, kspec, inv_norm, order, seq):
    d = D_MODEL
    n1, n2 = _fft_factors(seq)
    cst = _fft_consts(seq)
    z2d = z_in.reshape(n1 // 2, n2 * d)
    b = _fft_stage1(z2d, cst["f1_half"], cst["tre"], cst["tim"], n1, n2, d)
    c = _fft_mid(b, kspec, inv_norm, order, cst, n1, n2)
    out = _fft_last(c, gate.reshape(n1 // 2, n2 * d), z2d, skip_row, cst["gi"], n1, n2)
    return out.reshape(seq, d)


def _filter_spectrum(seq, fparams):
    d = D_MODEL
    od = HY_ORDER * d
    n1, n2 = _fft_factors(seq)
    cst = _fft_consts(seq)
    b, asum = _filter_stage1(seq, *fparams)
    kspec = _fft_stage2(b, cst["m2f"], n1, n2, od)
    return kspec, asum


def _proj_res_kernel(a_ref, w_ref, b_ref, g_ref, x_ref, o_ref):
    y = _dot(a_ref[...], w_ref[...]) + b_ref[...]
    o_ref[...] = x_ref[...] + g_ref[...] * y


def _proj_res(a, w_bf, b_row, gate_row, xres):
    n, d = xres.shape
    tm = min(ROW_TILE, n)
    row = pl.BlockSpec((tm, d), lambda i: (i, 0))
    return pl.pallas_call(
        _proj_res_kernel,
        grid=(n // tm,),
        in_specs=[row, _const_spec((d, d)), _const_spec((1, d)), _const_spec((1, d)), row],
        out_specs=row,
        out_shape=jax.ShapeDtypeStruct((n, d), F32),
        compiler_params=_cp("parallel"),
        name="proj_residual",
    )(a, w_bf, b_row, gate_row, xres)


def _ffn_in_kernel(x_ref, mod_ref, wt_ref, h_ref, aff_ref):
    h = _norm_mod(x_ref[...], mod_ref[...])
    hi = h.astype(BF16)
    lo = (h - hi.astype(F32)).astype(BF16)
    wt = wt_ref[...]
    whi = wt.astype(BF16)
    wlo = (wt - whi.astype(F32)).astype(BF16)
    logits = _dot_nt(whi, hi) + (_dot_nt(whi, lo) + _dot_nt(wlo, hi))
    m = jnp.max(logits, axis=0, keepdims=True)
    p = jnp.exp(logits - m)
    aff_ref[...] = p / jnp.sum(p, axis=0, keepdims=True)
    h_ref[...] = hi


def _ffn_in(x, mod, w_router):
    n, d = x.shape
    e = N_EXPERTS
    tm = min(ROW_TILE, n)
    return pl.pallas_call(
        _ffn_in_kernel,
        grid=(n // tm,),
        in_specs=[pl.BlockSpec((tm, d), lambda i: (i, 0)), _const_spec((8, d)), _const_spec((e, d))],
        out_specs=[pl.BlockSpec((tm, d), lambda i: (i, 0)), pl.BlockSpec((e, tm), lambda i: (0, i))],
        out_shape=[jax.ShapeDtypeStruct((n, d), BF16), jax.ShapeDtypeStruct((e, n), F32)],
        compiler_params=_cp("parallel"),
        name="moe_router",
    )(x, mod, w_router.T)


def _select_kernel(a_ref, pos_ref, s0_ref, sel_ref, *, cap, nblk):
    e = N_EXPERTS
    bits = pltpu.bitcast(a_ref[...], I32)

    def bisect(i, thr):
        cand = thr | jnp.left_shift(jnp.int32(1), 30 - i)
        cnt = jnp.sum(jnp.where(bits >= cand, 1.0, 0.0), axis=1, keepdims=True)
        return jnp.where(cnt >= cap, cand, thr)

    thr = lax.fori_loop(0, 31, bisect, jnp.zeros((e, 1), I32))
    n_gt = jnp.sum(jnp.where(bits > thr, 1.0, 0.0), axis=1, keepdims=True)
    need = cap - n_gt
    r = lax.broadcasted_iota(I32, (TOK_BLK, TOK_BLK), 0)
    c = lax.broadcasted_iota(I32, (TOK_BLK, TOK_BLK), 1)
    upper = jnp.where(r < c, 1.0, 0.0).astype(BF16)

    def pass1(j, carry):
        sl = pl.ds(pl.multiple_of(j * TOK_BLK, TOK_BLK), TOK_BLK)
        bj = pltpu.bitcast(a_ref[:, sl], I32)
        eq = jnp.where(bj == thr, 1.0, 0.0)
        rank = _dot(eq.astype(BF16), upper) + carry
        keep = jnp.logical_or(bj > thr, jnp.logical_and(bj == thr, rank < need))
        sel_ref[:, sl] = jnp.where(keep, 1.0, 0.0)
        return carry + jnp.sum(eq, axis=1, keepdims=True)

    lax.fori_loop(0, nblk, pass1, jnp.zeros((e, 1), F32))

    def pass2(j, carry):
        sl = pl.ds(pl.multiple_of(j * TOK_BLK, TOK_BLK), TOK_BLK)
        s = sel_ref[:, sl]
        slot = _dot(s.astype(BF16), upper) + carry
        pos_ref[:, sl] = jnp.where(s > 0.5, slot, -1.0).astype(I32)
        s0_ref[j] = jnp.broadcast_to(carry, (e, 128)).astype(I32)
        return carry + jnp.sum(s, axis=1, keepdims=True)

    total = lax.fori_loop(0, nblk, pass2, jnp.zeros((e, 1), F32))
    s0_ref[nblk] = jnp.broadcast_to(total, (e, 128)).astype(I32)


def _select(aff_t, cap):
    e, n = aff_t.shape
    nblk = n // TOK_BLK
    return pl.pallas_call(
        functools.partial(_select_kernel, cap=cap, nblk=nblk),
        out_shape=[jax.ShapeDtypeStruct((e, n), I32), jax.ShapeDtypeStruct((nblk + 1, e, 128), I32)],
        scratch_shapes=[pltpu.VMEM((e, n), F32)],
        compiler_params=pltpu.CompilerParams(vmem_limit_bytes=VMEM_LIMIT),
        name="moe_select",
    )(aff_t)


def _block_windows(s0_ref, blk, e_idx):
    s0 = s0_ref[blk * N_EXPERTS + e_idx]
    s1 = s0_ref[(blk + 1) * N_EXPERTS + e_idx]
    start = lax.shift_left(lax.shift_right_logical(s0, 4), 4)
    nwin = jnp.where(s1 > s0, lax.shift_right_logical(s1 - start + (WIN - 1), WIN_SHIFT), 0)
    return start, nwin


def _token_block(i):
    start = i * TOK_BLK
    return pl.ds(start if isinstance(i, int) else pl.multiple_of(start, TOK_BLK), TOK_BLK)


def _gather_kernel(s0_ref, pos_ref, h_ref, xe_ref, *, sub, nchunk):
    e_idx = pl.program_id(0)
    xe_ref[...] = jnp.zeros_like(xe_ref)
    rows = lax.broadcasted_iota(I32, (WIN, TOK_BLK), 0)

    def window(blk, base):
        tok = _token_block(blk)
        base = pl.multiple_of(base, ROW_ALIGN)
        onehot = jnp.where(rows == pos_ref[0, :, tok] - base, 1.0, 0.0).astype(BF16)
        got = _dot(onehot, h_ref[tok, :]).astype(BF16)
        win = pl.ds(base, WIN)
        xe_ref[0, win, :] = xe_ref[0, win, :] + got

    def chunk(c, carry):
        plan = [_block_windows(s0_ref, c * sub + i, e_idx) for i in range(sub)]
        single = functools.reduce(jnp.logical_and, [nwin <= 1 for _, nwin in plan])

        @pl.when(single)
        def _():
            for i, (start, _) in enumerate(plan):
                window(c * sub + i, start)

        @pl.when(jnp.logical_not(single))
        def _():
            def body(i, carry2):
                start, nwin = _block_windows(s0_ref, c * sub + i, e_idx)
                return lax.fori_loop(0, nwin, lambda w, cc: (window(c * sub + i, start + w * WIN), cc)[1], carry2)

            lax.fori_loop(0, sub, body, 0)

        return carry

    lax.fori_loop(0, nchunk, chunk, 0)


def _gather(s0_flat, pos, h, cap_pad):
    e, n = pos.shape
    d = h.shape[1]
    sub = min(8, n // TOK_BLK)
    return pl.pallas_call(
        functools.partial(_gather_kernel, sub=sub, nchunk=n // (sub * TOK_BLK)),
        grid_spec=pltpu.PrefetchScalarGridSpec(
            num_scalar_prefetch=1,
            grid=(e,),
            in_specs=[pl.BlockSpec((1, 1, n), lambda ei, s0: (ei, 0, 0)),
                      pl.BlockSpec((n, d), lambda ei, s0: (0, 0), pipeline_mode=pl.Buffered(1))],
            out_specs=pl.BlockSpec((1, cap_pad, d), lambda ei, s0: (ei, 0, 0)),
        ),
        out_shape=jax.ShapeDtypeStruct((e, cap_pad, d), BF16),
        compiler_params=_cp("parallel"),
        name="moe_gather",
    )(s0_flat, pos.reshape(e, 1, n), h)


def _expert_kernel(x_ref, wg_ref, wu_ref, wd_ref, y_ref, wg_scr, wu_scr, wd_scr, *, n_real):
    j = pl.program_id(1)

    @pl.when(j == 0)
    def _():
        wg_scr[...] = wg_ref[0].astype(BF16)
        wu_scr[...] = wu_ref[0].astype(BF16)
        wd_scr[...] = wd_ref[0].astype(BF16)

    @pl.when(j < n_real)
    def _():
        x = x_ref[0]
        g = _dot(x, wg_scr[...])
        u = _dot(x, wu_scr[...])
        a = (g / (1.0 + jnp.exp(-g))) * u
        y_ref[0] = _dot(a.astype(BF16), wd_scr[...]).astype(BF16)

    @pl.when(j >= n_real)
    def _():
        y_ref[0] = jnp.zeros_like(y_ref[0])


def _experts(xe, wg, wu, wd, layer, cap):
    e, cap_pad, d = xe.shape
    f = wg.shape[3]
    tm = cap_pad // 3 if cap_pad % (3 * ROW_ALIGN) == 0 and cap >= 3 * TOK_BLK else cap_pad
    tile = pl.BlockSpec((1, tm, d), lambda ei, j: (ei, j, 0))
    return pl.pallas_call(
        functools.partial(_expert_kernel, n_real=pl.cdiv(cap, tm)),
        grid=(e, cap_pad // tm),
        in_specs=[tile,
                  pl.BlockSpec((None, 1, d, f), lambda ei, j: (layer, ei, 0, 0)),
                  pl.BlockSpec((None, 1, d, f), lambda ei, j: (layer, ei, 0, 0)),
                  pl.BlockSpec((None, 1, f, d), lambda ei, j: (layer, ei, 0, 0))],
        out_specs=tile,
        out_shape=jax.ShapeDtypeStruct((e, cap_pad, d), BF16),
        scratch_shapes=[pltpu.VMEM((d, f), BF16), pltpu.VMEM((d, f), BF16), pltpu.VMEM((f, d), BF16)],
        compiler_params=_cp("parallel", "arbitrary"),
        name="moe_experts",
    )(xe, wg, wu, wd)


COMBINE_COLS = 512
COMBINE_BLKS = 2


def _combine_kernel(s0_ref, pos_ref, aff_ref, y_ref, x_ref, gate_ref, o_ref, acc_ref, *, nsub):
    t = pl.program_id(1)
    rows = lax.broadcasted_iota(I32, (WIN, TOK_BLK), 0)

    def selector(i, e, base):
        tok = _token_block(i)
        rel = pos_ref[e:e + 1, tok] - base
        return jnp.where(rows == rel, aff_ref[e:e + 1, tok], 0.0).astype(BF16)

    def contract(sel, ywin):
        return lax.dot_general(sel, ywin, (((0,), (0,)), ((), ())), preferred_element_type=F32)

    for i in range(nsub):
        plan = [_block_windows(s0_ref, t * nsub + i, e) for e in range(N_EXPERTS)]
        single = functools.reduce(jnp.logical_and, [nwin <= 1 for _, nwin in plan])
        tok = _token_block(i)

        @pl.when(single)
        def _():
            bases = [pl.multiple_of(start, ROW_ALIGN) for start, _ in plan]
            sel = jnp.concatenate([selector(i, e, b) for e, b in enumerate(bases)], axis=0)
            ywin = jnp.concatenate([y_ref[e, pl.ds(b, WIN), :] for e, b in enumerate(bases)], axis=0)
            o_ref[tok, :] = x_ref[tok, :] + gate_ref[...] * contract(sel, ywin)

        @pl.when(jnp.logical_not(single))
        def _():
            acc_ref[...] = jnp.zeros_like(acc_ref)
            for e, (start, nwin) in enumerate(plan):
                def window(w, carry):
                    base = pl.multiple_of(start + w * WIN, ROW_ALIGN)
                    acc_ref[...] += contract(selector(i, e, base), y_ref[e, pl.ds(base, WIN), :])
                    return carry

                lax.fori_loop(0, nwin, window, 0)
            o_ref[tok, :] = x_ref[tok, :] + gate_ref[...] * acc_ref[...]


def _combine(s0_flat, pos, aff_t, y, xres, gate_row):
    n, d = xres.shape
    e, cap_pad, _ = y.shape
    nsub = min(COMBINE_BLKS, n // TOK_BLK)
    rows = nsub * TOK_BLK
    cq = COMBINE_COLS
    tile = pl.BlockSpec((rows, cq), lambda q, t, s0: (t, q))
    return pl.pallas_call(
        functools.partial(_combine_kernel, nsub=nsub),
        grid_spec=pltpu.PrefetchScalarGridSpec(
            num_scalar_prefetch=1,
            grid=(d // cq, n // rows),
            in_specs=[pl.BlockSpec((e, rows), lambda q, t, s0: (0, t)),
                      pl.BlockSpec((e, rows), lambda q, t, s0: (0, t)),
                      pl.BlockSpec((e, cap_pad, cq), lambda q, t, s0: (0, 0, q), pipeline_mode=pl.Buffered(1)),
                      tile,
                      pl.BlockSpec((1, cq), lambda q, t, s0: (0, q))],
            out_specs=tile,
            scratch_shapes=[pltpu.VMEM((TOK_BLK, cq), F32)],
        ),
        out_shape=jax.ShapeDtypeStruct((n, d), F32),
        compiler_params=_cp("parallel", "arbitrary"),
        name="moe_combine",
    )(s0_flat, pos, aff_t, y, xres, gate_row)


def _moe_block(x, mod, gate_row, w_router, wg, wu, wd, layer):
    n = x.shape[0]
    cap = EC_CAPACITY * n // N_EXPERTS
    cap_pad = cap + TOK_BLK
    h, aff_t = _ffn_in(x, mod, w_router)
    pos, s0 = _select(aff_t, cap)
    s0_flat = s0[:, :, 0].reshape(-1)
    xe = _gather(s0_flat, pos, h, cap_pad)
    y = _experts(xe, wg, wu, wd, layer, cap)
    return _combine(s0_flat, pos, aff_t, y, x, gate_row)


QK_SCALE = (DA_HEAD_DIM ** -0.5) * math.log2(math.e)


def _split_bf16(x):
    hi = x.astype(BF16)
    return hi, (x - hi.astype(F32)).astype(BF16)


def _group_rms(u, gsel_ref, gain, eps):
    g = gsel_ref[...]
    hi, lo = _split_bf16(u * u)
    r = lax.rsqrt((_dot(hi, g) + _dot(lo, g)) * (1.0 / DA_HEAD_DIM) + eps)
    rhi, rlo = _split_bf16(r)
    return u * (_dot_nt(rhi, g) + _dot_nt(rlo, g)) * gain


def _rope(u, cos, sin_signed):
    d = u.shape[1]
    half = ROPE_AXIS_DIM // 2
    lane = lax.broadcasted_iota(I32, u.shape, 1)
    first = (lane & half) == 0
    swapped = jnp.where(first, pltpu.roll(u, d - half, 1), pltpu.roll(u, half, 1))
    return u * cos + swapped * sin_signed


def _qkv_kernel(*refs, rope):
    x_ref, mod_ref, w_ref, gsum_ref, qn_ref, kn_ref = refs[:6]
    q_ref, k_ref, v_ref = refs[-3:]
    if rope:
        cos_ref, sin_ref = refs[6:8]
    d = D_MODEL
    h = _norm_mod(x_ref[...], mod_ref[...]).astype(BF16)
    if rope:
        reps = d // cos_ref.shape[1]
        cos = _lane_tile(cos_ref[...], reps)
        sin = _lane_tile(sin_ref[...], reps)
    for part, (o_ref, gain_ref) in enumerate(((q_ref, qn_ref), (k_ref, kn_ref))):
        u = _dot(h, w_ref[:, part * d:(part + 1) * d])
        u = _group_rms(u, gsum_ref, gain_ref[...], NORM_EPS)
        if rope:
            u = _rope(u, cos, sin)
        if part == 0:
            u = u * QK_SCALE
        o_ref[...] = u.astype(BF16)
    v_ref[...] = _dot(h, w_ref[:, 2 * d:]).astype(BF16)


def _qkv(x, mod, w_bf, gsum, qn_row, kn_row, cos=None, sin=None, kv_rows=None, kv_offset=0, kv_into=None):
    n, d = x.shape
    tm = min(QKV_TILE, n)
    kv_rows = n if kv_rows is None else kv_rows
    off = kv_offset // tm
    assert kv_offset % tm == 0
    rope = cos is not None
    row = pl.BlockSpec((tm, d), lambda i: (i, 0))
    kv_row = pl.BlockSpec((tm, d), lambda i: (i + off, 0))
    in_specs = [row, _const_spec((8, d)), _const_spec((d, 3 * d)), _const_spec(gsum.shape),
                _const_spec((1, d)), _const_spec((1, d))]
    args = [x, mod, w_bf, gsum, qn_row, kn_row]
    if rope:
        tw = cos.shape[1]
        in_specs += [pl.BlockSpec((tm, tw), lambda i: (i, 0)), pl.BlockSpec((tm, tw), lambda i: (i, 0))]
        args += [cos, sin]
    aliases = {}
    if kv_into is not None:
        aliases = {len(args): 1, len(args) + 1: 2}
        in_specs += [pl.BlockSpec(memory_space=pl.ANY), pl.BlockSpec(memory_space=pl.ANY)]
        args += list(kv_into)
    kv = jax.ShapeDtypeStruct((kv_rows, d), BF16)
    return pl.pallas_call(
        functools.partial(_qkv_kernel, rope=rope),
        grid=(n // tm,),
        in_specs=in_specs,
        out_specs=[row, kv_row, kv_row],
        out_shape=[jax.ShapeDtypeStruct((n, d), BF16), kv, kv],
        input_output_aliases=aliases,
        compiler_params=_cp("parallel"),
        name="attn_qkv",
    )(*args)


NEG_BIG = -1e30
HEAD_UNROLL = 8


def _attn_kernel(q_ref, k_ref, v_ref, lam_ref, sub_ref, o_ref, m_scr, acc_scr, sa_scr, sb_scr, ma_scr, mb_scr,
                 pa_scr, pb_scr, vprev_scr, aa_scr, ab_scr, *,
                 lam_init):
    j = pl.program_id(1)
    nj = pl.num_programs(1)
    hd, vd = DA_HEAD_DIM, DA_V_DIM
    tk = k_ref.shape[0]

    last = DA_HEADS - 1

    @pl.when(j == 0)
    def _():
        m_scr[...] = jnp.full_like(m_scr, NEG_BIG)
        acc_scr[...] = jnp.zeros_like(acc_scr)
        pa_scr[...] = jnp.zeros_like(pa_scr)
        aa_scr[...] = jnp.ones_like(aa_scr)
        sb_scr[...] = jnp.full_like(sb_scr, 2.0 * NEG_BIG)
        mb_scr[...] = jnp.full_like(mb_scr, 2.0 * NEG_BIG)
        vprev_scr[...] = jnp.zeros_like(vprev_scr)

    lane = lax.broadcasted_iota(I32, (q_ref.shape[0], vd), 1)
    ones_col = jnp.where(lax.broadcasted_iota(I32, (tk, vd), 1) == 0, 1.0, 0.0).astype(BF16)

    def head_cols(h):
        return pl.ds(pl.multiple_of(h * vd, vd), vd)

    def scores(h, comp, s_ref, mx_ref):
        qb = q_ref[:, head_cols(h)]
        in_comp = (lane >= comp * hd) & (lane < (comp + 1) * hd)
        qm = jnp.where(in_comp, qb, jnp.zeros_like(qb))
        s = _dot_nt(qm, k_ref[:, head_cols(h)])
        s_ref[...] = s
        mx_ref[...] = jnp.broadcast_to(jnp.max(s, axis=1, keepdims=True), mx_ref.shape)

    def softmax_step(h, comp, s_ref, mx_ref, p_ref, alpha_ref):
        idx = 2 * h + comp
        m_prev = m_scr[idx]
        m_new = jnp.maximum(m_prev, mx_ref[...])
        alpha_ref[...] = jnp.exp2(m_prev - m_new)
        p_ref[...] = jnp.exp2(s_ref[...] - m_new[:, 0:1]).astype(BF16)
        m_scr[idx] = m_new

    def pv_step(h, comp, p_ref, alpha_ref):
        idx = 2 * h + comp
        pv = _dot(p_ref[...], jnp.concatenate([vprev_scr[...], ones_col], axis=1))
        acc_scr[idx] = _lane_tile(alpha_ref[...], 2) * acc_scr[idx] + pv

    def head(h, carry):
        hp = (h + last) & last
        scores(h, 0, sa_scr, ma_scr)
        pv_step(hp, 0, pa_scr, aa_scr)
        softmax_step(hp, 1, sb_scr, mb_scr, pb_scr, ab_scr)
        scores(h, 1, sb_scr, mb_scr)
        pv_step(hp, 1, pb_scr, ab_scr)
        softmax_step(h, 0, sa_scr, ma_scr, pa_scr, aa_scr)
        vprev_scr[...] = v_ref[:, head_cols(h)]
        return carry

    def head_group(g, carry):
        for u in range(HEAD_UNROLL):
            carry = head(HEAD_UNROLL * g + u, carry)
        return carry

    lax.fori_loop(0, DA_HEADS // HEAD_UNROLL, head_group, 0)

    @pl.when(j == nj - 1)
    def _():
        pv_step(last, 0, pa_scr, aa_scr)
        softmax_step(last, 1, sb_scr, mb_scr, pb_scr, ab_scr)
        pv_step(last, 1, pb_scr, ab_scr)
        lp = lam_ref[...]
        lam = (jnp.exp(jnp.sum(lp[0:1] * lp[1:2], axis=1, keepdims=True))
               - jnp.exp(jnp.sum(lp[2:3] * lp[3:4], axis=1, keepdims=True)) + lam_init)
        for h in range(DA_HEADS):
            a0 = acc_scr[2 * h]
            a1 = acc_scr[2 * h + 1]
            o = a0[:, :vd] / a0[:, vd:vd + 1] - lam * (a1[:, :vd] / a1[:, vd:vd + 1])
            ms = jnp.mean(o * o, axis=1, keepdims=True)
            o = o * lax.rsqrt(ms + SUBLN_EPS) * (sub_ref[...] * (1.0 - lam_init))
            o_ref[:, h * vd:(h + 1) * vd] = o.astype(BF16)


def _attention(q, k_all, v_all, lam_rows, subln_row, lam_init):
    n, d = q.shape
    nk = k_all.shape[0]
    tq = min(512, n)
    tk = 1280 if nk % 1280 == 0 else 256
    assert nk % tk == 0
    nc = 2 * DA_HEADS
    return pl.pallas_call(
        functools.partial(_attn_kernel, lam_init=lam_init),
        grid=(n // tq, nk // tk),
        in_specs=[pl.BlockSpec((tq, d), lambda i, j: (i, 0)),
                  pl.BlockSpec((tk, d), lambda i, j: (j, 0)),
                  pl.BlockSpec((tk, d), lambda i, j: (j, 0)),
                  _const_spec((8, DA_HEAD_DIM)), _const_spec((1, DA_V_DIM))],
        out_specs=pl.BlockSpec((tq, d), lambda i, j: (i, 0)),
        out_shape=jax.ShapeDtypeStruct((n, d), BF16),
        scratch_shapes=[pltpu.VMEM((nc, tq, DA_V_DIM), F32), pltpu.VMEM((nc, tq, 2 * DA_V_DIM), F32),
                        pltpu.VMEM((tq, tk), F32), pltpu.VMEM((tq, tk), F32),
                        pltpu.VMEM((tq, DA_V_DIM), F32), pltpu.VMEM((tq, DA_V_DIM), F32),
                        pltpu.VMEM((tq, tk), BF16), pltpu.VMEM((tq, tk), BF16),
                        pltpu.VMEM((tk, DA_V_DIM), BF16),
                        pltpu.VMEM((tq, DA_V_DIM), F32), pltpu.VMEM((tq, DA_V_DIM), F32)],
        compiler_params=_cp("parallel", "arbitrary"),
        name="diff_attention",
    )(q, k_all, v_all, lam_rows, subln_row)


def _rope_tables(n):
    lane = np.arange(2 * DA_HEAD_DIM) % DA_HEAD_DIM
    nf = ROPE_AXIS_DIM // 2
    inv = ROPE_THETA ** (-np.arange(0, ROPE_AXIS_DIM, 2, dtype=np.float32) / ROPE_AXIS_DIM)
    by_row = jnp.asarray(np.where(lane < ROPE_AXIS_DIM, inv[lane % nf], 0.0).astype(np.float32))
    by_col = jnp.asarray(np.where(lane >= ROPE_AXIS_DIM, inv[lane % nf], 0.0).astype(np.float32))
    sign = jnp.asarray(np.where(lane % ROPE_AXIS_DIM < nf, -1.0, 1.0).astype(np.float32))
    t = jnp.arange(n, dtype=I32)
    row = (t // GRID_W).astype(F32)[:, None]
    col = (t % GRID_W).astype(F32)[:, None]
    ang = row * by_row[None] + col * by_col[None]
    return jnp.cos(ang), jnp.sin(ang) * sign[None]


def _hyena_layer(x, mod, gate_row, kspec, inv_norm, w_in_bf, b_in, conv_w, conv_b, skip, w_out_bf, b_out):
    seq = x.shape[0]
    v, x1, x2 = _hy_in(x, mod, w_in_bf, b_in, conv_w, conv_b)
    z = _long_conv_gate(v, x1, skip[0:1], kspec, inv_norm, 0, seq)
    z = _long_conv_gate(z, x2, skip[1:2], kspec, inv_norm, 1, seq)
    return _proj_res(z, w_out_bf, b_out.reshape(1, -1), gate_row, x)


def kernel(x, c, ctx, c_ctx, ada_w, ada_b, norm_mix, norm_ffn, hy_w_in, hy_b_in, hy_conv_w, hy_conv_b, hy_f_w1, hy_f_b1, hy_f_w2, hy_f_b2, hy_f_w3, hy_f_b3, hy_f_w4, hy_f_freq, hy_skip, hy_w_out, hy_b_out, da_w_qkv, da_q_norm, da_k_norm, da_lam_q1, da_lam_k1, da_lam_q2, da_lam_k2, da_subln, da_w_out, moe_router, moe_w_gate, moe_w_up, moe_w_down):
    d = D_MODEL
    depth = ada_w.shape[0]
    assert x.shape[0] == 1 and x.shape[2] == d
    xs = x[0]
    cs = ctx[0]
    cond8 = jnp.concatenate([c[0:1], c_ctx[None], jnp.zeros((6, d), F32)], axis=0)
    mods = _adaln(cond8, ada_w, ada_b)

    def mod_slice(i, row, k):
        return mods[i, row, k * d:(k + 1) * d][None]

    for i in range(depth):
        last = i == depth - 1
        j = i // 2
        mix_x = _mod_rows(norm_mix[i], mods[i], 0, 0)
        mix_c = _mod_rows(norm_mix[i], mods[i], 1, 0)
        if i % 2 == 0:
            fparams = (hy_f_w1[j], hy_f_b1[j], hy_f_w2[j], hy_f_b2[j], hy_f_w3[j], hy_f_b3[j], hy_f_w4[j],
                       hy_f_freq[j])
            shared = (hy_w_in[j].astype(BF16), hy_b_in[j], hy_conv_w[j], hy_conv_b[j], hy_skip[j],
                      hy_w_out[j].astype(BF16), hy_b_out[j])
            kspec, asum = _filter_spectrum(xs.shape[0], fparams)
            inv_norm = 1.0 / (asum[0:1] + HY_FILTER_EPS)
            new_x = _hyena_layer(xs, mix_x, mod_slice(i, 0, 2), kspec, inv_norm, *shared)
            if not last:
                kspec_c, asum_c = _filter_spectrum(cs.shape[0], fparams)
                inv_c = 1.0 / (asum_c[0:1] + HY_FILTER_EPS)
                cs = _hyena_layer(cs, mix_c, mod_slice(i, 1, 2), kspec_c, inv_c, *shared)
            xs = new_x
        else:
            lam_init = 0.8 - 0.6 * math.exp(-0.3 * i)
            w_qkv = da_w_qkv[j].astype(BF16)
            gidx = np.arange(d) // DA_HEAD_DIM
            gsum = jnp.asarray((gidx[:, None] == np.arange(128)[None]).astype(np.float32)).astype(BF16)
            qn = jnp.tile(da_q_norm[j], 2 * DA_HEADS)[None]
            kn = jnp.tile(da_k_norm[j], 2 * DA_HEADS)[None]
            cos, sin = _rope_tables(xs.shape[0])
            n_lat = xs.shape[0]
            nk = n_lat + cs.shape[0]
            qx, k_all, v_all = _qkv(xs, mix_x, w_qkv, gsum, qn, kn, cos, sin, kv_rows=nk)
            qc, k_all, v_all = _qkv(cs, mix_c, w_qkv, gsum, qn, kn, kv_rows=nk, kv_offset=n_lat,
                                    kv_into=(k_all, v_all))
            lam_rows = jnp.concatenate([da_lam_q1[j][None], da_lam_k1[j][None], da_lam_q2[j][None],
                                        da_lam_k2[j][None], jnp.zeros((4, DA_HEAD_DIM), F32)], axis=0)
            w_out = da_w_out[j].astype(BF16)
            zero_b = jnp.zeros((1, d), F32)
            ox = _attention(qx, k_all, v_all, lam_rows, da_subln[j][None], lam_init)
            new_x = _proj_res(ox, w_out, zero_b, mod_slice(i, 0, 2), xs)
            if not last:
                oc = _attention(qc, k_all[n_lat:], v_all[n_lat:], lam_rows, da_subln[j][None], lam_init)
                cs = _proj_res(oc, w_out, zero_b, mod_slice(i, 1, 2), cs)
            xs = new_x
        experts = (moe_w_gate, moe_w_up, moe_w_down, i)
        if not last:
            cs = _moe_block(cs, _mod_rows(norm_ffn[i], mods[i], 1, 3), mod_slice(i, 1, 5), moe_router[i], *experts)
        xs = _moe_block(xs, _mod_rows(norm_ffn[i], mods[i], 0, 3), mod_slice(i, 0, 5), moe_router[i], *experts)
    return xs[None]
```
